```python
import math, functools
import jax, jax.numpy as jnp
from jax import lax
import numpy as np

D_MODEL = 1024
BATCH = 16
SEQ = 256
DEPTH = 2
DEC_BATCH = 8
DEC_SEQ = 4096
PAST_LEN = 512

GRID_W = 64
H_A = 6
DA = 32
V_A = 2 * DA
W_B = 256
CONV_W = 3
H_C = 6
NOPE_C = 64
ROPE_C = 32
QK_C = NOPE_C + ROPE_C
V_C = 64
Q_RANK = 192
KV_RANK = 128
IN_COLS = 2 * H_A * 2 * DA + H_A * V_A + 3 * W_B + Q_RANK + KV_RANK + ROPE_C
MIX_W = H_A * V_A + W_B + H_C * V_C
D_FF = 2816
N_EXPERTS = 8
TOP_K = 2
MOE_FF = 1024
N_DENSE = (DEPTH + 1) // 2
N_MOE = DEPTH // 2
ROPE_THETA = 10000.0
Q_BLOCK = 128
EPS = 1e-6
F32 = jnp.float32

kernel_name = 'hybrid_diffattn_shortconv_mla_denoise_step'


def rmsnorm(x, g):
    xf = x.astype(F32)
    y = xf * lax.rsqrt(jnp.mean(xf * xf, axis=-1, keepdims=True) + EPS)
    return (y * g.astype(F32)).astype(x.dtype)


def split_in(proj):
    sizes = (H_A * 2 * DA, H_A * 2 * DA, H_A * V_A, W_B, W_B, W_B, Q_RANK, KV_RANK, ROPE_C)
    outs = []
    start = 0
    for s in sizes:
        outs.append(proj[..., start:start + s])
        start += s
    return outs


def grid_positions(T):
    t = jnp.arange(T, dtype=jnp.int32)
    return (t // GRID_W).astype(F32), (t % GRID_W).astype(F32)


def axial_rope(x, rows, cols):
    half = x.shape[-1] // 2
    n = half // 2
    inv = jnp.power(ROPE_THETA, -jnp.arange(n, dtype=F32) / n)
    shape = (rows.shape[0],) + (1,) * (x.ndim - 3) + (n,)

    def rot(xh, pos):
        ang = pos[:, None] * inv[None, :]
        cos = jnp.cos(ang).reshape(shape).astype(x.dtype)
        sin = jnp.sin(ang).reshape(shape).astype(x.dtype)
        x1, x2 = xh[..., :n], xh[..., n:]
        return jnp.concatenate([x1 * cos - x2 * sin, x1 * sin + x2 * cos], axis=-1)

    return jnp.concatenate([rot(x[..., :half], rows), rot(x[..., half:], cols)], axis=-1)


def rope_tail(x, rows, cols):
    return jnp.concatenate([x[..., :NOPE_C], axial_rope(x[..., NOPE_C:], rows, cols)], axis=-1)


def sweep_query_blocks(fn, q):
    B, T = q.shape[:2]
    nb = T // Q_BLOCK
    qb = jnp.moveaxis(q.reshape((B, nb, Q_BLOCK) + q.shape[2:]), 1, 0)
    out = jnp.moveaxis(lax.map(fn, qb), 0, 1)
    return out.reshape((B, T) + out.shape[3:])


def diff_attention(q, k, v, lam):
    scale = DA ** -0.5

    def block(qb):
        s = jnp.einsum('bqhcd,bkhcd->bhcqk', qb, k).astype(F32) * scale
        p = jax.nn.softmax(s, axis=-1)
        w = p[:, :, 0] - lam * p[:, :, 1]
        return jnp.einsum('bhqk,bkhe->bqhe', w.astype(v.dtype), v)

    return sweep_query_blocks(block, q)


def softmax_attention(q, k, v):
    scale = QK_C ** -0.5

    def block(qb):
        s = jnp.einsum('bqhd,bkhd->bhqk', qb, k).astype(F32) * scale
        p = jax.nn.softmax(s, axis=-1)
        return jnp.einsum('bhqk,bkhe->bqhe', p.astype(v.dtype), v)

    return sweep_query_blocks(block, q)


def conv3_centred(u, w, b):
    L = u.shape[-2]
    pad = [(0, 0)] * (u.ndim - 2) + [(1, 1), (0, 0)]
    up = jnp.pad(u, pad)
    return up[..., 0:L, :] * w[0] + up[..., 1:L + 1, :] * w[1] + up[..., 2:L + 2, :] * w[2] + b


def mla_keys(ckv_n, k_rope, w_kb, w_vb, knorm_c):
    B, S = ckv_n.shape[:2]
    k_nope = (ckv_n @ w_kb).reshape(B, S, H_C, NOPE_C)
    v = (ckv_n @ w_vb).reshape(B, S, H_C, V_C)
    k_r = jnp.broadcast_to(k_rope[:, :, None, :], (B, S, H_C, ROPE_C))
    k = rmsnorm(jnp.concatenate([k_nope, k_r], axis=-1), knorm_c)
    return k, v


def swiglu(x, wg, wu, wd):
    return (jax.nn.silu(x @ wg) * (x @ wu)) @ wd


def moe_ffn(x, router_w, wg, wu, wd):
    B, T, D = x.shape
    xt = x.reshape(B * T, D)
    logits = (xt @ router_w).astype(F32)
    topv, topi = lax.top_k(logits, TOP_K)
    gates = jax.nn.softmax(topv, axis=-1)
    combine = jnp.sum(jax.nn.one_hot(topi, N_EXPERTS, dtype=F32) * gates[..., None], axis=1).astype(x.dtype)
    out = jnp.zeros_like(xt)
    for e in range(N_EXPERTS):
        out = out + combine[:, e:e + 1] * swiglu(xt, wg[e], wu[e], wd[e])
    return out.reshape(B, T, D)


def token_mixer(h, lw, lam, l_init, ctx):
    B, T, _ = h.shape
    a_q, a_k, a_v, b_b, b_c, b_x, c_q, c_kv, c_kr = split_in(h @ lw['w_in'])
    q_a = rmsnorm(a_q.reshape(B, T, H_A, 2, DA), lw['qnorm_a'])
    k_a = rmsnorm(a_k.reshape(B, T, H_A, 2, DA), lw['knorm_a'])
    v_a = a_v.reshape(B, T, H_A, V_A)
    q_c = rmsnorm((rmsnorm(c_q, lw['norm_qa']) @ lw['w_qb']).reshape(B, T, H_C, QK_C), lw['qnorm_c'])
    ckv_n = rmsnorm(c_kv, lw['norm_kva'])
    k_c, v_c = mla_keys(ckv_n, c_kr, lw['w_kb'], lw['w_vb'], lw['knorm_c'])
    u = b_c * b_x
    if ctx is None:
        state = (k_a, v_a, ckv_n, c_kr)
        ka_all, va_all, kc_all, vc_all = k_a, v_a, k_c, v_c
        conv = conv3_centred(u, lw['conv_w'], lw['conv_b'])
    else:
        state = None
        rows, cols = grid_positions(T)
        q_a = axial_rope(q_a, rows, cols)
        k_a = axial_rope(k_a, rows, cols)
        q_c = rope_tail(q_c, rows, cols)
        k_c = rope_tail(k_c, rows, cols)
        ctx_k, ctx_v, ctx_ckv, ctx_kr = ctx
        kc_ctx, vc_ctx = mla_keys(ctx_ckv, ctx_kr, lw['w_kb'], lw['w_vb'], lw['knorm_c'])
        ka_all = jnp.concatenate([ctx_k, k_a], axis=1)
        va_all = jnp.concatenate([ctx_v, v_a], axis=1)
        kc_all = jnp.concatenate([kc_ctx, k_c], axis=1)
        vc_all = jnp.concatenate([vc_ctx, v_c], axis=1)
        rows_n = T // GRID_W
        conv = conv3_centred(u.reshape(B, rows_n, GRID_W, W_B), lw['conv_w'], lw['conv_b']).reshape(B, T, W_B)
    o_a = diff_attention(q_a, ka_all, va_all, lam)
    o_a = rmsnorm(o_a, lw['subln_a']) * (1.0 - l_init)
    o_c = softmax_attention(q_c, kc_all, vc_all)
    o_b = b_b * conv
    y = jnp.concatenate([o_a.reshape(B, T, H_A * V_A), o_b, o_c.reshape(B, T, H_C * V_C)], axis=-1) @ lw['w_out']
    return y, state


def trunk_layer(x, mod, lw, lam, l_init, ffn, ctx):
    shift1, scale1, gate1, shift2, scale2, gate2 = jnp.split(mod, 6, axis=-1)
    h = rmsnorm(x, lw['norm_mix']) * (1 + scale1) + shift1
    y, state = token_mixer(h, lw, lam, l_init, ctx)
    x = x + gate1 * y
    h = rmsnorm(x, lw['norm_ffn']) * (1 + scale2) + shift2
    x = x + gate2 * ffn(h)
    return x, state


def setup_inputs(seed: int = 0) -> dict:
    key = jax.random.key(seed)
    ks = iter(jax.random.split(key, 64))

    def nrm(shape, scale=1.0):
        return scale * jax.random.normal(next(ks), shape, jnp.float32)

    def gain(shape):
        return 1.0 + 0.05 * nrm(shape)

    s = D_MODEL ** -0.5
    return {
        'x_prompt': nrm((BATCH, SEQ, D_MODEL)),
        'x_sample': nrm((DEC_BATCH, DEC_SEQ, D_MODEL)),
        'cache_diff_k': nrm((DEC_BATCH, DEPTH, PAST_LEN, H_A, 2, DA)),
        'cache_diff_v': nrm((DEC_BATCH, DEPTH, PAST_LEN, H_A, V_A)),
        'cache_mla_ckv': nrm((DEC_BATCH, DEPTH, PAST_LEN, KV_RANK)),
        'cache_mla_krope': nrm((DEC_BATCH, DEPTH, PAST_LEN, ROPE_C)),
        'c': nrm((DEC_BATCH, D_MODEL)),
        'c_ctx': nrm((D_MODEL,)),
        'w_ada': nrm((DEPTH, D_MODEL, 6 * D_MODEL), 0.5 * s),
        'b_ada': nrm((DEPTH, 6 * D_MODEL), 0.02),
        'norm_mix': gain((DEPTH, D_MODEL)),
        'norm_ffn': gain((DEPTH, D_MODEL)),
        'w_in': nrm((DEPTH, D_MODEL, IN_COLS), s),
        'qnorm_a': gain((DEPTH, DA)),
        'knorm_a': gain((DEPTH, DA)),
        'lambda_q1': nrm((DEPTH, DA), 0.1),
        'lambda_k1': nrm((DEPTH, DA), 0.1),
        'lambda_q2': nrm((DEPTH, DA), 0.1),
        'lambda_k2': nrm((DEPTH, DA), 0.1),
        'subln_a': gain((DEPTH, V_A)),
        'conv_w': nrm((DEPTH, CONV_W, W_B), CONV_W ** -0.5),
        'conv_b': nrm((DEPTH, W_B), 0.02),
        'norm_qa': gain((DEPTH, Q_RANK)),
        'w_qb': nrm((DEPTH, Q_RANK, H_C * QK_C), Q_RANK ** -0.5),
        'norm_kva': gain((DEPTH, KV_RANK)),
        'w_kb': nrm((DEPTH, KV_RANK, H_C * NOPE_C), KV_RANK ** -0.5),
        'w_vb': nrm((DEPTH, KV_RANK, H_C * V_C), KV_RANK ** -0.5),
        'qnorm_c': gain((DEPTH, QK_C)),
        'knorm_c': gain((DEPTH, QK_C)),
        'w_out': nrm((DEPTH, MIX_W, D_MODEL), MIX_W ** -0.5),
        'ffn_w_gate': nrm((N_DENSE, D_MODEL, D_FF), s),
        'ffn_w_up': nrm((N_DENSE, D_MODEL, D_FF), s),
        'ffn_w_down': nrm((N_DENSE, D_FF, D_MODEL), D_FF ** -0.5),
        'router_w': nrm((N_MOE, D_MODEL, N_EXPERTS), s),
        'moe_w_gate': nrm((N_MOE, N_EXPERTS, D_MODEL, MOE_FF), s),
        'moe_w_up': nrm((N_MOE, N_EXPERTS, D_MODEL, MOE_FF), s),
        'moe_w_down': nrm((N_MOE, N_EXPERTS, MOE_FF, D_MODEL), MOE_FF ** -0.5),
    }


def reference(x_prompt, x_sample, cache_diff_k, cache_diff_v, cache_mla_ckv, cache_mla_krope, c, c_ctx,
              w_ada, b_ada, norm_mix, norm_ffn, w_in, qnorm_a, knorm_a, lambda_q1, lambda_k1, lambda_q2,
              lambda_k2, subln_a, conv_w, conv_b, norm_qa, w_qb, norm_kva, w_kb, w_vb, qnorm_c, knorm_c,
              w_out, ffn_w_gate, ffn_w_up, ffn_w_down, router_w, moe_w_gate, moe_w_up, moe_w_down):
    xp = x_prompt
    xs = x_sample
    st_k, st_v, st_ckv, st_kr = [], [], [], []
    for l in range(DEPTH):
        lw = {
            'w_in': w_in[l], 'w_out': w_out[l], 'norm_mix': norm_mix[l], 'norm_ffn': norm_ffn[l],
            'qnorm_a': qnorm_a[l], 'knorm_a': knorm_a[l], 'subln_a': subln_a[l],
            'conv_w': conv_w[l], 'conv_b': conv_b[l],
            'norm_qa': norm_qa[l], 'w_qb': w_qb[l], 'norm_kva': norm_kva[l],
            'w_kb': w_kb[l], 'w_vb': w_vb[l], 'qnorm_c': qnorm_c[l], 'knorm_c': knorm_c[l],
        }
        l_init = 0.8 - 0.6 * math.exp(-0.3 * l)
        lam = (jnp.exp(jnp.sum(lambda_q1[l].astype(F32) * lambda_k1[l].astype(F32)))
               - jnp.exp(jnp.sum(lambda_q2[l].astype(F32) * lambda_k2[l].astype(F32))) + l_init)
        i = l // 2
        if l % 2 == 0:
            ffn = functools.partial(swiglu, wg=ffn_w_gate[i], wu=ffn_w_up[i], wd=ffn_w_down[i])
        else:
            ffn = functools.partial(moe_ffn, router_w=router_w[i], wg=moe_w_gate[i], wu=moe_w_up[i], wd=moe_w_down[i])
        mod_ctx = (jax.nn.silu(c_ctx)[None, :] @ w_ada[l] + b_ada[l])[:, None, :]
        mod_lat = (jax.nn.silu(c) @ w_ada[l] + b_ada[l])[:, None, :]
        xp, st = trunk_layer(xp, mod_ctx, lw, lam, l_init, ffn, None)
        st_k.append(st[0])
        st_v.append(st[1])
        st_ckv.append(st[2])
        st_kr.append(st[3])
        ctx = (cache_diff_k[:, l], cache_diff_v[:, l], cache_mla_ckv[:, l], cache_mla_krope[:, l])
        xs, _ = trunk_layer(xs, mod_lat, lw, lam, l_init, ffn, ctx)
    new_diff_k = jnp.stack(st_k, axis=1)
    new_diff_v = jnp.stack(st_v, axis=1)
    new_mla_ckv = jnp.stack(st_ckv, axis=1)
    new_mla_krope = jnp.stack(st_kr, axis=1)
    return (xp, xs, new_diff_k, new_diff_v, new_mla_ckv, new_mla_krope)
```

```python
import functools
import math

import jax
import jax.numpy as jnp
from jax import lax
from jax.experimental import pallas as pl
from jax.experimental.pallas import tpu as pltpu

F32 = jnp.float32
BF16 = jnp.bfloat16

GRID_W = 64
H_A = 6
DA = 32
V_A = 2 * DA
W_B = 256
H_C = 6
NOPE_C = 64
ROPE_C = 32
QK_C = NOPE_C + ROPE_C
V_C = 64
Q_RANK = 192
KV_RANK = 128
N_EXPERTS = 8
ROPE_THETA = 10000.0
EPS = 1e-6
LOG2E = 1.4426950408889634

LANES = 128
MXU_DIM = 256
HEAD_PAD = 128

A_W = H_A * 2 * DA
OFF_AQ = 0
OFF_AK = A_W
OFF_AV = 2 * A_W
OFF_BB = 3 * A_W
OFF_BC = OFF_BB + W_B
OFF_BX = OFF_BC + W_B
OFF_CKV = OFF_BX + W_B
OFF_CQ = OFF_CKV + KV_RANK
OFF_CKR = OFF_CQ + Q_RANK
KR_CHUNK = (OFF_CKR // LANES) * LANES
assert OFF_CKR - KR_CHUNK == NOPE_C
IN_EXT = KR_CHUNK + LANES
C_W = H_C * HEAD_PAD

VMEM_LIMIT = 56 * 1024 * 1024


def _cparams(sem):
    return pltpu.CompilerParams(dimension_semantics=sem, vmem_limit_bytes=VMEM_LIMIT)


def _const_spec(shape):
    nd = len(shape)
    return pl.BlockSpec(shape, lambda *_: (0,) * nd, pipeline_mode=pl.Buffered(1))


def _rms(x, axis=-1):
    return x * lax.rsqrt(jnp.mean(x * x, axis=axis, keepdims=True) + EPS)


def _split_dot(x, w):
    hi = x.astype(BF16)
    lo = (x - hi.astype(F32)).astype(BF16)
    return (jnp.dot(hi, w, preferred_element_type=F32)
            + jnp.dot(lo, w, preferred_element_type=F32))


def _ada_kernel(c_ref, w_ref, b_ref, o_ref):
    c = c_ref[...]
    s = (c * jax.nn.sigmoid(c)).astype(BF16)
    o_ref[0] = jnp.dot(s, w_ref[0].astype(BF16), preferred_element_type=F32) + b_ref[0]


def _ada(cvec, w_ada, b_ada):
    L, D, N6 = w_ada.shape
    R = cvec.shape[0]
    tn = 1536
    assert N6 % tn == 0
    return pl.pallas_call(
        _ada_kernel,
        grid=(L, N6 // tn),
        in_specs=[
            pl.BlockSpec((R, D), lambda l, j: (0, 0)),
            pl.BlockSpec((1, D, tn), lambda l, j: (l, 0, j)),
            pl.BlockSpec((1, 1, tn), lambda l, j: (l, 0, j)),
        ],
        out_specs=pl.BlockSpec((1, R, tn), lambda l, j: (l, 0, j)),
        out_shape=jax.ShapeDtypeStruct((L, R, N6), F32),
        compiler_params=_cparams(("arbitrary", "arbitrary")),
        name="ada_mod",
    )(cvec, w_ada, b_ada.reshape(L, 1, N6))


def _rope(x, tab_ref):
    up = pltpu.roll(x, LANES - 8, 1)
    dn = pltpu.roll(x, 8, 1)
    return x * tab_ref[0] + up * tab_ref[1] + dn * tab_ref[2]


def _mla_kv(ckv_n, kr_placed, wkb_ref, wvb_ref, gkc_ref, rope_ref):
    cb = ckv_n.astype(BF16)
    kn = jnp.dot(cb, wkb_ref[...], preferred_element_type=F32)
    vc = jnp.dot(cb, wvb_ref[...], preferred_element_type=F32)
    ks = []
    for h in range(H_C):
        kp = kn[:, h * HEAD_PAD:(h + 1) * HEAD_PAD] + kr_placed
        ss = jnp.sum(kp * kp, axis=-1, keepdims=True) * (1.0 / QK_C)
        kp = kp * lax.rsqrt(ss + EPS) * gkc_ref[...]
        if rope_ref is not None:
            kp = _rope(kp, rope_ref)
        ks.append(kp)
    return ks, vc


def _pre_mix_kernel(is_ctx, seg, *refs):
    (x_ref, mod_ref, nmix_ref, win_ref, b32_ref, gqk_ref, gqa_ref, gkva_ref, gqc_ref,
     gkc_ref, wqb_ref, wkb_ref, wvb_ref, cw_ref, cb_ref) = refs[:15]
    refs = refs[15:]
    if is_ctx:
        ra_ref = rc_ref = None
    else:
        ra_ref, rc_ref = refs[:2]
        refs = refs[2:]
    qa_ref, ka_ref, va_ref, ob_ref, qc_ref, kc_ref, vc_ref = refs[:7]
    refs = refs[7:]

    x = x_ref[...]
    shift1 = mod_ref[0, 0:1, :]
    scale1 = mod_ref[0, 1:2, :]
    h = _rms(x) * nmix_ref[...] * (1.0 + scale1) + shift1
    proj = jnp.dot(h.astype(BF16), win_ref[...], preferred_element_type=F32)

    aqk = proj[:, OFF_AQ:OFF_AV]
    sq = aqk * aqk
    ss = jnp.concatenate(
        [_split_dot(sq[:, c:c + MXU_DIM], b32_ref[...]) for c in range(0, 2 * A_W, MXU_DIM)], axis=-1)
    aqk = aqk * lax.rsqrt(ss * (1.0 / DA) + EPS) * gqk_ref[...]
    if is_ctx:
        ka_st_ref, va_st_ref, ckv_st_ref, ckr_st_ref = refs
        ka_st_ref[...] = aqk[:, A_W:]
    else:
        aqk = jnp.concatenate(
            [_rope(aqk[:, c:c + LANES], ra_ref) for c in range(0, 2 * A_W, LANES)], axis=-1)
    qa_ref[...] = (aqk[:, :A_W] * (DA ** -0.5 * LOG2E)).astype(BF16)
    ka_ref[...] = aqk[:, A_W:].astype(BF16)
    av = proj[:, OFF_AV:OFF_BB]
    va_ref[...] = av.astype(BF16)

    bb = proj[:, OFF_BB:OFF_BC]
    u = proj[:, OFF_BC:OFF_BX] * proj[:, OFF_BX:OFF_CKV]
    tm = u.shape[0]
    row = lax.broadcasted_iota(jnp.int32, u.shape, 0) & (seg - 1)
    prev = jnp.where(row == 0, 0.0, pltpu.roll(u, 1, 0))
    nxt = jnp.where(row == seg - 1, 0.0, pltpu.roll(u, tm - 1, 0))
    conv = prev * cw_ref[0:1, :] + u * cw_ref[1:2, :] + nxt * cw_ref[2:3, :] + cb_ref[...]
    ob_ref[...] = (bb * conv).astype(BF16)

    cq = proj[:, OFF_CQ:OFF_CKR]
    cqn = (_rms(cq) * gqa_ref[...]).astype(BF16)
    qc = jnp.dot(cqn, wqb_ref[...], preferred_element_type=F32)
    for hh in range(H_C):
        qh = qc[:, hh * HEAD_PAD:(hh + 1) * HEAD_PAD]
        s2 = jnp.sum(qh * qh, axis=-1, keepdims=True) * (1.0 / QK_C)
        qh = qh * lax.rsqrt(s2 + EPS) * gqc_ref[...]
        if not is_ctx:
            qh = _rope(qh, rc_ref)
        qc_ref[:, hh * HEAD_PAD:(hh + 1) * HEAD_PAD] = (qh * (QK_C ** -0.5 * LOG2E)).astype(BF16)
    ckv_n = _rms(proj[:, OFF_CKV:OFF_CQ]) * gkva_ref[...]
    krc = proj[:, KR_CHUNK:KR_CHUNK + LANES]
    lane = lax.broadcasted_iota(jnp.int32, krc.shape, 1)
    kr_placed = jnp.where((lane >= NOPE_C) & (lane < QK_C), krc, 0.0)
    ks, vc = _mla_kv(ckv_n, kr_placed, wkb_ref, wvb_ref, gkc_ref, rc_ref)
    for hh in range(H_C):
        kc_ref[:, hh * HEAD_PAD:(hh + 1) * HEAD_PAD] = ks[hh].astype(BF16)
    vc_ref[...] = vc.astype(BF16)
    if is_ctx:
        va_st_ref[...] = av
        ckv_st_ref[...] = ckv_n
        ckr_st_ref[...] = krc[:, NOPE_C:QK_C]


def _pre_mix(x, mod, lw, rope, *, is_ctx, tm, seg, tiles_per_mod):
    N, D = x.shape
    nt = N // tm
    row_spec = lambda w: pl.BlockSpec((tm, w), lambda i: (i, 0))
    in_specs = [
        row_spec(D),
        pl.BlockSpec((1, 6, D), lambda i: (i // tiles_per_mod, 0, 0)),
        _const_spec((1, D)),
        _const_spec((D, IN_EXT)),
        _const_spec((MXU_DIM, MXU_DIM)),
        _const_spec((1, 2 * A_W)),
        _const_spec((1, Q_RANK)),
        _const_spec((1, KV_RANK)),
        _const_spec((1, HEAD_PAD)),
        _const_spec((1, HEAD_PAD)),
        _const_spec((Q_RANK, C_W)),
        _const_spec((KV_RANK, C_W)),
        _const_spec((KV_RANK, H_C * V_C)),
        _const_spec((3, W_B)),
        _const_spec((1, W_B)),
    ]
    args = [x, mod, lw["norm_mix"], lw["w_in"], lw["b32"], lw["gqk"], lw["gqa"], lw["gkva"],
            lw["gqc"], lw["gkc"], lw["w_qb"], lw["w_kb"], lw["w_vb"], lw["conv_w"], lw["conv_b"]]
    if not is_ctx:
        tiles_per_seq = rope[0].shape[1] // tm
        rspec = pl.BlockSpec((3, tm, LANES), lambda i: (0, i % tiles_per_seq, 0))
        in_specs += [rspec, rspec]
        args += list(rope)
    widths = [A_W, A_W, A_W, W_B, C_W, C_W, H_C * V_C]
    out_specs = [row_spec(w) for w in widths]
    out_shape = [jax.ShapeDtypeStruct((N, w), BF16) for w in widths]
    if is_ctx:
        st_w = [A_W, A_W, KV_RANK, ROPE_C]
        out_specs += [row_spec(w) for w in st_w]
        out_shape += [jax.ShapeDtypeStruct((N, w), F32) for w in st_w]
    return pl.pallas_call(
        functools.partial(_pre_mix_kernel, is_ctx, seg),
        grid=(nt,),
        in_specs=in_specs,
        out_specs=out_specs,
        out_shape=out_shape,
        compiler_params=_cparams(("arbitrary",)),
        name="pre_mix_ctx" if is_ctx else "pre_mix_lat",
    )(*args)


def _cache_kv_kernel(ckv_ref, kr_ref, wkb_ref, wvb_ref, gkc_ref, kc_ref, vc_ref):
    ks, vc = _mla_kv(ckv_ref[...], kr_ref[...], wkb_ref, wvb_ref, gkc_ref, None)
    for hh in range(H_C):
        kc_ref[:, hh * HEAD_PAD:(hh + 1) * HEAD_PAD] = ks[hh].astype(BF16)
    vc_ref[...] = vc.astype(BF16)


def _cache_kv(ckv, kr_placed, lw, tm):
    N = ckv.shape[0]
    row_spec = lambda w: pl.BlockSpec((tm, w), lambda i: (i, 0))
    return pl.pallas_call(
        _cache_kv_kernel,
        grid=(N // tm,),
        in_specs=[row_spec(KV_RANK), row_spec(HEAD_PAD), _const_spec((KV_RANK, C_W)),
                  _const_spec((KV_RANK, H_C * V_C)), _const_spec((1, HEAD_PAD))],
        out_specs=[row_spec(C_W), row_spec(H_C * V_C)],
        out_shape=[jax.ShapeDtypeStruct((N, C_W), BF16), jax.ShapeDtypeStruct((N, H_C * V_C), BF16)],
        compiler_params=_cparams(("arbitrary",)),
        name="cache_kv",
    )(ckv, kr_placed, lw["w_kb"], lw["w_vb"], lw["gkc"])


def _attn_kernel(ncomp, nk, tk, l_init, *refs):
    if ncomp == 2:
        q_ref, k_ref, v_ref, lam_ref, g_ref, o_ref = refs
    else:
        q_ref, k_ref, v_ref, o_ref = refs
    H = q_ref.shape[1]
    tq = q_ref.shape[-1]
    E = v_ref.shape[3]

    def head(h, _):
        qs = [q_ref[0, h, c] for c in range(ncomp)]

        def chunk(j, carry):
            kj = k_ref[0, h, pl.ds(pl.multiple_of(j * tk, tk), tk), :]
            vj = v_ref[0, h, j]
            out = []
            for c in range(ncomp):
                m, l, acc = carry[c]
                s = jnp.dot(kj, qs[c], preferred_element_type=F32)
                m_new = jnp.maximum(m, jnp.max(s, axis=0, keepdims=True))
                alpha = jnp.exp2(m - m_new)
                p = jnp.exp2(s - m_new)
                l = alpha * l + jnp.sum(p, axis=0, keepdims=True)
                acc = alpha * acc + jnp.dot(vj, p.astype(BF16), preferred_element_type=F32)
                out.append((m_new, l, acc))
            return tuple(out)

        init = tuple((jnp.full((1, tq), -jnp.inf, F32), jnp.zeros((1, tq), F32),
                      jnp.zeros((E, tq), F32)) for _ in range(ncomp))
        res = lax.fori_loop(0, nk, chunk, init)
        os_ = [acc / l for (_, l, acc) in res]
        if ncomp == 2:
            lv = lam_ref[...]
            lam = (jnp.exp(jnp.sum(lv[0:1] * lv[1:2], axis=-1, keepdims=True))
                   - jnp.exp(jnp.sum(lv[2:3] * lv[3:4], axis=-1, keepdims=True)) + l_init)
            o = os_[0] - lam * os_[1]
            o = _rms(o, axis=0) * g_ref[...] * (1.0 - l_init)
        else:
            o = os_[0]
        o_ref[0, h] = o.astype(o_ref.dtype)
        return 0

    lax.fori_loop(0, H, head, 0)


def _attention(qT, k, vT, *, tq, tk, lam_vecs=None, subln=None, l_init=0.0):
    B, H, ncomp, d, T = qT.shape
    S = k.shape[2]
    nk = S // tk
    E = vT.shape[3]
    in_specs = [
        pl.BlockSpec((1, H, ncomp, d, tq), lambda b, i: (b, 0, 0, 0, i)),
        pl.BlockSpec((1, H, S, d), lambda b, i: (b, 0, 0, 0)),
        pl.BlockSpec((1, H, nk, E, tk), lambda b, i: (b, 0, 0, 0, 0)),
    ]
    args = [qT, k, vT]
    if ncomp == 2:
        in_specs += [pl.BlockSpec((4, DA), lambda b, i: (0, 0)),
                     pl.BlockSpec((E, 1), lambda b, i: (0, 0))]
        args += [lam_vecs, subln]
    return pl.pallas_call(
        functools.partial(_attn_kernel, ncomp, nk, tk, l_init),
        grid=(B, T // tq),
        in_specs=in_specs,
        out_specs=pl.BlockSpec((1, H, E, tq), lambda b, i: (b, 0, 0, i)),
        out_shape=jax.ShapeDtypeStruct((B, H, E, T), BF16),
        compiler_params=_cparams(("arbitrary", "arbitrary")),
        name="attn_diff" if ncomp == 2 else "attn_mla",
    )(*args)


def _post_head(x_ref, mix_ref, mod_ref, nffn_ref, wout_ref):
    y = jnp.dot(mix_ref[...], wout_ref[...], preferred_element_type=F32)
    x1 = x_ref[...] + mod_ref[0, 2:3, :] * y
    h2 = _rms(x1) * nffn_ref[...] * (1.0 + mod_ref[0, 4:5, :]) + mod_ref[0, 3:4, :]
    return x1, h2


def _swiglu(hb, wg, wu, wd):
    g = jnp.dot(hb, wg, preferred_element_type=F32)
    u = jnp.dot(hb, wu, preferred_element_type=F32)
    a = (g * jax.nn.sigmoid(g) * u).astype(BF16)
    return jnp.dot(a, wd, preferred_element_type=F32)


def _post_dense_kernel(fchunks, x_ref, mix_ref, mod_ref, nffn_ref, wout_ref, wg_ref, wu_ref, wd_ref, o_ref):
    x1, h2 = _post_head(x_ref, mix_ref, mod_ref, nffn_ref, wout_ref)
    hb = h2.astype(BF16)
    acc = None
    for (f0, f1) in fchunks:
        part = _swiglu(hb, wg_ref[:, f0:f1], wu_ref[:, f0:f1], wd_ref[f0:f1, :])
        acc = part if acc is None else acc + part
    o_ref[...] = x1 + mod_ref[0, 5:6, :] * acc


def _post_dense(x, mix, mod, lw, *, tm, tiles_per_mod):
    N, D = x.shape
    MW = mix.shape[1]
    FF = lw["wg"].shape[1]
    cut = (FF // 2 // MXU_DIM + 1) * MXU_DIM if FF > 2 * MXU_DIM else FF
    fchunks = ((0, cut), (cut, FF)) if cut < FF else ((0, FF),)
    row = lambda w: pl.BlockSpec((tm, w), lambda i: (i, 0))
    return pl.pallas_call(
        functools.partial(_post_dense_kernel, fchunks),
        grid=(N // tm,),
        in_specs=[row(D), row(MW), pl.BlockSpec((1, 6, D), lambda i: (i // tiles_per_mod, 0, 0)),
                  _const_spec((1, D)), _const_spec((MW, D)), _const_spec((D, FF)), _const_spec((D, FF)),
                  _const_spec((FF, D))],
        out_specs=row(D),
        out_shape=jax.ShapeDtypeStruct((N, D), F32),
        compiler_params=_cparams(("arbitrary",)),
        name="post_dense",
    )(x, mix, mod, lw["norm_ffn"], lw["w_out"], lw["wg"], lw["wu"], lw["wd"])


def _post_moe_kernel(x_ref, mix_ref, mod_ref, nffn_ref, wout_ref, rw_ref, wg_ref, wu_ref, wd_ref, o_ref,
                     x1_s, hb_s, comb_s, acc_s):
    e = pl.program_id(1)

    @pl.when(e == 0)
    def _():
        x1, h2 = _post_head(x_ref, mix_ref, mod_ref, nffn_ref, wout_ref)
        x1_s[...] = x1
        hb_s[...] = h2.astype(BF16)
        rw = rw_ref[...]
        rhi = rw.astype(BF16)
        rlo = (rw - rhi.astype(F32)).astype(BF16)
        hhi = h2.astype(BF16)
        hlo = (h2 - hhi.astype(F32)).astype(BF16)
        logits = (jnp.dot(hhi, rhi, preferred_element_type=F32)
                  + jnp.dot(hlo, rhi, preferred_element_type=F32)
                  + jnp.dot(hhi, rlo, preferred_element_type=F32))
        lane = lax.broadcasted_iota(jnp.int32, logits.shape, 1)
        neg = -jnp.inf
        lg = jnp.where(lane < N_EXPERTS, logits, neg)
        v1 = jnp.max(lg, axis=-1, keepdims=True)
        i1 = jnp.min(jnp.where(lg == v1, lane, LANES), axis=-1, keepdims=True)
        lg2 = jnp.where(lane == i1, neg, lg)
        v2 = jnp.max(lg2, axis=-1, keepdims=True)
        i2 = jnp.min(jnp.where(lg2 == v2, lane, LANES), axis=-1, keepdims=True)
        e2 = jnp.exp(v2 - v1)
        g1 = 1.0 / (1.0 + e2)
        g2 = e2 / (1.0 + e2)
        comb_s[...] = jnp.where(lane == i1, g1, 0.0) + jnp.where(lane == i2, g2, 0.0)
        acc_s[...] = jnp.zeros_like(acc_s)

    comb = comb_s[...]
    lane = lax.broadcasted_iota(jnp.int32, comb.shape, 1)
    w_e = jnp.sum(jnp.where(lane == e, comb, 0.0), axis=-1, keepdims=True)
    acc_s[...] += w_e * _swiglu(hb_s[...], wg_ref[0], wu_ref[0], wd_ref[0])

    @pl.when(e == pl.num_programs(1) - 1)
    def _():
        o_ref[...] = x1_s[...] + mod_ref[0, 5:6, :] * acc_s[...]


def _post_moe(x, mix, mod, lw, *, tm, tiles_per_mod):
    N, D = x.shape
    MW = mix.shape[1]
    E, _, FF = lw["wg"].shape
    row = lambda w: pl.BlockSpec((tm, w), lambda i, e: (i, 0))
    return pl.pallas_call(
        _post_moe_kernel,
        grid=(N // tm, E),
        in_specs=[row(D), row(MW), pl.BlockSpec((1, 6, D), lambda i, e: (i // tiles_per_mod, 0, 0)),
                  _const_spec((1, D)), _const_spec((MW, D)), _const_spec((D, LANES)),
                  pl.BlockSpec((1, D, FF), lambda i, e: (e, 0, 0)),
                  pl.BlockSpec((1, D, FF), lambda i, e: (e, 0, 0)),
                  pl.BlockSpec((1, FF, D), lambda i, e: (e, 0, 0))],
        out_specs=row(D),
        out_shape=jax.ShapeDtypeStruct((N, D), F32),
        scratch_shapes=[pltpu.VMEM((tm, D), F32), pltpu.VMEM((tm, D), BF16),
                        pltpu.VMEM((tm, LANES), F32), pltpu.VMEM((tm, D), F32)],
        compiler_params=_cparams(("arbitrary", "arbitrary")),
        name="post_moe",
    )(x, mix, mod, lw["norm_ffn"], lw["w_out"], lw["router"], lw["wg"], lw["wu"], lw["wd"])


def _rope_tables(T):
    t = jnp.arange(T, dtype=jnp.int32)
    rows = (t // GRID_W).astype(F32)
    cols = (t % GRID_W).astype(F32)
    n = ROPE_C // 4
    inv = jnp.power(ROPE_THETA, -jnp.arange(n, dtype=F32) / n)
    j = jnp.arange(ROPE_C)
    pos = jnp.where(j[None, :] < ROPE_C // 2, rows[:, None], cols[:, None])
    ang = pos * inv[j % n][None, :]
    cos = jnp.cos(ang)
    sin = jnp.sin(ang)
    first = ((j % (2 * n)) < n)[None, :]
    s_up = jnp.where(first, -sin, 0.0)
    s_dn = jnp.where(first, 0.0, sin)
    tab32 = jnp.stack([cos, s_up, s_dn])
    tab_a = jnp.tile(tab32, (1, 1, LANES // ROPE_C))
    ident = jnp.stack([jnp.ones((T, NOPE_C), F32), jnp.zeros((T, NOPE_C), F32), jnp.zeros((T, NOPE_C), F32)])
    tail = jnp.stack([jnp.ones((T, HEAD_PAD - QK_C), F32), jnp.zeros((T, HEAD_PAD - QK_C), F32),
                      jnp.zeros((T, HEAD_PAD - QK_C), F32)])
    tab_c = jnp.concatenate([ident, tab32, tail], axis=-1)
    return tab_a, tab_c


def _pad_heads(w, width):
    lead = w.shape[:-1]
    w = w.reshape(lead + (H_C, width))
    w = jnp.pad(w, [(0, 0)] * len(lead) + [(0, 0), (0, HEAD_PAD - width)])
    return w.reshape(lead + (C_W,))


def _layer_weights(l, p):
    w_in = p["w_in"][l]
    D = w_in.shape[0]
    o_cq = 3 * A_W + 3 * W_B
    o_ckv = o_cq + Q_RANK
    o_ckr = o_ckv + KV_RANK
    w_ext = jnp.concatenate(
        [w_in[:, :o_cq], w_in[:, o_ckv:o_ckr], w_in[:, o_cq:o_ckv], w_in[:, o_ckr:],
         jnp.zeros((D, IN_EXT - OFF_CKR - ROPE_C), F32)], axis=1).astype(BF16)
    g = jnp.arange(MXU_DIM) // DA
    lw = {
        "w_in": w_ext,
        "b32": (g[:, None] == g[None, :]).astype(BF16),
        "norm_mix": p["norm_mix"][l][None, :],
        "norm_ffn": p["norm_ffn"][l][None, :],
        "gqk": jnp.concatenate([jnp.tile(p["qnorm_a"][l], 2 * H_A), jnp.tile(p["knorm_a"][l], 2 * H_A)])[None, :],
        "gqa": p["norm_qa"][l][None, :],
        "gkva": p["norm_kva"][l][None, :],
        "gqc": jnp.pad(p["qnorm_c"][l], (0, HEAD_PAD - QK_C))[None, :],
        "gkc": jnp.pad(p["knorm_c"][l], (0, HEAD_PAD - QK_C))[None, :],
        "w_qb": _pad_heads(p["w_qb"][l], QK_C).astype(BF16),
        "w_kb": _pad_heads(p["w_kb"][l], NOPE_C).astype(BF16),
        "w_vb": p["w_vb"][l].astype(BF16),
        "conv_w": p["conv_w"][l],
        "conv_b": p["conv_b"][l][None, :],
        "w_out": p["w_out"][l].astype(BF16),
        "lam": jnp.stack([p["lambda_q1"][l], p["lambda_k1"][l], p["lambda_q2"][l], p["lambda_k2"][l]]),
        "subln": p["subln_a"][l][:, None],
    }
    i = l // 2
    if l % 2 == 0:
        lw["wg"] = p["ffn_w_gate"][i].astype(BF16)
        lw["wu"] = p["ffn_w_up"][i].astype(BF16)
        lw["wd"] = p["ffn_w_down"][i].astype(BF16)
    else:
        lw["router"] = jnp.pad(p["router_w"][i], ((0, 0), (0, LANES - N_EXPERTS)))
        lw["wg"] = p["moe_w_gate"][i].astype(BF16)
        lw["wu"] = p["moe_w_up"][i].astype(BF16)
        lw["wd"] = p["moe_w_down"][i].astype(BF16)
    return lw


def _diff_q_layout(qa, B, T):
    q = qa.reshape(B, T, H_A, 2, DA).transpose(0, 2, 3, 4, 1)
    z = jnp.zeros_like(q[:, :, 0])
    return jnp.stack([jnp.concatenate([q[:, :, 0], z], axis=2),
                      jnp.concatenate([z, q[:, :, 1]], axis=2)], axis=2)


def _keys_layout(k, B, S, H, d):
    return k.reshape(B, S, H, d).transpose(0, 2, 1, 3)


def _vals_layout(v, B, S, H, E, tk):
    v = v.reshape(B, S // tk, tk, H, E)
    return v.transpose(0, 3, 1, 4, 2)


def _mixer(x, mod, lw, l_init, *, B, T, ctx, rope):
    is_ctx = ctx is None
    if is_ctx:
        tm, seg, tiles_per_mod = T, T, x.shape[0] // T
    else:
        tm, seg, tiles_per_mod = min(512, T), GRID_W, T // min(512, T)
    outs = _pre_mix(x, mod, lw, rope, is_ctx=is_ctx, tm=tm, seg=seg, tiles_per_mod=tiles_per_mod)
    qa, ka, va, ob, qc, kc, vc = outs[:7]
    ka = ka.reshape(B, T, A_W)
    va = va.reshape(B, T, A_W)
    kc = kc.reshape(B, T, C_W)
    vc = vc.reshape(B, T, H_C * V_C)
    if not is_ctx:
        ctx_k, ctx_v, ctx_ckv, ctx_kr = ctx
        P = ctx_k.shape[1]
        kr_placed = jnp.pad(ctx_kr.reshape(B * P, ROPE_C), ((0, 0), (NOPE_C, HEAD_PAD - QK_C)))
        kc_ctx, vc_ctx = _cache_kv(ctx_ckv.reshape(B * P, KV_RANK), kr_placed, lw, tm=min(512, B * P))
        ka = jnp.concatenate([ctx_k.reshape(B, P, A_W).astype(BF16), ka], axis=1)
        va = jnp.concatenate([ctx_v.reshape(B, P, A_W).astype(BF16), va], axis=1)
        kc = jnp.concatenate([kc_ctx.reshape(B, P, C_W), kc], axis=1)
        vc = jnp.concatenate([vc_ctx.reshape(B, P, H_C * V_C), vc], axis=1)
    S = ka.shape[1]
    tq = min(256, T)
    tk = 512 if S % 512 == 0 else 256
    o_a = _attention(_diff_q_layout(qa, B, T), _keys_layout(ka, B, S, H_A, 2 * DA),
                     _vals_layout(va, B, S, H_A, V_A, tk), tq=tq, tk=tk,
                     lam_vecs=lw["lam"], subln=lw["subln"], l_init=l_init)
    qcT = qc.reshape(B, T, H_C, 1, HEAD_PAD).transpose(0, 2, 3, 4, 1)
    o_c = _attention(qcT, _keys_layout(kc, B, S, H_C, HEAD_PAD), _vals_layout(vc, B, S, H_C, V_C, tk),
                     tq=tq, tk=tk)
    o_a = o_a.transpose(0, 3, 1, 2).reshape(B * T, H_A * V_A)
    o_c = o_c.transpose(0, 3, 1, 2).reshape(B * T, H_C * V_C)
    mix = jnp.concatenate([o_a, ob, o_c], axis=-1)
    return mix, outs[7:]


def kernel(x_prompt, x_sample, cache_diff_k, cache_diff_v, cache_mla_ckv, cache_mla_krope, c, c_ctx, w_ada, b_ada, norm_mix, norm_ffn, w_in, qnorm_a, knorm_a, lambda_q1, lambda_k1, lambda_q2, lambda_k2, subln_a, conv_w, conv_b, norm_qa, w_qb, norm_kva, w_kb, w_vb, qnorm_c, knorm_c, w_out, ffn_w_gate, ffn_w_up, ffn_w_down, router_w, moe_w_gate, moe_w_up, moe_w_down):
    p = dict(norm_mix=norm_mix, norm_ffn=norm_ffn, w_in=w_in, qnorm_a=qnorm_a, knorm_a=knorm_a,
             lambda_q1=lambda_q1, lambda_k1=lambda_k1, lambda_q2=lambda_q2, lambda_k2=lambda_k2,
             subln_a=subln_a, conv_w=conv_w, conv_b=conv_b, norm_qa=norm_qa, w_qb=w_qb,
             norm_kva=norm_kva, w_kb=w_kb, w_vb=w_vb, qnorm_c=qnorm_c, knorm_c=knorm_c, w_out=w_out,
             ffn_w_gate=ffn_w_gate, ffn_w_up=ffn_w_up, ffn_w_down=ffn_w_down, router_w=router_w,
             moe_w_gate=moe_w_gate, moe_w_up=moe_w_up, moe_w_down=moe_w_down)
    Bc, Sc, D = x_prompt.shape
    Bl, T, _ = x_sample.shape
    L = w_in.shape[0]
    nrow = 16
    cvec = jnp.concatenate([c, c_ctx[None, :], jnp.zeros((nrow - Bl - 1, D), F32)], axis=0)
    mod = _ada(cvec, w_ada, b_ada)
    rope = _rope_tables(T)
    xp = x_prompt.reshape(Bc * Sc, D)
    xs = x_sample.reshape(Bl * T, D)
    st = [[], [], [], []]
    for l in range(L):
        lw = _layer_weights(l, p)
        l_init = 0.8 - 0.6 * math.exp(-0.3 * l)
        mod_lat = mod[l, :Bl].reshape(Bl, 6, D)
        mod_ctx = mod[l, Bl:Bl + 1].reshape(1, 6, D)
        post = _post_dense if l % 2 == 0 else _post_moe
        mix, state = _mixer(xp, mod_ctx, lw, l_init, B=Bc, T=Sc, ctx=None, rope=None)
        for acc, s in zip(st, state):
            acc.append(s)
        xp = post(xp, mix, mod_ctx, lw, tm=min(512, Bc * Sc), tiles_per_mod=Bc * Sc)
        ctx = (cache_diff_k[:, l], cache_diff_v[:, l], cache_mla_ckv[:, l], cache_mla_krope[:, l])
        mix, _ = _mixer(xs, mod_lat, lw, l_init, B=Bl, T=T, ctx=ctx, rope=rope)
        tm = min(512, T)
        xs = post(xs, mix, mod_lat, lw, tm=tm, tiles_per_mod=T // tm)
    new_k = jnp.stack(st[0], axis=1).reshape(Bc, Sc, L, H_A, 2, DA).transpose(0, 2, 1, 3, 4, 5)
    new_v = jnp.stack(st[1], axis=1).reshape(Bc, Sc, L, H_A, V_A).transpose(0, 2, 1, 3, 4)
    new_ckv = jnp.stack(st[2], axis=1).reshape(Bc, Sc, L, KV_RANK).transpose(0, 2, 1, 3)
    new_kr = jnp.stack(st[3], axis=1).reshape(Bc, Sc, L, ROPE_C).transpose(0, 2, 1, 3)
    return (xp.reshape(Bc, Sc, D), xs.reshape(Bl, T, D), new_k, new_v, new_ckv, new_kr)
```

```python
import functools
import math

import jax
import jax.numpy as jnp
from jax import lax
from jax.experimental import pallas as pl
from jax.experimental.pallas import tpu as pltpu

F32 = jnp.float32
BF16 = jnp.bfloat16

GRID_W = 64
H_A = 6
DA = 32
V_A = 2 * DA
W_B = 256
H_C = 6
NOPE_C = 64
ROPE_C = 32
QK_C = NOPE_C + ROPE_C
V_C = 64
Q_RANK = 192
KV_RANK = 128
N_EXPERTS = 8
ROPE_THETA = 10000.0
EPS = 1e-6
LOG2E = 1.4426950408889634

LANES = 128
MXU_DIM = 256
HEAD_PAD = 128

A_W = H_A * 2 * DA
OFF_AQ = 0
OFF_AK = A_W
OFF_AV = 2 * A_W
OFF_BB = 3 * A_W
OFF_BC = OFF_BB + W_B
OFF_BX = OFF_BC + W_B
OFF_CKV = OFF_BX + W_B
OFF_CQ = OFF_CKV + KV_RANK
OFF_CKR = OFF_CQ + Q_RANK
KR_CHUNK = (OFF_CKR // LANES) * LANES
assert OFF_CKR - KR_CHUNK == NOPE_C
IN_EXT = KR_CHUNK + LANES
C_W = H_C * HEAD_PAD

VMEM_LIMIT = 56 * 1024 * 1024


def _cparams(sem):
    return pltpu.CompilerParams(dimension_semantics=sem, vmem_limit_bytes=VMEM_LIMIT)


def _const_spec(shape):
    nd = len(shape)
    return pl.BlockSpec(shape, lambda *_: (0,) * nd, pipeline_mode=pl.Buffered(1))


def _rms(x, axis=-1):
    return x * lax.rsqrt(jnp.mean(x * x, axis=axis, keepdims=True) + EPS)


def _split_dot(x, w):
    hi = x.astype(BF16)
    lo = (x - hi.astype(F32)).astype(BF16)
    return (jnp.dot(hi, w, preferred_element_type=F32)
            + jnp.dot(lo, w, preferred_element_type=F32))


def _ada_kernel(c_ref, w_ref, b_ref, o_ref):
    c = c_ref[...]
    s = (c * jax.nn.sigmoid(c)).astype(BF16)
    o_ref[0] = jnp.dot(s, w_ref[0].astype(BF16), preferred_element_type=F32) + b_ref[0]


def _ada(cvec, w_ada, b_ada):
    L, D, N6 = w_ada.shape
    R = cvec.shape[0]
    tn = 1536
    assert N6 % tn == 0
    return pl.pallas_call(
        _ada_kernel,
        grid=(L, N6 // tn),
        in_specs=[
            pl.BlockSpec((R, D), lambda l, j: (0, 0)),
            pl.BlockSpec((1, D, tn), lambda l, j: (l, 0, j)),
            pl.BlockSpec((1, 1, tn), lambda l, j: (l, 0, j)),
        ],
        out_specs=pl.BlockSpec((1, R, tn), lambda l, j: (l, 0, j)),
        out_shape=jax.ShapeDtypeStruct((L, R, N6), F32),
        compiler_params=_cparams(("arbitrary", "arbitrary")),
        name="ada_mod",
    )(cvec, w_ada, b_ada.reshape(L, 1, N6))


def _rope(x, tab_ref):
    up = pltpu.roll(x, LANES - 8, 1)
    dn = pltpu.roll(x, 8, 1)
    return x * tab_ref[0] + up * tab_ref[1] + dn * tab_ref[2]


def _mla_kv(ckv_n, kr_placed, wkb_ref, wvb_ref, gkc_ref, rope_ref):
    cb = ckv_n.astype(BF16)
    kn = jnp.dot(cb, wkb_ref[...], preferred_element_type=F32)
    vc = jnp.dot(cb, wvb_ref[...], preferred_element_type=F32)
    ks = []
    for h in range(H_C):
        kp = kn[:, h * HEAD_PAD:(h + 1) * HEAD_PAD] + kr_placed
        ss = jnp.sum(kp * kp, axis=-1, keepdims=True) * (1.0 / QK_C)
        kp = kp * lax.rsqrt(ss + EPS) * gkc_ref[...]
        if rope_ref is not None:
            kp = _rope(kp, rope_ref)
        ks.append(kp)
    return ks, vc


def _pre_mix_kernel(is_ctx, seg, *refs):
    (x_ref, mod_ref, nmix_ref, win_ref, b32_ref, gqk_ref, gqa_ref, gkva_ref, gqc_ref,
     gkc_ref, wqb_ref, wkb_ref, wvb_ref, cw_ref, cb_ref) = refs[:15]
    refs = refs[15:]
    if is_ctx:
        ra_ref = rc_ref = None
    else:
        ra_ref, rc_ref = refs[:2]
        refs = refs[2:]
    qa_ref, ka_ref, va_ref, ob_ref, qc_ref, kc_ref, vc_ref = refs[:7]
    refs = refs[7:]

    x = x_ref[...]
    shift1 = mod_ref[0, 0:1, :]
    scale1 = mod_ref[0, 1:2, :]
    h = _rms(x) * nmix_ref[...] * (1.0 + scale1) + shift1
    proj = jnp.dot(h.astype(BF16), win_ref[...], preferred_element_type=F32)

    aqk = proj[:, OFF_AQ:OFF_AV]
    sq = aqk * aqk
    ss = jnp.concatenate(
        [_split_dot(sq[:, c:c + MXU_DIM], b32_ref[...]) for c in range(0, 2 * A_W, MXU_DIM)], axis=-1)
    aqk = aqk * lax.rsqrt(ss * (1.0 / DA) + EPS) * gqk_ref[...]
    if is_ctx:
        ka_st_ref, va_st_ref, ckv_st_ref, ckr_st_ref = refs
        ka_st_ref[...] = aqk[:, A_W:]
    else:
        aqk = jnp.concatenate(
            [_rope(aqk[:, c:c + LANES], ra_ref) for c in range(0, 2 * A_W, LANES)], axis=-1)
    qa_ref[...] = (aqk[:, :A_W] * (DA ** -0.5 * LOG2E)).astype(BF16)
    ka_ref[...] = aqk[:, A_W:].astype(BF16)
    av = proj[:, OFF_AV:OFF_BB]
    va_ref[...] = av.astype(BF16)

    bb = proj[:, OFF_BB:OFF_BC]
    u = proj[:, OFF_BC:OFF_BX] * proj[:, OFF_BX:OFF_CKV]
    tm = u.shape[0]
    row = lax.broadcasted_iota(jnp.int32, u.shape, 0) & (seg - 1)
    prev = jnp.where(row == 0, 0.0, pltpu.roll(u, 1, 0))
    nxt = jnp.where(row == seg - 1, 0.0, pltpu.roll(u, tm - 1, 0))
    conv = prev * cw_ref[0:1, :] + u * cw_ref[1:2, :] + nxt * cw_ref[2:3, :] + cb_ref[...]
    ob_ref[...] = (bb * conv).astype(BF16)

    cq = proj[:, OFF_CQ:OFF_CKR]
    cqn = (_rms(cq) * gqa_ref[...]).astype(BF16)
    qc = jnp.dot(cqn, wqb_ref[...], preferred_element_type=F32)
    for hh in range(H_C):
        qh = qc[:, hh * HEAD_PAD:(hh + 1) * HEAD_PAD]
        s2 = jnp.sum(qh * qh, axis=-1, keepdims=True) * (1.0 / QK_C)
        qh = qh * lax.rsqrt(s2 + EPS) * gqc_ref[...]
        if not is_ctx:
            qh = _rope(qh, rc_ref)
        qc_ref[:, hh * HEAD_PAD:(hh + 1) * HEAD_PAD] = (qh * (QK_C ** -0.5 * LOG2E)).astype(BF16)
    ckv_n = _rms(proj[:, OFF_CKV:OFF_CQ]) * gkva_ref[...]
    krc = proj[:, KR_CHUNK:KR_CHUNK + LANES]
    lane = lax.broadcasted_iota(jnp.int32, krc.shape, 1)
    kr_placed = jnp.where((lane >= NOPE_C) & (lane < QK_C), krc, 0.0)
    ks, vc = _mla_kv(ckv_n, kr_placed, wkb_ref, wvb_ref, gkc_ref, rc_ref)
    for hh in range(H_C):
        kc_ref[:, hh * HEAD_PAD:(hh + 1) * HEAD_PAD] = ks[hh].astype(BF16)
    vc_ref[...] = vc.astype(BF16)
    if is_ctx:
        va_st_ref[...] = av
        ckv_st_ref[...] = ckv_n
        ckr_st_ref[...] = krc[:, NOPE_C:QK_C]


def _pre_mix(x, mod, lw, rope, *, is_ctx, tm, seg, tiles_per_mod):
    N, D = x.shape
    nt = N // tm
    row_spec = lambda w: pl.BlockSpec((tm, w), lambda i: (i, 0))
    in_specs = [
        row_spec(D),
        pl.BlockSpec((1, 6, D), lambda i: (i // tiles_per_mod, 0, 0)),
        _const_spec((1, D)),
        _const_spec((D, IN_EXT)),
        _const_spec((MXU_DIM, MXU_DIM)),
        _const_spec((1, 2 * A_W)),
        _const_spec((1, Q_RANK)),
        _const_spec((1, KV_RANK)),
        _const_spec((1, HEAD_PAD)),
        _const_spec((1, HEAD_PAD)),
        _const_spec((Q_RANK, C_W)),
        _const_spec((KV_RANK, C_W)),
        _const_spec((KV_RANK, H_C * V_C)),
        _const_spec((3, W_B)),
        _const_spec((1, W_B)),
    ]
    args = [x, mod, lw["norm_mix"], lw["w_in"], lw["b32"], lw["gqk"], lw["gqa"], lw["gkva"],
            lw["gqc"], lw["gkc"], lw["w_qb"], lw["w_kb"], lw["w_vb"], lw["conv_w"], lw["conv_b"]]
    if not is_ctx:
        tiles_per_seq = rope[0].shape[1] // tm
        rspec = pl.BlockSpec((3, tm, LANES), lambda i: (0, i % tiles_per_seq, 0))
        in_specs += [rspec, rspec]
        args += list(rope)
    widths = [A_W, A_W, A_W, W_B, C_W, C_W, H_C * V_C]
    out_specs = [row_spec(w) for w in widths]
    out_shape = [jax.ShapeDtypeStruct((N, w), BF16) for w in widths]
    if is_ctx:
        st_w = [A_W, A_W, KV_RANK, ROPE_C]
        out_specs += [row_spec(w) for w in st_w]
        out_shape += [jax.ShapeDtypeStruct((N, w), F32) for w in st_w]
    return pl.pallas_call(
        functools.partial(_pre_mix_kernel, is_ctx, seg),
        grid=(nt,),
        in_specs=in_specs,
        out_specs=out_specs,
        out_shape=out_shape,
        compiler_params=_cparams(("arbitrary",)),
        name="pre_mix_ctx" if is_ctx else "pre_mix_lat",
    )(*args)


def _cache_kv_kernel(ckv_ref, kr_ref, wkb_ref, wvb_ref, gkc_ref, kc_ref, vc_ref):
    ks, vc = _mla_kv(ckv_ref[...], kr_ref[...], wkb_ref, wvb_ref, gkc_ref, None)
    for hh in range(H_C):
        kc_ref[:, hh * HEAD_PAD:(hh + 1) * HEAD_PAD] = ks[hh].astype(BF16)
    vc_ref[...] = vc.astype(BF16)


def _cache_kv(ckv, kr_placed, lw, tm):
    N = ckv.shape[0]
    row_spec = lambda w: pl.BlockSpec((tm, w), lambda i: (i, 0))
    return pl.pallas_call(
        _cache_kv_kernel,
        grid=(N // tm,),
        in_specs=[row_spec(KV_RANK), row_spec(HEAD_PAD), _const_spec((KV_RANK, C_W)),
                  _const_spec((KV_RANK, H_C * V_C)), _const_spec((1, HEAD_PAD))],
        out_specs=[row_spec(C_W), row_spec(H_C * V_C)],
        out_shape=[jax.ShapeDtypeStruct((N, C_W), BF16), jax.ShapeDtypeStruct((N, H_C * V_C), BF16)],
        compiler_params=_cparams(("arbitrary",)),
        name="cache_kv",
    )(ckv, kr_placed, lw["w_kb"], lw["w_vb"], lw["gkc"])


ROW_GROUP = 32
V_ROWS = V_A + 16


def _colmax(s):
    tk, tq = s.shape
    r = jnp.max(s.reshape(tk // ROW_GROUP, ROW_GROUP, tq), axis=0)
    return jnp.max(r, axis=0, keepdims=True)


def _attn_kernel(diff, G, nk, tk, l_init, *refs):
    if diff:
        q_ref, k_ref, v_ref, lam_ref, g_ref, o_ref, sa, sb = refs
    else:
        q_ref, k_ref, v_ref, o_ref, sa, sb = refs
    tq = q_ref.shape[-1]
    bufs = (sa, sb)
    gpt = 1 if nk % 2 == 0 else (2 if G % 2 == 0 else G)
    ntrip = G // gpt

    def kv_index(g, u):
        return g if diff else 2 * g + u

    def qk(g, j, buf):
        for u in range(2):
            if u == 0 or not diff:
                kj = k_ref[0, kv_index(g, u), pl.ds(j * tk, tk), :]
            buf[u] = jnp.dot(kj, q_ref[0, g, u], preferred_element_type=F32)

    def softmax(buf, carry):
        ps, out = [], []
        for u in range(2):
            m, acc = carry[u]
            s = buf[u]
            m_new = jnp.maximum(m, _colmax(s))
            alpha = jnp.exp2(m - m_new)
            ps.append(jnp.exp2(s - m_new).astype(BF16))
            out.append((m_new, alpha * acc))
        return ps, out

    def pv(g, j, ps, st):
        res = []
        for u in range(2):
            if u == 0 or not diff:
                vj = v_ref[0, kv_index(g, u), j]
            res.append((st[u][0], st[u][1] + jnp.dot(vj, ps[u], preferred_element_type=F32)))
        return res

    def finish(g, res):
        os_ = [acc[:V_A] / acc[V_A:V_A + 1] for (_, acc) in res]
        if diff:
            lv = lam_ref[...]
            lam = (jnp.exp(jnp.sum(lv[0:1] * lv[1:2], axis=-1, keepdims=True))
                   - jnp.exp(jnp.sum(lv[2:3] * lv[3:4], axis=-1, keepdims=True)) + l_init)
            o = os_[0] - lam * os_[1]
            o = _rms(o, axis=0) * g_ref[...] * (1.0 - l_init)
            o_ref[0, g] = o.astype(o_ref.dtype)
        else:
            for u in range(2):
                o_ref[0, 2 * g + u] = os_[u].astype(o_ref.dtype)

    def trip(t, _):
        step = 0
        for gi in range(gpt):
            g = t * gpt + gi
            carry = [(jnp.full((1, tq), -jnp.inf, F32), jnp.zeros((V_ROWS, tq), F32)) for _ in range(2)]
            for j in range(nk):
                src, dst = bufs[step % 2], bufs[(step + 1) % 2]
                if j + 1 < nk:
                    qk(g, j + 1, dst)
                elif gi + 1 < gpt:
                    qk(g + 1, 0, dst)
                elif ntrip > 1:
                    qk(jnp.minimum(g + 1, G - 1), 0, dst)
                ps, st = softmax(src, carry)
                carry = pv(g, j, ps, st)
                step += 1
            finish(g, carry)
        return 0

    qk(0, 0, sa)
    if ntrip > 1:
        lax.fori_loop(0, ntrip, trip, 0)
    else:
        trip(0, 0)


def _attention(qT, k, vT, *, diff, tq, tk, lam_vecs=None, subln=None, l_init=0.0):
    B, G, _, d, T = qT.shape
    Hk, S = k.shape[1:3]
    nk = S // tk
    Ho = G if diff else 2 * G
    in_specs = [
        pl.BlockSpec((1, G, 2, d, tq), lambda b, i: (b, 0, 0, 0, i)),
        pl.BlockSpec((1, Hk, S, d), lambda b, i: (b, 0, 0, 0)),
        pl.BlockSpec((1, Hk, nk, V_ROWS, tk), lambda b, i: (b, 0, 0, 0, 0)),
    ]
    args = [qT, k, vT]
    if diff:
        in_specs += [pl.BlockSpec((4, DA), lambda b, i: (0, 0)),
                     pl.BlockSpec((V_A, 1), lambda b, i: (0, 0))]
        args += [lam_vecs, subln]
    return pl.pallas_call(
        functools.partial(_attn_kernel, diff, G, nk, tk, l_init),
        grid=(B, T // tq),
        in_specs=in_specs,
        out_specs=pl.BlockSpec((1, Ho, V_A, tq), lambda b, i: (b, 0, 0, i)),
        out_shape=jax.ShapeDtypeStruct((B, Ho, V_A, T), BF16),
        scratch_shapes=[pltpu.VMEM((2, tk, tq), F32), pltpu.VMEM((2, tk, tq), F32)],
        compiler_params=_cparams(("arbitrary", "arbitrary")),
        name="attn_diff" if diff else "attn_mla",
    )(*args)


def _post_head(x_ref, mix_ref, mod_ref, nffn_ref, wout_ref):
    y = jnp.dot(mix_ref[...], wout_ref[...], preferred_element_type=F32)
    x1 = x_ref[...] + mod_ref[0, 2:3, :] * y
    h2 = _rms(x1) * nffn_ref[...] * (1.0 + mod_ref[0, 4:5, :]) + mod_ref[0, 3:4, :]
    return x1, h2


def _swiglu(hb, wg, wu, wd):
    g = jnp.dot(hb, wg, preferred_element_type=F32)
    u = jnp.dot(hb, wu, preferred_element_type=F32)
    a = (g * jax.nn.sigmoid(g) * u).astype(BF16)
    return jnp.dot(a, wd, preferred_element_type=F32)


def _post_dense_kernel(fchunks, x_ref, mix_ref, mod_ref, nffn_ref, wout_ref, wg_ref, wu_ref, wd_ref, o_ref):
    x1, h2 = _post_head(x_ref, mix_ref, mod_ref, nffn_ref, wout_ref)
    hb = h2.astype(BF16)
    acc = None
    for (f0, f1) in fchunks:
        part = _swiglu(hb, wg_ref[:, f0:f1], wu_ref[:, f0:f1], wd_ref[f0:f1, :])
        acc = part if acc is None else acc + part
    o_ref[...] = x1 + mod_ref[0, 5:6, :] * acc


def _post_dense(x, mix, mod, lw, *, tm, tiles_per_mod):
    N, D = x.shape
    MW = mix.shape[1]
    FF = lw["wg"].shape[1]
    cut = (FF // 2 // MXU_DIM + 1) * MXU_DIM if FF > 2 * MXU_DIM else FF
    fchunks = ((0, cut), (cut, FF)) if cut < FF else ((0, FF),)
    row = lambda w: pl.BlockSpec((tm, w), lambda i: (i, 0))
    return pl.pallas_call(
        functools.partial(_post_dense_kernel, fchunks),
        grid=(N // tm,),
        in_specs=[row(D), row(MW), pl.BlockSpec((1, 6, D), lambda i: (i // tiles_per_mod, 0, 0)),
                  _const_spec((1, D)), _const_spec((MW, D)), _const_spec((D, FF)), _const_spec((D, FF)),
                  _const_spec((FF, D))],
        out_specs=row(D),
        out_shape=jax.ShapeDtypeStruct((N, D), F32),
        compiler_params=_cparams(("arbitrary",)),
        name="post_dense",
    )(x, mix, mod, lw["norm_ffn"], lw["w_out"], lw["wg"], lw["wu"], lw["wd"])


def _post_moe_kernel(x_ref, mix_ref, mod_ref, nffn_ref, wout_ref, rw_ref, wg_ref, wu_ref, wd_ref, o_ref,
                     x1_s, hb_s, comb_s, acc_s):
    e = pl.program_id(1)

    @pl.when(e == 0)
    def _():
        x1, h2 = _post_head(x_ref, mix_ref, mod_ref, nffn_ref, wout_ref)
        x1_s[...] = x1
        hb_s[...] = h2.astype(BF16)
        rw = rw_ref[...]
        rhi = rw.astype(BF16)
        rlo = (rw - rhi.astype(F32)).astype(BF16)
        hhi = h2.astype(BF16)
        hlo = (h2 - hhi.astype(F32)).astype(BF16)
        logits = (jnp.dot(hhi, rhi, preferred_element_type=F32)
                  + jnp.dot(hlo, rhi, preferred_element_type=F32)
                  + jnp.dot(hhi, rlo, preferred_element_type=F32))
        lane = lax.broadcasted_iota(jnp.int32, logits.shape, 1)
        neg = -jnp.inf
        lg = jnp.where(lane < N_EXPERTS, logits, neg)
        v1 = jnp.max(lg, axis=-1, keepdims=True)
        i1 = jnp.min(jnp.where(lg == v1, lane, LANES), axis=-1, keepdims=True)
        lg2 = jnp.where(lane == i1, neg, lg)
        v2 = jnp.max(lg2, axis=-1, keepdims=True)
        i2 = jnp.min(jnp.where(lg2 == v2, lane, LANES), axis=-1, keepdims=True)
        e2 = jnp.exp(v2 - v1)
        g1 = 1.0 / (1.0 + e2)
        g2 = e2 / (1.0 + e2)
        comb_s[...] = jnp.where(lane == i1, g1, 0.0) + jnp.where(lane == i2, g2, 0.0)
        acc_s[...] = jnp.zeros_like(acc_s)

    comb = comb_s[...]
    lane = lax.broadcasted_iota(jnp.int32, comb.shape, 1)
    w_e = jnp.sum(jnp.where(lane == e, comb, 0.0), axis=-1, keepdims=True)
    acc_s[...] += w_e * _swiglu(hb_s[...], wg_ref[0], wu_ref[0], wd_ref[0])

    @pl.when(e == pl.num_programs(1) - 1)
    def _():
        o_ref[...] = x1_s[...] + mod_ref[0, 5:6, :] * acc_s[...]


def _post_moe(x, mix, mod, lw, *, tm, tiles_per_mod):
    N, D = x.shape
    MW = mix.shape[1]
    E, _, FF = lw["wg"].shape
    row = lambda w: pl.BlockSpec((tm, w), lambda i, e: (i, 0))
    return pl.pallas_call(
        _post_moe_kernel,
        grid=(N // tm, E),
        in_specs=[row(D), row(MW), pl.BlockSpec((1, 6, D), lambda i, e: (i // tiles_per_mod, 0, 0)),
                  _const_spec((1, D)), _const_spec((MW, D)), _const_spec((D, LANES)),
                  pl.BlockSpec((1, D, FF), lambda i, e: (e, 0, 0)),
                  pl.BlockSpec((1, D, FF), lambda i, e: (e, 0, 0)),
                  pl.BlockSpec((1, FF, D), lambda i, e: (e, 0, 0))],
        out_specs=row(D),
        out_shape=jax.ShapeDtypeStruct((N, D), F32),
        scratch_shapes=[pltpu.VMEM((tm, D), F32), pltpu.VMEM((tm, D), BF16),
                        pltpu.VMEM((tm, LANES), F32), pltpu.VMEM((tm, D), F32)],
        compiler_params=_cparams(("arbitrary", "arbitrary")),
        name="post_moe",
    )(x, mix, mod, lw["norm_ffn"], lw["w_out"], lw["router"], lw["wg"], lw["wu"], lw["wd"])


def _rope_tables(T):
    t = jnp.arange(T, dtype=jnp.int32)
    rows = (t // GRID_W).astype(F32)
    cols = (t % GRID_W).astype(F32)
    n = ROPE_C // 4
    inv = jnp.power(ROPE_THETA, -jnp.arange(n, dtype=F32) / n)
    j = jnp.arange(ROPE_C)
    pos = jnp.where(j[None, :] < ROPE_C // 2, rows[:, None], cols[:, None])
    ang = pos * inv[j % n][None, :]
    cos = jnp.cos(ang)
    sin = jnp.sin(ang)
    first = ((j % (2 * n)) < n)[None, :]
    s_up = jnp.where(first, -sin, 0.0)
    s_dn = jnp.where(first, 0.0, sin)
    tab32 = jnp.stack([cos, s_up, s_dn])
    tab_a = jnp.tile(tab32, (1, 1, LANES // ROPE_C))
    ident = jnp.stack([jnp.ones((T, NOPE_C), F32), jnp.zeros((T, NOPE_C), F32), jnp.zeros((T, NOPE_C), F32)])
    tail = jnp.stack([jnp.ones((T, HEAD_PAD - QK_C), F32), jnp.zeros((T, HEAD_PAD - QK_C), F32),
                      jnp.zeros((T, HEAD_PAD - QK_C), F32)])
    tab_c = jnp.concatenate([ident, tab32, tail], axis=-1)
    return tab_a, tab_c


def _pad_heads(w, width):
    lead = w.shape[:-1]
    w = w.reshape(lead + (H_C, width))
    w = jnp.pad(w, [(0, 0)] * len(lead) + [(0, 0), (0, HEAD_PAD - width)])
    return w.reshape(lead + (C_W,))


def _layer_weights(l, p):
    w_in = p["w_in"][l]
    D = w_in.shape[0]
    o_cq = 3 * A_W + 3 * W_B
    o_ckv = o_cq + Q_RANK
    o_ckr = o_ckv + KV_RANK
    w_ext = jnp.concatenate(
        [w_in[:, :o_cq], w_in[:, o_ckv:o_ckr], w_in[:, o_cq:o_ckv], w_in[:, o_ckr:],
         jnp.zeros((D, IN_EXT - OFF_CKR - ROPE_C), F32)], axis=1).astype(BF16)
    g = jnp.arange(MXU_DIM) // DA
    lw = {
        "w_in": w_ext,
        "b32": (g[:, None] == g[None, :]).astype(BF16),
        "norm_mix": p["norm_mix"][l][None, :],
        "norm_ffn": p["norm_ffn"][l][None, :],
        "gqk": jnp.concatenate([jnp.tile(p["qnorm_a"][l], 2 * H_A), jnp.tile(p["knorm_a"][l], 2 * H_A)])[None, :],
        "gqa": p["norm_qa"][l][None, :],
        "gkva": p["norm_kva"][l][None, :],
        "gqc": jnp.pad(p["qnorm_c"][l], (0, HEAD_PAD - QK_C))[None, :],
        "gkc": jnp.pad(p["knorm_c"][l], (0, HEAD_PAD - QK_C))[None, :],
        "w_qb": _pad_heads(p["w_qb"][l], QK_C).astype(BF16),
        "w_kb": _pad_heads(p["w_kb"][l], NOPE_C).astype(BF16),
        "w_vb": p["w_vb"][l].astype(BF16),
        "conv_w": p["conv_w"][l],
        "conv_b": p["conv_b"][l][None, :],
        "w_out": p["w_out"][l].astype(BF16),
        "lam": jnp.stack([p["lambda_q1"][l], p["lambda_k1"][l], p["lambda_q2"][l], p["lambda_k2"][l]]),
        "subln": p["subln_a"][l][:, None],
    }
    i = l // 2
    if l % 2 == 0:
        lw["wg"] = p["ffn_w_gate"][i].astype(BF16)
        lw["wu"] = p["ffn_w_up"][i].astype(BF16)
        lw["wd"] = p["ffn_w_down"][i].astype(BF16)
    else:
        lw["router"] = jnp.pad(p["router_w"][i], ((0, 0), (0, LANES - N_EXPERTS)))
        lw["wg"] = p["moe_w_gate"][i].astype(BF16)
        lw["wu"] = p["moe_w_up"][i].astype(BF16)
        lw["wd"] = p["moe_w_down"][i].astype(BF16)
    return lw


def _diff_q_layout(qa, B, T):
    q = qa.reshape(B, T, H_A, 2, DA).transpose(0, 2, 3, 4, 1)
    z = jnp.zeros_like(q[:, :, 0])
    return jnp.stack([jnp.concatenate([q[:, :, 0], z], axis=2),
                      jnp.concatenate([z, q[:, :, 1]], axis=2)], axis=2)


def _keys_layout(k, B, S, H, d):
    return k.reshape(B, S, H, d).transpose(0, 2, 1, 3)


def _vals_layout(v, B, S, H, E, tk):
    v = v.reshape(B, S // tk, tk, H, E).transpose(0, 3, 1, 4, 2)
    ones = jnp.ones(v.shape[:3] + (1, tk), v.dtype)
    zeros = jnp.zeros(v.shape[:3] + (V_ROWS - E - 1, tk), v.dtype)
    return jnp.concatenate([v, ones, zeros], axis=3)


def _mixer(x, mod, lw, l_init, *, B, T, ctx, rope):
    is_ctx = ctx is None
    if is_ctx:
        tm, seg, tiles_per_mod = T, T, x.shape[0] // T
    else:
        tm, seg, tiles_per_mod = min(512, T), GRID_W, T // min(512, T)
    outs = _pre_mix(x, mod, lw, rope, is_ctx=is_ctx, tm=tm, seg=seg, tiles_per_mod=tiles_per_mod)
    qa, ka, va, ob, qc, kc, vc = outs[:7]
    ka = ka.reshape(B, T, A_W)
    va = va.reshape(B, T, A_W)
    kc = kc.reshape(B, T, C_W)
    vc = vc.reshape(B, T, H_C * V_C)
    if not is_ctx:
        ctx_k, ctx_v, ctx_ckv, ctx_kr = ctx
        P = ctx_k.shape[1]
        kr_placed = jnp.pad(ctx_kr.reshape(B * P, ROPE_C), ((0, 0), (NOPE_C, HEAD_PAD - QK_C)))
        kc_ctx, vc_ctx = _cache_kv(ctx_ckv.reshape(B * P, KV_RANK), kr_placed, lw, tm=min(512, B * P))
        ka = jnp.concatenate([ctx_k.reshape(B, P, A_W).astype(BF16), ka], axis=1)
        va = jnp.concatenate([ctx_v.reshape(B, P, A_W).astype(BF16), va], axis=1)
        kc = jnp.concatenate([kc_ctx.reshape(B, P, C_W), kc], axis=1)
        vc = jnp.concatenate([vc_ctx.reshape(B, P, H_C * V_C), vc], axis=1)
    S = ka.shape[1]
    tq = min(256, T)
    tk = 512 if S % 512 == 0 else 256
    o_a = _attention(_diff_q_layout(qa, B, T), _keys_layout(ka, B, S, H_A, 2 * DA),
                     _vals_layout(va, B, S, H_A, V_A, tk), diff=True, tq=tq, tk=tk,
                     lam_vecs=lw["lam"], subln=lw["subln"], l_init=l_init)
    qcT = qc.reshape(B, T, H_C // 2, 2, HEAD_PAD).transpose(0, 2, 3, 4, 1)
    o_c = _attention(qcT, _keys_layout(kc, B, S, H_C, HEAD_PAD), _vals_layout(vc, B, S, H_C, V_C, tk),
                     diff=False, tq=tq, tk=tk)
    o_a = o_a.transpose(0, 3, 1, 2).reshape(B * T, H_A * V_A)
    o_c = o_c.transpose(0, 3, 1, 2).reshape(B * T, H_C * V_C)
    mix = jnp.concatenate([o_a, ob, o_c], axis=-1)
    return mix, outs[7:]


def kernel(x_prompt, x_sample, cache_diff_k, cache_diff_v, cache_mla_ckv, cache_mla_krope, c, c_ctx, w_ada, b_ada, norm_mix, norm_ffn, w_in, qnorm_a, knorm_a, lambda_q1, lambda_k1, lambda_q2, lambda_k2, subln_a, conv_w, conv_b, norm_qa, w_qb, norm_kva, w_kb, w_vb, qnorm_c, knorm_c, w_out, ffn_w_gate, ffn_w_up, ffn_w_down, router_w, moe_w_gate, moe_w_up, moe_w_down):
    p = dict(norm_mix=norm_mix, norm_ffn=norm_ffn, w_in=w_in, qnorm_a=qnorm_a, knorm_a=knorm_a,
             lambda_q1=lambda_q1, lambda_k1=lambda_k1, lambda_q2=lambda_q2, lambda_k2=lambda_k2,
             subln_a=subln_a, conv_w=conv_w, conv_b=conv_b, norm_qa=norm_qa, w_qb=w_qb,
             norm_kva=norm_kva, w_kb=w_kb, w_vb=w_vb, qnorm_c=qnorm_c, knorm_c=knorm_c, w_out=w_out,
             ffn_w_gate=ffn_w_gate, ffn_w_up=ffn_w_up, ffn_w_down=ffn_w_down, router_w=router_w,
             moe_w_gate=moe_w_gate, moe_w_up=moe_w_up, moe_w_down=moe_w_down)
    Bc, Sc, D = x_prompt.shape
    Bl, T, _ = x_sample.shape
    L = w_in.shape[0]
    nrow = 16
    cvec = jnp.concatenate([c, c_ctx[None, :], jnp.zeros((nrow - Bl - 1, D), F32)], axis=0)
    mod = _ada(cvec, w_ada, b_ada)
    rope = _rope_tables(T)
    xp = x_prompt.reshape(Bc * Sc, D)
    xs = x_sample.reshape(Bl * T, D)
    st = [[], [], [], []]
    for l in range(L):
        lw = _layer_weights(l, p)
        l_init = 0.8 - 0.6 * math.exp(-0.3 * l)
        mod_lat = mod[l, :Bl].reshape(Bl, 6, D)
        mod_ctx = mod[l, Bl:Bl + 1].reshape(1, 6, D)
        post = _post_dense if l % 2 == 0 else _post_moe
        mix, state = _mixer(xp, mod_ctx, lw, l_init, B=Bc, T=Sc, ctx=None, rope=None)
        for acc, s in zip(st, state):
            acc.append(s)
        xp = post(xp, mix, mod_ctx, lw, tm=min(512, Bc * Sc), tiles_per_mod=Bc * Sc)
        ctx = (cache_diff_k[:, l], cache_diff_v[:, l], cache_mla_ckv[:, l], cache_mla_krope[:, l])
        mix, _ = _mixer(xs, mod_lat, lw, l_init, B=Bl, T=T, ctx=ctx, rope=rope)
        tm = min(512, T)
        xs = post(xs, mix, mod_lat, lw, tm=tm, tiles_per_mod=T // tm)
    new_k = jnp.stack(st[0], axis=1).reshape(Bc, Sc, L, H_A, 2, DA).transpose(0, 2, 1, 3, 4, 5)
    new_v = jnp.stack(st[1], axis=1).reshape(Bc, Sc, L, H_A, V_A).transpose(0, 2, 1, 3, 4)
    new_ckv = jnp.stack(st[2], axis=1).reshape(Bc, Sc, L, KV_RANK).transpose(0, 2, 1, 3)
    new_kr = jnp.stack(st[3], axis=1).reshape(Bc, Sc, L, ROPE_C).transpose(0, 2, 1, 3)
    return (xp.reshape(Bc, Sc, D), xs.reshape(Bl, T, D), new_k, new_v, new_ckv, new_kr)
```

```python
import functools
import math

import jax
import jax.numpy as jnp
from jax import lax
from jax.experimental import pallas as pl
from jax.experimental.pallas import tpu as pltpu

F32 = jnp.float32
BF16 = jnp.bfloat16

GRID_W = 64
H_A = 6
DA = 32
V_A = 2 * DA
W_B = 256
H_C = 6
NOPE_C = 64
ROPE_C = 32
QK_C = NOPE_C + ROPE_C
V_C = 64
Q_RANK = 192
KV_RANK = 128
N_EXPERTS = 8
ROPE_THETA = 10000.0
EPS = 1e-6
LOG2E = 1.4426950408889634

LANES = 128
MXU_DIM = 256
HEAD_PAD = 128

A_W = H_A * 2 * DA
OFF_AQ = 0
OFF_AK = A_W
OFF_AV = 2 * A_W
OFF_BB = 3 * A_W
OFF_BC = OFF_BB + W_B
OFF_BX = OFF_BC + W_B
OFF_CKV = OFF_BX + W_B
OFF_CQ = OFF_CKV + KV_RANK
OFF_CKR = OFF_CQ + Q_RANK
KR_CHUNK = (OFF_CKR // LANES) * LANES
assert OFF_CKR - KR_CHUNK == NOPE_C
IN_EXT = KR_CHUNK + LANES
C_W = H_C * HEAD_PAD
MIX_A = H_A * V_A
MIX_C = H_C * V_C

ROW_GROUP = 32
V_ROWS = V_A + 16
assert V_A == V_C

VMEM_LIMIT = 56 * 1024 * 1024


def _cparams(sem):
    return pltpu.CompilerParams(dimension_semantics=sem, vmem_limit_bytes=VMEM_LIMIT)


def _const_spec(shape):
    nd = len(shape)
    return pl.BlockSpec(shape, lambda *_: (0,) * nd, pipeline_mode=pl.Buffered(1))


def _rms(x, axis=-1):
    return x * lax.rsqrt(jnp.mean(x * x, axis=axis, keepdims=True) + EPS)


def _split_dot(x, w):
    hi = x.astype(BF16)
    lo = (x - hi.astype(F32)).astype(BF16)
    return (jnp.dot(hi, w, preferred_element_type=F32)
            + jnp.dot(lo, w, preferred_element_type=F32))


def _ada_kernel(c_ref, w_ref, b_ref, o_ref):
    c = c_ref[...]
    s = (c * jax.nn.sigmoid(c)).astype(BF16)
    o_ref[0] = jnp.dot(s, w_ref[0].astype(BF16), preferred_element_type=F32) + b_ref[0]


def _ada(cvec, w_ada, b_ada):
    L, D, N6 = w_ada.shape
    R = cvec.shape[0]
    tn = 1536
    assert N6 % tn == 0
    return pl.pallas_call(
        _ada_kernel,
        grid=(L, N6 // tn),
        in_specs=[
            pl.BlockSpec((R, D), lambda l, j: (0, 0)),
            pl.BlockSpec((1, D, tn), lambda l, j: (l, 0, j)),
            pl.BlockSpec((1, 1, tn), lambda l, j: (l, 0, j)),
        ],
        out_specs=pl.BlockSpec((1, R, tn), lambda l, j: (l, 0, j)),
        out_shape=jax.ShapeDtypeStruct((L, R, N6), F32),
        compiler_params=_cparams(("arbitrary", "arbitrary")),
        name="ada_mod",
    )(cvec, w_ada, b_ada.reshape(L, 1, N6))


def _rope(x, tab_ref):
    up = pltpu.roll(x, LANES - 8, 1)
    dn = pltpu.roll(x, 8, 1)
    return x * tab_ref[0] + up * tab_ref[1] + dn * tab_ref[2]


def _store_vT(v, ref):
    tm = v.shape[0]
    vT = v.T.astype(BF16)
    row = lax.broadcasted_iota(jnp.int32, (V_ROWS - V_A, tm), 0)
    tail = jnp.where(row == 0, 1.0, 0.0).astype(BF16)
    for h in range(v.shape[1] // V_A):
        ref[0, h, 0:V_A, :] = vT[h * V_A:(h + 1) * V_A, :]
        ref[0, h, V_A:V_ROWS, :] = tail


def _mla_kv(ckv_n, kr_placed, wkb_ref, wvb_ref, gkc_ref, rope_ref, kc_ref, vcT_ref):
    cb = ckv_n.astype(BF16)
    kn = jnp.dot(cb, wkb_ref[...], preferred_element_type=F32)
    vc = jnp.dot(cb, wvb_ref[...], preferred_element_type=F32)
    for h in range(H_C):
        kp = kn[:, h * HEAD_PAD:(h + 1) * HEAD_PAD] + kr_placed
        ss = jnp.sum(kp * kp, axis=-1, keepdims=True) * (1.0 / QK_C)
        kp = kp * lax.rsqrt(ss + EPS) * gkc_ref[...]
        if rope_ref is not None:
            kp = _rope(kp, rope_ref)
        kc_ref[h] = kp.astype(BF16)
    _store_vT(vc, vcT_ref)


def _pre_mix_kernel(is_ctx, seg, *refs):
    (x_ref, mod_ref, nmix_ref, win_ref, b32_ref, gqk_ref, gqa_ref, gkva_ref, gqc_ref,
     gkc_ref, wqb_ref, wkb_ref, wvb_ref, cw_ref, cb_ref) = refs[:15]
    refs = refs[15:]
    if is_ctx:
        ra_ref = rc_ref = None
    else:
        ra_ref, rc_ref = refs[:2]
        refs = refs[2:]
    qaT_ref, ka_ref, vaT_ref, ob_ref, qcT_ref, kc_ref, vcT_ref = refs[:7]
    refs = refs[7:]

    x = x_ref[...]
    shift1 = mod_ref[0, 0:1, :]
    scale1 = mod_ref[0, 1:2, :]
    h = _rms(x) * nmix_ref[...] * (1.0 + scale1) + shift1
    proj = jnp.dot(h.astype(BF16), win_ref[...], preferred_element_type=F32)

    aqk = proj[:, OFF_AQ:OFF_AV]
    sq = aqk * aqk
    ss = jnp.concatenate(
        [_split_dot(sq[:, c:c + MXU_DIM], b32_ref[...]) for c in range(0, 2 * A_W, MXU_DIM)], axis=-1)
    aqk = aqk * lax.rsqrt(ss * (1.0 / DA) + EPS) * gqk_ref[...]
    if is_ctx:
        ka_st_ref, va_st_ref, ckv_st_ref, ckr_st_ref = refs
        ka_st_ref[...] = aqk[:, A_W:]
    else:
        aqk = jnp.concatenate(
            [_rope(aqk[:, c:c + LANES], ra_ref) for c in range(0, 2 * A_W, LANES)], axis=-1)
    qaT_ref[...] = (aqk[:, :A_W] * (DA ** -0.5 * LOG2E)).T.astype(BF16)
    for hh in range(H_A):
        ka_ref[hh] = aqk[:, A_W + hh * 2 * DA:A_W + (hh + 1) * 2 * DA].astype(BF16)
    av = proj[:, OFF_AV:OFF_BB]
    _store_vT(av, vaT_ref)

    bb = proj[:, OFF_BB:OFF_BC]
    u = proj[:, OFF_BC:OFF_BX] * proj[:, OFF_BX:OFF_CKV]
    tm = u.shape[0]
    row = lax.broadcasted_iota(jnp.int32, u.shape, 0) & (seg - 1)
    prev = jnp.where(row == 0, 0.0, pltpu.roll(u, 1, 0))
    nxt = jnp.where(row == seg - 1, 0.0, pltpu.roll(u, tm - 1, 0))
    conv = prev * cw_ref[0:1, :] + u * cw_ref[1:2, :] + nxt * cw_ref[2:3, :] + cb_ref[...]
    ob_ref[...] = (bb * conv).astype(BF16)

    cq = proj[:, OFF_CQ:OFF_CKR]
    cqn = (_rms(cq) * gqa_ref[...]).astype(BF16)
    qc = jnp.dot(cqn, wqb_ref[...], preferred_element_type=F32)
    qhs = []
    for hh in range(H_C):
        qh = qc[:, hh * HEAD_PAD:(hh + 1) * HEAD_PAD]
        s2 = jnp.sum(qh * qh, axis=-1, keepdims=True) * (1.0 / QK_C)
        qh = qh * lax.rsqrt(s2 + EPS) * gqc_ref[...]
        if not is_ctx:
            qh = _rope(qh, rc_ref)
        qhs.append(qh * (QK_C ** -0.5 * LOG2E))
    qcT_ref[...] = jnp.concatenate(qhs, axis=-1).T.astype(BF16)
    ckv_n = _rms(proj[:, OFF_CKV:OFF_CQ]) * gkva_ref[...]
    krc = proj[:, KR_CHUNK:KR_CHUNK + LANES]
    lane = lax.broadcasted_iota(jnp.int32, krc.shape, 1)
    kr_placed = jnp.where((lane >= NOPE_C) & (lane < QK_C), krc, 0.0)
    _mla_kv(ckv_n, kr_placed, wkb_ref, wvb_ref, gkc_ref, rc_ref, kc_ref, vcT_ref)
    if is_ctx:
        va_st_ref[...] = av
        ckv_st_ref[...] = ckv_n
        ckr_st_ref[...] = krc[:, NOPE_C:QK_C]


def _pre_mix(x, mod, lw, rope, *, B, T, is_ctx, tm, seg):
    N, D = x.shape
    nt = N // tm
    tps = T // tm
    row_spec = lambda w: pl.BlockSpec((tm, w), lambda i: (i, 0))
    in_specs = [
        row_spec(D),
        pl.BlockSpec((1, 6, D), (lambda i: (0, 0, 0)) if is_ctx else (lambda i: (i // tps, 0, 0))),
        _const_spec((1, D)),
        _const_spec((D, IN_EXT)),
        _const_spec((MXU_DIM, MXU_DIM)),
        _const_spec((1, 2 * A_W)),
        _const_spec((1, Q_RANK)),
        _const_spec((1, KV_RANK)),
        _const_spec((1, HEAD_PAD)),
        _const_spec((1, HEAD_PAD)),
        _const_spec((Q_RANK, C_W)),
        _const_spec((KV_RANK, C_W)),
        _const_spec((KV_RANK, MIX_C)),
        _const_spec((3, W_B)),
        _const_spec((1, W_B)),
    ]
    args = [x, mod, lw["norm_mix"], lw["w_in"], lw["b32"], lw["gqk"], lw["gqa"], lw["gkva"],
            lw["gqc"], lw["gkc"], lw["w_qb"], lw["w_kb"], lw["w_vb"], lw["conv_w"], lw["conv_b"]]
    if not is_ctx:
        rspec = pl.BlockSpec((3, tm, LANES), lambda i: (0, i % tps, 0))
        in_specs += [rspec, rspec]
        args += list(rope)
    colT = lambda w: pl.BlockSpec((w, tm), lambda i: (0, i))
    vT_spec = pl.BlockSpec((1, H_A, V_ROWS, tm), lambda i: (i // tps, 0, 0, i % tps))
    out_specs = [colT(A_W), pl.BlockSpec((H_A, tm, 2 * DA), lambda i: (0, i, 0)), vT_spec, row_spec(W_B),
                 colT(C_W), pl.BlockSpec((H_C, tm, HEAD_PAD), lambda i: (0, i, 0)), vT_spec]
    out_shape = [jax.ShapeDtypeStruct((A_W, N), BF16), jax.ShapeDtypeStruct((H_A, N, 2 * DA), BF16),
                 jax.ShapeDtypeStruct((B, H_A, V_ROWS, T), BF16), jax.ShapeDtypeStruct((N, W_B), BF16),
                 jax.ShapeDtypeStruct((C_W, N), BF16), jax.ShapeDtypeStruct((H_C, N, HEAD_PAD), BF16),
                 jax.ShapeDtypeStruct((B, H_C, V_ROWS, T), BF16)]
    if is_ctx:
        st_w = [A_W, A_W, KV_RANK, ROPE_C]
        out_specs += [row_spec(w) for w in st_w]
        out_shape += [jax.ShapeDtypeStruct((N, w), F32) for w in st_w]
    return pl.pallas_call(
        functools.partial(_pre_mix_kernel, is_ctx, seg),
        grid=(nt,),
        in_specs=in_specs,
        out_specs=out_specs,
        out_shape=out_shape,
        compiler_params=_cparams(("arbitrary",)),
        name="pre_mix_ctx" if is_ctx else "pre_mix_lat",
    )(*args)


def _cache_kv_kernel(ckv_ref, kr_ref, wkb_ref, wvb_ref, gkc_ref, kc_ref, vcT_ref):
    _mla_kv(ckv_ref[...], kr_ref[...], wkb_ref, wvb_ref, gkc_ref, None, kc_ref, vcT_ref)


def _cache_kv(ckv, kr_placed, lw, B, P):
    row_spec = lambda w: pl.BlockSpec((P, w), lambda i: (i, 0))
    return pl.pallas_call(
        _cache_kv_kernel,
        grid=(B,),
        in_specs=[row_spec(KV_RANK), row_spec(HEAD_PAD), _const_spec((KV_RANK, C_W)),
                  _const_spec((KV_RANK, MIX_C)), _const_spec((1, HEAD_PAD))],
        out_specs=[pl.BlockSpec((H_C, P, HEAD_PAD), lambda i: (0, i, 0)),
                   pl.BlockSpec((1, H_C, V_ROWS, P), lambda i: (i, 0, 0, 0))],
        out_shape=[jax.ShapeDtypeStruct((H_C, B * P, HEAD_PAD), BF16),
                   jax.ShapeDtypeStruct((B, H_C, V_ROWS, P), BF16)],
        compiler_params=_cparams(("arbitrary",)),
        name="cache_kv",
    )(ckv, kr_placed, lw["w_kb"], lw["w_vb"], lw["gkc"])


def _colmax(s):
    tk, tq = s.shape
    r = jnp.max(s.reshape(tk // ROW_GROUP, ROW_GROUP, tq), axis=0)
    return jnp.max(r, axis=0, keepdims=True)


def _attn_kernel(diff, has_ctx, G, chunks, tk, l_init, *refs):
    refs = list(refs)
    q_ref = refs.pop(0)
    if has_ctx:
        kc_ref, kl_ref, vc_ref, vl_ref = refs[:4]
        refs = refs[4:]
    else:
        kl_ref, vl_ref = refs[:2]
        kc_ref = vc_ref = None
        refs = refs[2:]
    if diff:
        lam_ref, g_ref = refs[:2]
        refs = refs[2:]
    o_ref, sa, sb, os_ref = refs
    tq = q_ref.shape[-1]
    bufs = (sa, sb)
    nk = len(chunks)
    H = G if diff else 2 * G
    dq = q_ref.shape[0] // H
    gpt = 1 if nk % 2 == 0 else (2 if G % 2 == 0 else G)
    ntrip = G // gpt

    def head_of(g, u):
        return g if diff else 2 * g + u

    def q_rows(h):
        start = h * dq
        if not isinstance(start, int):
            start = pl.multiple_of(start, dq)
        return q_ref[pl.ds(start, dq), :]

    def q_units(g):
        if diff:
            qh = q_rows(g)
            row = lax.broadcasted_iota(jnp.int32, qh.shape, 0)
            zero = jnp.zeros_like(qh)
            return [jnp.where(row < DA, qh, zero), jnp.where(row >= DA, qh, zero)]
        return [q_rows(2 * g + u) for u in range(2)]

    def kslice(h, j):
        src, off = chunks[j]
        return (kc_ref if src == 0 else kl_ref)[h, pl.ds(off, tk), :]

    def vslice(h, j):
        src, off = chunks[j]
        return (vc_ref if src == 0 else vl_ref)[0, h, :, pl.ds(off, tk)]

    def qk(g, qs, j, buf):
        for u in range(2):
            if u == 0 or not diff:
                kj = kslice(head_of(g, u), j)
            buf[u] = jnp.dot(kj, qs[u], preferred_element_type=F32)

    def softmax(buf, carry):
        ps, out = [], []
        for u in range(2):
            m, acc = carry[u]
            s = buf[u]
            m_new = jnp.maximum(m, _colmax(s))
            alpha = jnp.exp2(m - m_new)
            ps.append(jnp.exp2(s - m_new).astype(BF16))
            out.append((m_new, alpha * acc))
        return ps, out

    def pv(g, j, ps, st):
        res = []
        for u in range(2):
            if u == 0 or not diff:
                vj = vslice(head_of(g, u), j)
            res.append((st[u][0], st[u][1] + jnp.dot(vj, ps[u], preferred_element_type=F32)))
        return res

    def finish(g, res):
        os_ = [acc[:V_A] / acc[V_A:V_A + 1] for (_, acc) in res]
        if diff:
            lv = lam_ref[...]
            lam = (jnp.exp(jnp.sum(lv[0:1] * lv[1:2], axis=-1, keepdims=True))
                   - jnp.exp(jnp.sum(lv[2:3] * lv[3:4], axis=-1, keepdims=True)) + l_init)
            o = os_[0] - lam * os_[1]
            os_ref[g] = _rms(o, axis=0) * g_ref[...] * (1.0 - l_init)
        else:
            for u in range(2):
                os_ref[2 * g + u] = os_[u]

    def trip(t, _):
        step = 0
        g = t * gpt
        qs = q_units(g)
        for gi in range(gpt):
            g = t * gpt + gi
            carry = [(jnp.full((1, tq), -jnp.inf, F32), jnp.zeros((V_ROWS, tq), F32)) for _ in range(2)]
            for j in range(nk):
                src, dst = bufs[step % 2], bufs[(step + 1) % 2]
                qs_next = qs
                if j + 1 < nk:
                    qk(g, qs, j + 1, dst)
                elif gi + 1 < gpt:
                    qs_next = q_units(g + 1)
                    qk(g + 1, qs_next, 0, dst)
                elif ntrip > 1:
                    g_next = jnp.minimum(g + 1, G - 1)
                    qk(g_next, q_units(g_next), 0, dst)
                ps, st = softmax(src, carry)
                carry = pv(g, j, ps, st)
                qs = qs_next
                step += 1
            finish(g, carry)
        return 0

    qk(0, q_units(0), 0, sa)
    if ntrip > 1:
        lax.fori_loop(0, ntrip, trip, 0)
    else:
        trip(0, 0)
    for c in range(H // 2):
        pair = jnp.concatenate([os_ref[2 * c], os_ref[2 * c + 1]], axis=0)
        o_ref[:, c * LANES:(c + 1) * LANES] = pair.T.astype(o_ref.dtype)


def _attention(qT, k_lat, v_lat, k_ctx=None, v_ctx=None, *, diff, tq, tk, lam_vecs=None, subln=None,
               l_init=0.0):
    B, H, _, T = v_lat.shape
    N = qT.shape[1]
    d = k_lat.shape[2]
    has_ctx = k_ctx is not None
    chunks = [(1, o) for o in range(0, T, tk)]
    nq = T // tq
    in_specs = [pl.BlockSpec((qT.shape[0], tq), lambda b, i: (0, b * nq + i))]
    args = [qT]
    if has_ctx:
        P = v_ctx.shape[3]
        chunks = [(0, o) for o in range(0, P, tk)] + chunks
        in_specs += [pl.BlockSpec((H, P, d), lambda b, i: (0, b, 0)),
                     pl.BlockSpec((H, T, d), lambda b, i: (0, b, 0)),
                     pl.BlockSpec((1, H, V_ROWS, P), lambda b, i: (b, 0, 0, 0)),
                     pl.BlockSpec((1, H, V_ROWS, T), lambda b, i: (b, 0, 0, 0))]
        args += [k_ctx, k_lat, v_ctx, v_lat]
    else:
        in_specs += [pl.BlockSpec((H, T, d), lambda b, i: (0, b, 0)),
                     pl.BlockSpec((1, H, V_ROWS, T), lambda b, i: (b, 0, 0, 0))]
        args += [k_lat, v_lat]
    if diff:
        in_specs += [pl.BlockSpec((4, DA), lambda b, i: (0, 0)),
                     pl.BlockSpec((V_A, 1), lambda b, i: (0, 0))]
        args += [lam_vecs, subln]
    G = H if diff else H // 2
    return pl.pallas_call(
        functools.partial(_attn_kernel, diff, has_ctx, G, tuple(chunks), tk, l_init),
        grid=(B, nq),
        in_specs=in_specs,
        out_specs=pl.BlockSpec((tq, H * V_A), lambda b, i: (b * nq + i, 0)),
        out_shape=jax.ShapeDtypeStruct((N, H * V_A), BF16),
        scratch_shapes=[pltpu.VMEM((2, tk, tq), F32), pltpu.VMEM((2, tk, tq), F32),
                        pltpu.VMEM((H, V_A, tq), F32)],
        compiler_params=_cparams(("arbitrary", "arbitrary")),
        name="attn_diff" if diff else "attn_mla",
    )(*args)


def _post_head(x_ref, oa_ref, ob_ref, oc_ref, mod_ref, nffn_ref, wout_ref):
    y = (jnp.dot(oa_ref[...], wout_ref[0:MIX_A, :], preferred_element_type=F32)
         + jnp.dot(ob_ref[...], wout_ref[MIX_A:MIX_A + W_B, :], preferred_element_type=F32)
         + jnp.dot(oc_ref[...], wout_ref[MIX_A + W_B:, :], preferred_element_type=F32))
    x1 = x_ref[...] + mod_ref[0, 2:3, :] * y
    h2 = _rms(x1) * nffn_ref[...] * (1.0 + mod_ref[0, 4:5, :]) + mod_ref[0, 3:4, :]
    return x1, h2


def _swiglu(hb, wg, wu, wd):
    g = jnp.dot(hb, wg, preferred_element_type=F32)
    u = jnp.dot(hb, wu, preferred_element_type=F32)
    a = (g * jax.nn.sigmoid(g) * u).astype(BF16)
    return jnp.dot(a, wd, preferred_element_type=F32)


def _post_dense_kernel(fchunks, x_ref, oa_ref, ob_ref, oc_ref, mod_ref, nffn_ref, wout_ref,
                       wg_ref, wu_ref, wd_ref, o_ref):
    x1, h2 = _post_head(x_ref, oa_ref, ob_ref, oc_ref, mod_ref, nffn_ref, wout_ref)
    hb = h2.astype(BF16)
    acc = None
    for (f0, f1) in fchunks:
        part = _swiglu(hb, wg_ref[:, f0:f1], wu_ref[:, f0:f1], wd_ref[f0:f1, :])
        acc = part if acc is None else acc + part
    o_ref[...] = x1 + mod_ref[0, 5:6, :] * acc


def _post_dense(x, oa, ob, oc, mod, lw, *, tm, tiles_per_mod):
    N, D = x.shape
    FF = lw["wg"].shape[1]
    cut = (FF // 2 // MXU_DIM + 1) * MXU_DIM if FF > 2 * MXU_DIM else FF
    fchunks = ((0, cut), (cut, FF)) if cut < FF else ((0, FF),)
    row = lambda w: pl.BlockSpec((tm, w), lambda i: (i, 0))
    return pl.pallas_call(
        functools.partial(_post_dense_kernel, fchunks),
        grid=(N // tm,),
        in_specs=[row(D), row(MIX_A), row(W_B), row(MIX_C),
                  pl.BlockSpec((1, 6, D), lambda i: (i // tiles_per_mod, 0, 0)),
                  _const_spec((1, D)), _const_spec(lw["w_out"].shape), _const_spec((D, FF)),
                  _const_spec((D, FF)), _const_spec((FF, D))],
        out_specs=row(D),
        out_shape=jax.ShapeDtypeStruct((N, D), F32),
        compiler_params=_cparams(("arbitrary",)),
        name="post_dense",
    )(x, oa, ob, oc, mod, lw["norm_ffn"], lw["w_out"], lw["wg"], lw["wu"], lw["wd"])


def _post_moe_kernel(x_ref, oa_ref, ob_ref, oc_ref, mod_ref, nffn_ref, wout_ref, rw_ref,
                     wg_ref, wu_ref, wd_ref, o_ref, x1_s, hb_s, comb_s, acc_s):
    e = pl.program_id(1)

    @pl.when(e == 0)
    def _():
        x1, h2 = _post_head(x_ref, oa_ref, ob_ref, oc_ref, mod_ref, nffn_ref, wout_ref)
        x1_s[...] = x1
        hb_s[...] = h2.astype(BF16)
        rw = rw_ref[...]
        rhi = rw.astype(BF16)
        rlo = (rw - rhi.astype(F32)).astype(BF16)
        hhi = h2.astype(BF16)
        hlo = (h2 - hhi.astype(F32)).astype(BF16)
        logits = (jnp.dot(hhi, rhi, preferred_element_type=F32)
                  + jnp.dot(hlo, rhi, preferred_element_type=F32)
                  + jnp.dot(hhi, rlo, preferred_element_type=F32))
        lane = lax.broadcasted_iota(jnp.int32, logits.shape, 1)
        neg = -jnp.inf
        lg = jnp.where(lane < N_EXPERTS, logits, neg)
        v1 = jnp.max(lg, axis=-1, keepdims=True)
        i1 = jnp.min(jnp.where(lg == v1, lane, LANES), axis=-1, keepdims=True)
        lg2 = jnp.where(lane == i1, neg, lg)
        v2 = jnp.max(lg2, axis=-1, keepdims=True)
        i2 = jnp.min(jnp.where(lg2 == v2, lane, LANES), axis=-1, keepdims=True)
        e2 = jnp.exp(v2 - v1)
        g1 = 1.0 / (1.0 + e2)
        g2 = e2 / (1.0 + e2)
        comb_s[...] = jnp.where(lane == i1, g1, 0.0) + jnp.where(lane == i2, g2, 0.0)
        acc_s[...] = jnp.zeros_like(acc_s)

    comb = comb_s[...]
    lane = lax.broadcasted_iota(jnp.int32, comb.shape, 1)
    w_e = jnp.sum(jnp.where(lane == e, comb, 0.0), axis=-1, keepdims=True)
    acc_s[...] += w_e * _swiglu(hb_s[...], wg_ref[0], wu_ref[0], wd_ref[0])

    @pl.when(e == pl.num_programs(1) - 1)
    def _():
        o_ref[...] = x1_s[...] + mod_ref[0, 5:6, :] * acc_s[...]


def _post_moe(x, oa, ob, oc, mod, lw, *, tm, tiles_per_mod):
    N, D = x.shape
    E, _, FF = lw["wg"].shape
    row = lambda w: pl.BlockSpec((tm, w), lambda i, e: (i, 0))
    return pl.pallas_call(
        _post_moe_kernel,
        grid=(N // tm, E),
        in_specs=[row(D), row(MIX_A), row(W_B), row(MIX_C),
                  pl.BlockSpec((1, 6, D), lambda i, e: (i // tiles_per_mod, 0, 0)),
                  _const_spec((1, D)), _const_spec(lw["w_out"].shape), _const_spec((D, LANES)),
                  pl.BlockSpec((1, D, FF), lambda i, e: (e, 0, 0)),
                  pl.BlockSpec((1, D, FF), lambda i, e: (e, 0, 0)),
                  pl.BlockSpec((1, FF, D), lambda i, e: (e, 0, 0))],
        out_specs=row(D),
        out_shape=jax.ShapeDtypeStruct((N, D), F32),
        scratch_shapes=[pltpu.VMEM((tm, D), F32), pltpu.VMEM((tm, D), BF16),
                        pltpu.VMEM((tm, LANES), F32), pltpu.VMEM((tm, D), F32)],
        compiler_params=_cparams(("arbitrary", "arbitrary")),
        name="post_moe",
    )(x, oa, ob, oc, mod, lw["norm_ffn"], lw["w_out"], lw["router"], lw["wg"], lw["wu"], lw["wd"])


def _rope_tables(T):
    t = jnp.arange(T, dtype=jnp.int32)
    rows = (t // GRID_W).astype(F32)
    cols = (t % GRID_W).astype(F32)
    n = ROPE_C // 4
    inv = jnp.power(ROPE_THETA, -jnp.arange(n, dtype=F32) / n)
    j = jnp.arange(ROPE_C)
    pos = jnp.where(j[None, :] < ROPE_C // 2, rows[:, None], cols[:, None])
    ang = pos * inv[j % n][None, :]
    cos = jnp.cos(ang)
    sin = jnp.sin(ang)
    first = ((j % (2 * n)) < n)[None, :]
    s_up = jnp.where(first, -sin, 0.0)
    s_dn = jnp.where(first, 0.0, sin)
    tab32 = jnp.stack([cos, s_up, s_dn])
    tab_a = jnp.tile(tab32, (1, 1, LANES // ROPE_C))
    ident = jnp.stack([jnp.ones((T, NOPE_C), F32), jnp.zeros((T, NOPE_C), F32), jnp.zeros((T, NOPE_C), F32)])
    tail = jnp.stack([jnp.ones((T, HEAD_PAD - QK_C), F32), jnp.zeros((T, HEAD_PAD - QK_C), F32),
                      jnp.zeros((T, HEAD_PAD - QK_C), F32)])
    tab_c = jnp.concatenate([ident, tab32, tail], axis=-1)
    return tab_a, tab_c


def _pad_heads(w, width):
    lead = w.shape[:-1]
    w = w.reshape(lead + (H_C, width))
    w = jnp.pad(w, [(0, 0)] * len(lead) + [(0, 0), (0, HEAD_PAD - width)])
    return w.reshape(lead + (C_W,))


def _layer_weights(l, p):
    w_in = p["w_in"][l]
    D = w_in.shape[0]
    o_cq = 3 * A_W + 3 * W_B
    o_ckv = o_cq + Q_RANK
    o_ckr = o_ckv + KV_RANK
    w_ext = jnp.concatenate(
        [w_in[:, :o_cq], w_in[:, o_ckv:o_ckr], w_in[:, o_cq:o_ckv], w_in[:, o_ckr:],
         jnp.zeros((D, IN_EXT - OFF_CKR - ROPE_C), F32)], axis=1).astype(BF16)
    g = jnp.arange(MXU_DIM) // DA
    lw = {
        "w_in": w_ext,
        "b32": (g[:, None] == g[None, :]).astype(BF16),
        "norm_mix": p["norm_mix"][l][None, :],
        "norm_ffn": p["norm_ffn"][l][None, :],
        "gqk": jnp.concatenate([jnp.tile(p["qnorm_a"][l], 2 * H_A), jnp.tile(p["knorm_a"][l], 2 * H_A)])[None, :],
        "gqa": p["norm_qa"][l][None, :],
        "gkva": p["norm_kva"][l][None, :],
        "gqc": jnp.pad(p["qnorm_c"][l], (0, HEAD_PAD - QK_C))[None, :],
        "gkc": jnp.pad(p["knorm_c"][l], (0, HEAD_PAD - QK_C))[None, :],
        "w_qb": _pad_heads(p["w_qb"][l], QK_C).astype(BF16),
        "w_kb": _pad_heads(p["w_kb"][l], NOPE_C).astype(BF16),
        "w_vb": p["w_vb"][l].astype(BF16),
        "conv_w": p["conv_w"][l],
        "conv_b": p["conv_b"][l][None, :],
        "w_out": p["w_out"][l].astype(BF16),
        "lam": jnp.stack([p["lambda_q1"][l], p["lambda_k1"][l], p["lambda_q2"][l], p["lambda_k2"][l]]),
        "subln": p["subln_a"][l][:, None],
    }
    i = l // 2
    if l % 2 == 0:
        lw["wg"] = p["ffn_w_gate"][i].astype(BF16)
        lw["wu"] = p["ffn_w_up"][i].astype(BF16)
        lw["wd"] = p["ffn_w_down"][i].astype(BF16)
    else:
        lw["router"] = jnp.pad(p["router_w"][i], ((0, 0), (0, LANES - N_EXPERTS)))
        lw["wg"] = p["moe_w_gate"][i].astype(BF16)
        lw["wu"] = p["moe_w_up"][i].astype(BF16)
        lw["wd"] = p["moe_w_down"][i].astype(BF16)
    return lw


def _cached_vT(v, B, P):
    vT = v.transpose(0, 2, 3, 1).astype(BF16)
    ones = jnp.ones((B, H_A, 1, P), BF16)
    zeros = jnp.zeros((B, H_A, V_ROWS - V_A - 1, P), BF16)
    return jnp.concatenate([vT, ones, zeros], axis=2)


def _mixer(x, mod, lw, l_init, *, B, T, ctx, rope):
    is_ctx = ctx is None
    tm = T if is_ctx else min(512, T)
    seg = T if is_ctx else GRID_W
    outs = _pre_mix(x, mod, lw, rope, B=B, T=T, is_ctx=is_ctx, tm=tm, seg=seg)
    qaT, ka, vaT, ob, qcT, kc, vcT = outs[:7]
    tq = min(256, T)
    if is_ctx:
        tk = 512 if T % 512 == 0 else 256
        o_a = _attention(qaT, ka, vaT, diff=True, tq=tq, tk=tk, lam_vecs=lw["lam"], subln=lw["subln"],
                         l_init=l_init)
        o_c = _attention(qcT, kc, vcT, diff=False, tq=tq, tk=tk)
    else:
        ctx_k, ctx_v, ctx_ckv, ctx_kr = ctx
        P = ctx_k.shape[1]
        tk = 512 if (P % 512 == 0 and T % 512 == 0) else 256
        kr_placed = jnp.pad(ctx_kr.reshape(B * P, ROPE_C), ((0, 0), (NOPE_C, HEAD_PAD - QK_C)))
        kc_ctx, vc_ctx = _cache_kv(ctx_ckv.reshape(B * P, KV_RANK), kr_placed, lw, B, P)
        ka_ctx = ctx_k.reshape(B * P, H_A, 2 * DA).transpose(1, 0, 2).astype(BF16)
        o_a = _attention(qaT, ka, vaT, ka_ctx, _cached_vT(ctx_v, B, P), diff=True, tq=tq, tk=tk,
                         lam_vecs=lw["lam"], subln=lw["subln"], l_init=l_init)
        o_c = _attention(qcT, kc, vcT, kc_ctx, vc_ctx, diff=False, tq=tq, tk=tk)
    return (o_a, ob, o_c), outs[7:]


def kernel(x_prompt, x_sample, cache_diff_k, cache_diff_v, cache_mla_ckv, cache_mla_krope, c, c_ctx, w_ada, b_ada, norm_mix, norm_ffn, w_in, qnorm_a, knorm_a, lambda_q1, lambda_k1, lambda_q2, lambda_k2, subln_a, conv_w, conv_b, norm_qa, w_qb, norm_kva, w_kb, w_vb, qnorm_c, knorm_c, w_out, ffn_w_gate, ffn_w_up, ffn_w_down, router_w, moe_w_gate, moe_w_up, moe_w_down):
    p = dict(norm_mix=norm_mix, norm_ffn=norm_ffn, w_in=w_in, qnorm_a=qnorm_a, knorm_a=knorm_a,
             lambda_q1=lambda_q1, lambda_k1=lambda_k1, lambda_q2=lambda_q2, lambda_k2=lambda_k2,
             subln_a=subln_a, conv_w=conv_w, conv_b=conv_b, norm_qa=norm_qa, w_qb=w_qb,
             norm_kva=norm_kva, w_kb=w_kb, w_vb=w_vb, qnorm_c=qnorm_c, knorm_c=knorm_c, w_out=w_out,
             ffn_w_gate=ffn_w_gate, ffn_w_up=ffn_w_up, ffn_w_down=ffn_w_down, router_w=router_w,
             moe_w_gate=moe_w_gate, moe_w_up=moe_w_up, moe_w_down=moe_w_down)
    Bc, Sc, D = x_prompt.shape
    Bl, T, _ = x_sample.shape
    L = w_in.shape[0]
    nrow = 16
    cvec = jnp.concatenate([c, c_ctx[None, :], jnp.zeros((nrow - Bl - 1, D), F32)], axis=0)
    mod = _ada(cvec, w_ada, b_ada)
    rope = _rope_tables(T)
    xp = x_prompt.reshape(Bc * Sc, D)
    xs = x_sample.reshape(Bl * T, D)
    st = [[], [], [], []]
    for l in range(L):
        lw = _layer_weights(l, p)
        l_init = 0.8 - 0.6 * math.exp(-0.3 * l)
        mod_lat = mod[l, :Bl].reshape(Bl, 6, D)
        mod_ctx = mod[l, Bl:Bl + 1].reshape(1, 6, D)
        post = _post_dense if l % 2 == 0 else _post_moe
        heads, state = _mixer(xp, mod_ctx, lw, l_init, B=Bc, T=Sc, ctx=None, rope=None)
        for acc, s in zip(st, state):
            acc.append(s)
        xp = post(xp, *heads, mod_ctx, lw, tm=min(512, Bc * Sc), tiles_per_mod=Bc * Sc)
        ctx = (cache_diff_k[:, l], cache_diff_v[:, l], cache_mla_ckv[:, l], cache_mla_krope[:, l])
        heads, _ = _mixer(xs, mod_lat, lw, l_init, B=Bl, T=T, ctx=ctx, rope=rope)
        tm = min(512, T)
        xs = post(xs, *heads, mod_lat, lw, tm=tm, tiles_per_mod=T // tm)
    new_k = jnp.stack(st[0], axis=1).reshape(Bc, Sc, L, H_A, 2, DA).transpose(0, 2, 1, 3, 4, 5)
    new_v = jnp.stack(st[1], axis=1).reshape(Bc, Sc, L, H_A, V_A).transpose(0, 2, 1, 3, 4)
    new_ckv = jnp.stack(st[2], axis=1).reshape(Bc, Sc, L, KV_RANK).transpose(0, 2, 1, 3)
    new_kr = jnp.stack(st[3], axis=1).reshape(Bc, Sc, L, ROPE_C).transpose(0, 2, 1, 3)
    return (xp.reshape(Bc, Sc, D), xs.reshape(Bl, T, D), new_k, new_v, new_ckv, new_kr)
```

```python
import functools
import math

import jax
import jax.numpy as jnp
from jax import lax
from jax.experimental import pallas as pl
from jax.experimental.pallas import tpu as pltpu

F32 = jnp.float32
BF16 = jnp.bfloat16

GRID_W = 64
H_A = 6
DA = 32
V_A = 2 * DA
W_B = 256
H_C = 6
NOPE_C = 64
ROPE_C = 32
QK_C = NOPE_C + ROPE_C
V_C = 64
Q_RANK = 192
KV_RANK = 128
N_EXPERTS = 8
ROPE_THETA = 10000.0
EPS = 1e-6
LOG2E = 1.4426950408889634

LANES = 128
MXU_DIM = 256
HEAD_PAD = 128

A_W = H_A * 2 * DA
OFF_AQ = 0
OFF_AK = A_W
OFF_AV = 2 * A_W
OFF_BB = 3 * A_W
OFF_BC = OFF_BB + W_B
OFF_BX = OFF_BC + W_B
OFF_CKV = OFF_BX + W_B
OFF_CQ = OFF_CKV + KV_RANK
OFF_CKR = OFF_CQ + Q_RANK
KR_CHUNK = (OFF_CKR // LANES) * LANES
assert OFF_CKR - KR_CHUNK == NOPE_C
IN_EXT = KR_CHUNK + LANES
C_W = H_C * HEAD_PAD
MIX_A = H_A * V_A
MIX_C = H_C * V_C

ROW_GROUP = 32
V_ROWS = V_A + 16
assert V_A == V_C

VMEM_LIMIT = 56 * 1024 * 1024


def _cparams(sem):
    return pltpu.CompilerParams(dimension_semantics=sem, vmem_limit_bytes=VMEM_LIMIT)


def _const_spec(shape):
    nd = len(shape)
    return pl.BlockSpec(shape, lambda *_: (0,) * nd, pipeline_mode=pl.Buffered(1))


def _rms(x, axis=-1):
    return x * lax.rsqrt(jnp.mean(x * x, axis=axis, keepdims=True) + EPS)


def _split_dot(x, w):
    hi = x.astype(BF16)
    lo = (x - hi.astype(F32)).astype(BF16)
    return (jnp.dot(hi, w, preferred_element_type=F32)
            + jnp.dot(lo, w, preferred_element_type=F32))


def _ada_kernel(c_ref, w_ref, b_ref, o_ref):
    c = c_ref[...]
    s = (c * jax.nn.sigmoid(c)).astype(BF16)
    o_ref[0] = jnp.dot(s, w_ref[0].astype(BF16), preferred_element_type=F32) + b_ref[0]


def _ada(cvec, w_ada, b_ada):
    L, D, N6 = w_ada.shape
    R = cvec.shape[0]
    tn = 1536
    assert N6 % tn == 0
    return pl.pallas_call(
        _ada_kernel,
        grid=(L, N6 // tn),
        in_specs=[
            pl.BlockSpec((R, D), lambda l, j: (0, 0)),
            pl.BlockSpec((1, D, tn), lambda l, j: (l, 0, j)),
            pl.BlockSpec((1, 1, tn), lambda l, j: (l, 0, j)),
        ],
        out_specs=pl.BlockSpec((1, R, tn), lambda l, j: (l, 0, j)),
        out_shape=jax.ShapeDtypeStruct((L, R, N6), F32),
        compiler_params=_cparams(("arbitrary", "arbitrary")),
        name="ada_mod",
    )(cvec, w_ada, b_ada.reshape(L, 1, N6))


def _rope(x, tab_ref):
    up = pltpu.roll(x, LANES - 8, 1)
    dn = pltpu.roll(x, 8, 1)
    return x * tab_ref[0] + up * tab_ref[1] + dn * tab_ref[2]


def _store_vT(v, ref):
    tm = v.shape[0]
    vT = v.T.astype(BF16)
    row = lax.broadcasted_iota(jnp.int32, (V_ROWS - V_A, tm), 0)
    tail = jnp.where(row == 0, 1.0, 0.0).astype(BF16)
    for h in range(v.shape[1] // V_A):
        ref[0, h, 0:V_A, :] = vT[h * V_A:(h + 1) * V_A, :]
        ref[0, h, V_A:V_ROWS, :] = tail


def _mla_kv(ckv_n, kr_placed, wkb_ref, wvb_ref, gkc_ref, rope_ref, kc_ref, vcT_ref):
    cb = ckv_n.astype(BF16)
    kn = jnp.dot(cb, wkb_ref[...], preferred_element_type=F32)
    vc = jnp.dot(cb, wvb_ref[...], preferred_element_type=F32)
    for h in range(H_C):
        kp = kn[:, h * HEAD_PAD:(h + 1) * HEAD_PAD] + kr_placed
        ss = jnp.sum(kp * kp, axis=-1, keepdims=True) * (1.0 / QK_C)
        kp = kp * lax.rsqrt(ss + EPS) * gkc_ref[...]
        if rope_ref is not None:
            kp = _rope(kp, rope_ref)
        kc_ref[h] = kp.astype(BF16)
    _store_vT(vc, vcT_ref)


def _pre_mix_kernel(is_ctx, seg, *refs):
    (x_ref, mod_ref, nmix_ref, win_ref, b32_ref, gqk_ref, gqa_ref, gkva_ref, gqc_ref,
     gkc_ref, wqb_ref, wkb_ref, wvb_ref, cw_ref, cb_ref) = refs[:15]
    refs = refs[15:]
    if is_ctx:
        ra_ref = rc_ref = None
    else:
        ra_ref, rc_ref = refs[:2]
        refs = refs[2:]
    qaT_ref, ka_ref, vaT_ref, ob_ref, qcT_ref, kc_ref, vcT_ref = refs[:7]
    refs = refs[7:]

    x = x_ref[...]
    shift1 = mod_ref[0, 0:1, :]
    scale1 = mod_ref[0, 1:2, :]
    h = _rms(x) * nmix_ref[...] * (1.0 + scale1) + shift1
    proj = jnp.dot(h.astype(BF16), win_ref[...], preferred_element_type=F32)

    aqk = proj[:, OFF_AQ:OFF_AV]
    sq = aqk * aqk
    ss = jnp.concatenate(
        [_split_dot(sq[:, c:c + MXU_DIM], b32_ref[...]) for c in range(0, 2 * A_W, MXU_DIM)], axis=-1)
    aqk = aqk * lax.rsqrt(ss * (1.0 / DA) + EPS) * gqk_ref[...]
    if is_ctx:
        ka_st_ref, va_st_ref, ckv_st_ref, ckr_st_ref = refs
        ka_st_ref[...] = aqk[:, A_W:]
    else:
        aqk = jnp.concatenate(
            [_rope(aqk[:, c:c + LANES], ra_ref) for c in range(0, 2 * A_W, LANES)], axis=-1)
    qaT_ref[...] = (aqk[:, :A_W] * (DA ** -0.5 * LOG2E)).T.astype(BF16)
    for hh in range(H_A):
        ka_ref[hh] = aqk[:, A_W + hh * 2 * DA:A_W + (hh + 1) * 2 * DA].astype(BF16)
    av = proj[:, OFF_AV:OFF_BB]
    _store_vT(av, vaT_ref)

    bb = proj[:, OFF_BB:OFF_BC]
    u = proj[:, OFF_BC:OFF_BX] * proj[:, OFF_BX:OFF_CKV]
    tm = u.shape[0]
    row = lax.broadcasted_iota(jnp.int32, u.shape, 0) & (seg - 1)
    prev = jnp.where(row == 0, 0.0, pltpu.roll(u, 1, 0))
    nxt = jnp.where(row == seg - 1, 0.0, pltpu.roll(u, tm - 1, 0))
    conv = prev * cw_ref[0:1, :] + u * cw_ref[1:2, :] + nxt * cw_ref[2:3, :] + cb_ref[...]
    ob_ref[...] = (bb * conv).astype(BF16)

    cq = proj[:, OFF_CQ:OFF_CKR]
    cqn = (_rms(cq) * gqa_ref[...]).astype(BF16)
    qc = jnp.dot(cqn, wqb_ref[...], preferred_element_type=F32)
    qhs = []
    for hh in range(H_C):
        qh = qc[:, hh * HEAD_PAD:(hh + 1) * HEAD_PAD]
        s2 = jnp.sum(qh * qh, axis=-1, keepdims=True) * (1.0 / QK_C)
        qh = qh * lax.rsqrt(s2 + EPS) * gqc_ref[...]
        if not is_ctx:
            qh = _rope(qh, rc_ref)
        qhs.append(qh * (QK_C ** -0.5 * LOG2E))
    qcT_ref[...] = jnp.concatenate(qhs, axis=-1).T.astype(BF16)
    ckv_n = _rms(proj[:, OFF_CKV:OFF_CQ]) * gkva_ref[...]
    krc = proj[:, KR_CHUNK:KR_CHUNK + LANES]
    lane = lax.broadcasted_iota(jnp.int32, krc.shape, 1)
    kr_placed = jnp.where((lane >= NOPE_C) & (lane < QK_C), krc, 0.0)
    _mla_kv(ckv_n, kr_placed, wkb_ref, wvb_ref, gkc_ref, rc_ref, kc_ref, vcT_ref)
    if is_ctx:
        va_st_ref[...] = av
        ckv_st_ref[...] = ckv_n
        ckr_st_ref[...] = krc[:, NOPE_C:QK_C]


def _pre_mix(x, mod, lw, rope, *, B, T, is_ctx, tm, seg):
    N, D = x.shape
    nt = N // tm
    tps = T // tm
    row_spec = lambda w: pl.BlockSpec((tm, w), lambda i: (i, 0))
    in_specs = [
        row_spec(D),
        pl.BlockSpec((1, 6, D), (lambda i: (0, 0, 0)) if is_ctx else (lambda i: (i // tps, 0, 0))),
        _const_spec((1, D)),
        _const_spec((D, IN_EXT)),
        _const_spec((MXU_DIM, MXU_DIM)),
        _const_spec((1, 2 * A_W)),
        _const_spec((1, Q_RANK)),
        _const_spec((1, KV_RANK)),
        _const_spec((1, HEAD_PAD)),
        _const_spec((1, HEAD_PAD)),
        _const_spec((Q_RANK, C_W)),
        _const_spec((KV_RANK, C_W)),
        _const_spec((KV_RANK, MIX_C)),
        _const_spec((3, W_B)),
        _const_spec((1, W_B)),
    ]
    args = [x, mod, lw["norm_mix"], lw["w_in"], lw["b32"], lw["gqk"], lw["gqa"], lw["gkva"],
            lw["gqc"], lw["gkc"], lw["w_qb"], lw["w_kb"], lw["w_vb"], lw["conv_w"], lw["conv_b"]]
    if not is_ctx:
        rspec = pl.BlockSpec((3, tm, LANES), lambda i: (0, i % tps, 0))
        in_specs += [rspec, rspec]
        args += list(rope)
    colT = lambda w: pl.BlockSpec((w, tm), lambda i: (0, i))
    vT_spec = pl.BlockSpec((1, H_A, V_ROWS, tm), lambda i: (i // tps, 0, 0, i % tps))
    out_specs = [colT(A_W), pl.BlockSpec((H_A, tm, 2 * DA), lambda i: (0, i, 0)), vT_spec, row_spec(W_B),
                 colT(C_W), pl.BlockSpec((H_C, tm, HEAD_PAD), lambda i: (0, i, 0)), vT_spec]
    out_shape = [jax.ShapeDtypeStruct((A_W, N), BF16), jax.ShapeDtypeStruct((H_A, N, 2 * DA), BF16),
                 jax.ShapeDtypeStruct((B, H_A, V_ROWS, T), BF16), jax.ShapeDtypeStruct((N, W_B), BF16),
                 jax.ShapeDtypeStruct((C_W, N), BF16), jax.ShapeDtypeStruct((H_C, N, HEAD_PAD), BF16),
                 jax.ShapeDtypeStruct((B, H_C, V_ROWS, T), BF16)]
    if is_ctx:
        st_w = [A_W, A_W, KV_RANK, ROPE_C]
        out_specs += [row_spec(w) for w in st_w]
        out_shape += [jax.ShapeDtypeStruct((N, w), F32) for w in st_w]
    return pl.pallas_call(
        functools.partial(_pre_mix_kernel, is_ctx, seg),
        grid=(nt,),
        in_specs=in_specs,
        out_specs=out_specs,
        out_shape=out_shape,
        compiler_params=_cparams(("arbitrary",)),
        name="pre_mix_ctx" if is_ctx else "pre_mix_lat",
    )(*args)


def _cache_kv_kernel(ckv_ref, kr_ref, wkb_ref, wvb_ref, gkc_ref, kc_ref, vcT_ref):
    _mla_kv(ckv_ref[...], kr_ref[...], wkb_ref, wvb_ref, gkc_ref, None, kc_ref, vcT_ref)


def _cache_kv(ckv, kr_placed, lw, B, P):
    row_spec = lambda w: pl.BlockSpec((P, w), lambda i: (i, 0))
    return pl.pallas_call(
        _cache_kv_kernel,
        grid=(B,),
        in_specs=[row_spec(KV_RANK), row_spec(HEAD_PAD), _const_spec((KV_RANK, C_W)),
                  _const_spec((KV_RANK, MIX_C)), _const_spec((1, HEAD_PAD))],
        out_specs=[pl.BlockSpec((H_C, P, HEAD_PAD), lambda i: (0, i, 0)),
                   pl.BlockSpec((1, H_C, V_ROWS, P), lambda i: (i, 0, 0, 0))],
        out_shape=[jax.ShapeDtypeStruct((H_C, B * P, HEAD_PAD), BF16),
                   jax.ShapeDtypeStruct((B, H_C, V_ROWS, P), BF16)],
        compiler_params=_cparams(("arbitrary",)),
        name="cache_kv",
    )(ckv, kr_placed, lw["w_kb"], lw["w_vb"], lw["gkc"])


def _colmax(s):
    tk, tq = s.shape
    r = jnp.max(s.reshape(tk // ROW_GROUP, ROW_GROUP, tq), axis=0)
    return jnp.max(r, axis=0, keepdims=True)


def _attn_kernel(diff, has_ctx, G, chunks, tk, l_init, *refs):
    refs = list(refs)
    q_ref = refs.pop(0)
    if has_ctx:
        kc_ref, kl_ref, vc_ref, vl_ref = refs[:4]
        refs = refs[4:]
    else:
        kl_ref, vl_ref = refs[:2]
        kc_ref = vc_ref = None
        refs = refs[2:]
    if diff:
        lam_ref, g_ref = refs[:2]
        refs = refs[2:]
    o_ref, sa, sb, os_ref = refs
    tq = q_ref.shape[-1]
    uw = tq if diff else tq // 2
    bufs = (sa, sb)
    nk = len(chunks)
    dq = q_ref.shape[0] // G
    gpt = 1 if nk % 2 == 0 else (2 if G % 2 == 0 else G)
    ntrip = G // gpt

    def q_units(g):
        start = g * dq
        if not isinstance(start, int):
            start = pl.multiple_of(start, dq)
        qh = q_ref[pl.ds(start, dq), :]
        if diff:
            row = lax.broadcasted_iota(jnp.int32, qh.shape, 0)
            zero = jnp.zeros_like(qh)
            return [jnp.where(row < DA, qh, zero), jnp.where(row >= DA, qh, zero)]
        return [qh[:, :uw], qh[:, uw:]]

    def kslice(h, j):
        src, off = chunks[j]
        return (kc_ref if src == 0 else kl_ref)[h, pl.ds(off, tk), :]

    def vslice(h, j):
        src, off = chunks[j]
        return (vc_ref if src == 0 else vl_ref)[0, h, :, pl.ds(off, tk)]

    def qk(g, qs, j, buf):
        kj = kslice(g, j)
        for u in range(2):
            buf[u] = jnp.dot(kj, qs[u], preferred_element_type=F32)

    def softmax(buf, carry):
        ps, out = [], []
        for u in range(2):
            m, acc = carry[u]
            s = buf[u]
            m_new = jnp.maximum(m, _colmax(s))
            alpha = jnp.exp2(m - m_new)
            ps.append(jnp.exp2(s - m_new).astype(BF16))
            out.append((m_new, alpha * acc))
        return ps, out

    def pv(g, j, ps, st):
        vj = vslice(g, j)
        return [(st[u][0], st[u][1] + jnp.dot(vj, ps[u], preferred_element_type=F32)) for u in range(2)]

    def finish(g, res):
        os_ = [acc[:V_A] / acc[V_A:V_A + 1] for (_, acc) in res]
        if diff:
            lv = lam_ref[...]
            lam = (jnp.exp(jnp.sum(lv[0:1] * lv[1:2], axis=-1, keepdims=True))
                   - jnp.exp(jnp.sum(lv[2:3] * lv[3:4], axis=-1, keepdims=True)) + l_init)
            o = os_[0] - lam * os_[1]
            os_ref[g] = _rms(o, axis=0) * g_ref[...] * (1.0 - l_init)
        else:
            os_ref[g, :, 0:uw] = os_[0]
            os_ref[g, :, uw:tq] = os_[1]

    def trip(t, _):
        step = 0
        g = t * gpt
        qs = q_units(g)
        for gi in range(gpt):
            g = t * gpt + gi
            carry = [(jnp.full((1, uw), -jnp.inf, F32), jnp.zeros((V_ROWS, uw), F32)) for _ in range(2)]
            for j in range(nk):
                src, dst = bufs[step % 2], bufs[(step + 1) % 2]
                qs_next = qs
                if j + 1 < nk:
                    qk(g, qs, j + 1, dst)
                elif gi + 1 < gpt:
                    qs_next = q_units(g + 1)
                    qk(g + 1, qs_next, 0, dst)
                elif ntrip > 1:
                    g_next = jnp.minimum(g + 1, G - 1)
                    qk(g_next, q_units(g_next), 0, dst)
                ps, st = softmax(src, carry)
                carry = pv(g, j, ps, st)
                qs = qs_next
                step += 1
            finish(g, carry)
        return 0

    qk(0, q_units(0), 0, sa)
    if ntrip > 1:
        lax.fori_loop(0, ntrip, trip, 0)
    else:
        trip(0, 0)
    for c in range(G // 2):
        pair = jnp.concatenate([os_ref[2 * c], os_ref[2 * c + 1]], axis=0)
        o_ref[:, c * LANES:(c + 1) * LANES] = pair.T.astype(o_ref.dtype)


def _attention(qT, k_lat, v_lat, k_ctx=None, v_ctx=None, *, diff, tq, tk, lam_vecs=None, subln=None,
               l_init=0.0):
    B, H, _, T = v_lat.shape
    N = qT.shape[1]
    d = k_lat.shape[2]
    has_ctx = k_ctx is not None
    chunks = [(1, o) for o in range(0, T, tk)]
    nq = T // tq
    in_specs = [pl.BlockSpec((qT.shape[0], tq), lambda b, i: (0, b * nq + i))]
    args = [qT]
    if has_ctx:
        P = v_ctx.shape[3]
        chunks = [(0, o) for o in range(0, P, tk)] + chunks
        in_specs += [pl.BlockSpec((H, P, d), lambda b, i: (0, b, 0)),
                     pl.BlockSpec((H, T, d), lambda b, i: (0, b, 0)),
                     pl.BlockSpec((1, H, V_ROWS, P), lambda b, i: (b, 0, 0, 0)),
                     pl.BlockSpec((1, H, V_ROWS, T), lambda b, i: (b, 0, 0, 0))]
        args += [k_ctx, k_lat, v_ctx, v_lat]
    else:
        in_specs += [pl.BlockSpec((H, T, d), lambda b, i: (0, b, 0)),
                     pl.BlockSpec((1, H, V_ROWS, T), lambda b, i: (b, 0, 0, 0))]
        args += [k_lat, v_lat]
    if diff:
        in_specs += [pl.BlockSpec((4, DA), lambda b, i: (0, 0)),
                     pl.BlockSpec((V_A, 1), lambda b, i: (0, 0))]
        args += [lam_vecs, subln]
    uw = tq if diff else tq // 2
    return pl.pallas_call(
        functools.partial(_attn_kernel, diff, has_ctx, H, tuple(chunks), tk, l_init),
        grid=(B, nq),
        in_specs=in_specs,
        out_specs=pl.BlockSpec((tq, H * V_A), lambda b, i: (b * nq + i, 0)),
        out_shape=jax.ShapeDtypeStruct((N, H * V_A), BF16),
        scratch_shapes=[pltpu.VMEM((2, tk, uw), F32), pltpu.VMEM((2, tk, uw), F32),
                        pltpu.VMEM((H, V_A, tq), F32)],
        compiler_params=_cparams(("arbitrary", "arbitrary")),
        name="attn_diff" if diff else "attn_mla",
    )(*args)


def _post_head(x_ref, oa_ref, ob_ref, oc_ref, mod_ref, nffn_ref, wout_ref):
    y = (jnp.dot(oa_ref[...], wout_ref[0:MIX_A, :], preferred_element_type=F32)
         + jnp.dot(ob_ref[...], wout_ref[MIX_A:MIX_A + W_B, :], preferred_element_type=F32)
         + jnp.dot(oc_ref[...], wout_ref[MIX_A + W_B:, :], preferred_element_type=F32))
    x1 = x_ref[...] + mod_ref[0, 2:3, :] * y
    h2 = _rms(x1) * nffn_ref[...] * (1.0 + mod_ref[0, 4:5, :]) + mod_ref[0, 3:4, :]
    return x1, h2


def _swiglu(hb, wg, wu, wd):
    g = jnp.dot(hb, wg, preferred_element_type=F32)
    u = jnp.dot(hb, wu, preferred_element_type=F32)
    a = (g * jax.nn.sigmoid(g) * u).astype(BF16)
    return jnp.dot(a, wd, preferred_element_type=F32)


def _post_dense_kernel(fchunks, x_ref, oa_ref, ob_ref, oc_ref, mod_ref, nffn_ref, wout_ref,
                       wg_ref, wu_ref, wd_ref, o_ref):
    x1, h2 = _post_head(x_ref, oa_ref, ob_ref, oc_ref, mod_ref, nffn_ref, wout_ref)
    hb = h2.astype(BF16)
    acc = None
    for (f0, f1) in fchunks:
        part = _swiglu(hb, wg_ref[:, f0:f1], wu_ref[:, f0:f1], wd_ref[f0:f1, :])
        acc = part if acc is None else acc + part
    o_ref[...] = x1 + mod_ref[0, 5:6, :] * acc


def _post_dense(x, oa, ob, oc, mod, lw, *, tm, tiles_per_mod):
    N, D = x.shape
    FF = lw["wg"].shape[1]
    cut = (FF // 2 // MXU_DIM + 1) * MXU_DIM if FF > 2 * MXU_DIM else FF
    fchunks = ((0, cut), (cut, FF)) if cut < FF else ((0, FF),)
    row = lambda w: pl.BlockSpec((tm, w), lambda i: (i, 0))
    return pl.pallas_call(
        functools.partial(_post_dense_kernel, fchunks),
        grid=(N // tm,),
        in_specs=[row(D), row(MIX_A), row(W_B), row(MIX_C),
                  pl.BlockSpec((1, 6, D), lambda i: (i // tiles_per_mod, 0, 0)),
                  _const_spec((1, D)), _const_spec(lw["w_out"].shape), _const_spec((D, FF)),
                  _const_spec((D, FF)), _const_spec((FF, D))],
        out_specs=row(D),
        out_shape=jax.ShapeDtypeStruct((N, D), F32),
        compiler_params=_cparams(("arbitrary",)),
        name="post_dense",
    )(x, oa, ob, oc, mod, lw["norm_ffn"], lw["w_out"], lw["wg"], lw["wu"], lw["wd"])


def _post_moe_kernel(x_ref, oa_ref, ob_ref, oc_ref, mod_ref, nffn_ref, wout_ref, rw_ref,
                     wg_ref, wu_ref, wd_ref, o_ref, x1_s, hb_s, comb_s, acc_s):
    e = pl.program_id(1)

    @pl.when(e == 0)
    def _():
        x1, h2 = _post_head(x_ref, oa_ref, ob_ref, oc_ref, mod_ref, nffn_ref, wout_ref)
        x1_s[...] = x1
        hb_s[...] = h2.astype(BF16)
        rw = rw_ref[...]
        rhi = rw.astype(BF16)
        rlo = (rw - rhi.astype(F32)).astype(BF16)
        hhi = h2.astype(BF16)
        hlo = (h2 - hhi.astype(F32)).astype(BF16)
        logits = (jnp.dot(hhi, rhi, preferred_element_type=F32)
                  + jnp.dot(hlo, rhi, preferred_element_type=F32)
                  + jnp.dot(hhi, rlo, preferred_element_type=F32))
        lane = lax.broadcasted_iota(jnp.int32, logits.shape, 1)
        neg = -jnp.inf
        lg = jnp.where(lane < N_EXPERTS, logits, neg)
        v1 = jnp.max(lg, axis=-1, keepdims=True)
        i1 = jnp.min(jnp.where(lg == v1, lane, LANES), axis=-1, keepdims=True)
        lg2 = jnp.where(lane == i1, neg, lg)
        v2 = jnp.max(lg2, axis=-1, keepdims=True)
        i2 = jnp.min(jnp.where(lg2 == v2, lane, LANES), axis=-1, keepdims=True)
        e2 = jnp.exp(v2 - v1)
        g1 = 1.0 / (1.0 + e2)
        g2 = e2 / (1.0 + e2)
        comb_s[...] = jnp.where(lane == i1, g1, 0.0) + jnp.where(lane == i2, g2, 0.0)
        acc_s[...] = jnp.zeros_like(acc_s)

    comb = comb_s[...]
    lane = lax.broadcasted_iota(jnp.int32, comb.shape, 1)
    w_e = jnp.sum(jnp.where(lane == e, comb, 0.0), axis=-1, keepdims=True)
    acc_s[...] += w_e * _swiglu(hb_s[...], wg_ref[0], wu_ref[0], wd_ref[0])

    @pl.when(e == pl.num_programs(1) - 1)
    def _():
        o_ref[...] = x1_s[...] + mod_ref[0, 5:6, :] * acc_s[...]


def _post_moe(x, oa, ob, oc, mod, lw, *, tm, tiles_per_mod):
    N, D = x.shape
    E, _, FF = lw["wg"].shape
    row = lambda w: pl.BlockSpec((tm, w), lambda i, e: (i, 0))
    return pl.pallas_call(
        _post_moe_kernel,
        grid=(N // tm, E),
        in_specs=[row(D), row(MIX_A), row(W_B), row(MIX_C),
                  pl.BlockSpec((1, 6, D), lambda i, e: (i // tiles_per_mod, 0, 0)),
                  _const_spec((1, D)), _const_spec(lw["w_out"].shape), _const_spec((D, LANES)),
                  pl.BlockSpec((1, D, FF), lambda i, e: (e, 0, 0)),
                  pl.BlockSpec((1, D, FF), lambda i, e: (e, 0, 0)),
                  pl.BlockSpec((1, FF, D), lambda i, e: (e, 0, 0))],
        out_specs=row(D),
        out_shape=jax.ShapeDtypeStruct((N, D), F32),
        scratch_shapes=[pltpu.VMEM((tm, D), F32), pltpu.VMEM((tm, D), BF16),
                        pltpu.VMEM((tm, LANES), F32), pltpu.VMEM((tm, D), F32)],
        compiler_params=_cparams(("arbitrary", "arbitrary")),
        name="post_moe",
    )(x, oa, ob, oc, mod, lw["norm_ffn"], lw["w_out"], lw["router"], lw["wg"], lw["wu"], lw["wd"])


def _rope_tables(T):
    t = jnp.arange(T, dtype=jnp.int32)
    rows = (t // GRID_W).astype(F32)
    cols = (t % GRID_W).astype(F32)
    n = ROPE_C // 4
    inv = jnp.power(ROPE_THETA, -jnp.arange(n, dtype=F32) / n)
    j = jnp.arange(ROPE_C)
    pos = jnp.where(j[None, :] < ROPE_C // 2, rows[:, None], cols[:, None])
    ang = pos * inv[j % n][None, :]
    cos = jnp.cos(ang)
    sin = jnp.sin(ang)
    first = ((j % (2 * n)) < n)[None, :]
    s_up = jnp.where(first, -sin, 0.0)
    s_dn = jnp.where(first, 0.0, sin)
    tab32 = jnp.stack([cos, s_up, s_dn])
    tab_a = jnp.tile(tab32, (1, 1, LANES // ROPE_C))
    ident = jnp.stack([jnp.ones((T, NOPE_C), F32), jnp.zeros((T, NOPE_C), F32), jnp.zeros((T, NOPE_C), F32)])
    tail = jnp.stack([jnp.ones((T, HEAD_PAD - QK_C), F32), jnp.zeros((T, HEAD_PAD - QK_C), F32),
                      jnp.zeros((T, HEAD_PAD - QK_C), F32)])
    tab_c = jnp.concatenate([ident, tab32, tail], axis=-1)
    return tab_a, tab_c


def _pad_heads(w, width):
    lead = w.shape[:-1]
    w = w.reshape(lead + (H_C, width))
    w = jnp.pad(w, [(0, 0)] * len(lead) + [(0, 0), (0, HEAD_PAD - width)])
    return w.reshape(lead + (C_W,))


def _layer_weights(l, p):
    w_in = p["w_in"][l]
    D = w_in.shape[0]
    o_cq = 3 * A_W + 3 * W_B
    o_ckv = o_cq + Q_RANK
    o_ckr = o_ckv + KV_RANK
    w_ext = jnp.concatenate(
        [w_in[:, :o_cq], w_in[:, o_ckv:o_ckr], w_in[:, o_cq:o_ckv], w_in[:, o_ckr:],
         jnp.zeros((D, IN_EXT - OFF_CKR - ROPE_C), F32)], axis=1).astype(BF16)
    g = jnp.arange(MXU_DIM) // DA
    lw = {
        "w_in": w_ext,
        "b32": (g[:, None] == g[None, :]).astype(BF16),
        "norm_mix": p["norm_mix"][l][None, :],
        "norm_ffn": p["norm_ffn"][l][None, :],
        "gqk": jnp.concatenate([jnp.tile(p["qnorm_a"][l], 2 * H_A), jnp.tile(p["knorm_a"][l], 2 * H_A)])[None, :],
        "gqa": p["norm_qa"][l][None, :],
        "gkva": p["norm_kva"][l][None, :],
        "gqc": jnp.pad(p["qnorm_c"][l], (0, HEAD_PAD - QK_C))[None, :],
        "gkc": jnp.pad(p["knorm_c"][l], (0, HEAD_PAD - QK_C))[None, :],
        "w_qb": _pad_heads(p["w_qb"][l], QK_C).astype(BF16),
        "w_kb": _pad_heads(p["w_kb"][l], NOPE_C).astype(BF16),
        "w_vb": p["w_vb"][l].astype(BF16),
        "conv_w": p["conv_w"][l],
        "conv_b": p["conv_b"][l][None, :],
        "w_out": p["w_out"][l].astype(BF16),
        "lam": jnp.stack([p["lambda_q1"][l], p["lambda_k1"][l], p["lambda_q2"][l], p["lambda_k2"][l]]),
        "subln": p["subln_a"][l][:, None],
    }
    i = l // 2
    if l % 2 == 0:
        lw["wg"] = p["ffn_w_gate"][i].astype(BF16)
        lw["wu"] = p["ffn_w_up"][i].astype(BF16)
        lw["wd"] = p["ffn_w_down"][i].astype(BF16)
    else:
        lw["router"] = jnp.pad(p["router_w"][i], ((0, 0), (0, LANES - N_EXPERTS)))
        lw["wg"] = p["moe_w_gate"][i].astype(BF16)
        lw["wu"] = p["moe_w_up"][i].astype(BF16)
        lw["wd"] = p["moe_w_down"][i].astype(BF16)
    return lw


def _cached_vT(v, B, P):
    vT = v.transpose(0, 2, 3, 1).astype(BF16)
    ones = jnp.ones((B, H_A, 1, P), BF16)
    zeros = jnp.zeros((B, H_A, V_ROWS - V_A - 1, P), BF16)
    return jnp.concatenate([vT, ones, zeros], axis=2)


def _mixer(x, mod, lw, l_init, *, B, T, ctx, rope):
    is_ctx = ctx is None
    tm = T if is_ctx else min(512, T)
    seg = T if is_ctx else GRID_W
    outs = _pre_mix(x, mod, lw, rope, B=B, T=T, is_ctx=is_ctx, tm=tm, seg=seg)
    qaT, ka, vaT, ob, qcT, kc, vcT = outs[:7]
    tq = min(256, T)
    tq_c = min(2 * tq, T)
    if is_ctx:
        tk = 512 if T % 512 == 0 else 256
        o_a = _attention(qaT, ka, vaT, diff=True, tq=tq, tk=tk, lam_vecs=lw["lam"], subln=lw["subln"],
                         l_init=l_init)
        o_c = _attention(qcT, kc, vcT, diff=False, tq=tq_c, tk=tk)
    else:
        ctx_k, ctx_v, ctx_ckv, ctx_kr = ctx
        P = ctx_k.shape[1]
        tk = 512 if (P % 512 == 0 and T % 512 == 0) else 256
        kr_placed = jnp.pad(ctx_kr.reshape(B * P, ROPE_C), ((0, 0), (NOPE_C, HEAD_PAD - QK_C)))
        kc_ctx, vc_ctx = _cache_kv(ctx_ckv.reshape(B * P, KV_RANK), kr_placed, lw, B, P)
        ka_ctx = ctx_k.reshape(B * P, H_A, 2 * DA).transpose(1, 0, 2).astype(BF16)
        o_a = _attention(qaT, ka, vaT, ka_ctx, _cached_vT(ctx_v, B, P), diff=True, tq=tq, tk=tk,
                         lam_vecs=lw["lam"], subln=lw["subln"], l_init=l_init)
        o_c = _attention(qcT, kc, vcT, kc_ctx, vc_ctx, diff=False, tq=tq_c, tk=tk)
    return (o_a, ob, o_c), outs[7:]


def kernel(x_prompt, x_sample, cache_diff_k, cache_diff_v, cache_mla_ckv, cache_mla_krope, c, c_ctx, w_ada, b_ada, norm_mix, norm_ffn, w_in, qnorm_a, knorm_a, lambda_q1, lambda_k1, lambda_q2, lambda_k2, subln_a, conv_w, conv_b, norm_qa, w_qb, norm_kva, w_kb, w_vb, qnorm_c, knorm_c, w_out, ffn_w_gate, ffn_w_up, ffn_w_down, router_w, moe_w_gate, moe_w_up, moe_w_down):
    p = dict(norm_mix=norm_mix, norm_ffn=norm_ffn, w_in=w_in, qnorm_a=qnorm_a, knorm_a=knorm_a,
             lambda_q1=lambda_q1, lambda_k1=lambda_k1, lambda_q2=lambda_q2, lambda_k2=lambda_k2,
             subln_a=subln_a, conv_w=conv_w, conv_b=conv_b, norm_qa=norm_qa, w_qb=w_qb,
             norm_kva=norm_kva, w_kb=w_kb, w_vb=w_vb, qnorm_c=qnorm_c, knorm_c=knorm_c, w_out=w_out,
             ffn_w_gate=ffn_w_gate, ffn_w_up=ffn_w_up, ffn_w_down=ffn_w_down, router_w=router_w,
             moe_w_gate=moe_w_gate, moe_w_up=moe_w_up, moe_w_down=moe_w_down)
    Bc, Sc, D = x_prompt.shape
    Bl, T, _ = x_sample.shape
    L = w_in.shape[0]
    nrow = 16
    cvec = jnp.concatenate([c, c_ctx[None, :], jnp.zeros((nrow - Bl - 1, D), F32)], axis=0)
    mod = _ada(cvec, w_ada, b_ada)
    rope = _rope_tables(T)
    xp = x_prompt.reshape(Bc * Sc, D)
    xs = x_sample.reshape(Bl * T, D)
    st = [[], [], [], []]
    for l in range(L):
        lw = _layer_weights(l, p)
        l_init = 0.8 - 0.6 * math.exp(-0.3 * l)
        mod_lat = mod[l, :Bl].reshape(Bl, 6, D)
        mod_ctx = mod[l, Bl:Bl + 1].reshape(1, 6, D)
        post = _post_dense if l % 2 == 0 else _post_moe
        heads, state = _mixer(xp, mod_ctx, lw, l_init, B=Bc, T=Sc, ctx=None, rope=None)
        for acc, s in zip(st, state):
            acc.append(s)
        xp = post(xp, *heads, mod_ctx, lw, tm=min(512, Bc * Sc), tiles_per_mod=Bc * Sc)
        ctx = (cache_diff_k[:, l], cache_diff_v[:, l], cache_mla_ckv[:, l], cache_mla_krope[:, l])
        heads, _ = _mixer(xs, mod_lat, lw, l_init, B=Bl, T=T, ctx=ctx, rope=rope)
        tm = min(512, T)
        xs = post(xs, *heads, mod_lat, lw, tm=tm, tiles_per_mod=T // tm)
    new_k = jnp.stack(st[0], axis=1).reshape(Bc, Sc, L, H_A, 2, DA).transpose(0, 2, 1, 3, 4, 5)
    new_v = jnp.stack(st[1], axis=1).reshape(Bc, Sc, L, H_A, V_A).transpose(0, 2, 1, 3, 4)
    new_ckv = jnp.stack(st[2], axis=1).reshape(Bc, Sc, L, KV_RANK).transpose(0, 2, 1, 3)
    new_kr = jnp.stack(st[3], axis=1).reshape(Bc, Sc, L, ROPE_C).transpose(0, 2, 1, 3)
    return (xp.reshape(Bc, Sc, D), xs.reshape(Bl, T, D), new_k, new_v, new_ckv, new_kr)
```

```python
import functools
import math

import jax
import jax.numpy as jnp
from jax import lax
from jax.experimental import pallas as pl
from jax.experimental.pallas import tpu as pltpu

F32 = jnp.float32
BF16 = jnp.bfloat16

GRID_W = 64
H_A = 6
DA = 32
V_A = 2 * DA
W_B = 256
H_C = 6
NOPE_C = 64
ROPE_C = 32
QK_C = NOPE_C + ROPE_C
V_C = 64
Q_RANK = 192
KV_RANK = 128
N_EXPERTS = 8
ROPE_THETA = 10000.0
EPS = 1e-6
LOG2E = 1.4426950408889634

LANES = 128
MXU_DIM = 256
HEAD_PAD = 128

A_W = H_A * 2 * DA
OFF_AQ = 0
OFF_AK = A_W
OFF_AV = 2 * A_W
OFF_BB = 3 * A_W
OFF_BC = OFF_BB + W_B
OFF_BX = OFF_BC + W_B
OFF_CKV = OFF_BX + W_B
OFF_CQ = OFF_CKV + KV_RANK
OFF_CKR = OFF_CQ + Q_RANK
KR_CHUNK = (OFF_CKR // LANES) * LANES
assert OFF_CKR - KR_CHUNK == NOPE_C
IN_EXT = KR_CHUNK + LANES
C_W = H_C * HEAD_PAD
MIX_A = H_A * V_A
MIX_C = H_C * V_C

ROW_GROUP = 32
V_ROWS = V_A + 16
assert V_A == V_C

VMEM_LIMIT = 56 * 1024 * 1024


def _cparams(sem):
    return pltpu.CompilerParams(dimension_semantics=sem, vmem_limit_bytes=VMEM_LIMIT)


def _const_spec(shape):
    nd = len(shape)
    return pl.BlockSpec(shape, lambda *_: (0,) * nd, pipeline_mode=pl.Buffered(1))


def _rms(x, axis=-1):
    return x * lax.rsqrt(jnp.mean(x * x, axis=axis, keepdims=True) + EPS)


def _split_dot(x, w):
    hi = x.astype(BF16)
    lo = (x - hi.astype(F32)).astype(BF16)
    return (jnp.dot(hi, w, preferred_element_type=F32)
            + jnp.dot(lo, w, preferred_element_type=F32))


def _ada_kernel(c_ref, w_ref, b_ref, o_ref):
    c = c_ref[...]
    s = (c * jax.nn.sigmoid(c)).astype(BF16)
    o_ref[0] = jnp.dot(s, w_ref[0].astype(BF16), preferred_element_type=F32) + b_ref[0]


def _ada(cvec, w_ada, b_ada):
    L, D, N6 = w_ada.shape
    R = cvec.shape[0]
    tn = 1536
    assert N6 % tn == 0
    return pl.pallas_call(
        _ada_kernel,
        grid=(L, N6 // tn),
        in_specs=[
            pl.BlockSpec((R, D), lambda l, j: (0, 0)),
            pl.BlockSpec((1, D, tn), lambda l, j: (l, 0, j)),
            pl.BlockSpec((1, 1, tn), lambda l, j: (l, 0, j)),
        ],
        out_specs=pl.BlockSpec((1, R, tn), lambda l, j: (l, 0, j)),
        out_shape=jax.ShapeDtypeStruct((L, R, N6), F32),
        compiler_params=_cparams(("arbitrary", "arbitrary")),
        name="ada_mod",
    )(cvec, w_ada, b_ada.reshape(L, 1, N6))


def _rope(x, tab_ref):
    up = pltpu.roll(x, LANES - 8, 1)
    dn = pltpu.roll(x, 8, 1)
    return x * tab_ref[0] + up * tab_ref[1] + dn * tab_ref[2]


def _store_vT(v, ref):
    tm = v.shape[0]
    vT = v.T.astype(BF16)
    row = lax.broadcasted_iota(jnp.int32, (V_ROWS - V_A, tm), 0)
    tail = jnp.where(row == 0, 1.0, 0.0).astype(BF16)
    for h in range(v.shape[1] // V_A):
        ref[0, h, 0:V_A, :] = vT[h * V_A:(h + 1) * V_A, :]
        ref[0, h, V_A:V_ROWS, :] = tail


def _mla_kv(ckv_n, kr_placed, wkb_ref, wvb_ref, gkc_ref, rope_ref, kc_ref, vcT_ref):
    cb = ckv_n.astype(BF16)
    kn = jnp.dot(cb, wkb_ref[...], preferred_element_type=F32)
    vc = jnp.dot(cb, wvb_ref[...], preferred_element_type=F32)
    for h in range(H_C):
        kp = kn[:, h * HEAD_PAD:(h + 1) * HEAD_PAD] + kr_placed
        ss = jnp.sum(kp * kp, axis=-1, keepdims=True) * (1.0 / QK_C)
        kp = kp * lax.rsqrt(ss + EPS) * gkc_ref[...]
        if rope_ref is not None:
            kp = _rope(kp, rope_ref)
        kc_ref[h] = kp.astype(BF16)
    _store_vT(vc, vcT_ref)


def _pre_mix_kernel(is_ctx, seg, nsub, *refs):
    refs = list(refs)
    sub = refs[0].shape[0] // nsub
    row_dim = {0: 0}
    n_in = 15 if is_ctx else 17
    if not is_ctx:
        row_dim.update({15: 1, 16: 1})
    for k, dim in enumerate((1, 1, 3, 0, 1, 1, 3) + ((0, 0, 0, 0) if is_ctx else ())):
        row_dim[n_in + k] = dim
    for r in range(nsub):
        views = []
        for pos, ref in enumerate(refs):
            if pos in row_dim:
                idx = [slice(None)] * len(ref.shape)
                idx[row_dim[pos]] = pl.ds(r * sub, sub)
                ref = ref.at[tuple(idx)]
            views.append(ref)
        _pre_mix_rows(is_ctx, seg, *views)


def _pre_mix_rows(is_ctx, seg, *refs):
    (x_ref, mod_ref, nmix_ref, win_ref, b32_ref, gqk_ref, gqa_ref, gkva_ref, gqc_ref,
     gkc_ref, wqb_ref, wkb_ref, wvb_ref, cw_ref, cb_ref) = refs[:15]
    refs = refs[15:]
    if is_ctx:
        ra_ref = rc_ref = None
    else:
        ra_ref, rc_ref = refs[:2]
        refs = refs[2:]
    qaT_ref, ka_ref, vaT_ref, ob_ref, qcT_ref, kc_ref, vcT_ref = refs[:7]
    refs = refs[7:]

    x = x_ref[...]
    shift1 = mod_ref[0, 0:1, :]
    scale1 = mod_ref[0, 1:2, :]
    h = _rms(x) * nmix_ref[...] * (1.0 + scale1) + shift1
    proj = jnp.dot(h.astype(BF16), win_ref[...], preferred_element_type=F32)

    aqk = proj[:, OFF_AQ:OFF_AV]
    sq = aqk * aqk
    ss = jnp.concatenate(
        [_split_dot(sq[:, c:c + MXU_DIM], b32_ref[...]) for c in range(0, 2 * A_W, MXU_DIM)], axis=-1)
    aqk = aqk * lax.rsqrt(ss * (1.0 / DA) + EPS) * gqk_ref[...]
    if is_ctx:
        ka_st_ref, va_st_ref, ckv_st_ref, ckr_st_ref = refs
        ka_st_ref[...] = aqk[:, A_W:]
    else:
        aqk = jnp.concatenate(
            [_rope(aqk[:, c:c + LANES], ra_ref) for c in range(0, 2 * A_W, LANES)], axis=-1)
    qaT_ref[...] = (aqk[:, :A_W] * (DA ** -0.5 * LOG2E)).T.astype(BF16)
    for hh in range(H_A):
        ka_ref[hh] = aqk[:, A_W + hh * 2 * DA:A_W + (hh + 1) * 2 * DA].astype(BF16)
    av = proj[:, OFF_AV:OFF_BB]
    _store_vT(av, vaT_ref)

    bb = proj[:, OFF_BB:OFF_BC]
    u = proj[:, OFF_BC:OFF_BX] * proj[:, OFF_BX:OFF_CKV]
    tm = u.shape[0]
    row = lax.broadcasted_iota(jnp.int32, u.shape, 0) & (seg - 1)
    prev = jnp.where(row == 0, 0.0, pltpu.roll(u, 1, 0))
    nxt = jnp.where(row == seg - 1, 0.0, pltpu.roll(u, tm - 1, 0))
    conv = prev * cw_ref[0:1, :] + u * cw_ref[1:2, :] + nxt * cw_ref[2:3, :] + cb_ref[...]
    ob_ref[...] = (bb * conv).astype(BF16)

    cq = proj[:, OFF_CQ:OFF_CKR]
    cqn = (_rms(cq) * gqa_ref[...]).astype(BF16)
    qc = jnp.dot(cqn, wqb_ref[...], preferred_element_type=F32)
    qhs = []
    for hh in range(H_C):
        qh = qc[:, hh * HEAD_PAD:(hh + 1) * HEAD_PAD]
        s2 = jnp.sum(qh * qh, axis=-1, keepdims=True) * (1.0 / QK_C)
        qh = qh * lax.rsqrt(s2 + EPS) * gqc_ref[...]
        if not is_ctx:
            qh = _rope(qh, rc_ref)
        qhs.append(qh * (QK_C ** -0.5 * LOG2E))
    qcT_ref[...] = jnp.concatenate(qhs, axis=-1).T.astype(BF16)
    ckv_n = _rms(proj[:, OFF_CKV:OFF_CQ]) * gkva_ref[...]
    krc = proj[:, KR_CHUNK:KR_CHUNK + LANES]
    lane = lax.broadcasted_iota(jnp.int32, krc.shape, 1)
    kr_placed = jnp.where((lane >= NOPE_C) & (lane < QK_C), krc, 0.0)
    _mla_kv(ckv_n, kr_placed, wkb_ref, wvb_ref, gkc_ref, rc_ref, kc_ref, vcT_ref)
    if is_ctx:
        va_st_ref[...] = av
        ckv_st_ref[...] = ckv_n
        ckr_st_ref[...] = krc[:, NOPE_C:QK_C]


def _pre_mix(x, mod, lw, rope, *, B, T, is_ctx, tm, seg, nsub):
    N, D = x.shape
    assert (tm // nsub) % seg == 0 and (tm // nsub) % LANES == 0
    nt = N // tm
    tps = T // tm
    row_spec = lambda w: pl.BlockSpec((tm, w), lambda i: (i, 0))
    in_specs = [
        row_spec(D),
        pl.BlockSpec((1, 6, D), (lambda i: (0, 0, 0)) if is_ctx else (lambda i: (i // tps, 0, 0))),
        _const_spec((1, D)),
        _const_spec((D, IN_EXT)),
        _const_spec((MXU_DIM, MXU_DIM)),
        _const_spec((1, 2 * A_W)),
        _const_spec((1, Q_RANK)),
        _const_spec((1, KV_RANK)),
        _const_spec((1, HEAD_PAD)),
        _const_spec((1, HEAD_PAD)),
        _const_spec((Q_RANK, C_W)),
        _const_spec((KV_RANK, C_W)),
        _const_spec((KV_RANK, MIX_C)),
        _const_spec((3, W_B)),
        _const_spec((1, W_B)),
    ]
    args = [x, mod, lw["norm_mix"], lw["w_in"], lw["b32"], lw["gqk"], lw["gqa"], lw["gkva"],
            lw["gqc"], lw["gkc"], lw["w_qb"], lw["w_kb"], lw["w_vb"], lw["conv_w"], lw["conv_b"]]
    if not is_ctx:
        rspec = pl.BlockSpec((3, tm, LANES), lambda i: (0, i % tps, 0))
        in_specs += [rspec, rspec]
        args += list(rope)
    colT = lambda w: pl.BlockSpec((w, tm), lambda i: (0, i))
    vT_spec = pl.BlockSpec((1, H_A, V_ROWS, tm), lambda i: (i // tps, 0, 0, i % tps))
    out_specs = [colT(A_W), pl.BlockSpec((H_A, tm, 2 * DA), lambda i: (0, i, 0)), vT_spec, row_spec(W_B),
                 colT(C_W), pl.BlockSpec((H_C, tm, HEAD_PAD), lambda i: (0, i, 0)), vT_spec]
    out_shape = [jax.ShapeDtypeStruct((A_W, N), BF16), jax.ShapeDtypeStruct((H_A, N, 2 * DA), BF16),
                 jax.ShapeDtypeStruct((B, H_A, V_ROWS, T), BF16), jax.ShapeDtypeStruct((N, W_B), BF16),
                 jax.ShapeDtypeStruct((C_W, N), BF16), jax.ShapeDtypeStruct((H_C, N, HEAD_PAD), BF16),
                 jax.ShapeDtypeStruct((B, H_C, V_ROWS, T), BF16)]
    if is_ctx:
        st_w = [A_W, A_W, KV_RANK, ROPE_C]
        out_specs += [row_spec(w) for w in st_w]
        out_shape += [jax.ShapeDtypeStruct((N, w), F32) for w in st_w]
    return pl.pallas_call(
        functools.partial(_pre_mix_kernel, is_ctx, seg, nsub),
        grid=(nt,),
        in_specs=in_specs,
        out_specs=out_specs,
        out_shape=out_shape,
        compiler_params=_cparams(("arbitrary",)),
        name="pre_mix_ctx" if is_ctx else "pre_mix_lat",
    )(*args)


def _cache_kv_kernel(ckv_ref, kr_ref, wkb_ref, wvb_ref, gkc_ref, kc_ref, vcT_ref):
    _mla_kv(ckv_ref[...], kr_ref[...], wkb_ref, wvb_ref, gkc_ref, None, kc_ref, vcT_ref)


def _cache_kv(ckv, kr_placed, lw, B, P):
    row_spec = lambda w: pl.BlockSpec((P, w), lambda i: (i, 0))
    return pl.pallas_call(
        _cache_kv_kernel,
        grid=(B,),
        in_specs=[row_spec(KV_RANK), row_spec(HEAD_PAD), _const_spec((KV_RANK, C_W)),
                  _const_spec((KV_RANK, MIX_C)), _const_spec((1, HEAD_PAD))],
        out_specs=[pl.BlockSpec((H_C, P, HEAD_PAD), lambda i: (0, i, 0)),
                   pl.BlockSpec((1, H_C, V_ROWS, P), lambda i: (i, 0, 0, 0))],
        out_shape=[jax.ShapeDtypeStruct((H_C, B * P, HEAD_PAD), BF16),
                   jax.ShapeDtypeStruct((B, H_C, V_ROWS, P), BF16)],
        compiler_params=_cparams(("arbitrary",)),
        name="cache_kv",
    )(ckv, kr_placed, lw["w_kb"], lw["w_vb"], lw["gkc"])


def _colmax(s):
    tk, tq = s.shape
    r = jnp.max(s.reshape(tk // ROW_GROUP, ROW_GROUP, tq), axis=0)
    return jnp.max(r, axis=0, keepdims=True)


def _attn_kernel(diff, has_ctx, G, chunks, tk, l_init, *refs):
    refs = list(refs)
    q_ref = refs.pop(0)
    if has_ctx:
        kc_ref, kl_ref, vc_ref, vl_ref = refs[:4]
        refs = refs[4:]
    else:
        kl_ref, vl_ref = refs[:2]
        kc_ref = vc_ref = None
        refs = refs[2:]
    if diff:
        lam_ref, g_ref = refs[:2]
        refs = refs[2:]
    o_ref, sa, sb, os_ref = refs
    tq = q_ref.shape[-1]
    uw = tq if diff else tq // 2
    bufs = (sa, sb)
    nk = len(chunks)
    dq = q_ref.shape[0] // G
    gpt = 1 if nk % 2 == 0 else (2 if G % 2 == 0 else G)
    ntrip = G // gpt

    def q_units(g):
        start = g * dq
        if not isinstance(start, int):
            start = pl.multiple_of(start, dq)
        qh = q_ref[pl.ds(start, dq), :]
        if diff:
            row = lax.broadcasted_iota(jnp.int32, qh.shape, 0)
            zero = jnp.zeros_like(qh)
            return [jnp.where(row < DA, qh, zero), jnp.where(row >= DA, qh, zero)]
        return [qh[:, :uw], qh[:, uw:]]

    def kslice(h, j):
        src, off = chunks[j]
        return (kc_ref if src == 0 else kl_ref)[h, pl.ds(off, tk), :]

    def vslice(h, j):
        src, off = chunks[j]
        return (vc_ref if src == 0 else vl_ref)[0, h, :, pl.ds(off, tk)]

    def qk(g, qs, j, buf):
        kj = kslice(g, j)
        for u in range(2):
            buf[u] = jnp.dot(kj, qs[u], preferred_element_type=F32)

    def softmax(buf, carry):
        ps, out = [], []
        for u in range(2):
            m, acc = carry[u]
            s = buf[u]
            m_new = jnp.maximum(m, _colmax(s))
            alpha = jnp.exp2(m - m_new)
            ps.append(jnp.exp2(s - m_new).astype(BF16))
            out.append((m_new, alpha * acc))
        return ps, out

    def pv(g, j, ps, st):
        vj = vslice(g, j)
        return [(st[u][0], st[u][1] + jnp.dot(vj, ps[u], preferred_element_type=F32)) for u in range(2)]

    def finish(g, res):
        os_ = [acc[:V_A] / acc[V_A:V_A + 1] for (_, acc) in res]
        if diff:
            lv = lam_ref[...]
            lam = (jnp.exp(jnp.sum(lv[0:1] * lv[1:2], axis=-1, keepdims=True))
                   - jnp.exp(jnp.sum(lv[2:3] * lv[3:4], axis=-1, keepdims=True)) + l_init)
            o = os_[0] - lam * os_[1]
            os_ref[g] = _rms(o, axis=0) * g_ref[...] * (1.0 - l_init)
        else:
            os_ref[g, :, 0:uw] = os_[0]
            os_ref[g, :, uw:tq] = os_[1]

    def trip(t, _):
        step = 0
        g = t * gpt
        qs = q_units(g)
        for gi in range(gpt):
            g = t * gpt + gi
            carry = [(jnp.full((1, uw), -jnp.inf, F32), jnp.zeros((V_ROWS, uw), F32)) for _ in range(2)]
            for j in range(nk):
                src, dst = bufs[step % 2], bufs[(step + 1) % 2]
                qs_next = qs
                if j + 1 < nk:
                    qk(g, qs, j + 1, dst)
                elif gi + 1 < gpt:
                    qs_next = q_units(g + 1)
                    qk(g + 1, qs_next, 0, dst)
                elif ntrip > 1:
                    g_next = jnp.minimum(g + 1, G - 1)
                    qk(g_next, q_units(g_next), 0, dst)
                ps, st = softmax(src, carry)
                carry = pv(g, j, ps, st)
                qs = qs_next
                step += 1
            finish(g, carry)
        return 0

    qk(0, q_units(0), 0, sa)
    if ntrip > 1:
        lax.fori_loop(0, ntrip, trip, 0)
    else:
        trip(0, 0)
    for c in range(G // 2):
        pair = jnp.concatenate([os_ref[2 * c], os_ref[2 * c + 1]], axis=0)
        o_ref[:, c * LANES:(c + 1) * LANES] = pair.T.astype(o_ref.dtype)


def _attention(qT, k_lat, v_lat, k_ctx=None, v_ctx=None, *, diff, tq, tk, lam_vecs=None, subln=None,
               l_init=0.0):
    B, H, _, T = v_lat.shape
    N = qT.shape[1]
    d = k_lat.shape[2]
    has_ctx = k_ctx is not None
    chunks = [(1, o) for o in range(0, T, tk)]
    nq = T // tq
    in_specs = [pl.BlockSpec((qT.shape[0], tq), lambda b, i: (0, b * nq + i))]
    args = [qT]
    if has_ctx:
        P = v_ctx.shape[3]
        chunks = [(0, o) for o in range(0, P, tk)] + chunks
        in_specs += [pl.BlockSpec((H, P, d), lambda b, i: (0, b, 0)),
                     pl.BlockSpec((H, T, d), lambda b, i: (0, b, 0)),
                     pl.BlockSpec((1, H, V_ROWS, P), lambda b, i: (b, 0, 0, 0)),
                     pl.BlockSpec((1, H, V_ROWS, T), lambda b, i: (b, 0, 0, 0))]
        args += [k_ctx, k_lat, v_ctx, v_lat]
    else:
        in_specs += [pl.BlockSpec((H, T, d), lambda b, i: (0, b, 0)),
                     pl.BlockSpec((1, H, V_ROWS, T), lambda b, i: (b, 0, 0, 0))]
        args += [k_lat, v_lat]
    if diff:
        in_specs += [pl.BlockSpec((4, DA), lambda b, i: (0, 0)),
                     pl.BlockSpec((V_A, 1), lambda b, i: (0, 0))]
        args += [lam_vecs, subln]
    uw = tq if diff else tq // 2
    return pl.pallas_call(
        functools.partial(_attn_kernel, diff, has_ctx, H, tuple(chunks), tk, l_init),
        grid=(B, nq),
        in_specs=in_specs,
        out_specs=pl.BlockSpec((tq, H * V_A), lambda b, i: (b * nq + i, 0)),
        out_shape=jax.ShapeDtypeStruct((N, H * V_A), BF16),
        scratch_shapes=[pltpu.VMEM((2, tk, uw), F32), pltpu.VMEM((2, tk, uw), F32),
                        pltpu.VMEM((H, V_A, tq), F32)],
        compiler_params=_cparams(("arbitrary", "arbitrary")),
        name="attn_diff" if diff else "attn_mla",
    )(*args)


def _post_head(x_ref, oa_ref, ob_ref, oc_ref, mod_ref, nffn_ref, wout_ref):
    y = (jnp.dot(oa_ref[...], wout_ref[0:MIX_A, :], preferred_element_type=F32)
         + jnp.dot(ob_ref[...], wout_ref[MIX_A:MIX_A + W_B, :], preferred_element_type=F32)
         + jnp.dot(oc_ref[...], wout_ref[MIX_A + W_B:, :], preferred_element_type=F32))
    x1 = x_ref[...] + mod_ref[0, 2:3, :] * y
    h2 = _rms(x1) * nffn_ref[...] * (1.0 + mod_ref[0, 4:5, :]) + mod_ref[0, 3:4, :]
    return x1, h2


def _swiglu(hb, wg, wu, wd):
    g = jnp.dot(hb, wg, preferred_element_type=F32)
    u = jnp.dot(hb, wu, preferred_element_type=F32)
    a = (g * jax.nn.sigmoid(g) * u).astype(BF16)
    return jnp.dot(a, wd, preferred_element_type=F32)


def _post_dense_kernel(fchunks, x_ref, oa_ref, ob_ref, oc_ref, mod_ref, nffn_ref, wout_ref,
                       wg_ref, wu_ref, wd_ref, o_ref):
    x1, h2 = _post_head(x_ref, oa_ref, ob_ref, oc_ref, mod_ref, nffn_ref, wout_ref)
    hb = h2.astype(BF16)
    acc = None
    for (f0, f1) in fchunks:
        part = _swiglu(hb, wg_ref[:, f0:f1], wu_ref[:, f0:f1], wd_ref[f0:f1, :])
        acc = part if acc is None else acc + part
    o_ref[...] = x1 + mod_ref[0, 5:6, :] * acc


def _post_dense(x, oa, ob, oc, mod, lw, *, tm, tiles_per_mod):
    N, D = x.shape
    FF = lw["wg"].shape[1]
    cut = (FF // 2 // MXU_DIM + 1) * MXU_DIM if FF > 2 * MXU_DIM else FF
    fchunks = ((0, cut), (cut, FF)) if cut < FF else ((0, FF),)
    row = lambda w: pl.BlockSpec((tm, w), lambda i: (i, 0))
    return pl.pallas_call(
        functools.partial(_post_dense_kernel, fchunks),
        grid=(N // tm,),
        in_specs=[row(D), row(MIX_A), row(W_B), row(MIX_C),
                  pl.BlockSpec((1, 6, D), lambda i: (i // tiles_per_mod, 0, 0)),
                  _const_spec((1, D)), _const_spec(lw["w_out"].shape), _const_spec((D, FF)),
                  _const_spec((D, FF)), _const_spec((FF, D))],
        out_specs=row(D),
        out_shape=jax.ShapeDtypeStruct((N, D), F32),
        compiler_params=_cparams(("arbitrary",)),
        name="post_dense",
    )(x, oa, ob, oc, mod, lw["norm_ffn"], lw["w_out"], lw["wg"], lw["wu"], lw["wd"])


TOK_SUB = 8
MOE_ROWS = 256


def _to_token_tiles(x, ref):
    tm = x.shape[0]
    for a in range(TOK_SUB):
        ref[pl.ds(a, tm, stride=TOK_SUB), :] = x[:, a * LANES:(a + 1) * LANES]


def _from_token_tiles(ref, tm):
    return jnp.concatenate([ref[pl.ds(a, tm, stride=TOK_SUB), :] for a in range(TOK_SUB)], axis=-1)


def _router_kernel(x_ref, oa_ref, ob_ref, oc_ref, mod_ref, nffn_ref, wout_ref, rw_ref,
                   x1_ref, h2t_ref, route_ref):
    x1, h2 = _post_head(x_ref, oa_ref, ob_ref, oc_ref, mod_ref, nffn_ref, wout_ref)
    x1_ref[...] = x1
    _to_token_tiles(h2, h2t_ref)
    rw = rw_ref[...]
    rhi = rw.astype(BF16)
    rlo = (rw - rhi.astype(F32)).astype(BF16)
    hhi = h2.astype(BF16)
    hlo = (h2 - hhi.astype(F32)).astype(BF16)
    logits = (jnp.dot(hhi, rhi, preferred_element_type=F32)
              + jnp.dot(hlo, rhi, preferred_element_type=F32)
              + jnp.dot(hhi, rlo, preferred_element_type=F32))
    lane = lax.broadcasted_iota(jnp.int32, logits.shape, 1)
    neg = -jnp.inf
    lg = jnp.where(lane < N_EXPERTS, logits, neg)
    v1 = jnp.max(lg, axis=-1, keepdims=True)
    i1 = jnp.min(jnp.where(lg == v1, lane, LANES), axis=-1, keepdims=True)
    lg2 = jnp.where(lane == i1, neg, lg)
    v2 = jnp.max(lg2, axis=-1, keepdims=True)
    i2 = jnp.min(jnp.where(lg2 == v2, lane, LANES), axis=-1, keepdims=True)
    e2 = jnp.exp(v2 - v1)
    g1 = 1.0 / (1.0 + e2)
    g2 = e2 / (1.0 + e2)
    route_ref[...] = jnp.where(lane == 0, g1, jnp.where(lane == 1, g2, jnp.where(
        lane == 2, i1.astype(F32), jnp.where(lane == 3, i2.astype(F32), 0.0))))


def _router(x, oa, ob, oc, mod, lw, *, tm, tiles_per_mod):
    N, D = x.shape
    assert D == TOK_SUB * LANES
    row = lambda w: pl.BlockSpec((tm, w), lambda i: (i, 0))
    return pl.pallas_call(
        _router_kernel,
        grid=(N // tm,),
        in_specs=[row(D), row(MIX_A), row(W_B), row(MIX_C),
                  pl.BlockSpec((1, 6, D), lambda i: (i // tiles_per_mod, 0, 0)),
                  _const_spec((1, D)), _const_spec(lw["w_out"].shape), _const_spec((D, LANES))],
        out_specs=[row(D), pl.BlockSpec((tm * TOK_SUB, LANES), lambda i: (i, 0)), row(LANES)],
        out_shape=[jax.ShapeDtypeStruct((N, D), F32), jax.ShapeDtypeStruct((N * TOK_SUB, LANES), F32),
                   jax.ShapeDtypeStruct((N, LANES), F32)],
        compiler_params=_cparams(("arbitrary",)),
        name="moe_router",
    )(x, oa, ob, oc, mod, lw["norm_ffn"], lw["w_out"], lw["router"])


def _moe_plan(route, N):
    R, E = MOE_ROWS, N_EXPERTS
    es = route[:, 2:4].astype(jnp.int32).reshape(-1)
    oh = (es[:, None] == jnp.arange(E, dtype=jnp.int32)[None, :]).astype(jnp.int32)
    rank = jnp.sum((jnp.cumsum(oh, axis=0) - oh) * oh, axis=1)
    cnt = jnp.sum(oh, axis=0)
    gsz = ((cnt + R - 1) // R) * R
    gend = jnp.cumsum(gsz)
    pos = (gend - gsz)[es] + rank
    m_rows = 2 * N + E * R
    slot = jnp.arange(2 * N, dtype=jnp.int32)
    src_tok = jnp.zeros((m_rows,), jnp.int32).at[pos].set(slot // 2)
    used = jnp.zeros((m_rows,), jnp.int32).at[pos].set(1)
    spare = 2 * N + jnp.cumsum(1 - used) - 1
    dst_row = jnp.where(used == 1, jnp.zeros((m_rows,), jnp.int32).at[pos].set((slot % 2) * N + slot // 2),
                        spare).astype(jnp.int32)
    tile_start = jnp.arange(m_rows // R, dtype=jnp.int32) * R
    tile_expert = jnp.minimum(jnp.sum(tile_start[:, None] >= gend[None, :], axis=1), E - 1).astype(jnp.int32)
    return tile_expert, src_tok, dst_row


def _experts_kernel(te_ref, src_ref, dst_ref, h2t_hbm, wg_ref, wu_ref, wd_ref, y_hbm,
                    xbuf, ybuf, gsem, ssem):
    R = MOE_ROWS
    i = pl.program_id(0)
    n = pl.num_programs(0)
    slot = i % 2
    tile_rows = R * TOK_SUB

    def gather_copy(tile, buf_slot, r):
        tok = src_ref[tile * R + r]
        return pltpu.make_async_copy(h2t_hbm.at[pl.ds(tok * TOK_SUB, TOK_SUB), :],
                                     xbuf.at[buf_slot, pl.ds(r * TOK_SUB, TOK_SUB), :], gsem.at[buf_slot])

    def scatter_copy(tile, buf_slot, r):
        row = dst_ref[tile * R + r]
        return pltpu.make_async_copy(ybuf.at[buf_slot, pl.ds(r * TOK_SUB, TOK_SUB), :],
                                     y_hbm.at[pl.ds(row * TOK_SUB, TOK_SUB), :], ssem.at[buf_slot])

    def wait_gather(buf_slot):
        pltpu.make_async_copy(h2t_hbm.at[pl.ds(0, tile_rows), :], xbuf.at[buf_slot], gsem.at[buf_slot]).wait()

    def wait_scatter(buf_slot):
        pltpu.make_async_copy(ybuf.at[buf_slot], y_hbm.at[pl.ds(0, tile_rows), :], ssem.at[buf_slot]).wait()

    @pl.when(i == 0)
    def _():
        for r in range(R):
            gather_copy(0, 0, r).start()

    wait_gather(slot)

    @pl.when(i + 1 < n)
    def _():
        for r in range(R):
            gather_copy(i + 1, 1 - slot, r).start()

    xb = _from_token_tiles(xbuf.at[slot], R).astype(BF16)
    y = _swiglu(xb, wg_ref[0], wu_ref[0], wd_ref[0])

    @pl.when(i >= 2)
    def _():
        wait_scatter(slot)

    _to_token_tiles(y, ybuf.at[slot])
    for r in range(R):
        scatter_copy(i, slot, r).start()

    @pl.when(i == n - 1)
    def _():
        wait_scatter(slot)

        @pl.when(n >= 2)
        def _():
            wait_scatter(1 - slot)


def _experts(h2t, plan, lw, N):
    tile_expert, src_tok, dst_row = plan
    E, D, FF = lw["wg"].shape
    R = MOE_ROWS
    n_tiles = tile_expert.shape[0]
    m_rows = n_tiles * R
    wspec = lambda shape: pl.BlockSpec((1,) + shape, lambda i, te, src, dst: (te[i], 0, 0))
    grid_spec = pltpu.PrefetchScalarGridSpec(
        num_scalar_prefetch=3,
        grid=(n_tiles,),
        in_specs=[pl.BlockSpec(memory_space=pl.ANY), wspec((D, FF)), wspec((D, FF)), wspec((FF, D))],
        out_specs=pl.BlockSpec(memory_space=pl.ANY),
        scratch_shapes=[pltpu.VMEM((2, R * TOK_SUB, LANES), F32), pltpu.VMEM((2, R * TOK_SUB, LANES), F32),
                        pltpu.SemaphoreType.DMA((2,)), pltpu.SemaphoreType.DMA((2,))],
    )
    return pl.pallas_call(
        _experts_kernel,
        grid_spec=grid_spec,
        out_shape=jax.ShapeDtypeStruct((m_rows * TOK_SUB, LANES), F32),
        compiler_params=_cparams(("arbitrary",)),
        name="moe_experts",
    )(tile_expert, src_tok, dst_row, h2t, lw["wg"], lw["wu"], lw["wd"])


def _combine_kernel(x1_ref, y0_ref, y1_ref, route_ref, mod_ref, o_ref):
    tm = x1_ref.shape[0]
    g = route_ref[...]
    y = g[:, 0:1] * _from_token_tiles(y0_ref, tm) + g[:, 1:2] * _from_token_tiles(y1_ref, tm)
    o_ref[...] = x1_ref[...] + mod_ref[0, 5:6, :] * y


def _combine(x1, y, route, mod, *, tm, tiles_per_mod):
    N, D = x1.shape
    nt = N // tm
    ytile = lambda k: pl.BlockSpec((tm * TOK_SUB, LANES), lambda i: (k * nt + i, 0))
    row = lambda w: pl.BlockSpec((tm, w), lambda i: (i, 0))
    return pl.pallas_call(
        _combine_kernel,
        grid=(nt,),
        in_specs=[row(D), ytile(0), ytile(1), row(LANES),
                  pl.BlockSpec((1, 6, D), lambda i: (i // tiles_per_mod, 0, 0))],
        out_specs=row(D),
        out_shape=jax.ShapeDtypeStruct((N, D), F32),
        compiler_params=_cparams(("arbitrary",)),
        name="moe_combine",
    )(x1, y, y, route, mod)


def _post_moe(x, oa, ob, oc, mod, lw, *, tm, tiles_per_mod):
    N = x.shape[0]
    x1, h2t, route = _router(x, oa, ob, oc, mod, lw, tm=tm, tiles_per_mod=tiles_per_mod)
    y = _experts(h2t, _moe_plan(route, N), lw, N)
    return _combine(x1, y, route, mod, tm=tm, tiles_per_mod=tiles_per_mod)


def _rope_tables(T):
    t = jnp.arange(T, dtype=jnp.int32)
    rows = (t // GRID_W).astype(F32)
    cols = (t % GRID_W).astype(F32)
    n = ROPE_C // 4
    inv = jnp.power(ROPE_THETA, -jnp.arange(n, dtype=F32) / n)
    j = jnp.arange(ROPE_C)
    pos = jnp.where(j[None, :] < ROPE_C // 2, rows[:, None], cols[:, None])
    ang = pos * inv[j % n][None, :]
    cos = jnp.cos(ang)
    sin = jnp.sin(ang)
    first = ((j % (2 * n)) < n)[None, :]
    s_up = jnp.where(first, -sin, 0.0)
    s_dn = jnp.where(first, 0.0, sin)
    tab32 = jnp.stack([cos, s_up, s_dn])
    tab_a = jnp.tile(tab32, (1, 1, LANES // ROPE_C))
    ident = jnp.stack([jnp.ones((T, NOPE_C), F32), jnp.zeros((T, NOPE_C), F32), jnp.zeros((T, NOPE_C), F32)])
    tail = jnp.stack([jnp.ones((T, HEAD_PAD - QK_C), F32), jnp.zeros((T, HEAD_PAD - QK_C), F32),
                      jnp.zeros((T, HEAD_PAD - QK_C), F32)])
    tab_c = jnp.concatenate([ident, tab32, tail], axis=-1)
    return tab_a, tab_c


def _pad_heads(w, width):
    lead = w.shape[:-1]
    w = w.reshape(lead + (H_C, width))
    w = jnp.pad(w, [(0, 0)] * len(lead) + [(0, 0), (0, HEAD_PAD - width)])
    return w.reshape(lead + (C_W,))


def _layer_weights(l, p):
    w_in = p["w_in"][l]
    D = w_in.shape[0]
    o_cq = 3 * A_W + 3 * W_B
    o_ckv = o_cq + Q_RANK
    o_ckr = o_ckv + KV_RANK
    w_ext = jnp.concatenate(
        [w_in[:, :o_cq], w_in[:, o_ckv:o_ckr], w_in[:, o_cq:o_ckv], w_in[:, o_ckr:],
         jnp.zeros((D, IN_EXT - OFF_CKR - ROPE_C), F32)], axis=1).astype(BF16)
    g = jnp.arange(MXU_DIM) // DA
    lw = {
        "w_in": w_ext,
        "b32": (g[:, None] == g[None, :]).astype(BF16),
        "norm_mix": p["norm_mix"][l][None, :],
        "norm_ffn": p["norm_ffn"][l][None, :],
        "gqk": jnp.concatenate([jnp.tile(p["qnorm_a"][l], 2 * H_A), jnp.tile(p["knorm_a"][l], 2 * H_A)])[None, :],
        "gqa": p["norm_qa"][l][None, :],
        "gkva": p["norm_kva"][l][None, :],
        "gqc": jnp.pad(p["qnorm_c"][l], (0, HEAD_PAD - QK_C))[None, :],
        "gkc": jnp.pad(p["knorm_c"][l], (0, HEAD_PAD - QK_C))[None, :],
        "w_qb": _pad_heads(p["w_qb"][l], QK_C).astype(BF16),
        "w_kb": _pad_heads(p["w_kb"][l], NOPE_C).astype(BF16),
        "w_vb": p["w_vb"][l].astype(BF16),
        "conv_w": p["conv_w"][l],
        "conv_b": p["conv_b"][l][None, :],
        "w_out": p["w_out"][l].astype(BF16),
        "lam": jnp.stack([p["lambda_q1"][l], p["lambda_k1"][l], p["lambda_q2"][l], p["lambda_k2"][l]]),
        "subln": p["subln_a"][l][:, None],
    }
    i = l // 2
    if l % 2 == 0:
        lw["wg"] = p["ffn_w_gate"][i].astype(BF16)
        lw["wu"] = p["ffn_w_up"][i].astype(BF16)
        lw["wd"] = p["ffn_w_down"][i].astype(BF16)
    else:
        lw["router"] = jnp.pad(p["router_w"][i], ((0, 0), (0, LANES - N_EXPERTS)))
        lw["wg"] = p["moe_w_gate"][i].astype(BF16)
        lw["wu"] = p["moe_w_up"][i].astype(BF16)
        lw["wd"] = p["moe_w_down"][i].astype(BF16)
    return lw


def _cached_vT(v, B, P):
    vT = v.transpose(0, 2, 3, 1).astype(BF16)
    ones = jnp.ones((B, H_A, 1, P), BF16)
    zeros = jnp.zeros((B, H_A, V_ROWS - V_A - 1, P), BF16)
    return jnp.concatenate([vT, ones, zeros], axis=2)


def _mixer(x, mod, lw, l_init, *, B, T, ctx, rope):
    is_ctx = ctx is None
    tm = T if is_ctx else min(512, T)
    seg = T if is_ctx else GRID_W
    nsub = 1 if is_ctx else max(1, tm // 256)
    outs = _pre_mix(x, mod, lw, rope, B=B, T=T, is_ctx=is_ctx, tm=tm, seg=seg, nsub=nsub)
    qaT, ka, vaT, ob, qcT, kc, vcT = outs[:7]
    tq = min(256, T)
    tq_c = min(2 * tq, T)
    if is_ctx:
        tk = 512 if T % 512 == 0 else 256
        o_a = _attention(qaT, ka, vaT, diff=True, tq=tq, tk=tk, lam_vecs=lw["lam"], subln=lw["subln"],
                         l_init=l_init)
        o_c = _attention(qcT, kc, vcT, diff=False, tq=tq_c, tk=tk)
    else:
        ctx_k, ctx_v, ctx_ckv, ctx_kr = ctx
        P = ctx_k.shape[1]
        tk = 512 if (P % 512 == 0 and T % 512 == 0) else 256
        kr_placed = jnp.pad(ctx_kr.reshape(B * P, ROPE_C), ((0, 0), (NOPE_C, HEAD_PAD - QK_C)))
        kc_ctx, vc_ctx = _cache_kv(ctx_ckv.reshape(B * P, KV_RANK), kr_placed, lw, B, P)
        ka_ctx = ctx_k.reshape(B * P, H_A, 2 * DA).transpose(1, 0, 2).astype(BF16)
        o_a = _attention(qaT, ka, vaT, ka_ctx, _cached_vT(ctx_v, B, P), diff=True, tq=tq, tk=tk,
                         lam_vecs=lw["lam"], subln=lw["subln"], l_init=l_init)
        o_c = _attention(qcT, kc, vcT, kc_ctx, vc_ctx, diff=False, tq=tq_c, tk=tk)
    return (o_a, ob, o_c), outs[7:]


def kernel(x_prompt, x_sample, cache_diff_k, cache_diff_v, cache_mla_ckv, cache_mla_krope, c, c_ctx, w_ada, b_ada, norm_mix, norm_ffn, w_in, qnorm_a, knorm_a, lambda_q1, lambda_k1, lambda_q2, lambda_k2, subln_a, conv_w, conv_b, norm_qa, w_qb, norm_kva, w_kb, w_vb, qnorm_c, knorm_c, w_out, ffn_w_gate, ffn_w_up, ffn_w_down, router_w, moe_w_gate, moe_w_up, moe_w_down):
    p = dict(norm_mix=norm_mix, norm_ffn=norm_ffn, w_in=w_in, qnorm_a=qnorm_a, knorm_a=knorm_a,
             lambda_q1=lambda_q1, lambda_k1=lambda_k1, lambda_q2=lambda_q2, lambda_k2=lambda_k2,
             subln_a=subln_a, conv_w=conv_w, conv_b=conv_b, norm_qa=norm_qa, w_qb=w_qb,
             norm_kva=norm_kva, w_kb=w_kb, w_vb=w_vb, qnorm_c=qnorm_c, knorm_c=knorm_c, w_out=w_out,
             ffn_w_gate=ffn_w_gate, ffn_w_up=ffn_w_up, ffn_w_down=ffn_w_down, router_w=router_w,
             moe_w_gate=moe_w_gate, moe_w_up=moe_w_up, moe_w_down=moe_w_down)
    Bc, Sc, D = x_prompt.shape
    Bl, T, _ = x_sample.shape
    L = w_in.shape[0]
    nrow = 16
    cvec = jnp.concatenate([c, c_ctx[None, :], jnp.zeros((nrow - Bl - 1, D), F32)], axis=0)
    mod = _ada(cvec, w_ada, b_ada)
    rope = _rope_tables(T)
    xp = x_prompt.reshape(Bc * Sc, D)
    xs = x_sample.reshape(Bl * T, D)
    st = [[], [], [], []]
    for l in range(L):
        lw = _layer_weights(l, p)
        l_init = 0.8 - 0.6 * math.exp(-0.3 * l)
        mod_lat = mod[l, :Bl].reshape(Bl, 6, D)
        mod_ctx = mod[l, Bl:Bl + 1].reshape(1, 6, D)
        post = _post_dense if l % 2 == 0 else _post_moe
        heads, state = _mixer(xp, mod_ctx, lw, l_init, B=Bc, T=Sc, ctx=None, rope=None)
        for acc, s in zip(st, state):
            acc.append(s)
        xp = post(xp, *heads, mod_ctx, lw, tm=min(512, Bc * Sc), tiles_per_mod=Bc * Sc)
        ctx = (cache_diff_k[:, l], cache_diff_v[:, l], cache_mla_ckv[:, l], cache_mla_krope[:, l])
        heads, _ = _mixer(xs, mod_lat, lw, l_init, B=Bl, T=T, ctx=ctx, rope=rope)
        tm = min(512, T)
        xs = post(xs, *heads, mod_lat, lw, tm=tm, tiles_per_mod=T // tm)
    new_k = jnp.stack(st[0], axis=1).reshape(Bc, Sc, L, H_A, 2, DA).transpose(0, 2, 1, 3, 4, 5)
    new_v = jnp.stack(st[1], axis=1).reshape(Bc, Sc, L, H_A, V_A).transpose(0, 2, 1, 3, 4)
    new_ckv = jnp.stack(st[2], axis=1).reshape(Bc, Sc, L, KV_RANK).transpose(0, 2, 1, 3)
    new_kr = jnp.stack(st[3], axis=1).reshape(Bc, Sc, L, ROPE_C).transpose(0, 2, 1, 3)
    return (xp.reshape(Bc, Sc, D), xs.reshape(Bl, T, D), new_k, new_v, new_ckv, new_kr)
```

```python
import functools
import math

import jax
import jax.numpy as jnp
from jax import lax
from jax.experimental import pallas as pl
from jax.experimental.pallas import tpu as pltpu

F32 = jnp.float32
BF16 = jnp.bfloat16

GRID_W = 64
H_A = 6
DA = 32
V_A = 2 * DA
W_B = 256
H_C = 6
NOPE_C = 64
ROPE_C = 32
QK_C = NOPE_C + ROPE_C
V_C = 64
Q_RANK = 192
KV_RANK = 128
N_EXPERTS = 8
ROPE_THETA = 10000.0
EPS = 1e-6
LOG2E = 1.4426950408889634

LANES = 128
MXU_DIM = 256
HEAD_PAD = 128

A_W = H_A * 2 * DA
OFF_AQ = 0
OFF_AK = A_W
OFF_AV = 2 * A_W
OFF_BB = 3 * A_W
OFF_BC = OFF_BB + W_B
OFF_BX = OFF_BC + W_B
OFF_CKV = OFF_BX + W_B
OFF_CQ = OFF_CKV + KV_RANK
OFF_CKR = OFF_CQ + Q_RANK
KR_CHUNK = (OFF_CKR // LANES) * LANES
assert OFF_CKR - KR_CHUNK == NOPE_C
IN_EXT = KR_CHUNK + LANES
C_W = H_C * HEAD_PAD
MIX_A = H_A * V_A
MIX_C = H_C * V_C

ROW_GROUP = 32
V_ROWS = V_A + 16
assert V_A == V_C

VMEM_LIMIT = 56 * 1024 * 1024


def _cparams(sem):
    return pltpu.CompilerParams(dimension_semantics=sem, vmem_limit_bytes=VMEM_LIMIT)


def _const_spec(shape):
    nd = len(shape)
    return pl.BlockSpec(shape, lambda *_: (0,) * nd, pipeline_mode=pl.Buffered(1))


def _rms(x, axis=-1):
    return x * lax.rsqrt(jnp.mean(x * x, axis=axis, keepdims=True) + EPS)


def _split_dot(x, w):
    hi = x.astype(BF16)
    lo = (x - hi.astype(F32)).astype(BF16)
    return (jnp.dot(hi, w, preferred_element_type=F32)
            + jnp.dot(lo, w, preferred_element_type=F32))


def _ada_kernel(c_ref, w_ref, b_ref, o_ref):
    c = c_ref[...]
    s = (c * jax.nn.sigmoid(c)).astype(BF16)
    o_ref[0] = jnp.dot(s, w_ref[0].astype(BF16), preferred_element_type=F32) + b_ref[0]


def _ada(cvec, w_ada, b_ada):
    L, D, N6 = w_ada.shape
    R = cvec.shape[0]
    tn = 1536
    assert N6 % tn == 0
    return pl.pallas_call(
        _ada_kernel,
        grid=(L, N6 // tn),
        in_specs=[
            pl.BlockSpec((R, D), lambda l, j: (0, 0)),
            pl.BlockSpec((1, D, tn), lambda l, j: (l, 0, j)),
            pl.BlockSpec((1, 1, tn), lambda l, j: (l, 0, j)),
        ],
        out_specs=pl.BlockSpec((1, R, tn), lambda l, j: (l, 0, j)),
        out_shape=jax.ShapeDtypeStruct((L, R, N6), F32),
        compiler_params=_cparams(("arbitrary", "arbitrary")),
        name="ada_mod",
    )(cvec, w_ada, b_ada.reshape(L, 1, N6))


def _rope(x, tab_ref):
    up = pltpu.roll(x, LANES - 8, 1)
    dn = pltpu.roll(x, 8, 1)
    return x * tab_ref[0] + up * tab_ref[1] + dn * tab_ref[2]


def _store_vT(v, ref):
    tm = v.shape[0]
    vT = v.T.astype(BF16)
    row = lax.broadcasted_iota(jnp.int32, (V_ROWS - V_A, tm), 0)
    tail = jnp.where(row == 0, 1.0, 0.0).astype(BF16)
    for h in range(v.shape[1] // V_A):
        ref[0, h, 0:V_A, :] = vT[h * V_A:(h + 1) * V_A, :]
        ref[0, h, V_A:V_ROWS, :] = tail


def _mla_kv(ckv_n, kr_placed, wkb_ref, wvb_ref, gkc_ref, rope_ref, kc_ref, vcT_ref):
    cb = ckv_n.astype(BF16)
    kn = jnp.dot(cb, wkb_ref[...], preferred_element_type=F32)
    vc = jnp.dot(cb, wvb_ref[...], preferred_element_type=F32)
    for h in range(H_C):
        kp = kn[:, h * HEAD_PAD:(h + 1) * HEAD_PAD] + kr_placed
        ss = jnp.sum(kp * kp, axis=-1, keepdims=True) * (1.0 / QK_C)
        kp = kp * lax.rsqrt(ss + EPS) * gkc_ref[...]
        if rope_ref is not None:
            kp = _rope(kp, rope_ref)
        kc_ref[h] = kp.astype(BF16)
    _store_vT(vc, vcT_ref)


def _pre_mix_kernel(is_ctx, seg, nsub, *refs):
    refs = list(refs)
    sub = refs[0].shape[0] // nsub
    row_dim = {0: 0}
    n_in = 15 if is_ctx else 17
    if not is_ctx:
        row_dim.update({15: 1, 16: 1})
    for k, dim in enumerate((1, 1, 3, 0, 1, 1, 3) + ((0, 0, 0, 0) if is_ctx else ())):
        row_dim[n_in + k] = dim
    for r in range(nsub):
        views = []
        for pos, ref in enumerate(refs):
            if pos in row_dim:
                idx = [slice(None)] * len(ref.shape)
                idx[row_dim[pos]] = pl.ds(r * sub, sub)
                ref = ref.at[tuple(idx)]
            views.append(ref)
        _pre_mix_rows(is_ctx, seg, *views)


def _pre_mix_rows(is_ctx, seg, *refs):
    (x_ref, mod_ref, nmix_ref, win_ref, b32_ref, gqk_ref, gqa_ref, gkva_ref, gqc_ref,
     gkc_ref, wqb_ref, wkb_ref, wvb_ref, cw_ref, cb_ref) = refs[:15]
    refs = refs[15:]
    if is_ctx:
        ra_ref = rc_ref = None
    else:
        ra_ref, rc_ref = refs[:2]
        refs = refs[2:]
    qaT_ref, ka_ref, vaT_ref, ob_ref, qcT_ref, kc_ref, vcT_ref = refs[:7]
    refs = refs[7:]

    x = x_ref[...]
    shift1 = mod_ref[0, 0:1, :]
    scale1 = mod_ref[0, 1:2, :]
    h = _rms(x) * nmix_ref[...] * (1.0 + scale1) + shift1
    proj = jnp.dot(h.astype(BF16), win_ref[...], preferred_element_type=F32)

    aqk = proj[:, OFF_AQ:OFF_AV]
    sq = aqk * aqk
    ss = jnp.concatenate(
        [_split_dot(sq[:, c:c + MXU_DIM], b32_ref[...]) for c in range(0, 2 * A_W, MXU_DIM)], axis=-1)
    aqk = aqk * lax.rsqrt(ss * (1.0 / DA) + EPS) * gqk_ref[...]
    if is_ctx:
        ka_st_ref, va_st_ref, ckv_st_ref, ckr_st_ref = refs
        ka_st_ref[...] = aqk[:, A_W:]
    else:
        aqk = jnp.concatenate(
            [_rope(aqk[:, c:c + LANES], ra_ref) for c in range(0, 2 * A_W, LANES)], axis=-1)
    qaT_ref[...] = (aqk[:, :A_W] * (DA ** -0.5 * LOG2E)).T.astype(BF16)
    for hh in range(H_A):
        ka_ref[hh] = aqk[:, A_W + hh * 2 * DA:A_W + (hh + 1) * 2 * DA].astype(BF16)
    av = proj[:, OFF_AV:OFF_BB]
    _store_vT(av, vaT_ref)

    bb = proj[:, OFF_BB:OFF_BC]
    u = proj[:, OFF_BC:OFF_BX] * proj[:, OFF_BX:OFF_CKV]
    tm = u.shape[0]
    row = lax.broadcasted_iota(jnp.int32, u.shape, 0) & (seg - 1)
    prev = jnp.where(row == 0, 0.0, pltpu.roll(u, 1, 0))
    nxt = jnp.where(row == seg - 1, 0.0, pltpu.roll(u, tm - 1, 0))
    conv = prev * cw_ref[0:1, :] + u * cw_ref[1:2, :] + nxt * cw_ref[2:3, :] + cb_ref[...]
    ob_ref[...] = (bb * conv).astype(BF16)

    cq = proj[:, OFF_CQ:OFF_CKR]
    cqn = (_rms(cq) * gqa_ref[...]).astype(BF16)
    qc = jnp.dot(cqn, wqb_ref[...], preferred_element_type=F32)
    qhs = []
    for hh in range(H_C):
        qh = qc[:, hh * HEAD_PAD:(hh + 1) * HEAD_PAD]
        s2 = jnp.sum(qh * qh, axis=-1, keepdims=True) * (1.0 / QK_C)
        qh = qh * lax.rsqrt(s2 + EPS) * gqc_ref[...]
        if not is_ctx:
            qh = _rope(qh, rc_ref)
        qhs.append(qh * (QK_C ** -0.5 * LOG2E))
    qcT_ref[...] = jnp.concatenate(qhs, axis=-1).T.astype(BF16)
    ckv_n = _rms(proj[:, OFF_CKV:OFF_CQ]) * gkva_ref[...]
    krc = proj[:, KR_CHUNK:KR_CHUNK + LANES]
    lane = lax.broadcasted_iota(jnp.int32, krc.shape, 1)
    kr_placed = jnp.where((lane >= NOPE_C) & (lane < QK_C), krc, 0.0)
    _mla_kv(ckv_n, kr_placed, wkb_ref, wvb_ref, gkc_ref, rc_ref, kc_ref, vcT_ref)
    if is_ctx:
        va_st_ref[...] = av
        ckv_st_ref[...] = ckv_n
        ckr_st_ref[...] = krc[:, NOPE_C:QK_C]


def _pre_mix(x, mod, lw, rope, *, B, T, is_ctx, tm, seg, nsub):
    N, D = x.shape
    assert (tm // nsub) % seg == 0 and (tm // nsub) % LANES == 0
    nt = N // tm
    tps = T // tm
    row_spec = lambda w: pl.BlockSpec((tm, w), lambda i: (i, 0))
    in_specs = [
        row_spec(D),
        pl.BlockSpec((1, 6, D), (lambda i: (0, 0, 0)) if is_ctx else (lambda i: (i // tps, 0, 0))),
        _const_spec((1, D)),
        _const_spec((D, IN_EXT)),
        _const_spec((MXU_DIM, MXU_DIM)),
        _const_spec((1, 2 * A_W)),
        _const_spec((1, Q_RANK)),
        _const_spec((1, KV_RANK)),
        _const_spec((1, HEAD_PAD)),
        _const_spec((1, HEAD_PAD)),
        _const_spec((Q_RANK, C_W)),
        _const_spec((KV_RANK, C_W)),
        _const_spec((KV_RANK, MIX_C)),
        _const_spec((3, W_B)),
        _const_spec((1, W_B)),
    ]
    args = [x, mod, lw["norm_mix"], lw["w_in"], lw["b32"], lw["gqk"], lw["gqa"], lw["gkva"],
            lw["gqc"], lw["gkc"], lw["w_qb"], lw["w_kb"], lw["w_vb"], lw["conv_w"], lw["conv_b"]]
    if not is_ctx:
        rspec = pl.BlockSpec((3, tm, LANES), lambda i: (0, i % tps, 0))
        in_specs += [rspec, rspec]
        args += list(rope)
    colT = lambda w: pl.BlockSpec((w, tm), lambda i: (0, i))
    vT_spec = pl.BlockSpec((1, H_A, V_ROWS, tm), lambda i: (i // tps, 0, 0, i % tps))
    out_specs = [colT(A_W), pl.BlockSpec((H_A, tm, 2 * DA), lambda i: (0, i, 0)), vT_spec, row_spec(W_B),
                 colT(C_W), pl.BlockSpec((H_C, tm, HEAD_PAD), lambda i: (0, i, 0)), vT_spec]
    out_shape = [jax.ShapeDtypeStruct((A_W, N), BF16), jax.ShapeDtypeStruct((H_A, N, 2 * DA), BF16),
                 jax.ShapeDtypeStruct((B, H_A, V_ROWS, T), BF16), jax.ShapeDtypeStruct((N, W_B), BF16),
                 jax.ShapeDtypeStruct((C_W, N), BF16), jax.ShapeDtypeStruct((H_C, N, HEAD_PAD), BF16),
                 jax.ShapeDtypeStruct((B, H_C, V_ROWS, T), BF16)]
    if is_ctx:
        st_w = [A_W, A_W, KV_RANK, ROPE_C]
        out_specs += [row_spec(w) for w in st_w]
        out_shape += [jax.ShapeDtypeStruct((N, w), F32) for w in st_w]
    return pl.pallas_call(
        functools.partial(_pre_mix_kernel, is_ctx, seg, nsub),
        grid=(nt,),
        in_specs=in_specs,
        out_specs=out_specs,
        out_shape=out_shape,
        compiler_params=_cparams(("arbitrary",)),
        name="pre_mix_ctx" if is_ctx else "pre_mix_lat",
    )(*args)


def _cache_kv_kernel(ckv_ref, kr_ref, wkb_ref, wvb_ref, gkc_ref, kc_ref, vcT_ref):
    _mla_kv(ckv_ref[...], kr_ref[...], wkb_ref, wvb_ref, gkc_ref, None, kc_ref, vcT_ref)


def _cache_kv(ckv, kr_placed, lw, B, P):
    row_spec = lambda w: pl.BlockSpec((P, w), lambda i: (i, 0))
    return pl.pallas_call(
        _cache_kv_kernel,
        grid=(B,),
        in_specs=[row_spec(KV_RANK), row_spec(HEAD_PAD), _const_spec((KV_RANK, C_W)),
                  _const_spec((KV_RANK, MIX_C)), _const_spec((1, HEAD_PAD))],
        out_specs=[pl.BlockSpec((H_C, P, HEAD_PAD), lambda i: (0, i, 0)),
                   pl.BlockSpec((1, H_C, V_ROWS, P), lambda i: (i, 0, 0, 0))],
        out_shape=[jax.ShapeDtypeStruct((H_C, B * P, HEAD_PAD), BF16),
                   jax.ShapeDtypeStruct((B, H_C, V_ROWS, P), BF16)],
        compiler_params=_cparams(("arbitrary",)),
        name="cache_kv",
    )(ckv, kr_placed, lw["w_kb"], lw["w_vb"], lw["gkc"])


def _colmax(s):
    tk, tq = s.shape
    r = jnp.max(s.reshape(tk // ROW_GROUP, ROW_GROUP, tq), axis=0)
    return jnp.max(r, axis=0, keepdims=True)


def _attn_kernel(diff, has_ctx, G, chunks, tk, l_init, *refs):
    refs = list(refs)
    q_ref = refs.pop(0)
    if has_ctx:
        kc_ref, kl_ref, vc_ref, vl_ref = refs[:4]
        refs = refs[4:]
    else:
        kl_ref, vl_ref = refs[:2]
        kc_ref = vc_ref = None
        refs = refs[2:]
    if diff:
        lam_ref, g_ref = refs[:2]
        refs = refs[2:]
    o_ref, sa, sb, os_ref = refs
    tq = q_ref.shape[-1]
    uw = tq if diff else tq // 2
    bufs = (sa, sb)
    nk = len(chunks)
    dq = q_ref.shape[0] // G
    gpt = 1 if nk % 2 == 0 else (2 if G % 2 == 0 else G)
    ntrip = G // gpt

    def q_units(g):
        start = g * dq
        if not isinstance(start, int):
            start = pl.multiple_of(start, dq)
        qh = q_ref[pl.ds(start, dq), :]
        if diff:
            row = lax.broadcasted_iota(jnp.int32, qh.shape, 0)
            zero = jnp.zeros_like(qh)
            return [jnp.where(row < DA, qh, zero), jnp.where(row >= DA, qh, zero)]
        return [qh[:, :uw], qh[:, uw:]]

    def kslice(h, j):
        src, off = chunks[j]
        return (kc_ref if src == 0 else kl_ref)[h, pl.ds(off, tk), :]

    def vslice(h, j):
        src, off = chunks[j]
        return (vc_ref if src == 0 else vl_ref)[0, h, :, pl.ds(off, tk)]

    def qk(g, qs, j, buf):
        kj = kslice(g, j)
        for u in range(2):
            buf[u] = jnp.dot(kj, qs[u], preferred_element_type=F32)

    def softmax(buf, carry):
        ps, out = [], []
        for u in range(2):
            m, acc = carry[u]
            s = buf[u]
            m_new = jnp.maximum(m, _colmax(s))
            alpha = jnp.exp2(m - m_new)
            ps.append(jnp.exp2(s - m_new).astype(BF16))
            out.append((m_new, alpha * acc))
        return ps, out

    def pv(g, j, ps, st):
        vj = vslice(g, j)
        return [(st[u][0], st[u][1] + jnp.dot(vj, ps[u], preferred_element_type=F32)) for u in range(2)]

    def finish(g, res):
        os_ = [acc[:V_A] / acc[V_A:V_A + 1] for (_, acc) in res]
        if diff:
            lv = lam_ref[...]
            lam = (jnp.exp(jnp.sum(lv[0:1] * lv[1:2], axis=-1, keepdims=True))
                   - jnp.exp(jnp.sum(lv[2:3] * lv[3:4], axis=-1, keepdims=True)) + l_init)
            o = os_[0] - lam * os_[1]
            os_ref[g] = _rms(o, axis=0) * g_ref[...] * (1.0 - l_init)
        else:
            os_ref[g, :, 0:uw] = os_[0]
            os_ref[g, :, uw:tq] = os_[1]

    def trip(t, _):
        step = 0
        g = t * gpt
        qs = q_units(g)
        for gi in range(gpt):
            g = t * gpt + gi
            carry = [(jnp.full((1, uw), -jnp.inf, F32), jnp.zeros((V_ROWS, uw), F32)) for _ in range(2)]
            for j in range(nk):
                src, dst = bufs[step % 2], bufs[(step + 1) % 2]
                qs_next = qs
                if j + 1 < nk:
                    qk(g, qs, j + 1, dst)
                elif gi + 1 < gpt:
                    qs_next = q_units(g + 1)
                    qk(g + 1, qs_next, 0, dst)
                elif ntrip > 1:
                    g_next = jnp.minimum(g + 1, G - 1)
                    qk(g_next, q_units(g_next), 0, dst)
                ps, st = softmax(src, carry)
                carry = pv(g, j, ps, st)
                qs = qs_next
                step += 1
            finish(g, carry)
        return 0

    qk(0, q_units(0), 0, sa)
    if ntrip > 1:
        lax.fori_loop(0, ntrip, trip, 0)
    else:
        trip(0, 0)
    for c in range(G // 2):
        pair = jnp.concatenate([os_ref[2 * c], os_ref[2 * c + 1]], axis=0)
        o_ref[:, c * LANES:(c + 1) * LANES] = pair.T.astype(o_ref.dtype)


def _attention(qT, k_lat, v_lat, k_ctx=None, v_ctx=None, *, diff, tq, tk, lam_vecs=None, subln=None,
               l_init=0.0):
    B, H, _, T = v_lat.shape
    N = qT.shape[1]
    d = k_lat.shape[2]
    has_ctx = k_ctx is not None
    chunks = [(1, o) for o in range(0, T, tk)]
    nq = T // tq
    in_specs = [pl.BlockSpec((qT.shape[0], tq), lambda b, i: (0, b * nq + i))]
    args = [qT]
    if has_ctx:
        P = v_ctx.shape[3]
        chunks = [(0, o) for o in range(0, P, tk)] + chunks
        in_specs += [pl.BlockSpec((H, P, d), lambda b, i: (0, b, 0)),
                     pl.BlockSpec((H, T, d), lambda b, i: (0, b, 0)),
                     pl.BlockSpec((1, H, V_ROWS, P), lambda b, i: (b, 0, 0, 0)),
                     pl.BlockSpec((1, H, V_ROWS, T), lambda b, i: (b, 0, 0, 0))]
        args += [k_ctx, k_lat, v_ctx, v_lat]
    else:
        in_specs += [pl.BlockSpec((H, T, d), lambda b, i: (0, b, 0)),
                     pl.BlockSpec((1, H, V_ROWS, T), lambda b, i: (b, 0, 0, 0))]
        args += [k_lat, v_lat]
    if diff:
        in_specs += [pl.BlockSpec((4, DA), lambda b, i: (0, 0)),
                     pl.BlockSpec((V_A, 1), lambda b, i: (0, 0))]
        args += [lam_vecs, subln]
    uw = tq if diff else tq // 2
    return pl.pallas_call(
        functools.partial(_attn_kernel, diff, has_ctx, H, tuple(chunks), tk, l_init),
        grid=(B, nq),
        in_specs=in_specs,
        out_specs=pl.BlockSpec((tq, H * V_A), lambda b, i: (b * nq + i, 0)),
        out_shape=jax.ShapeDtypeStruct((N, H * V_A), BF16),
        scratch_shapes=[pltpu.VMEM((2, tk, uw), F32), pltpu.VMEM((2, tk, uw), F32),
                        pltpu.VMEM((H, V_A, tq), F32)],
        compiler_params=_cparams(("arbitrary", "arbitrary")),
        name="attn_diff" if diff else "attn_mla",
    )(*args)


def _post_head(x_ref, oa_ref, ob_ref, oc_ref, mod_ref, nffn_ref, wout_ref):
    y = (jnp.dot(oa_ref[...], wout_ref[0:MIX_A, :], preferred_element_type=F32)
         + jnp.dot(ob_ref[...], wout_ref[MIX_A:MIX_A + W_B, :], preferred_element_type=F32)
         + jnp.dot(oc_ref[...], wout_ref[MIX_A + W_B:, :], preferred_element_type=F32))
    x1 = x_ref[...] + mod_ref[0, 2:3, :] * y
    h2 = _rms(x1) * nffn_ref[...] * (1.0 + mod_ref[0, 4:5, :]) + mod_ref[0, 3:4, :]
    return x1, h2


def _swiglu(hb, wg, wu, wd):
    g = jnp.dot(hb, wg, preferred_element_type=F32)
    u = jnp.dot(hb, wu, preferred_element_type=F32)
    a = (g * jax.nn.sigmoid(g) * u).astype(BF16)
    return jnp.dot(a, wd, preferred_element_type=F32)


def _post_dense_kernel(fchunks, x_ref, oa_ref, ob_ref, oc_ref, mod_ref, nffn_ref, wout_ref,
                       wg_ref, wu_ref, wd_ref, o_ref):
    x1, h2 = _post_head(x_ref, oa_ref, ob_ref, oc_ref, mod_ref, nffn_ref, wout_ref)
    hb = h2.astype(BF16)
    acc = None
    for (f0, f1) in fchunks:
        part = _swiglu(hb, wg_ref[:, f0:f1], wu_ref[:, f0:f1], wd_ref[f0:f1, :])
        acc = part if acc is None else acc + part
    o_ref[...] = x1 + mod_ref[0, 5:6, :] * acc


def _post_dense(x, oa, ob, oc, mod, lw, *, tm, tiles_per_mod):
    N, D = x.shape
    FF = lw["wg"].shape[1]
    cut = (FF // 2 // MXU_DIM + 1) * MXU_DIM if FF > 2 * MXU_DIM else FF
    fchunks = ((0, cut), (cut, FF)) if cut < FF else ((0, FF),)
    row = lambda w: pl.BlockSpec((tm, w), lambda i: (i, 0))
    return pl.pallas_call(
        functools.partial(_post_dense_kernel, fchunks),
        grid=(N // tm,),
        in_specs=[row(D), row(MIX_A), row(W_B), row(MIX_C),
                  pl.BlockSpec((1, 6, D), lambda i: (i // tiles_per_mod, 0, 0)),
                  _const_spec((1, D)), _const_spec(lw["w_out"].shape), _const_spec((D, FF)),
                  _const_spec((D, FF)), _const_spec((FF, D))],
        out_specs=row(D),
        out_shape=jax.ShapeDtypeStruct((N, D), F32),
        compiler_params=_cparams(("arbitrary",)),
        name="post_dense",
    )(x, oa, ob, oc, mod, lw["norm_ffn"], lw["w_out"], lw["wg"], lw["wu"], lw["wd"])


TOK_SUB = 8
MOE_ROWS = 256
DISPATCH_TOKENS = 128
COMBINE_TOKENS = 256


def _to_token_tiles(x, ref):
    tm = x.shape[0]
    for a in range(TOK_SUB):
        ref[pl.ds(a, tm, stride=TOK_SUB), :] = x[:, a * LANES:(a + 1) * LANES]


def _from_token_tiles(ref, tm):
    return jnp.concatenate([ref[pl.ds(a, tm, stride=TOK_SUB), :] for a in range(TOK_SUB)], axis=-1)


def _router_kernel(x_ref, oa_ref, ob_ref, oc_ref, mod_ref, nffn_ref, wout_ref, rw_ref,
                   x1_ref, h2t_ref, route_ref):
    x1, h2 = _post_head(x_ref, oa_ref, ob_ref, oc_ref, mod_ref, nffn_ref, wout_ref)
    x1_ref[...] = x1
    _to_token_tiles(h2, h2t_ref)
    rw = rw_ref[...]
    rhi = rw.astype(BF16)
    rlo = (rw - rhi.astype(F32)).astype(BF16)
    hhi = h2.astype(BF16)
    hlo = (h2 - hhi.astype(F32)).astype(BF16)
    logits = (jnp.dot(hhi, rhi, preferred_element_type=F32)
              + jnp.dot(hlo, rhi, preferred_element_type=F32)
              + jnp.dot(hhi, rlo, preferred_element_type=F32))
    lane = lax.broadcasted_iota(jnp.int32, logits.shape, 1)
    neg = -jnp.inf
    lg = jnp.where(lane < N_EXPERTS, logits, neg)
    v1 = jnp.max(lg, axis=-1, keepdims=True)
    i1 = jnp.min(jnp.where(lg == v1, lane, LANES), axis=-1, keepdims=True)
    lg2 = jnp.where(lane == i1, neg, lg)
    v2 = jnp.max(lg2, axis=-1, keepdims=True)
    i2 = jnp.min(jnp.where(lg2 == v2, lane, LANES), axis=-1, keepdims=True)
    e2 = jnp.exp(v2 - v1)
    g1 = 1.0 / (1.0 + e2)
    g2 = e2 / (1.0 + e2)
    route_ref[...] = jnp.where(lane == 0, g1, jnp.where(lane == 1, g2, jnp.where(
        lane == 2, i1.astype(F32), jnp.where(lane == 3, i2.astype(F32), 0.0))))


def _router(x, oa, ob, oc, mod, lw, *, tm, tiles_per_mod):
    N, D = x.shape
    assert D == TOK_SUB * LANES
    row = lambda w: pl.BlockSpec((tm, w), lambda i: (i, 0))
    return pl.pallas_call(
        _router_kernel,
        grid=(N // tm,),
        in_specs=[row(D), row(MIX_A), row(W_B), row(MIX_C),
                  pl.BlockSpec((1, 6, D), lambda i: (i // tiles_per_mod, 0, 0)),
                  _const_spec((1, D)), _const_spec(lw["w_out"].shape), _const_spec((D, LANES))],
        out_specs=[row(D), pl.BlockSpec((tm * TOK_SUB, LANES), lambda i: (i, 0)), row(LANES)],
        out_shape=[jax.ShapeDtypeStruct((N, D), F32), jax.ShapeDtypeStruct((N * TOK_SUB, LANES), F32),
                   jax.ShapeDtypeStruct((N, LANES), F32)],
        compiler_params=_cparams(("arbitrary",)),
        name="moe_router",
    )(x, oa, ob, oc, mod, lw["norm_ffn"], lw["w_out"], lw["router"])


def _moe_plan(route, N):
    R, E = MOE_ROWS, N_EXPERTS
    es = route[:, 2:4].astype(jnp.int32).reshape(-1)
    oh = (es[:, None] == jnp.arange(E, dtype=jnp.int32)[None, :]).astype(jnp.int32)
    rank = jnp.sum((jnp.cumsum(oh, axis=0) - oh) * oh, axis=1)
    cnt = jnp.sum(oh, axis=0)
    gsz = ((cnt + R - 1) // R) * R
    gend = jnp.cumsum(gsz)
    pos = ((gend - gsz)[es] + rank).astype(jnp.int32)
    n_tiles = (2 * N) // R + E
    tile_start = jnp.arange(n_tiles, dtype=jnp.int32) * R
    tile_expert = jnp.minimum(jnp.sum(tile_start[:, None] >= gend[None, :], axis=1), E - 1).astype(jnp.int32)
    return pos, tile_expert


def _tile_rows(ref, row):
    return ref.at[pl.ds(row * TOK_SUB, TOK_SUB), :]


def _dispatch_kernel(pos_ref, h2t_hbm, xs_in_hbm, xs_hbm, sem):
    del xs_in_hbm
    i = pl.program_id(0)
    n = pl.num_programs(0)
    tmd = DISPATCH_TOKENS

    def step_wait():
        pltpu.make_async_copy(h2t_hbm.at[pl.ds(0, 2 * tmd * TOK_SUB), :],
                              xs_hbm.at[pl.ds(0, 2 * tmd * TOK_SUB), :], sem.at[0]).wait()

    for r in range(tmd):
        t = i * tmd + r
        for k in range(2):
            pltpu.make_async_copy(_tile_rows(h2t_hbm, t), _tile_rows(xs_hbm, pos_ref[2 * t + k]),
                                  sem.at[0]).start()

    @pl.when(i > 0)
    def _():
        step_wait()

    @pl.when(i == n - 1)
    def _():
        step_wait()


def _dispatch(h2t, pos, n_rows):
    N = pos.shape[0] // 2
    xs0 = jnp.zeros((n_rows * TOK_SUB, LANES), F32)
    grid_spec = pltpu.PrefetchScalarGridSpec(
        num_scalar_prefetch=1,
        grid=(N // DISPATCH_TOKENS,),
        in_specs=[pl.BlockSpec(memory_space=pl.ANY), pl.BlockSpec(memory_space=pl.ANY)],
        out_specs=pl.BlockSpec(memory_space=pl.ANY),
        scratch_shapes=[pltpu.SemaphoreType.DMA((1,))],
    )
    return pl.pallas_call(
        _dispatch_kernel,
        grid_spec=grid_spec,
        out_shape=jax.ShapeDtypeStruct(xs0.shape, F32),
        input_output_aliases={2: 0},
        compiler_params=_cparams(("arbitrary",)),
        name="moe_dispatch",
    )(pos, h2t, xs0)


def _experts_kernel(te_ref, x_ref, wg_ref, wu_ref, wd_ref, y_ref):
    xb = _from_token_tiles(x_ref, MOE_ROWS).astype(BF16)
    _to_token_tiles(_swiglu(xb, wg_ref[0], wu_ref[0], wd_ref[0]), y_ref)


def _experts(xs, tile_expert, lw):
    E, D, FF = lw["wg"].shape
    R = MOE_ROWS
    n_tiles = tile_expert.shape[0]
    wspec = lambda shape: pl.BlockSpec((1,) + shape, lambda i, te: (te[i], 0, 0))
    rows = pl.BlockSpec((R * TOK_SUB, LANES), lambda i, te: (i, 0))
    grid_spec = pltpu.PrefetchScalarGridSpec(
        num_scalar_prefetch=1,
        grid=(n_tiles,),
        in_specs=[rows, wspec((D, FF)), wspec((D, FF)), wspec((FF, D))],
        out_specs=rows,
    )
    return pl.pallas_call(
        _experts_kernel,
        grid_spec=grid_spec,
        out_shape=jax.ShapeDtypeStruct(xs.shape, F32),
        compiler_params=_cparams(("arbitrary",)),
        name="moe_experts",
    )(tile_expert, xs, lw["wg"], lw["wu"], lw["wd"])


def _combine_kernel(pos_ref, x1_ref, ys_hbm, route_ref, mod_ref, o_ref, ybuf, sem):
    tm = x1_ref.shape[0]
    i = pl.program_id(0)
    n = pl.num_programs(0)
    slot = i % 2

    def issue(tile, buf_slot):
        for r in range(tm):
            t = tile * tm + r
            for k in range(2):
                pltpu.make_async_copy(_tile_rows(ys_hbm, pos_ref[2 * t + k]),
                                      _tile_rows(ybuf.at[buf_slot, k], r), sem.at[buf_slot]).start()

    @pl.when(i == 0)
    def _():
        issue(0, 0)

    for k in range(2):
        pltpu.make_async_copy(ys_hbm.at[pl.ds(0, tm * TOK_SUB), :], ybuf.at[slot, k], sem.at[slot]).wait()

    @pl.when(i + 1 < n)
    def _():
        issue(i + 1, 1 - slot)

    g = route_ref[...]
    y = (g[:, 0:1] * _from_token_tiles(ybuf.at[slot, 0], tm)
         + g[:, 1:2] * _from_token_tiles(ybuf.at[slot, 1], tm))
    o_ref[...] = x1_ref[...] + mod_ref[0, 5:6, :] * y


def _combine(x1, ys, pos, route, mod, *, tm, tiles_per_mod):
    N, D = x1.shape
    row = lambda w: pl.BlockSpec((tm, w), lambda i, pos: (i, 0))
    grid_spec = pltpu.PrefetchScalarGridSpec(
        num_scalar_prefetch=1,
        grid=(N // tm,),
        in_specs=[row(D), pl.BlockSpec(memory_space=pl.ANY), row(LANES),
                  pl.BlockSpec((1, 6, D), lambda i, pos: (i // tiles_per_mod, 0, 0))],
        out_specs=row(D),
        scratch_shapes=[pltpu.VMEM((2, 2, tm * TOK_SUB, LANES), F32), pltpu.SemaphoreType.DMA((2,))],
    )
    return pl.pallas_call(
        _combine_kernel,
        grid_spec=grid_spec,
        out_shape=jax.ShapeDtypeStruct((N, D), F32),
        compiler_params=_cparams(("arbitrary",)),
        name="moe_combine",
    )(pos, x1, ys, route, mod)


def _post_moe(x, oa, ob, oc, mod, lw, *, tm, tiles_per_mod):
    N = x.shape[0]
    x1, h2t, route = _router(x, oa, ob, oc, mod, lw, tm=tm, tiles_per_mod=tiles_per_mod)
    pos, tile_expert = _moe_plan(route, N)
    xs = _dispatch(h2t, pos, tile_expert.shape[0] * MOE_ROWS)
    ys = _experts(xs, tile_expert, lw)
    tmc = min(COMBINE_TOKENS, N)
    return _combine(x1, ys, pos, route, mod, tm=tmc, tiles_per_mod=tiles_per_mod * (tm // tmc))


def _rope_tables(T):
    t = jnp.arange(T, dtype=jnp.int32)
    rows = (t // GRID_W).astype(F32)
    cols = (t % GRID_W).astype(F32)
    n = ROPE_C // 4
    inv = jnp.power(ROPE_THETA, -jnp.arange(n, dtype=F32) / n)
    j = jnp.arange(ROPE_C)
    pos = jnp.where(j[None, :] < ROPE_C // 2, rows[:, None], cols[:, None])
    ang = pos * inv[j % n][None, :]
    cos = jnp.cos(ang)
    sin = jnp.sin(ang)
    first = ((j % (2 * n)) < n)[None, :]
    s_up = jnp.where(first, -sin, 0.0)
    s_dn = jnp.where(first, 0.0, sin)
    tab32 = jnp.stack([cos, s_up, s_dn])
    tab_a = jnp.tile(tab32, (1, 1, LANES // ROPE_C))
    ident = jnp.stack([jnp.ones((T, NOPE_C), F32), jnp.zeros((T, NOPE_C), F32), jnp.zeros((T, NOPE_C), F32)])
    tail = jnp.stack([jnp.ones((T, HEAD_PAD - QK_C), F32), jnp.zeros((T, HEAD_PAD - QK_C), F32),
                      jnp.zeros((T, HEAD_PAD - QK_C), F32)])
    tab_c = jnp.concatenate([ident, tab32, tail], axis=-1)
    return tab_a, tab_c


def _pad_heads(w, width):
    lead = w.shape[:-1]
    w = w.reshape(lead + (H_C, width))
    w = jnp.pad(w, [(0, 0)] * len(lead) + [(0, 0), (0, HEAD_PAD - width)])
    return w.reshape(lead + (C_W,))


def _layer_weights(l, p):
    w_in = p["w_in"][l]
    D = w_in.shape[0]
    o_cq = 3 * A_W + 3 * W_B
    o_ckv = o_cq + Q_RANK
    o_ckr = o_ckv + KV_RANK
    w_ext = jnp.concatenate(
        [w_in[:, :o_cq], w_in[:, o_ckv:o_ckr], w_in[:, o_cq:o_ckv], w_in[:, o_ckr:],
         jnp.zeros((D, IN_EXT - OFF_CKR - ROPE_C), F32)], axis=1).astype(BF16)
    g = jnp.arange(MXU_DIM) // DA
    lw = {
        "w_in": w_ext,
        "b32": (g[:, None] == g[None, :]).astype(BF16),
        "norm_mix": p["norm_mix"][l][None, :],
        "norm_ffn": p["norm_ffn"][l][None, :],
        "gqk": jnp.concatenate([jnp.tile(p["qnorm_a"][l], 2 * H_A), jnp.tile(p["knorm_a"][l], 2 * H_A)])[None, :],
        "gqa": p["norm_qa"][l][None, :],
        "gkva": p["norm_kva"][l][None, :],
        "gqc": jnp.pad(p["qnorm_c"][l], (0, HEAD_PAD - QK_C))[None, :],
        "gkc": jnp.pad(p["knorm_c"][l], (0, HEAD_PAD - QK_C))[None, :],
        "w_qb": _pad_heads(p["w_qb"][l], QK_C).astype(BF16),
        "w_kb": _pad_heads(p["w_kb"][l], NOPE_C).astype(BF16),
        "w_vb": p["w_vb"][l].astype(BF16),
        "conv_w": p["conv_w"][l],
        "conv_b": p["conv_b"][l][None, :],
        "w_out": p["w_out"][l].astype(BF16),
        "lam": jnp.stack([p["lambda_q1"][l], p["lambda_k1"][l], p["lambda_q2"][l], p["lambda_k2"][l]]),
        "subln": p["subln_a"][l][:, None],
    }
    i = l // 2
    if l % 2 == 0:
        lw["wg"] = p["ffn_w_gate"][i].astype(BF16)
        lw["wu"] = p["ffn_w_up"][i].astype(BF16)
        lw["wd"] = p["ffn_w_down"][i].astype(BF16)
    else:
        lw["router"] = jnp.pad(p["router_w"][i], ((0, 0), (0, LANES - N_EXPERTS)))
        lw["wg"] = p["moe_w_gate"][i].astype(BF16)
        lw["wu"] = p["moe_w_up"][i].astype(BF16)
        lw["wd"] = p["moe_w_down"][i].astype(BF16)
    return lw


def _cached_vT(v, B, P):
    vT = v.transpose(0, 2, 3, 1).astype(BF16)
    ones = jnp.ones((B, H_A, 1, P), BF16)
    zeros = jnp.zeros((B, H_A, V_ROWS - V_A - 1, P), BF16)
    return jnp.concatenate([vT, ones, zeros], axis=2)


def _mixer(x, mod, lw, l_init, *, B, T, ctx, rope):
    is_ctx = ctx is None
    tm = T if is_ctx else min(512, T)
    seg = T if is_ctx else GRID_W
    nsub = 1 if is_ctx else max(1, tm // 256)
    outs = _pre_mix(x, mod, lw, rope, B=B, T=T, is_ctx=is_ctx, tm=tm, seg=seg, nsub=nsub)
    qaT, ka, vaT, ob, qcT, kc, vcT = outs[:7]
    tq = min(256, T)
    tq_c = min(2 * tq, T)
    if is_ctx:
        tk = 512 if T % 512 == 0 else 256
        o_a = _attention(qaT, ka, vaT, diff=True, tq=tq, tk=tk, lam_vecs=lw["lam"], subln=lw["subln"],
                         l_init=l_init)
        o_c = _attention(qcT, kc, vcT, diff=False, tq=tq_c, tk=tk)
    else:
        ctx_k, ctx_v, ctx_ckv, ctx_kr = ctx
        P = ctx_k.shape[1]
        tk = 512 if (P % 512 == 0 and T % 512 == 0) else 256
        kr_placed = jnp.pad(ctx_kr.reshape(B * P, ROPE_C), ((0, 0), (NOPE_C, HEAD_PAD - QK_C)))
        kc_ctx, vc_ctx = _cache_kv(ctx_ckv.reshape(B * P, KV_RANK), kr_placed, lw, B, P)
        ka_ctx = ctx_k.reshape(B * P, H_A, 2 * DA).transpose(1, 0, 2).astype(BF16)
        o_a = _attention(qaT, ka, vaT, ka_ctx, _cached_vT(ctx_v, B, P), diff=True, tq=tq, tk=tk,
                         lam_vecs=lw["lam"], subln=lw["subln"], l_init=l_init)
        o_c = _attention(qcT, kc, vcT, kc_ctx, vc_ctx, diff=False, tq=tq_c, tk=tk)
    return (o_a, ob, o_c), outs[7:]


def kernel(x_prompt, x_sample, cache_diff_k, cache_diff_v, cache_mla_ckv, cache_mla_krope, c, c_ctx, w_ada, b_ada, norm_mix, norm_ffn, w_in, qnorm_a, knorm_a, lambda_q1, lambda_k1, lambda_q2, lambda_k2, subln_a, conv_w, conv_b, norm_qa, w_qb, norm_kva, w_kb, w_vb, qnorm_c, knorm_c, w_out, ffn_w_gate, ffn_w_up, ffn_w_down, router_w, moe_w_gate, moe_w_up, moe_w_down):
    p = dict(norm_mix=norm_mix, norm_ffn=norm_ffn, w_in=w_in, qnorm_a=qnorm_a, knorm_a=knorm_a,
             lambda_q1=lambda_q1, lambda_k1=lambda_k1, lambda_q2=lambda_q2, lambda_k2=lambda_k2,
             subln_a=subln_a, conv_w=conv_w, conv_b=conv_b, norm_qa=norm_qa, w_qb=w_qb,
             norm_kva=norm_kva, w_kb=w_kb, w_vb=w_vb, qnorm_c=qnorm_c, knorm_c=knorm_c, w_out=w_out,
             ffn_w_gate=ffn_w_gate, ffn_w_up=ffn_w_up, ffn_w_down=ffn_w_down, router_w=router_w,
             moe_w_gate=moe_w_gate, moe_w_up=moe_w_up, moe_w_down=moe_w_down)
    Bc, Sc, D = x_prompt.shape
    Bl, T, _ = x_sample.shape
    L = w_in.shape[0]
    nrow = 16
    cvec = jnp.concatenate([c, c_ctx[None, :], jnp.zeros((nrow - Bl - 1, D), F32)], axis=0)
    mod = _ada(cvec, w_ada, b_ada)
    rope = _rope_tables(T)
    xp = x_prompt.reshape(Bc * Sc, D)
    xs = x_sample.reshape(Bl * T, D)
    st = [[], [], [], []]
    for l in range(L):
        lw = _layer_weights(l, p)
        l_init = 0.8 - 0.6 * math.exp(-0.3 * l)
        mod_lat = mod[l, :Bl].reshape(Bl, 6, D)
        mod_ctx = mod[l, Bl:Bl + 1].reshape(1, 6, D)
        post = _post_dense if l % 2 == 0 else _post_moe
        heads, state = _mixer(xp, mod_ctx, lw, l_init, B=Bc, T=Sc, ctx=None, rope=None)
        for acc, s in zip(st, state):
            acc.append(s)
        xp = post(xp, *heads, mod_ctx, lw, tm=min(512, Bc * Sc), tiles_per_mod=Bc * Sc)
        ctx = (cache_diff_k[:, l], cache_diff_v[:, l], cache_mla_ckv[:, l], cache_mla_krope[:, l])
        heads, _ = _mixer(xs, mod_lat, lw, l_init, B=Bl, T=T, ctx=ctx, rope=rope)
        tm = min(512, T)
        xs = post(xs, *heads, mod_lat, lw, tm=tm, tiles_per_mod=T // tm)
    new_k = jnp.stack(st[0], axis=1).reshape(Bc, Sc, L, H_A, 2, DA).transpose(0, 2, 1, 3, 4, 5)
    new_v = jnp.stack(st[1], axis=1).reshape(Bc, Sc, L, H_A, V_A).transpose(0, 2, 1, 3, 4)
    new_ckv = jnp.stack(st[2], axis=1).reshape(Bc, Sc, L, KV_RANK).transpose(0, 2, 1, 3)
    new_kr = jnp.stack(st[3], axis=1).reshape(Bc, Sc, L, ROPE_C).transpose(0, 2, 1, 3)
    return (xp.reshape(Bc, Sc, D), xs.reshape(Bl, T, D), new_k, new_v, new_ckv, new_kr)
```

```python
import functools
import math

import jax
import jax.numpy as jnp
from jax import lax
from jax.experimental import pallas as pl
from jax.experimental.pallas import tpu as pltpu

F32 = jnp.float32
BF16 = jnp.bfloat16

GRID_W = 64
H_A = 6
DA = 32
V_A = 2 * DA
W_B = 256
H_C = 6
NOPE_C = 64
ROPE_C = 32
QK_C = NOPE_C + ROPE_C
V_C = 64
Q_RANK = 192
KV_RANK = 128
N_EXPERTS = 8
ROPE_THETA = 10000.0
EPS = 1e-6
LOG2E = 1.4426950408889634

LANES = 128
MXU_DIM = 256
HEAD_PAD = 128

A_W = H_A * 2 * DA
OFF_AQ = 0
OFF_AK = A_W
OFF_AV = 2 * A_W
OFF_BB = 3 * A_W
OFF_BC = OFF_BB + W_B
OFF_BX = OFF_BC + W_B
OFF_CKV = OFF_BX + W_B
OFF_CQ = OFF_CKV + KV_RANK
OFF_CKR = OFF_CQ + Q_RANK
KR_CHUNK = (OFF_CKR // LANES) * LANES
assert OFF_CKR - KR_CHUNK == NOPE_C
IN_EXT = KR_CHUNK + LANES
C_W = H_C * HEAD_PAD
MIX_A = H_A * V_A
MIX_C = H_C * V_C

ROW_GROUP = 32
V_ROWS = V_A + 16
assert V_A == V_C

VMEM_LIMIT = 56 * 1024 * 1024


def _cparams(sem):
    return pltpu.CompilerParams(dimension_semantics=sem, vmem_limit_bytes=VMEM_LIMIT)


def _const_spec(shape):
    nd = len(shape)
    return pl.BlockSpec(shape, lambda *_: (0,) * nd, pipeline_mode=pl.Buffered(1))


def _rms(x, axis=-1):
    return x * lax.rsqrt(jnp.mean(x * x, axis=axis, keepdims=True) + EPS)


def _split_dot(x, w):
    hi = x.astype(BF16)
    lo = (x - hi.astype(F32)).astype(BF16)
    return (jnp.dot(hi, w, preferred_element_type=F32)
            + jnp.dot(lo, w, preferred_element_type=F32))


def _ada_kernel(c_ref, w_ref, b_ref, o_ref):
    c = c_ref[...]
    s = (c * jax.nn.sigmoid(c)).astype(BF16)
    o_ref[0] = jnp.dot(s, w_ref[0].astype(BF16), preferred_element_type=F32) + b_ref[0]


def _ada(cvec, w_ada, b_ada):
    L, D, N6 = w_ada.shape
    R = cvec.shape[0]
    tn = 1536
    assert N6 % tn == 0
    return pl.pallas_call(
        _ada_kernel,
        grid=(L, N6 // tn),
        in_specs=[
            pl.BlockSpec((R, D), lambda l, j: (0, 0)),
            pl.BlockSpec((1, D, tn), lambda l, j: (l, 0, j)),
            pl.BlockSpec((1, 1, tn), lambda l, j: (l, 0, j)),
        ],
        out_specs=pl.BlockSpec((1, R, tn), lambda l, j: (l, 0, j)),
        out_shape=jax.ShapeDtypeStruct((L, R, N6), F32),
        compiler_params=_cparams(("arbitrary", "arbitrary")),
        name="ada_mod",
    )(cvec, w_ada, b_ada.reshape(L, 1, N6))


def _rope(x, tab_ref):
    up = pltpu.roll(x, LANES - 8, 1)
    dn = pltpu.roll(x, 8, 1)
    return x * tab_ref[0] + up * tab_ref[1] + dn * tab_ref[2]


def _store_vT(v, ref):
    tm = v.shape[0]
    vT = v.T.astype(BF16)
    row = lax.broadcasted_iota(jnp.int32, (V_ROWS - V_A, tm), 0)
    tail = jnp.where(row == 0, 1.0, 0.0).astype(BF16)
    for h in range(v.shape[1] // V_A):
        ref[0, h, 0:V_A, :] = vT[h * V_A:(h + 1) * V_A, :]
        ref[0, h, V_A:V_ROWS, :] = tail


def _mla_kv(ckv_n, kr_placed, wkb_ref, wvb_ref, gkc_ref, rope_ref, kc_ref, vcT_ref):
    cb = ckv_n.astype(BF16)
    kn = jnp.dot(cb, wkb_ref[...], preferred_element_type=F32)
    vc = jnp.dot(cb, wvb_ref[...], preferred_element_type=F32)
    for h in range(H_C):
        kp = kn[:, h * HEAD_PAD:(h + 1) * HEAD_PAD] + kr_placed
        ss = jnp.sum(kp * kp, axis=-1, keepdims=True) * (1.0 / QK_C)
        kp = kp * lax.rsqrt(ss + EPS) * gkc_ref[...]
        if rope_ref is not None:
            kp = _rope(kp, rope_ref)
        kc_ref[h] = kp.astype(BF16)
    _store_vT(vc, vcT_ref)


def _pre_mix_kernel(is_ctx, seg, nsub, *refs):
    refs = list(refs)
    sub = refs[0].shape[0] // nsub
    row_dim = {0: 0}
    n_in = 15 if is_ctx else 17
    if not is_ctx:
        row_dim.update({15: 1, 16: 1})
    for k, dim in enumerate((1, 1, 3, 0, 1, 1, 3) + ((0, 0, 0, 0) if is_ctx else ())):
        row_dim[n_in + k] = dim
    for r in range(nsub):
        views = []
        for pos, ref in enumerate(refs):
            if pos in row_dim:
                idx = [slice(None)] * len(ref.shape)
                idx[row_dim[pos]] = pl.ds(r * sub, sub)
                ref = ref.at[tuple(idx)]
            views.append(ref)
        _pre_mix_rows(is_ctx, seg, *views)


def _pre_mix_rows(is_ctx, seg, *refs):
    (x_ref, mod_ref, nmix_ref, win_ref, b32_ref, gqk_ref, gqa_ref, gkva_ref, gqc_ref,
     gkc_ref, wqb_ref, wkb_ref, wvb_ref, cw_ref, cb_ref) = refs[:15]
    refs = refs[15:]
    if is_ctx:
        ra_ref = rc_ref = None
    else:
        ra_ref, rc_ref = refs[:2]
        refs = refs[2:]
    qaT_ref, ka_ref, vaT_ref, ob_ref, qcT_ref, kc_ref, vcT_ref = refs[:7]
    refs = refs[7:]

    x = x_ref[...]
    shift1 = mod_ref[0, 0:1, :]
    scale1 = mod_ref[0, 1:2, :]
    h = _rms(x) * nmix_ref[...] * (1.0 + scale1) + shift1
    proj = jnp.dot(h.astype(BF16), win_ref[...], preferred_element_type=F32)

    aqk = proj[:, OFF_AQ:OFF_AV]
    sq = aqk * aqk
    ss = jnp.concatenate(
        [_split_dot(sq[:, c:c + MXU_DIM], b32_ref[...]) for c in range(0, 2 * A_W, MXU_DIM)], axis=-1)
    aqk = aqk * lax.rsqrt(ss * (1.0 / DA) + EPS) * gqk_ref[...]
    if is_ctx:
        ka_st_ref, va_st_ref, ckv_st_ref, ckr_st_ref = refs
        ka_st_ref[...] = aqk[:, A_W:]
    else:
        aqk = jnp.concatenate(
            [_rope(aqk[:, c:c + LANES], ra_ref) for c in range(0, 2 * A_W, LANES)], axis=-1)
    qaT_ref[...] = (aqk[:, :A_W] * (DA ** -0.5 * LOG2E)).T.astype(BF16)
    for hh in range(H_A):
        ka_ref[hh] = aqk[:, A_W + hh * 2 * DA:A_W + (hh + 1) * 2 * DA].astype(BF16)
    av = proj[:, OFF_AV:OFF_BB]
    _store_vT(av, vaT_ref)

    bb = proj[:, OFF_BB:OFF_BC]
    u = proj[:, OFF_BC:OFF_BX] * proj[:, OFF_BX:OFF_CKV]
    tm = u.shape[0]
    row = lax.broadcasted_iota(jnp.int32, u.shape, 0) & (seg - 1)
    prev = jnp.where(row == 0, 0.0, pltpu.roll(u, 1, 0))
    nxt = jnp.where(row == seg - 1, 0.0, pltpu.roll(u, tm - 1, 0))
    conv = prev * cw_ref[0:1, :] + u * cw_ref[1:2, :] + nxt * cw_ref[2:3, :] + cb_ref[...]
    ob_ref[...] = (bb * conv).astype(BF16)

    cq = proj[:, OFF_CQ:OFF_CKR]
    cqn = (_rms(cq) * gqa_ref[...]).astype(BF16)
    qc = jnp.dot(cqn, wqb_ref[...], preferred_element_type=F32)
    qhs = []
    for hh in range(H_C):
        qh = qc[:, hh * HEAD_PAD:(hh + 1) * HEAD_PAD]
        s2 = jnp.sum(qh * qh, axis=-1, keepdims=True) * (1.0 / QK_C)
        qh = qh * lax.rsqrt(s2 + EPS) * gqc_ref[...]
        if not is_ctx:
            qh = _rope(qh, rc_ref)
        qhs.append(qh * (QK_C ** -0.5 * LOG2E))
    qcT_ref[...] = jnp.concatenate(qhs, axis=-1).T.astype(BF16)
    ckv_n = _rms(proj[:, OFF_CKV:OFF_CQ]) * gkva_ref[...]
    krc = proj[:, KR_CHUNK:KR_CHUNK + LANES]
    lane = lax.broadcasted_iota(jnp.int32, krc.shape, 1)
    kr_placed = jnp.where((lane >= NOPE_C) & (lane < QK_C), krc, 0.0)
    _mla_kv(ckv_n, kr_placed, wkb_ref, wvb_ref, gkc_ref, rc_ref, kc_ref, vcT_ref)
    if is_ctx:
        va_st_ref[...] = av
        ckv_st_ref[...] = ckv_n
        ckr_st_ref[...] = krc[:, NOPE_C:QK_C]


def _pre_mix(x, mod, lw, rope, *, B, T, is_ctx, tm, seg, nsub):
    N, D = x.shape
    assert (tm // nsub) % seg == 0 and (tm // nsub) % LANES == 0
    nt = N // tm
    tps = T // tm
    row_spec = lambda w: pl.BlockSpec((tm, w), lambda i: (i, 0))
    in_specs = [
        row_spec(D),
        pl.BlockSpec((1, 6, D), (lambda i: (0, 0, 0)) if is_ctx else (lambda i: (i // tps, 0, 0))),
        _const_spec((1, D)),
        _const_spec((D, IN_EXT)),
        _const_spec((MXU_DIM, MXU_DIM)),
        _const_spec((1, 2 * A_W)),
        _const_spec((1, Q_RANK)),
        _const_spec((1, KV_RANK)),
        _const_spec((1, HEAD_PAD)),
        _const_spec((1, HEAD_PAD)),
        _const_spec((Q_RANK, C_W)),
        _const_spec((KV_RANK, C_W)),
        _const_spec((KV_RANK, MIX_C)),
        _const_spec((3, W_B)),
        _const_spec((1, W_B)),
    ]
    args = [x, mod, lw["norm_mix"], lw["w_in"], lw["b32"], lw["gqk"], lw["gqa"], lw["gkva"],
            lw["gqc"], lw["gkc"], lw["w_qb"], lw["w_kb"], lw["w_vb"], lw["conv_w"], lw["conv_b"]]
    if not is_ctx:
        rspec = pl.BlockSpec((3, tm, LANES), lambda i: (0, i % tps, 0))
        in_specs += [rspec, rspec]
        args += list(rope)
    colT = lambda w: pl.BlockSpec((w, tm), lambda i: (0, i))
    vT_spec = pl.BlockSpec((1, H_A, V_ROWS, tm), lambda i: (i // tps, 0, 0, i % tps))
    out_specs = [colT(A_W), pl.BlockSpec((H_A, tm, 2 * DA), lambda i: (0, i, 0)), vT_spec, row_spec(W_B),
                 colT(C_W), pl.BlockSpec((H_C, tm, HEAD_PAD), lambda i: (0, i, 0)), vT_spec]
    out_shape = [jax.ShapeDtypeStruct((A_W, N), BF16), jax.ShapeDtypeStruct((H_A, N, 2 * DA), BF16),
                 jax.ShapeDtypeStruct((B, H_A, V_ROWS, T), BF16), jax.ShapeDtypeStruct((N, W_B), BF16),
                 jax.ShapeDtypeStruct((C_W, N), BF16), jax.ShapeDtypeStruct((H_C, N, HEAD_PAD), BF16),
                 jax.ShapeDtypeStruct((B, H_C, V_ROWS, T), BF16)]
    if is_ctx:
        st_w = [A_W, A_W, KV_RANK, ROPE_C]
        out_specs += [row_spec(w) for w in st_w]
        out_shape += [jax.ShapeDtypeStruct((N, w), F32) for w in st_w]
    return pl.pallas_call(
        functools.partial(_pre_mix_kernel, is_ctx, seg, nsub),
        grid=(nt,),
        in_specs=in_specs,
        out_specs=out_specs,
        out_shape=out_shape,
        compiler_params=_cparams(("arbitrary",)),
        name="pre_mix_ctx" if is_ctx else "pre_mix_lat",
    )(*args)


def _cache_kv_kernel(ckv_ref, kr_ref, wkb_ref, wvb_ref, gkc_ref, kc_ref, vcT_ref):
    _mla_kv(ckv_ref[...], kr_ref[...], wkb_ref, wvb_ref, gkc_ref, None, kc_ref, vcT_ref)


def _cache_kv(ckv, kr_placed, lw, B, P):
    row_spec = lambda w: pl.BlockSpec((P, w), lambda i: (i, 0))
    return pl.pallas_call(
        _cache_kv_kernel,
        grid=(B,),
        in_specs=[row_spec(KV_RANK), row_spec(HEAD_PAD), _const_spec((KV_RANK, C_W)),
                  _const_spec((KV_RANK, MIX_C)), _const_spec((1, HEAD_PAD))],
        out_specs=[pl.BlockSpec((H_C, P, HEAD_PAD), lambda i: (0, i, 0)),
                   pl.BlockSpec((1, H_C, V_ROWS, P), lambda i: (i, 0, 0, 0))],
        out_shape=[jax.ShapeDtypeStruct((H_C, B * P, HEAD_PAD), BF16),
                   jax.ShapeDtypeStruct((B, H_C, V_ROWS, P), BF16)],
        compiler_params=_cparams(("arbitrary",)),
        name="cache_kv",
    )(ckv, kr_placed, lw["w_kb"], lw["w_vb"], lw["gkc"])


def _colmax(s):
    tk, tq = s.shape
    r = jnp.max(s.reshape(tk // ROW_GROUP, ROW_GROUP, tq), axis=0)
    return jnp.max(r, axis=0, keepdims=True)


def _attn_kernel(diff, has_ctx, G, chunks, tk, l_init, *refs):
    refs = list(refs)
    q_ref = refs.pop(0)
    if has_ctx:
        kc_ref, kl_ref, vc_ref, vl_ref = refs[:4]
        refs = refs[4:]
    else:
        kl_ref, vl_ref = refs[:2]
        kc_ref = vc_ref = None
        refs = refs[2:]
    if diff:
        lam_ref, g_ref = refs[:2]
        refs = refs[2:]
    o_ref, sa, sb, os_ref = refs
    tq = q_ref.shape[-1]
    uw = tq if diff else tq // 2
    bufs = (sa, sb)
    nk = len(chunks)
    dq = q_ref.shape[0] // G
    gpt = 1 if nk % 2 == 0 else (2 if G % 2 == 0 else G)
    ntrip = G // gpt

    def q_units(g):
        start = g * dq
        if not isinstance(start, int):
            start = pl.multiple_of(start, dq)
        qh = q_ref[pl.ds(start, dq), :]
        if diff:
            row = lax.broadcasted_iota(jnp.int32, qh.shape, 0)
            zero = jnp.zeros_like(qh)
            return [jnp.where(row < DA, qh, zero), jnp.where(row >= DA, qh, zero)]
        return [qh[:, :uw], qh[:, uw:]]

    def kslice(h, j):
        src, off = chunks[j]
        return (kc_ref if src == 0 else kl_ref)[h, pl.ds(off, tk), :]

    def vslice(h, j):
        src, off = chunks[j]
        return (vc_ref if src == 0 else vl_ref)[0, h, :, pl.ds(off, tk)]

    def qk(g, qs, j, buf):
        kj = kslice(g, j)
        for u in range(2):
            buf[u] = jnp.dot(kj, qs[u], preferred_element_type=F32)

    def softmax(buf, carry):
        ps, out = [], []
        for u in range(2):
            m, acc = carry[u]
            s = buf[u]
            m_new = jnp.maximum(m, _colmax(s))
            alpha = jnp.exp2(m - m_new)
            ps.append(jnp.exp2(s - m_new).astype(BF16))
            out.append((m_new, alpha * acc))
        return ps, out

    def pv(g, j, ps, st):
        vj = vslice(g, j)
        return [(st[u][0], st[u][1] + jnp.dot(vj, ps[u], preferred_element_type=F32)) for u in range(2)]

    def finish(g, res):
        os_ = [acc[:V_A] / acc[V_A:V_A + 1] for (_, acc) in res]
        if diff:
            lv = lam_ref[...]
            lam = (jnp.exp(jnp.sum(lv[0:1] * lv[1:2], axis=-1, keepdims=True))
                   - jnp.exp(jnp.sum(lv[2:3] * lv[3:4], axis=-1, keepdims=True)) + l_init)
            o = os_[0] - lam * os_[1]
            os_ref[g] = _rms(o, axis=0) * g_ref[...] * (1.0 - l_init)
        else:
            os_ref[g, :, 0:uw] = os_[0]
            os_ref[g, :, uw:tq] = os_[1]

    def trip(t, _):
        step = 0
        g = t * gpt
        qs = q_units(g)
        for gi in range(gpt):
            g = t * gpt + gi
            carry = [(jnp.full((1, uw), -jnp.inf, F32), jnp.zeros((V_ROWS, uw), F32)) for _ in range(2)]
            for j in range(nk):
                src, dst = bufs[step % 2], bufs[(step + 1) % 2]
                qs_next = qs
                if j + 1 < nk:
                    qk(g, qs, j + 1, dst)
                elif gi + 1 < gpt:
                    qs_next = q_units(g + 1)
                    qk(g + 1, qs_next, 0, dst)
                elif ntrip > 1:
                    g_next = jnp.minimum(g + 1, G - 1)
                    qk(g_next, q_units(g_next), 0, dst)
                ps, st = softmax(src, carry)
                carry = pv(g, j, ps, st)
                qs = qs_next
                step += 1
            finish(g, carry)
        return 0

    qk(0, q_units(0), 0, sa)
    if ntrip > 1:
        lax.fori_loop(0, ntrip, trip, 0)
    else:
        trip(0, 0)
    for c in range(G // 2):
        pair = jnp.concatenate([os_ref[2 * c], os_ref[2 * c + 1]], axis=0)
        o_ref[:, c * LANES:(c + 1) * LANES] = pair.T.astype(o_ref.dtype)


def _attention(qT, k_lat, v_lat, k_ctx=None, v_ctx=None, *, diff, tq, tk, lam_vecs=None, subln=None,
               l_init=0.0):
    B, H, _, T = v_lat.shape
    N = qT.shape[1]
    d = k_lat.shape[2]
    has_ctx = k_ctx is not None
    chunks = [(1, o) for o in range(0, T, tk)]
    nq = T // tq
    in_specs = [pl.BlockSpec((qT.shape[0], tq), lambda b, i: (0, b * nq + i))]
    args = [qT]
    if has_ctx:
        P = v_ctx.shape[3]
        chunks = [(0, o) for o in range(0, P, tk)] + chunks
        in_specs += [pl.BlockSpec((H, P, d), lambda b, i: (0, b, 0)),
                     pl.BlockSpec((H, T, d), lambda b, i: (0, b, 0)),
                     pl.BlockSpec((1, H, V_ROWS, P), lambda b, i: (b, 0, 0, 0)),
                     pl.BlockSpec((1, H, V_ROWS, T), lambda b, i: (b, 0, 0, 0))]
        args += [k_ctx, k_lat, v_ctx, v_lat]
    else:
        in_specs += [pl.BlockSpec((H, T, d), lambda b, i: (0, b, 0)),
                     pl.BlockSpec((1, H, V_ROWS, T), lambda b, i: (b, 0, 0, 0))]
        args += [k_lat, v_lat]
    if diff:
        in_specs += [pl.BlockSpec((4, DA), lambda b, i: (0, 0)),
                     pl.BlockSpec((V_A, 1), lambda b, i: (0, 0))]
        args += [lam_vecs, subln]
    uw = tq if diff else tq // 2
    return pl.pallas_call(
        functools.partial(_attn_kernel, diff, has_ctx, H, tuple(chunks), tk, l_init),
        grid=(B, nq),
        in_specs=in_specs,
        out_specs=pl.BlockSpec((tq, H * V_A), lambda b, i: (b * nq + i, 0)),
        out_shape=jax.ShapeDtypeStruct((N, H * V_A), BF16),
        scratch_shapes=[pltpu.VMEM((2, tk, uw), F32), pltpu.VMEM((2, tk, uw), F32),
                        pltpu.VMEM((H, V_A, tq), F32)],
        compiler_params=_cparams(("arbitrary", "arbitrary")),
        name="attn_diff" if diff else "attn_mla",
    )(*args)


def _post_head(x_ref, oa_ref, ob_ref, oc_ref, mod_ref, nffn_ref, wout_ref):
    y = (jnp.dot(oa_ref[...], wout_ref[0:MIX_A, :], preferred_element_type=F32)
         + jnp.dot(ob_ref[...], wout_ref[MIX_A:MIX_A + W_B, :], preferred_element_type=F32)
         + jnp.dot(oc_ref[...], wout_ref[MIX_A + W_B:, :], preferred_element_type=F32))
    x1 = x_ref[...] + mod_ref[0, 2:3, :] * y
    h2 = _rms(x1) * nffn_ref[...] * (1.0 + mod_ref[0, 4:5, :]) + mod_ref[0, 3:4, :]
    return x1, h2


def _swiglu(hb, wg, wu, wd):
    g = jnp.dot(hb, wg, preferred_element_type=F32)
    u = jnp.dot(hb, wu, preferred_element_type=F32)
    a = (g * jax.nn.sigmoid(g) * u).astype(BF16)
    return jnp.dot(a, wd, preferred_element_type=F32)


def _post_dense_kernel(fchunks, x_ref, oa_ref, ob_ref, oc_ref, mod_ref, nffn_ref, wout_ref,
                       wg_ref, wu_ref, wd_ref, o_ref):
    x1, h2 = _post_head(x_ref, oa_ref, ob_ref, oc_ref, mod_ref, nffn_ref, wout_ref)
    hb = h2.astype(BF16)
    acc = None
    for (f0, f1) in fchunks:
        part = _swiglu(hb, wg_ref[:, f0:f1], wu_ref[:, f0:f1], wd_ref[f0:f1, :])
        acc = part if acc is None else acc + part
    o_ref[...] = x1 + mod_ref[0, 5:6, :] * acc


def _post_dense(x, oa, ob, oc, mod, lw, *, tm, tiles_per_mod):
    N, D = x.shape
    FF = lw["wg"].shape[1]
    cut = (FF // 2 // MXU_DIM + 1) * MXU_DIM if FF > 2 * MXU_DIM else FF
    fchunks = ((0, cut), (cut, FF)) if cut < FF else ((0, FF),)
    row = lambda w: pl.BlockSpec((tm, w), lambda i: (i, 0))
    return pl.pallas_call(
        functools.partial(_post_dense_kernel, fchunks),
        grid=(N // tm,),
        in_specs=[row(D), row(MIX_A), row(W_B), row(MIX_C),
                  pl.BlockSpec((1, 6, D), lambda i: (i // tiles_per_mod, 0, 0)),
                  _const_spec((1, D)), _const_spec(lw["w_out"].shape), _const_spec((D, FF)),
                  _const_spec((D, FF)), _const_spec((FF, D))],
        out_specs=row(D),
        out_shape=jax.ShapeDtypeStruct((N, D), F32),
        compiler_params=_cparams(("arbitrary",)),
        name="post_dense",
    )(x, oa, ob, oc, mod, lw["norm_ffn"], lw["w_out"], lw["wg"], lw["wu"], lw["wd"])


TOK_SUB = 8
MOE_ROWS = 256
DISPATCH_TOKENS = 256
COMBINE_TOKENS = 256


def _to_token_tiles(x, ref):
    tm = x.shape[0]
    for a in range(TOK_SUB):
        ref[pl.ds(a, tm, stride=TOK_SUB), :] = x[:, a * LANES:(a + 1) * LANES]


def _from_token_tiles(ref, tm):
    return jnp.concatenate([ref[pl.ds(a, tm, stride=TOK_SUB), :] for a in range(TOK_SUB)], axis=-1)


def _router_kernel(x_ref, oa_ref, ob_ref, oc_ref, mod_ref, nffn_ref, wout_ref, rw_ref,
                   x1_ref, h2t_ref, route_ref):
    x1, h2 = _post_head(x_ref, oa_ref, ob_ref, oc_ref, mod_ref, nffn_ref, wout_ref)
    x1_ref[...] = x1
    _to_token_tiles(h2, h2t_ref)
    rw = rw_ref[...]
    rhi = rw.astype(BF16)
    rlo = (rw - rhi.astype(F32)).astype(BF16)
    hhi = h2.astype(BF16)
    hlo = (h2 - hhi.astype(F32)).astype(BF16)
    logits = (jnp.dot(hhi, rhi, preferred_element_type=F32)
              + jnp.dot(hlo, rhi, preferred_element_type=F32)
              + jnp.dot(hhi, rlo, preferred_element_type=F32))
    lane = lax.broadcasted_iota(jnp.int32, logits.shape, 1)
    neg = -jnp.inf
    lg = jnp.where(lane < N_EXPERTS, logits, neg)
    v1 = jnp.max(lg, axis=-1, keepdims=True)
    i1 = jnp.min(jnp.where(lg == v1, lane, LANES), axis=-1, keepdims=True)
    lg2 = jnp.where(lane == i1, neg, lg)
    v2 = jnp.max(lg2, axis=-1, keepdims=True)
    i2 = jnp.min(jnp.where(lg2 == v2, lane, LANES), axis=-1, keepdims=True)
    e2 = jnp.exp(v2 - v1)
    g1 = 1.0 / (1.0 + e2)
    g2 = e2 / (1.0 + e2)
    route_ref[...] = jnp.where(lane == 0, g1, jnp.where(lane == 1, g2, jnp.where(
        lane == 2, i1.astype(F32), jnp.where(lane == 3, i2.astype(F32), 0.0))))


def _router(x, oa, ob, oc, mod, lw, *, tm, tiles_per_mod):
    N, D = x.shape
    assert D == TOK_SUB * LANES
    row = lambda w: pl.BlockSpec((tm, w), lambda i: (i, 0))
    return pl.pallas_call(
        _router_kernel,
        grid=(N // tm,),
        in_specs=[row(D), row(MIX_A), row(W_B), row(MIX_C),
                  pl.BlockSpec((1, 6, D), lambda i: (i // tiles_per_mod, 0, 0)),
                  _const_spec((1, D)), _const_spec(lw["w_out"].shape), _const_spec((D, LANES))],
        out_specs=[row(D), pl.BlockSpec((tm * TOK_SUB, LANES), lambda i: (i, 0)), row(LANES)],
        out_shape=[jax.ShapeDtypeStruct((N, D), F32), jax.ShapeDtypeStruct((N * TOK_SUB, LANES), F32),
                   jax.ShapeDtypeStruct((N, LANES), F32)],
        compiler_params=_cparams(("arbitrary",)),
        name="moe_router",
    )(x, oa, ob, oc, mod, lw["norm_ffn"], lw["w_out"], lw["router"])


def _moe_plan(route, N):
    R, E = MOE_ROWS, N_EXPERTS
    es = route[:, 2:4].astype(jnp.int32).reshape(-1)
    oh = (es[:, None] == jnp.arange(E, dtype=jnp.int32)[None, :]).astype(jnp.int32)
    rank = jnp.sum((jnp.cumsum(oh, axis=0) - oh) * oh, axis=1)
    cnt = jnp.sum(oh, axis=0)
    gsz = ((cnt + R - 1) // R) * R
    gend = jnp.cumsum(gsz)
    pos = ((gend - gsz)[es] + rank).astype(jnp.int32)
    n_tiles = (2 * N) // R + E
    tile_start = jnp.arange(n_tiles, dtype=jnp.int32) * R
    tile_expert = jnp.minimum(jnp.sum(tile_start[:, None] >= gend[None, :], axis=1), E - 1).astype(jnp.int32)
    return pos, tile_expert


def _tile_rows(ref, row):
    return ref.at[pl.ds(row * TOK_SUB, TOK_SUB), :]


def _dispatch_kernel(pos_ref, h2t_ref, xs_in_hbm, xs_hbm, buf, sem):
    del xs_in_hbm
    i = pl.program_id(0)
    n = pl.num_programs(0)
    tmd = DISPATCH_TOKENS
    slot = i % 2

    def step_wait(buf_slot):
        for _ in range(2):
            pltpu.make_async_copy(buf.at[buf_slot], xs_hbm.at[pl.ds(0, tmd * TOK_SUB), :],
                                  sem.at[buf_slot]).wait()

    @pl.when(i >= 2)
    def _():
        step_wait(slot)

    buf[slot] = h2t_ref[...]
    for r in range(tmd):
        t = i * tmd + r
        for k in range(2):
            pltpu.make_async_copy(_tile_rows(buf.at[slot], r), _tile_rows(xs_hbm, pos_ref[2 * t + k]),
                                  sem.at[slot]).start()

    @pl.when(i == n - 1)
    def _():
        step_wait(slot)

        @pl.when(n >= 2)
        def _():
            step_wait(1 - slot)


def _dispatch(h2t, pos, n_rows):
    N = pos.shape[0] // 2
    xs0 = jnp.zeros((n_rows * TOK_SUB, LANES), F32)
    tile = DISPATCH_TOKENS * TOK_SUB
    grid_spec = pltpu.PrefetchScalarGridSpec(
        num_scalar_prefetch=1,
        grid=(N // DISPATCH_TOKENS,),
        in_specs=[pl.BlockSpec((tile, LANES), lambda i, pos: (i, 0)), pl.BlockSpec(memory_space=pl.ANY)],
        out_specs=pl.BlockSpec(memory_space=pl.ANY),
        scratch_shapes=[pltpu.VMEM((2, tile, LANES), F32), pltpu.SemaphoreType.DMA((2,))],
    )
    return pl.pallas_call(
        _dispatch_kernel,
        grid_spec=grid_spec,
        out_shape=jax.ShapeDtypeStruct(xs0.shape, F32),
        input_output_aliases={2: 0},
        compiler_params=_cparams(("arbitrary",)),
        name="moe_dispatch",
    )(pos, h2t, xs0)


def _experts_kernel(te_ref, x_ref, wg_ref, wu_ref, wd_ref, y_ref):
    xb = _from_token_tiles(x_ref, MOE_ROWS).astype(BF16)
    _to_token_tiles(_swiglu(xb, wg_ref[0], wu_ref[0], wd_ref[0]), y_ref)


def _experts(xs, tile_expert, lw):
    E, D, FF = lw["wg"].shape
    R = MOE_ROWS
    n_tiles = tile_expert.shape[0]
    wspec = lambda shape: pl.BlockSpec((1,) + shape, lambda i, te: (te[i], 0, 0))
    rows = pl.BlockSpec((R * TOK_SUB, LANES), lambda i, te: (i, 0))
    grid_spec = pltpu.PrefetchScalarGridSpec(
        num_scalar_prefetch=1,
        grid=(n_tiles,),
        in_specs=[rows, wspec((D, FF)), wspec((D, FF)), wspec((FF, D))],
        out_specs=rows,
    )
    return pl.pallas_call(
        _experts_kernel,
        grid_spec=grid_spec,
        out_shape=jax.ShapeDtypeStruct(xs.shape, F32),
        compiler_params=_cparams(("arbitrary",)),
        name="moe_experts",
    )(tile_expert, xs, lw["wg"], lw["wu"], lw["wd"])


def _combine_kernel(pos_ref, x1_ref, ys_hbm, route_ref, mod_ref, o_ref, ybuf, sem):
    tm = x1_ref.shape[0]
    i = pl.program_id(0)
    n = pl.num_programs(0)
    slot = i % 2

    def issue(tile, buf_slot):
        for r in range(tm):
            t = tile * tm + r
            for k in range(2):
                pltpu.make_async_copy(_tile_rows(ys_hbm, pos_ref[2 * t + k]),
                                      _tile_rows(ybuf.at[buf_slot, k], r), sem.at[buf_slot]).start()

    @pl.when(i == 0)
    def _():
        issue(0, 0)

    for k in range(2):
        pltpu.make_async_copy(ys_hbm.at[pl.ds(0, tm * TOK_SUB), :], ybuf.at[slot, k], sem.at[slot]).wait()

    @pl.when(i + 1 < n)
    def _():
        issue(i + 1, 1 - slot)

    g = route_ref[...]
    y = (g[:, 0:1] * _from_token_tiles(ybuf.at[slot, 0], tm)
         + g[:, 1:2] * _from_token_tiles(ybuf.at[slot, 1], tm))
    o_ref[...] = x1_ref[...] + mod_ref[0, 5:6, :] * y


def _combine(x1, ys, pos, route, mod, *, tm, tiles_per_mod):
    N, D = x1.shape
    row = lambda w: pl.BlockSpec((tm, w), lambda i, pos: (i, 0))
    grid_spec = pltpu.PrefetchScalarGridSpec(
        num_scalar_prefetch=1,
        grid=(N // tm,),
        in_specs=[row(D), pl.BlockSpec(memory_space=pl.ANY), row(LANES),
                  pl.BlockSpec((1, 6, D), lambda i, pos: (i // tiles_per_mod, 0, 0))],
        out_specs=row(D),
        scratch_shapes=[pltpu.VMEM((2, 2, tm * TOK_SUB, LANES), F32), pltpu.SemaphoreType.DMA((2,))],
    )
    return pl.pallas_call(
        _combine_kernel,
        grid_spec=grid_spec,
        out_shape=jax.ShapeDtypeStruct((N, D), F32),
        compiler_params=_cparams(("arbitrary",)),
        name="moe_combine",
    )(pos, x1, ys, route, mod)


def _post_moe(x, oa, ob, oc, mod, lw, *, tm, tiles_per_mod):
    N = x.shape[0]
    x1, h2t, route = _router(x, oa, ob, oc, mod, lw, tm=tm, tiles_per_mod=tiles_per_mod)
    pos, tile_expert = _moe_plan(route, N)
    xs = _dispatch(h2t, pos, tile_expert.shape[0] * MOE_ROWS)
    ys = _experts(xs, tile_expert, lw)
    tmc = min(COMBINE_TOKENS, N)
    return _combine(x1, ys, pos, route, mod, tm=tmc, tiles_per_mod=tiles_per_mod * (tm // tmc))


def _rope_tables(T):
    t = jnp.arange(T, dtype=jnp.int32)
    rows = (t // GRID_W).astype(F32)
    cols = (t % GRID_W).astype(F32)
    n = ROPE_C // 4
    inv = jnp.power(ROPE_THETA, -jnp.arange(n, dtype=F32) / n)
    j = jnp.arange(ROPE_C)
    pos = jnp.where(j[None, :] < ROPE_C // 2, rows[:, None], cols[:, None])
    ang = pos * inv[j % n][None, :]
    cos = jnp.cos(ang)
    sin = jnp.sin(ang)
    first = ((j % (2 * n)) < n)[None, :]
    s_up = jnp.where(first, -sin, 0.0)
    s_dn = jnp.where(first, 0.0, sin)
    tab32 = jnp.stack([cos, s_up, s_dn])
    tab_a = jnp.tile(tab32, (1, 1, LANES // ROPE_C))
    ident = jnp.stack([jnp.ones((T, NOPE_C), F32), jnp.zeros((T, NOPE_C), F32), jnp.zeros((T, NOPE_C), F32)])
    tail = jnp.stack([jnp.ones((T, HEAD_PAD - QK_C), F32), jnp.zeros((T, HEAD_PAD - QK_C), F32),
                      jnp.zeros((T, HEAD_PAD - QK_C), F32)])
    tab_c = jnp.concatenate([ident, tab32, tail], axis=-1)
    return tab_a, tab_c


def _pad_heads(w, width):
    lead = w.shape[:-1]
    w = w.reshape(lead + (H_C, width))
    w = jnp.pad(w, [(0, 0)] * len(lead) + [(0, 0), (0, HEAD_PAD - width)])
    return w.reshape(lead + (C_W,))


def _layer_weights(l, p):
    w_in = p["w_in"][l]
    D = w_in.shape[0]
    o_cq = 3 * A_W + 3 * W_B
    o_ckv = o_cq + Q_RANK
    o_ckr = o_ckv + KV_RANK
    w_ext = jnp.concatenate(
        [w_in[:, :o_cq], w_in[:, o_ckv:o_ckr], w_in[:, o_cq:o_ckv], w_in[:, o_ckr:],
         jnp.zeros((D, IN_EXT - OFF_CKR - ROPE_C), F32)], axis=1).astype(BF16)
    g = jnp.arange(MXU_DIM) // DA
    lw = {
        "w_in": w_ext,
        "b32": (g[:, None] == g[None, :]).astype(BF16),
        "norm_mix": p["norm_mix"][l][None, :],
        "norm_ffn": p["norm_ffn"][l][None, :],
        "gqk": jnp.concatenate([jnp.tile(p["qnorm_a"][l], 2 * H_A), jnp.tile(p["knorm_a"][l], 2 * H_A)])[None, :],
        "gqa": p["norm_qa"][l][None, :],
        "gkva": p["norm_kva"][l][None, :],
        "gqc": jnp.pad(p["qnorm_c"][l], (0, HEAD_PAD - QK_C))[None, :],
        "gkc": jnp.pad(p["knorm_c"][l], (0, HEAD_PAD - QK_C))[None, :],
        "w_qb": _pad_heads(p["w_qb"][l], QK_C).astype(BF16),
        "w_kb": _pad_heads(p["w_kb"][l], NOPE_C).astype(BF16),
        "w_vb": p["w_vb"][l].astype(BF16),
        "conv_w": p["conv_w"][l],
        "conv_b": p["conv_b"][l][None, :],
        "w_out": p["w_out"][l].astype(BF16),
        "lam": jnp.stack([p["lambda_q1"][l], p["lambda_k1"][l], p["lambda_q2"][l], p["lambda_k2"][l]]),
        "subln": p["subln_a"][l][:, None],
    }
    i = l // 2
    if l % 2 == 0:
        lw["wg"] = p["ffn_w_gate"][i].astype(BF16)
        lw["wu"] = p["ffn_w_up"][i].astype(BF16)
        lw["wd"] = p["ffn_w_down"][i].astype(BF16)
    else:
        lw["router"] = jnp.pad(p["router_w"][i], ((0, 0), (0, LANES - N_EXPERTS)))
        lw["wg"] = p["moe_w_gate"][i].astype(BF16)
        lw["wu"] = p["moe_w_up"][i].astype(BF16)
        lw["wd"] = p["moe_w_down"][i].astype(BF16)
    return lw


def _cached_vT(v, B, P):
    vT = v.transpose(0, 2, 3, 1).astype(BF16)
    ones = jnp.ones((B, H_A, 1, P), BF16)
    zeros = jnp.zeros((B, H_A, V_ROWS - V_A - 1, P), BF16)
    return jnp.concatenate([vT, ones, zeros], axis=2)


def _mixer(x, mod, lw, l_init, *, B, T, ctx, rope):
    is_ctx = ctx is None
    tm = T if is_ctx else min(512, T)
    seg = T if is_ctx else GRID_W
    nsub = 1 if is_ctx else max(1, tm // 256)
    outs = _pre_mix(x, mod, lw, rope, B=B, T=T, is_ctx=is_ctx, tm=tm, seg=seg, nsub=nsub)
    qaT, ka, vaT, ob, qcT, kc, vcT = outs[:7]
    tq = min(256, T)
    tq_c = min(2 * tq, T)
    if is_ctx:
        tk = 512 if T % 512 == 0 else 256
        o_a = _attention(qaT, ka, vaT, diff=True, tq=tq, tk=tk, lam_vecs=lw["lam"], subln=lw["subln"],
                         l_init=l_init)
        o_c = _attention(qcT, kc, vcT, diff=False, tq=tq_c, tk=tk)
    else:
        ctx_k, ctx_v, ctx_ckv, ctx_kr = ctx
        P = ctx_k.shape[1]
        tk = 512 if (P % 512 == 0 and T % 512 == 0) else 256
        kr_placed = jnp.pad(ctx_kr.reshape(B * P, ROPE_C), ((0, 0), (NOPE_C, HEAD_PAD - QK_C)))
        kc_ctx, vc_ctx = _cache_kv(ctx_ckv.reshape(B * P, KV_RANK), kr_placed, lw, B, P)
        ka_ctx = ctx_k.reshape(B * P, H_A, 2 * DA).transpose(1, 0, 2).astype(BF16)
        o_a = _attention(qaT, ka, vaT, ka_ctx, _cached_vT(ctx_v, B, P), diff=True, tq=tq, tk=tk,
                         lam_vecs=lw["lam"], subln=lw["subln"], l_init=l_init)
        o_c = _attention(qcT, kc, vcT, kc_ctx, vc_ctx, diff=False, tq=tq_c, tk=tk)
    return (o_a, ob, o_c), outs[7:]


def kernel(x_prompt, x_sample, cache_diff_k, cache_diff_v, cache_mla_ckv, cache_mla_krope, c, c_ctx, w_ada, b_ada, norm_mix, norm_ffn, w_in, qnorm_a, knorm_a, lambda_q1, lambda_k1, lambda_q2, lambda_k2, subln_a, conv_w, conv_b, norm_qa, w_qb, norm_kva, w_kb, w_vb, qnorm_c, knorm_c, w_out, ffn_w_gate, ffn_w_up, ffn_w_down, router_w, moe_w_gate, moe_w_up, moe_w_down):
    p = dict(norm_mix=norm_mix, norm_ffn=norm_ffn, w_in=w_in, qnorm_a=qnorm_a, knorm_a=knorm_a,
             lambda_q1=lambda_q1, lambda_k1=lambda_k1, lambda_q2=lambda_q2, lambda_k2=lambda_k2,
             subln_a=subln_a, conv_w=conv_w, conv_b=conv_b, norm_qa=norm_qa, w_qb=w_qb,
             norm_kva=norm_kva, w_kb=w_kb, w_vb=w_vb, qnorm_c=qnorm_c, knorm_c=knorm_c, w_out=w_out,
             ffn_w_gate=ffn_w_gate, ffn_w_up=ffn_w_up, ffn_w_down=ffn_w_down, router_w=router_w,
             moe_w_gate=moe_w_gate, moe_w_up=moe_w_up, moe_w_down=moe_w_down)
    Bc, Sc, D = x_prompt.shape
    Bl, T, _ = x_sample.shape
    L = w_in.shape[0]
    nrow = 16
    cvec = jnp.concatenate([c, c_ctx[None, :], jnp.zeros((nrow - Bl - 1, D), F32)], axis=0)
    mod = _ada(cvec, w_ada, b_ada)
    rope = _rope_tables(T)
    xp = x_prompt.reshape(Bc * Sc, D)
    xs = x_sample.reshape(Bl * T, D)
    st = [[], [], [], []]
    for l in range(L):
        lw = _layer_weights(l, p)
        l_init = 0.8 - 0.6 * math.exp(-0.3 * l)
        mod_lat = mod[l, :Bl].reshape(Bl, 6, D)
        mod_ctx = mod[l, Bl:Bl + 1].reshape(1, 6, D)
        post = _post_dense if l % 2 == 0 else _post_moe
        heads, state = _mixer(xp, mod_ctx, lw, l_init, B=Bc, T=Sc, ctx=None, rope=None)
        for acc, s in zip(st, state):
            acc.append(s)
        xp = post(xp, *heads, mod_ctx, lw, tm=min(512, Bc * Sc), tiles_per_mod=Bc * Sc)
        ctx = (cache_diff_k[:, l], cache_diff_v[:, l], cache_mla_ckv[:, l], cache_mla_krope[:, l])
        heads, _ = _mixer(xs, mod_lat, lw, l_init, B=Bl, T=T, ctx=ctx, rope=rope)
        tm = min(512, T)
        xs = post(xs, *heads, mod_lat, lw, tm=tm, tiles_per_mod=T // tm)
    new_k = jnp.stack(st[0], axis=1).reshape(Bc, Sc, L, H_A, 2, DA).transpose(0, 2, 1, 3, 4, 5)
    new_v = jnp.stack(st[1], axis=1).reshape(Bc, Sc, L, H_A, V_A).transpose(0, 2, 1, 3, 4)
    new_ckv = jnp.stack(st[2], axis=1).reshape(Bc, Sc, L, KV_RANK).transpose(0, 2, 1, 3)
    new_kr = jnp.stack(st[3], axis=1).reshape(Bc, Sc, L, ROPE_C).transpose(0, 2, 1, 3)
    return (xp.reshape(Bc, Sc, D), xs.reshape(Bl, T, D), new_k, new_v, new_ckv, new_kr)
```

```python
import functools
import math

import jax
import jax.numpy as jnp
from jax import lax
from jax.experimental import pallas as pl
from jax.experimental.pallas import tpu as pltpu

F32 = jnp.float32
BF16 = jnp.bfloat16

GRID_W = 64
H_A = 6
DA = 32
V_A = 2 * DA
W_B = 256
H_C = 6
NOPE_C = 64
ROPE_C = 32
QK_C = NOPE_C + ROPE_C
V_C = 64
Q_RANK = 192
KV_RANK = 128
N_EXPERTS = 8
ROPE_THETA = 10000.0
EPS = 1e-6
LOG2E = 1.4426950408889634

LANES = 128
MXU_DIM = 256
HEAD_PAD = 128

A_W = H_A * 2 * DA
OFF_AQ = 0
OFF_AK = A_W
OFF_AV = 2 * A_W
OFF_BB = 3 * A_W
OFF_BC = OFF_BB + W_B
OFF_BX = OFF_BC + W_B
OFF_CKV = OFF_BX + W_B
OFF_CQ = OFF_CKV + KV_RANK
OFF_CKR = OFF_CQ + Q_RANK
KR_CHUNK = (OFF_CKR // LANES) * LANES
assert OFF_CKR - KR_CHUNK == NOPE_C
IN_EXT = KR_CHUNK + LANES
C_W = H_C * HEAD_PAD
MIX_A = H_A * V_A
MIX_C = H_C * V_C

ROW_GROUP = 32
V_ROWS = V_A + 16
assert V_A == V_C

VMEM_LIMIT = 56 * 1024 * 1024


def _cparams(sem):
    return pltpu.CompilerParams(dimension_semantics=sem, vmem_limit_bytes=VMEM_LIMIT)


def _const_spec(shape):
    nd = len(shape)
    return pl.BlockSpec(shape, lambda *_: (0,) * nd, pipeline_mode=pl.Buffered(1))


def _rms(x, axis=-1):
    return x * lax.rsqrt(jnp.mean(x * x, axis=axis, keepdims=True) + EPS)


def _split_dot(x, w):
    hi = x.astype(BF16)
    lo = (x - hi.astype(F32)).astype(BF16)
    return (jnp.dot(hi, w, preferred_element_type=F32)
            + jnp.dot(lo, w, preferred_element_type=F32))


def _ada_kernel(c_ref, w_ref, b_ref, o_ref):
    c = c_ref[...]
    s = (c * jax.nn.sigmoid(c)).astype(BF16)
    o_ref[0] = jnp.dot(s, w_ref[0].astype(BF16), preferred_element_type=F32) + b_ref[0]


def _ada(cvec, w_ada, b_ada):
    L, D, N6 = w_ada.shape
    R = cvec.shape[0]
    tn = 1536
    assert N6 % tn == 0
    return pl.pallas_call(
        _ada_kernel,
        grid=(L, N6 // tn),
        in_specs=[
            pl.BlockSpec((R, D), lambda l, j: (0, 0)),
            pl.BlockSpec((1, D, tn), lambda l, j: (l, 0, j)),
            pl.BlockSpec((1, 1, tn), lambda l, j: (l, 0, j)),
        ],
        out_specs=pl.BlockSpec((1, R, tn), lambda l, j: (l, 0, j)),
        out_shape=jax.ShapeDtypeStruct((L, R, N6), F32),
        compiler_params=_cparams(("arbitrary", "arbitrary")),
        name="ada_mod",
    )(cvec, w_ada, b_ada.reshape(L, 1, N6))


def _rope(x, tab_ref):
    up = pltpu.roll(x, LANES - 8, 1)
    dn = pltpu.roll(x, 8, 1)
    return x * tab_ref[0] + up * tab_ref[1] + dn * tab_ref[2]


def _store_vT(v, ref):
    tm = v.shape[0]
    vT = v.T.astype(BF16)
    row = lax.broadcasted_iota(jnp.int32, (V_ROWS - V_A, tm), 0)
    tail = jnp.where(row == 0, 1.0, 0.0).astype(BF16)
    for h in range(v.shape[1] // V_A):
        ref[0, h, 0:V_A, :] = vT[h * V_A:(h + 1) * V_A, :]
        ref[0, h, V_A:V_ROWS, :] = tail


def _mla_kv(ckv_n, kr_placed, wkb_ref, wvb_ref, gkc_ref, rope_ref, kc_ref, vcT_ref):
    cb = ckv_n.astype(BF16)
    kn = jnp.dot(cb, wkb_ref[...], preferred_element_type=F32)
    vc = jnp.dot(cb, wvb_ref[...], preferred_element_type=F32)
    for h in range(H_C):
        kp = kn[:, h * HEAD_PAD:(h + 1) * HEAD_PAD] + kr_placed
        ss = jnp.sum(kp * kp, axis=-1, keepdims=True) * (1.0 / QK_C)
        kp = kp * lax.rsqrt(ss + EPS) * gkc_ref[...]
        if rope_ref is not None:
            kp = _rope(kp, rope_ref)
        kc_ref[h] = kp.astype(BF16)
    _store_vT(vc, vcT_ref)


def _pre_mix_kernel(is_ctx, seg, nsub, *refs):
    refs = list(refs)
    sub = refs[0].shape[0] // nsub
    row_dim = {0: 0}
    n_in = 15 if is_ctx else 17
    if not is_ctx:
        row_dim.update({15: 1, 16: 1})
    for k, dim in enumerate((1, 1, 3, 0, 1, 1, 3) + ((0, 0, 0, 0) if is_ctx else ())):
        row_dim[n_in + k] = dim
    for r in range(nsub):
        views = []
        for pos, ref in enumerate(refs):
            if pos in row_dim:
                idx = [slice(None)] * len(ref.shape)
                idx[row_dim[pos]] = pl.ds(r * sub, sub)
                ref = ref.at[tuple(idx)]
            views.append(ref)
        _pre_mix_rows(is_ctx, seg, *views)


def _pre_mix_rows(is_ctx, seg, *refs):
    (x_ref, mod_ref, nmix_ref, win_ref, b32_ref, gqk_ref, gqa_ref, gkva_ref, gqc_ref,
     gkc_ref, wqb_ref, wkb_ref, wvb_ref, cw_ref, cb_ref) = refs[:15]
    refs = refs[15:]
    if is_ctx:
        ra_ref = rc_ref = None
    else:
        ra_ref, rc_ref = refs[:2]
        refs = refs[2:]
    qaT_ref, ka_ref, vaT_ref, ob_ref, qcT_ref, kc_ref, vcT_ref = refs[:7]
    refs = refs[7:]

    x = x_ref[...]
    shift1 = mod_ref[0, 0:1, :]
    scale1 = mod_ref[0, 1:2, :]
    h = _rms(x) * nmix_ref[...] * (1.0 + scale1) + shift1
    proj = jnp.dot(h.astype(BF16), win_ref[...], preferred_element_type=F32)

    aqk = proj[:, OFF_AQ:OFF_AV]
    sq = aqk * aqk
    ss = jnp.concatenate(
        [_split_dot(sq[:, c:c + MXU_DIM], b32_ref[...]) for c in range(0, 2 * A_W, MXU_DIM)], axis=-1)
    aqk = aqk * lax.rsqrt(ss * (1.0 / DA) + EPS) * gqk_ref[...]
    if is_ctx:
        ka_st_ref, va_st_ref, ckv_st_ref, ckr_st_ref = refs
        ka_st_ref[...] = aqk[:, A_W:]
    else:
        aqk = jnp.concatenate(
            [_rope(aqk[:, c:c + LANES], ra_ref) for c in range(0, 2 * A_W, LANES)], axis=-1)
    qaT_ref[...] = (aqk[:, :A_W] * (DA ** -0.5 * LOG2E)).T.astype(BF16)
    for hh in range(H_A):
        ka_ref[hh] = aqk[:, A_W + hh * 2 * DA:A_W + (hh + 1) * 2 * DA].astype(BF16)
    av = proj[:, OFF_AV:OFF_BB]
    _store_vT(av, vaT_ref)

    bb = proj[:, OFF_BB:OFF_BC]
    u = proj[:, OFF_BC:OFF_BX] * proj[:, OFF_BX:OFF_CKV]
    tm = u.shape[0]
    row = lax.broadcasted_iota(jnp.int32, u.shape, 0) & (seg - 1)
    prev = jnp.where(row == 0, 0.0, pltpu.roll(u, 1, 0))
    nxt = jnp.where(row == seg - 1, 0.0, pltpu.roll(u, tm - 1, 0))
    conv = prev * cw_ref[0:1, :] + u * cw_ref[1:2, :] + nxt * cw_ref[2:3, :] + cb_ref[...]
    ob_ref[...] = (bb * conv).astype(BF16)

    cq = proj[:, OFF_CQ:OFF_CKR]
    cqn = (_rms(cq) * gqa_ref[...]).astype(BF16)
    qc = jnp.dot(cqn, wqb_ref[...], preferred_element_type=F32)
    qhs = []
    for hh in range(H_C):
        qh = qc[:, hh * HEAD_PAD:(hh + 1) * HEAD_PAD]
        s2 = jnp.sum(qh * qh, axis=-1, keepdims=True) * (1.0 / QK_C)
        qh = qh * lax.rsqrt(s2 + EPS) * gqc_ref[...]
        if not is_ctx:
            qh = _rope(qh, rc_ref)
        qhs.append(qh * (QK_C ** -0.5 * LOG2E))
    qcT_ref[...] = jnp.concatenate(qhs, axis=-1).T.astype(BF16)
    ckv_n = _rms(proj[:, OFF_CKV:OFF_CQ]) * gkva_ref[...]
    krc = proj[:, KR_CHUNK:KR_CHUNK + LANES]
    lane = lax.broadcasted_iota(jnp.int32, krc.shape, 1)
    kr_placed = jnp.where((lane >= NOPE_C) & (lane < QK_C), krc, 0.0)
    _mla_kv(ckv_n, kr_placed, wkb_ref, wvb_ref, gkc_ref, rc_ref, kc_ref, vcT_ref)
    if is_ctx:
        va_st_ref[...] = av
        ckv_st_ref[...] = ckv_n
        ckr_st_ref[...] = krc[:, NOPE_C:QK_C]


def _pre_mix(x, mod, lw, rope, *, B, T, is_ctx, tm, seg, nsub):
    N, D = x.shape
    assert (tm // nsub) % seg == 0 and (tm // nsub) % LANES == 0
    nt = N // tm
    tps = T // tm
    row_spec = lambda w: pl.BlockSpec((tm, w), lambda i: (i, 0))
    in_specs = [
        row_spec(D),
        pl.BlockSpec((1, 6, D), (lambda i: (0, 0, 0)) if is_ctx else (lambda i: (i // tps, 0, 0))),
        _const_spec((1, D)),
        _const_spec((D, IN_EXT)),
        _const_spec((MXU_DIM, MXU_DIM)),
        _const_spec((1, 2 * A_W)),
        _const_spec((1, Q_RANK)),
        _const_spec((1, KV_RANK)),
        _const_spec((1, HEAD_PAD)),
        _const_spec((1, HEAD_PAD)),
        _const_spec((Q_RANK, C_W)),
        _const_spec((KV_RANK, C_W)),
        _const_spec((KV_RANK, MIX_C)),
        _const_spec((3, W_B)),
        _const_spec((1, W_B)),
    ]
    args = [x, mod, lw["norm_mix"], lw["w_in"], lw["b32"], lw["gqk"], lw["gqa"], lw["gkva"],
            lw["gqc"], lw["gkc"], lw["w_qb"], lw["w_kb"], lw["w_vb"], lw["conv_w"], lw["conv_b"]]
    if not is_ctx:
        rspec = pl.BlockSpec((3, tm, LANES), lambda i: (0, i % tps, 0))
        in_specs += [rspec, rspec]
        args += list(rope)
    colT = lambda w: pl.BlockSpec((w, tm), lambda i: (0, i))
    vT_spec = pl.BlockSpec((1, H_A, V_ROWS, tm), lambda i: (i // tps, 0, 0, i % tps))
    out_specs = [colT(A_W), pl.BlockSpec((H_A, tm, 2 * DA), lambda i: (0, i, 0)), vT_spec, row_spec(W_B),
                 colT(C_W), pl.BlockSpec((H_C, tm, HEAD_PAD), lambda i: (0, i, 0)), vT_spec]
    out_shape = [jax.ShapeDtypeStruct((A_W, N), BF16), jax.ShapeDtypeStruct((H_A, N, 2 * DA), BF16),
                 jax.ShapeDtypeStruct((B, H_A, V_ROWS, T), BF16), jax.ShapeDtypeStruct((N, W_B), BF16),
                 jax.ShapeDtypeStruct((C_W, N), BF16), jax.ShapeDtypeStruct((H_C, N, HEAD_PAD), BF16),
                 jax.ShapeDtypeStruct((B, H_C, V_ROWS, T), BF16)]
    if is_ctx:
        st_w = [A_W, A_W, KV_RANK, ROPE_C]
        out_specs += [row_spec(w) for w in st_w]
        out_shape += [jax.ShapeDtypeStruct((N, w), F32) for w in st_w]
    return pl.pallas_call(
        functools.partial(_pre_mix_kernel, is_ctx, seg, nsub),
        grid=(nt,),
        in_specs=in_specs,
        out_specs=out_specs,
        out_shape=out_shape,
        compiler_params=_cparams(("arbitrary",)),
        name="pre_mix_ctx" if is_ctx else "pre_mix_lat",
    )(*args)


def _cache_kv_kernel(ckv_ref, kr_ref, wkb_ref, wvb_ref, gkc_ref, kc_ref, vcT_ref):
    _mla_kv(ckv_ref[...], kr_ref[...], wkb_ref, wvb_ref, gkc_ref, None, kc_ref, vcT_ref)


def _cache_kv(ckv, kr_placed, lw, B, P):
    row_spec = lambda w: pl.BlockSpec((P, w), lambda i: (i, 0))
    return pl.pallas_call(
        _cache_kv_kernel,
        grid=(B,),
        in_specs=[row_spec(KV_RANK), row_spec(HEAD_PAD), _const_spec((KV_RANK, C_W)),
                  _const_spec((KV_RANK, MIX_C)), _const_spec((1, HEAD_PAD))],
        out_specs=[pl.BlockSpec((H_C, P, HEAD_PAD), lambda i: (0, i, 0)),
                   pl.BlockSpec((1, H_C, V_ROWS, P), lambda i: (i, 0, 0, 0))],
        out_shape=[jax.ShapeDtypeStruct((H_C, B * P, HEAD_PAD), BF16),
                   jax.ShapeDtypeStruct((B, H_C, V_ROWS, P), BF16)],
        compiler_params=_cparams(("arbitrary",)),
        name="cache_kv",
    )(ckv, kr_placed, lw["w_kb"], lw["w_vb"], lw["gkc"])


def _colmax(s):
    tk, tq = s.shape
    r = jnp.max(s.reshape(tk // ROW_GROUP, ROW_GROUP, tq), axis=0)
    return jnp.max(r, axis=0, keepdims=True)


def _attn_kernel(diff, has_ctx, G, chunks, tk, l_init, *refs):
    refs = list(refs)
    q_ref = refs.pop(0)
    if has_ctx:
        kc_ref, kl_ref, vc_ref, vl_ref = refs[:4]
        refs = refs[4:]
    else:
        kl_ref, vl_ref = refs[:2]
        kc_ref = vc_ref = None
        refs = refs[2:]
    if diff:
        lam_ref, g_ref = refs[:2]
        refs = refs[2:]
    o_ref, sa, sb, os_ref = refs
    tq = q_ref.shape[-1]
    uw = tq if diff else tq // 2
    bufs = (sa, sb)
    nk = len(chunks)
    dq = q_ref.shape[0] // G
    gpt = 1 if nk % 2 == 0 else (2 if G % 2 == 0 else G)
    ntrip = G // gpt

    def q_units(g):
        start = g * dq
        if not isinstance(start, int):
            start = pl.multiple_of(start, dq)
        qh = q_ref[pl.ds(start, dq), :]
        if diff:
            row = lax.broadcasted_iota(jnp.int32, qh.shape, 0)
            zero = jnp.zeros_like(qh)
            return [jnp.where(row < DA, qh, zero), jnp.where(row >= DA, qh, zero)]
        return [qh[:, :uw], qh[:, uw:]]

    def kslice(h, j):
        src, off = chunks[j]
        return (kc_ref if src == 0 else kl_ref)[h, pl.ds(off, tk), :]

    def vslice(h, j):
        src, off = chunks[j]
        return (vc_ref if src == 0 else vl_ref)[0, h, :, pl.ds(off, tk)]

    def qk(g, qs, j, buf):
        kj = kslice(g, j)
        for u in range(2):
            buf[u] = jnp.dot(kj, qs[u], preferred_element_type=F32)

    def softmax(buf, carry):
        ps, out = [], []
        for u in range(2):
            m, acc = carry[u]
            s = buf[u]
            m_new = jnp.maximum(m, _colmax(s))
            alpha = jnp.exp2(m - m_new)
            ps.append(jnp.exp2(s - m_new).astype(BF16))
            out.append((m_new, alpha * acc))
        return ps, out

    def pv(g, j, ps, st):
        vj = vslice(g, j)
        return [(st[u][0], st[u][1] + jnp.dot(vj, ps[u], preferred_element_type=F32)) for u in range(2)]

    def finish(g, res):
        os_ = [acc[:V_A] / acc[V_A:V_A + 1] for (_, acc) in res]
        if diff:
            lv = lam_ref[...]
            lam = (jnp.exp(jnp.sum(lv[0:1] * lv[1:2], axis=-1, keepdims=True))
                   - jnp.exp(jnp.sum(lv[2:3] * lv[3:4], axis=-1, keepdims=True)) + l_init)
            o = os_[0] - lam * os_[1]
            os_ref[g] = _rms(o, axis=0) * g_ref[...] * (1.0 - l_init)
        else:
            os_ref[g, :, 0:uw] = os_[0]
            os_ref[g, :, uw:tq] = os_[1]

    def trip(t, _):
        step = 0
        g = t * gpt
        qs = q_units(g)
        for gi in range(gpt):
            g = t * gpt + gi
            carry = [(jnp.full((1, uw), -jnp.inf, F32), jnp.zeros((V_ROWS, uw), F32)) for _ in range(2)]
            for j in range(nk):
                src, dst = bufs[step % 2], bufs[(step + 1) % 2]
                qs_next = qs
                if j + 1 < nk:
                    qk(g, qs, j + 1, dst)
                elif gi + 1 < gpt:
                    qs_next = q_units(g + 1)
                    qk(g + 1, qs_next, 0, dst)
                elif ntrip > 1:
                    g_next = jnp.minimum(g + 1, G - 1)
                    qk(g_next, q_units(g_next), 0, dst)
                ps, st = softmax(src, carry)
                carry = pv(g, j, ps, st)
                qs = qs_next
                step += 1
            finish(g, carry)
        return 0

    qk(0, q_units(0), 0, sa)
    if ntrip > 1:
        lax.fori_loop(0, ntrip, trip, 0)
    else:
        trip(0, 0)
    for c in range(G // 2):
        pair = jnp.concatenate([os_ref[2 * c], os_ref[2 * c + 1]], axis=0)
        o_ref[:, c * LANES:(c + 1) * LANES] = pair.T.astype(o_ref.dtype)


def _attention(qT, k_lat, v_lat, k_ctx=None, v_ctx=None, *, diff, tq, tk, lam_vecs=None, subln=None,
               l_init=0.0):
    B, H, _, T = v_lat.shape
    N = qT.shape[1]
    d = k_lat.shape[2]
    has_ctx = k_ctx is not None
    chunks = [(1, o) for o in range(0, T, tk)]
    nq = T // tq
    in_specs = [pl.BlockSpec((qT.shape[0], tq), lambda b, i: (0, b * nq + i))]
    args = [qT]
    if has_ctx:
        P = v_ctx.shape[3]
        chunks = [(0, o) for o in range(0, P, tk)] + chunks
        in_specs += [pl.BlockSpec((H, P, d), lambda b, i: (0, b, 0)),
                     pl.BlockSpec((H, T, d), lambda b, i: (0, b, 0)),
                     pl.BlockSpec((1, H, V_ROWS, P), lambda b, i: (b, 0, 0, 0)),
                     pl.BlockSpec((1, H, V_ROWS, T), lambda b, i: (b, 0, 0, 0))]
        args += [k_ctx, k_lat, v_ctx, v_lat]
    else:
        in_specs += [pl.BlockSpec((H, T, d), lambda b, i: (0, b, 0)),
                     pl.BlockSpec((1, H, V_ROWS, T), lambda b, i: (b, 0, 0, 0))]
        args += [k_lat, v_lat]
    if diff:
        in_specs += [pl.BlockSpec((4, DA), lambda b, i: (0, 0)),
                     pl.BlockSpec((V_A, 1), lambda b, i: (0, 0))]
        args += [lam_vecs, subln]
    uw = tq if diff else tq // 2
    return pl.pallas_call(
        functools.partial(_attn_kernel, diff, has_ctx, H, tuple(chunks), tk, l_init),
        grid=(B, nq),
        in_specs=in_specs,
        out_specs=pl.BlockSpec((tq, H * V_A), lambda b, i: (b * nq + i, 0)),
        out_shape=jax.ShapeDtypeStruct((N, H * V_A), BF16),
        scratch_shapes=[pltpu.VMEM((2, tk, uw), F32), pltpu.VMEM((2, tk, uw), F32),
                        pltpu.VMEM((H, V_A, tq), F32)],
        compiler_params=_cparams(("arbitrary", "arbitrary")),
        name="attn_diff" if diff else "attn_mla",
    )(*args)


def _post_head(x_ref, oa_ref, ob_ref, oc_ref, mod_ref, nffn_ref, wout_ref):
    y = (jnp.dot(oa_ref[...], wout_ref[0:MIX_A, :], preferred_element_type=F32)
         + jnp.dot(ob_ref[...], wout_ref[MIX_A:MIX_A + W_B, :], preferred_element_type=F32)
         + jnp.dot(oc_ref[...], wout_ref[MIX_A + W_B:, :], preferred_element_type=F32))
    x1 = x_ref[...] + mod_ref[0, 2:3, :] * y
    h2 = _rms(x1) * nffn_ref[...] * (1.0 + mod_ref[0, 4:5, :]) + mod_ref[0, 3:4, :]
    return x1, h2


def _swiglu(hb, wg, wu, wd):
    g = jnp.dot(hb, wg, preferred_element_type=F32)
    u = jnp.dot(hb, wu, preferred_element_type=F32)
    a = (g * jax.nn.sigmoid(g) * u).astype(BF16)
    return jnp.dot(a, wd, preferred_element_type=F32)


def _post_dense_kernel(fchunks, x_ref, oa_ref, ob_ref, oc_ref, mod_ref, nffn_ref, wout_ref,
                       wg_ref, wu_ref, wd_ref, o_ref):
    x1, h2 = _post_head(x_ref, oa_ref, ob_ref, oc_ref, mod_ref, nffn_ref, wout_ref)
    hb = h2.astype(BF16)
    acc = None
    for (f0, f1) in fchunks:
        part = _swiglu(hb, wg_ref[:, f0:f1], wu_ref[:, f0:f1], wd_ref[f0:f1, :])
        acc = part if acc is None else acc + part
    o_ref[...] = x1 + mod_ref[0, 5:6, :] * acc


def _post_dense(x, oa, ob, oc, mod, lw, *, tm, tiles_per_mod):
    N, D = x.shape
    FF = lw["wg"].shape[1]
    cut = (FF // 2 // MXU_DIM + 1) * MXU_DIM if FF > 2 * MXU_DIM else FF
    fchunks = ((0, cut), (cut, FF)) if cut < FF else ((0, FF),)
    row = lambda w: pl.BlockSpec((tm, w), lambda i: (i, 0))
    return pl.pallas_call(
        functools.partial(_post_dense_kernel, fchunks),
        grid=(N // tm,),
        in_specs=[row(D), row(MIX_A), row(W_B), row(MIX_C),
                  pl.BlockSpec((1, 6, D), lambda i: (i // tiles_per_mod, 0, 0)),
                  _const_spec((1, D)), _const_spec(lw["w_out"].shape), _const_spec((D, FF)),
                  _const_spec((D, FF)), _const_spec((FF, D))],
        out_specs=row(D),
        out_shape=jax.ShapeDtypeStruct((N, D), F32),
        compiler_params=_cparams(("arbitrary",)),
        name="post_dense",
    )(x, oa, ob, oc, mod, lw["norm_ffn"], lw["w_out"], lw["wg"], lw["wu"], lw["wd"])


TOK_SUB = 8
MOE_ROWS = 256
DISPATCH_TOKENS = 256
COMBINE_TOKENS = 256


def _to_token_tiles(x, ref):
    tm = x.shape[0]
    for a in range(TOK_SUB):
        ref[pl.ds(a, tm, stride=TOK_SUB), :] = x[:, a * LANES:(a + 1) * LANES]


def _from_token_tiles(ref, tm):
    return jnp.concatenate([ref[pl.ds(a, tm, stride=TOK_SUB), :] for a in range(TOK_SUB)], axis=-1)


def _router_kernel(x_ref, oa_ref, ob_ref, oc_ref, mod_ref, nffn_ref, wout_ref, rw_ref,
                   x1_ref, h2t_ref, route_ref):
    x1, h2 = _post_head(x_ref, oa_ref, ob_ref, oc_ref, mod_ref, nffn_ref, wout_ref)
    x1_ref[...] = x1
    _to_token_tiles(h2, h2t_ref)
    rw = rw_ref[...]
    rhi = rw.astype(BF16)
    rlo = (rw - rhi.astype(F32)).astype(BF16)
    hhi = h2.astype(BF16)
    hlo = (h2 - hhi.astype(F32)).astype(BF16)
    both = jnp.dot(hhi, jnp.concatenate([rhi, rlo], axis=-1), preferred_element_type=F32)
    logits = (both[:, :LANES] + both[:, LANES:]
              + jnp.dot(hlo, rhi, preferred_element_type=F32))
    lane = lax.broadcasted_iota(jnp.int32, logits.shape, 1)
    neg = -jnp.inf
    lg = jnp.where(lane < N_EXPERTS, logits, neg)
    v1 = jnp.max(lg, axis=-1, keepdims=True)
    i1 = jnp.min(jnp.where(lg == v1, lane, LANES), axis=-1, keepdims=True)
    lg2 = jnp.where(lane == i1, neg, lg)
    v2 = jnp.max(lg2, axis=-1, keepdims=True)
    i2 = jnp.min(jnp.where(lg2 == v2, lane, LANES), axis=-1, keepdims=True)
    e2 = jnp.exp(v2 - v1)
    g1 = 1.0 / (1.0 + e2)
    g2 = e2 / (1.0 + e2)
    route_ref[...] = jnp.where(lane == 0, g1, jnp.where(lane == 1, g2, jnp.where(
        lane == 2, i1.astype(F32), jnp.where(lane == 3, i2.astype(F32), 0.0))))


def _router(x, oa, ob, oc, mod, lw, *, tm, tiles_per_mod):
    N, D = x.shape
    assert D == TOK_SUB * LANES
    row = lambda w: pl.BlockSpec((tm, w), lambda i: (i, 0))
    return pl.pallas_call(
        _router_kernel,
        grid=(N // tm,),
        in_specs=[row(D), row(MIX_A), row(W_B), row(MIX_C),
                  pl.BlockSpec((1, 6, D), lambda i: (i // tiles_per_mod, 0, 0)),
                  _const_spec((1, D)), _const_spec(lw["w_out"].shape), _const_spec((D, LANES))],
        out_specs=[row(D), pl.BlockSpec((tm * TOK_SUB, LANES), lambda i: (i, 0)), row(LANES)],
        out_shape=[jax.ShapeDtypeStruct((N, D), F32), jax.ShapeDtypeStruct((N * TOK_SUB, LANES), F32),
                   jax.ShapeDtypeStruct((N, LANES), F32)],
        compiler_params=_cparams(("arbitrary",)),
        name="moe_router",
    )(x, oa, ob, oc, mod, lw["norm_ffn"], lw["w_out"], lw["router"])


def _moe_plan(route, N):
    R, E = MOE_ROWS, N_EXPERTS
    es = route[:, 2:4].astype(jnp.int32).reshape(-1)
    oh = (es[:, None] == jnp.arange(E, dtype=jnp.int32)[None, :]).astype(jnp.int32)
    rank = jnp.sum((jnp.cumsum(oh, axis=0) - oh) * oh, axis=1)
    cnt = jnp.sum(oh, axis=0)
    gsz = ((cnt + R - 1) // R) * R
    gend = jnp.cumsum(gsz)
    pos = ((gend - gsz)[es] + rank).astype(jnp.int32)
    n_tiles = (2 * N) // R + E
    tile_start = jnp.arange(n_tiles, dtype=jnp.int32) * R
    tile_expert = jnp.minimum(jnp.sum(tile_start[:, None] >= gend[None, :], axis=1), E - 1).astype(jnp.int32)
    return pos, tile_expert


def _tile_rows(ref, row):
    return ref.at[pl.ds(row * TOK_SUB, TOK_SUB), :]


def _dispatch_kernel(pos_ref, h2t_ref, xs_in_hbm, xs_hbm, buf, sem):
    del xs_in_hbm
    i = pl.program_id(0)
    n = pl.num_programs(0)
    tmd = DISPATCH_TOKENS
    slot = i % 2

    def step_wait(buf_slot):
        for _ in range(2):
            pltpu.make_async_copy(buf.at[buf_slot], xs_hbm.at[pl.ds(0, tmd * TOK_SUB), :],
                                  sem.at[buf_slot]).wait()

    @pl.when(i >= 2)
    def _():
        step_wait(slot)

    buf[slot] = h2t_ref[...]
    for r in range(tmd):
        t = i * tmd + r
        for k in range(2):
            pltpu.make_async_copy(_tile_rows(buf.at[slot], r), _tile_rows(xs_hbm, pos_ref[2 * t + k]),
                                  sem.at[slot]).start(priority=k)

    @pl.when(i == n - 1)
    def _():
        step_wait(slot)

        @pl.when(n >= 2)
        def _():
            step_wait(1 - slot)


def _dispatch(h2t, pos, n_rows):
    N = pos.shape[0] // 2
    xs0 = jnp.zeros((n_rows * TOK_SUB, LANES), F32)
    tile = DISPATCH_TOKENS * TOK_SUB
    grid_spec = pltpu.PrefetchScalarGridSpec(
        num_scalar_prefetch=1,
        grid=(N // DISPATCH_TOKENS,),
        in_specs=[pl.BlockSpec((tile, LANES), lambda i, pos: (i, 0)), pl.BlockSpec(memory_space=pl.ANY)],
        out_specs=pl.BlockSpec(memory_space=pl.ANY),
        scratch_shapes=[pltpu.VMEM((2, tile, LANES), F32), pltpu.SemaphoreType.DMA((2,))],
    )
    return pl.pallas_call(
        _dispatch_kernel,
        grid_spec=grid_spec,
        out_shape=jax.ShapeDtypeStruct(xs0.shape, F32),
        input_output_aliases={2: 0},
        compiler_params=_cparams(("arbitrary",)),
        name="moe_dispatch",
    )(pos, h2t, xs0)


def _experts_kernel(te_ref, x_ref, wg_ref, wu_ref, wd_ref, y_ref):
    xb = _from_token_tiles(x_ref, MOE_ROWS).astype(BF16)
    _to_token_tiles(_swiglu(xb, wg_ref[0], wu_ref[0], wd_ref[0]), y_ref)


def _experts(xs, tile_expert, lw):
    E, D, FF = lw["wg"].shape
    R = MOE_ROWS
    n_tiles = tile_expert.shape[0]
    wspec = lambda shape: pl.BlockSpec((1,) + shape, lambda i, te: (te[i], 0, 0))
    rows = pl.BlockSpec((R * TOK_SUB, LANES), lambda i, te: (i, 0))
    grid_spec = pltpu.PrefetchScalarGridSpec(
        num_scalar_prefetch=1,
        grid=(n_tiles,),
        in_specs=[rows, wspec((D, FF)), wspec((D, FF)), wspec((FF, D))],
        out_specs=rows,
    )
    return pl.pallas_call(
        _experts_kernel,
        grid_spec=grid_spec,
        out_shape=jax.ShapeDtypeStruct(xs.shape, F32),
        compiler_params=_cparams(("arbitrary",)),
        name="moe_experts",
    )(tile_expert, xs, lw["wg"], lw["wu"], lw["wd"])


def _combine_kernel(pos_ref, x1_ref, ys_hbm, route_ref, mod_ref, o_ref, ybuf, sem):
    tm = x1_ref.shape[0]
    i = pl.program_id(0)
    n = pl.num_programs(0)
    slot = i % 2

    def issue(tile, buf_slot):
        for r in range(tm):
            t = tile * tm + r
            for k in range(2):
                pltpu.make_async_copy(_tile_rows(ys_hbm, pos_ref[2 * t + k]),
                                      _tile_rows(ybuf.at[buf_slot, k], r), sem.at[buf_slot]).start(priority=k)

    @pl.when(i == 0)
    def _():
        issue(0, 0)

    for k in range(2):
        pltpu.make_async_copy(ys_hbm.at[pl.ds(0, tm * TOK_SUB), :], ybuf.at[slot, k], sem.at[slot]).wait()

    @pl.when(i + 1 < n)
    def _():
        issue(i + 1, 1 - slot)

    g = route_ref[...]
    y = (g[:, 0:1] * _from_token_tiles(ybuf.at[slot, 0], tm)
         + g[:, 1:2] * _from_token_tiles(ybuf.at[slot, 1], tm))
    o_ref[...] = x1_ref[...] + mod_ref[0, 5:6, :] * y


def _combine(x1, ys, pos, route, mod, *, tm, tiles_per_mod):
    N, D = x1.shape
    row = lambda w: pl.BlockSpec((tm, w), lambda i, pos: (i, 0))
    grid_spec = pltpu.PrefetchScalarGridSpec(
        num_scalar_prefetch=1,
        grid=(N // tm,),
        in_specs=[row(D), pl.BlockSpec(memory_space=pl.ANY), row(LANES),
                  pl.BlockSpec((1, 6, D), lambda i, pos: (i // tiles_per_mod, 0, 0))],
        out_specs=row(D),
        scratch_shapes=[pltpu.VMEM((2, 2, tm * TOK_SUB, LANES), F32), pltpu.SemaphoreType.DMA((2,))],
    )
    return pl.pallas_call(
        _combine_kernel,
        grid_spec=grid_spec,
        out_shape=jax.ShapeDtypeStruct((N, D), F32),
        compiler_params=_cparams(("arbitrary",)),
        name="moe_combine",
    )(pos, x1, ys, route, mod)


def _post_moe(x, oa, ob, oc, mod, lw, *, tm, tiles_per_mod):
    N = x.shape[0]
    x1, h2t, route = _router(x, oa, ob, oc, mod, lw, tm=tm, tiles_per_mod=tiles_per_mod)
    pos, tile_expert = _moe_plan(route, N)
    xs = _dispatch(h2t, pos, tile_expert.shape[0] * MOE_ROWS)
    ys = _experts(xs, tile_expert, lw)
    tmc = min(COMBINE_TOKENS, N)
    return _combine(x1, ys, pos, route, mod, tm=tmc, tiles_per_mod=tiles_per_mod * (tm // tmc))


def _rope_tables(T):
    t = jnp.arange(T, dtype=jnp.int32)
    rows = (t // GRID_W).astype(F32)
    cols = (t % GRID_W).astype(F32)
    n = ROPE_C // 4
    inv = jnp.power(ROPE_THETA, -jnp.arange(n, dtype=F32) / n)
    j = jnp.arange(ROPE_C)
    pos = jnp.where(j[None, :] < ROPE_C // 2, rows[:, None], cols[:, None])
    ang = pos * inv[j % n][None, :]
    cos = jnp.cos(ang)
    sin = jnp.sin(ang)
    first = ((j % (2 * n)) < n)[None, :]
    s_up = jnp.where(first, -sin, 0.0)
    s_dn = jnp.where(first, 0.0, sin)
    tab32 = jnp.stack([cos, s_up, s_dn])
    tab_a = jnp.tile(tab32, (1, 1, LANES // ROPE_C))
    ident = jnp.stack([jnp.ones((T, NOPE_C), F32), jnp.zeros((T, NOPE_C), F32), jnp.zeros((T, NOPE_C), F32)])
    tail = jnp.stack([jnp.ones((T, HEAD_PAD - QK_C), F32), jnp.zeros((T, HEAD_PAD - QK_C), F32),
                      jnp.zeros((T, HEAD_PAD - QK_C), F32)])
    tab_c = jnp.concatenate([ident, tab32, tail], axis=-1)
    return tab_a, tab_c


def _pad_heads(w, width):
    lead = w.shape[:-1]
    w = w.reshape(lead + (H_C, width))
    w = jnp.pad(w, [(0, 0)] * len(lead) + [(0, 0), (0, HEAD_PAD - width)])
    return w.reshape(lead + (C_W,))


def _layer_weights(l, p):
    w_in = p["w_in"][l]
    D = w_in.shape[0]
    o_cq = 3 * A_W + 3 * W_B
    o_ckv = o_cq + Q_RANK
    o_ckr = o_ckv + KV_RANK
    w_ext = jnp.concatenate(
        [w_in[:, :o_cq], w_in[:, o_ckv:o_ckr], w_in[:, o_cq:o_ckv], w_in[:, o_ckr:],
         jnp.zeros((D, IN_EXT - OFF_CKR - ROPE_C), F32)], axis=1).astype(BF16)
    g = jnp.arange(MXU_DIM) // DA
    lw = {
        "w_in": w_ext,
        "b32": (g[:, None] == g[None, :]).astype(BF16),
        "norm_mix": p["norm_mix"][l][None, :],
        "norm_ffn": p["norm_ffn"][l][None, :],
        "gqk": jnp.concatenate([jnp.tile(p["qnorm_a"][l], 2 * H_A), jnp.tile(p["knorm_a"][l], 2 * H_A)])[None, :],
        "gqa": p["norm_qa"][l][None, :],
        "gkva": p["norm_kva"][l][None, :],
        "gqc": jnp.pad(p["qnorm_c"][l], (0, HEAD_PAD - QK_C))[None, :],
        "gkc": jnp.pad(p["knorm_c"][l], (0, HEAD_PAD - QK_C))[None, :],
        "w_qb": _pad_heads(p["w_qb"][l], QK_C).astype(BF16),
        "w_kb": _pad_heads(p["w_kb"][l], NOPE_C).astype(BF16),
        "w_vb": p["w_vb"][l].astype(BF16),
        "conv_w": p["conv_w"][l],
        "conv_b": p["conv_b"][l][None, :],
        "w_out": p["w_out"][l].astype(BF16),
        "lam": jnp.stack([p["lambda_q1"][l], p["lambda_k1"][l], p["lambda_q2"][l], p["lambda_k2"][l]]),
        "subln": p["subln_a"][l][:, None],
    }
    i = l // 2
    if l % 2 == 0:
        lw["wg"] = p["ffn_w_gate"][i].astype(BF16)
        lw["wu"] = p["ffn_w_up"][i].astype(BF16)
        lw["wd"] = p["ffn_w_down"][i].astype(BF16)
    else:
        lw["router"] = jnp.pad(p["router_w"][i], ((0, 0), (0, LANES - N_EXPERTS)))
        lw["wg"] = p["moe_w_gate"][i].astype(BF16)
        lw["wu"] = p["moe_w_up"][i].astype(BF16)
        lw["wd"] = p["moe_w_down"][i].astype(BF16)
    return lw


def _cached_vT(v, B, P):
    vT = v.transpose(0, 2, 3, 1).astype(BF16)
    ones = jnp.ones((B, H_A, 1, P), BF16)
    zeros = jnp.zeros((B, H_A, V_ROWS - V_A - 1, P), BF16)
    return jnp.concatenate([vT, ones, zeros], axis=2)


def _mixer(x, mod, lw, l_init, *, B, T, ctx, rope):
    is_ctx = ctx is None
    tm = T if is_ctx else min(512, T)
    seg = T if is_ctx else GRID_W
    nsub = 1 if is_ctx else max(1, tm // 256)
    outs = _pre_mix(x, mod, lw, rope, B=B, T=T, is_ctx=is_ctx, tm=tm, seg=seg, nsub=nsub)
    qaT, ka, vaT, ob, qcT, kc, vcT = outs[:7]
    tq = min(256, T)
    tq_c = min(2 * tq, T)
    if is_ctx:
        tk = 512 if T % 512 == 0 else 256
        o_a = _attention(qaT, ka, vaT, diff=True, tq=tq, tk=tk, lam_vecs=lw["lam"], subln=lw["subln"],
                         l_init=l_init)
        o_c = _attention(qcT, kc, vcT, diff=False, tq=tq_c, tk=tk)
    else:
        ctx_k, ctx_v, ctx_ckv, ctx_kr = ctx
        P = ctx_k.shape[1]
        tk = 512 if (P % 512 == 0 and T % 512 == 0) else 256
        kr_placed = jnp.pad(ctx_kr.reshape(B * P, ROPE_C), ((0, 0), (NOPE_C, HEAD_PAD - QK_C)))
        kc_ctx, vc_ctx = _cache_kv(ctx_ckv.reshape(B * P, KV_RANK), kr_placed, lw, B, P)
        ka_ctx = ctx_k.reshape(B * P, H_A, 2 * DA).transpose(1, 0, 2).astype(BF16)
        o_a = _attention(qaT, ka, vaT, ka_ctx, _cached_vT(ctx_v, B, P), diff=True, tq=tq, tk=tk,
                         lam_vecs=lw["lam"], subln=lw["subln"], l_init=l_init)
        o_c = _attention(qcT, kc, vcT, kc_ctx, vc_ctx, diff=False, tq=tq_c, tk=tk)
    return (o_a, ob, o_c), outs[7:]


def kernel(x_prompt, x_sample, cache_diff_k, cache_diff_v, cache_mla_ckv, cache_mla_krope, c, c_ctx, w_ada, b_ada, norm_mix, norm_ffn, w_in, qnorm_a, knorm_a, lambda_q1, lambda_k1, lambda_q2, lambda_k2, subln_a, conv_w, conv_b, norm_qa, w_qb, norm_kva, w_kb, w_vb, qnorm_c, knorm_c, w_out, ffn_w_gate, ffn_w_up, ffn_w_down, router_w, moe_w_gate, moe_w_up, moe_w_down):
    p = dict(norm_mix=norm_mix, norm_ffn=norm_ffn, w_in=w_in, qnorm_a=qnorm_a, knorm_a=knorm_a,
             lambda_q1=lambda_q1, lambda_k1=lambda_k1, lambda_q2=lambda_q2, lambda_k2=lambda_k2,
             subln_a=subln_a, conv_w=conv_w, conv_b=conv_b, norm_qa=norm_qa, w_qb=w_qb,
             norm_kva=norm_kva, w_kb=w_kb, w_vb=w_vb, qnorm_c=qnorm_c, knorm_c=knorm_c, w_out=w_out,
             ffn_w_gate=ffn_w_gate, ffn_w_up=ffn_w_up, ffn_w_down=ffn_w_down, router_w=router_w,
             moe_w_gate=moe_w_gate, moe_w_up=moe_w_up, moe_w_down=moe_w_down)
    Bc, Sc, D = x_prompt.shape
    Bl, T, _ = x_sample.shape
    L = w_in.shape[0]
    nrow = 16
    cvec = jnp.concatenate([c, c_ctx[None, :], jnp.zeros((nrow - Bl - 1, D), F32)], axis=0)
    mod = _ada(cvec, w_ada, b_ada)
    rope = _rope_tables(T)
    xp = x_prompt.reshape(Bc * Sc, D)
    xs = x_sample.reshape(Bl * T, D)
    st = [[], [], [], []]
    for l in range(L):
        lw = _layer_weights(l, p)
        l_init = 0.8 - 0.6 * math.exp(-0.3 * l)
        mod_lat = mod[l, :Bl].reshape(Bl, 6, D)
        mod_ctx = mod[l, Bl:Bl + 1].reshape(1, 6, D)
        post = _post_dense if l % 2 == 0 else _post_moe
        heads, state = _mixer(xp, mod_ctx, lw, l_init, B=Bc, T=Sc, ctx=None, rope=None)
        for acc, s in zip(st, state):
            acc.append(s)
        xp = post(xp, *heads, mod_ctx, lw, tm=min(512, Bc * Sc), tiles_per_mod=Bc * Sc)
        ctx = (cache_diff_k[:, l], cache_diff_v[:, l], cache_mla_ckv[:, l], cache_mla_krope[:, l])
        heads, _ = _mixer(xs, mod_lat, lw, l_init, B=Bl, T=T, ctx=ctx, rope=rope)
        tm = min(512, T)
        xs = post(xs, *heads, mod_lat, lw, tm=tm, tiles_per_mod=T // tm)
    new_k = jnp.stack(st[0], axis=1).reshape(Bc, Sc, L, H_A, 2, DA).transpose(0, 2, 1, 3, 4, 5)
    new_v = jnp.stack(st[1], axis=1).reshape(Bc, Sc, L, H_A, V_A).transpose(0, 2, 1, 3, 4)
    new_ckv = jnp.stack(st[2], axis=1).reshape(Bc, Sc, L, KV_RANK).transpose(0, 2, 1, 3)
    new_kr = jnp.stack(st[3], axis=1).reshape(Bc, Sc, L, ROPE_C).transpose(0, 2, 1, 3)
    return (xp.reshape(Bc, Sc, D), xs.reshape(Bl, T, D), new_k, new_v, new_ckv, new_kr)
```

```python
import functools
import math

import jax
import jax.numpy as jnp
from jax import lax
from jax.experimental import pallas as pl
from jax.experimental.pallas import tpu as pltpu

F32 = jnp.float32
BF16 = jnp.bfloat16

GRID_W = 64
H_A = 6
DA = 32
V_A = 2 * DA
W_B = 256
H_C = 6
NOPE_C = 64
ROPE_C = 32
QK_C = NOPE_C + ROPE_C
V_C = 64
Q_RANK = 192
KV_RANK = 128
N_EXPERTS = 8
ROPE_THETA = 10000.0
EPS = 1e-6
LOG2E = 1.4426950408889634

LANES = 128
MXU_DIM = 256
HEAD_PAD = 128

A_W = H_A * 2 * DA
OFF_AQ = 0
OFF_AK = A_W
OFF_AV = 2 * A_W
OFF_BB = 3 * A_W
OFF_BC = OFF_BB + W_B
OFF_BX = OFF_BC + W_B
OFF_CKV = OFF_BX + W_B
OFF_CQ = OFF_CKV + KV_RANK
OFF_CKR = OFF_CQ + Q_RANK
KR_CHUNK = (OFF_CKR // LANES) * LANES
assert OFF_CKR - KR_CHUNK == NOPE_C
IN_EXT = KR_CHUNK + LANES
C_W = H_C * HEAD_PAD
MIX_A = H_A * V_A
MIX_C = H_C * V_C

ROW_GROUP = 32
V_ROWS = V_A + 16
assert V_A == V_C

VMEM_LIMIT = 56 * 1024 * 1024


def _cparams(sem):
    return pltpu.CompilerParams(dimension_semantics=sem, vmem_limit_bytes=VMEM_LIMIT)


def _const_spec(shape):
    nd = len(shape)
    return pl.BlockSpec(shape, lambda *_: (0,) * nd, pipeline_mode=pl.Buffered(1))


def _rms(x, axis=-1):
    return x * lax.rsqrt(jnp.mean(x * x, axis=axis, keepdims=True) + EPS)


def _split_dot(x, w):
    hi = x.astype(BF16)
    lo = (x - hi.astype(F32)).astype(BF16)
    return (jnp.dot(hi, w, preferred_element_type=F32)
            + jnp.dot(lo, w, preferred_element_type=F32))


def _ada_kernel(c_ref, w_ref, b_ref, o_ref):
    c = c_ref[...]
    s = (c * jax.nn.sigmoid(c)).astype(BF16)
    o_ref[0] = jnp.dot(s, w_ref[0].astype(BF16), preferred_element_type=F32) + b_ref[0]


def _ada(cvec, w_ada, b_ada):
    L, D, N6 = w_ada.shape
    R = cvec.shape[0]
    tn = 1536
    assert N6 % tn == 0
    return pl.pallas_call(
        _ada_kernel,
        grid=(L, N6 // tn),
        in_specs=[
            pl.BlockSpec((R, D), lambda l, j: (0, 0)),
            pl.BlockSpec((1, D, tn), lambda l, j: (l, 0, j)),
            pl.BlockSpec((1, 1, tn), lambda l, j: (l, 0, j)),
        ],
        out_specs=pl.BlockSpec((1, R, tn), lambda l, j: (l, 0, j)),
        out_shape=jax.ShapeDtypeStruct((L, R, N6), F32),
        compiler_params=_cparams(("arbitrary", "arbitrary")),
        name="ada_mod",
    )(cvec, w_ada, b_ada.reshape(L, 1, N6))


def _rope(x, tab_ref):
    up = pltpu.roll(x, LANES - 8, 1)
    dn = pltpu.roll(x, 8, 1)
    return x * tab_ref[0] + up * tab_ref[1] + dn * tab_ref[2]


def _rope_T(xT, tab_ref, starts):
    nb = 8
    blocks = [xT[nb * i:nb * (i + 1)] for i in range(xT.shape[0] // nb)]
    for g in starts:
        for k in range(2):
            c, s = tab_ref[2 * k], tab_ref[2 * k + 1]
            b = g // nb + 2 * k
            x1, x2 = blocks[b], blocks[b + 1]
            blocks[b] = x1 * c - x2 * s
            blocks[b + 1] = x1 * s + x2 * c
    return jnp.concatenate(blocks, axis=0)


def _store_vT(v, ref):
    tm = v.shape[0]
    vT = v.T.astype(BF16)
    row = lax.broadcasted_iota(jnp.int32, (V_ROWS - V_A, tm), 0)
    tail = jnp.where(row == 0, 1.0, 0.0).astype(BF16)
    for h in range(v.shape[1] // V_A):
        ref[0, h, 0:V_A, :] = vT[h * V_A:(h + 1) * V_A, :]
        ref[0, h, V_A:V_ROWS, :] = tail


def _head_rms(x, gain_ref, ones_ref):
    sq = x * x
    ss = jnp.concatenate(
        [_split_dot(sq[:, c:c + MXU_DIM], ones_ref[...]) for c in range(0, x.shape[1], MXU_DIM)], axis=-1)
    gain = jnp.concatenate([gain_ref[...]] * (x.shape[1] // HEAD_PAD), axis=-1)
    return x * lax.rsqrt(ss * (1.0 / QK_C) + EPS) * gain


def _mla_kv(ckv_n, kr_placed, wkb_ref, wvb_ref, gkc_ref, ones_ref, rope_ref, kc_ref, vcT_ref):
    cb = ckv_n.astype(BF16)
    kn = jnp.dot(cb, wkb_ref[...], preferred_element_type=F32)
    vc = jnp.dot(cb, wvb_ref[...], preferred_element_type=F32)
    kall = _head_rms(kn + jnp.concatenate([kr_placed] * H_C, axis=-1), gkc_ref, ones_ref)
    for h in range(H_C):
        kp = kall[:, h * HEAD_PAD:(h + 1) * HEAD_PAD]
        if rope_ref is not None:
            kp = _rope(kp, rope_ref)
        kc_ref[h] = kp.astype(BF16)
    _store_vT(vc, vcT_ref)


def _pre_mix_kernel(is_ctx, seg, nsub, *refs):
    refs = list(refs)
    sub = refs[0].shape[0] // nsub
    row_dim = {0: 0}
    n_in = 15 if is_ctx else 18
    if not is_ctx:
        row_dim.update({15: 1, 16: 1, 17: 2})
    for k, dim in enumerate((1, 1, 3, 0, 1, 1, 3) + ((0, 0, 0, 0) if is_ctx else ())):
        row_dim[n_in + k] = dim
    x_ref, mod_ref, nmix_ref, win_ref = refs[:4]
    projs = []
    for r in range(nsub):
        x = x_ref[pl.ds(r * sub, sub), :]
        h = _rms(x) * nmix_ref[...] * (1.0 + mod_ref[0, 1:2, :]) + mod_ref[0, 0:1, :]
        projs.append(jnp.dot(h.astype(BF16), win_ref[...], preferred_element_type=F32))
    for r in range(nsub):
        views = []
        for pos, ref in enumerate(refs):
            if pos in row_dim:
                idx = [slice(None)] * len(ref.shape)
                idx[row_dim[pos]] = pl.ds(r * sub, sub)
                ref = ref.at[tuple(idx)]
            views.append(ref)
        _pre_mix_rows(is_ctx, seg, projs[r], *views)


def _pre_mix_rows(is_ctx, seg, proj, *refs):
    (_, _, _, _, ones_ref, gqk_ref, gqa_ref, gkva_ref, gqc_ref,
     gkc_ref, wqb_ref, wkb_ref, wvb_ref, cw_ref, cb_ref) = refs[:15]
    b32_ref, b128_ref = ones_ref.at[0], ones_ref.at[1]
    refs = refs[15:]
    if is_ctx:
        ra_ref = rc_ref = None
    else:
        ra_ref, rc_ref, rT_ref = refs[:3]
        refs = refs[3:]
    qaT_ref, ka_ref, vaT_ref, ob_ref, qcT_ref, kc_ref, vcT_ref = refs[:7]
    refs = refs[7:]

    aqk = proj[:, OFF_AQ:OFF_AV]
    sq = aqk * aqk
    ss = jnp.concatenate(
        [_split_dot(sq[:, c:c + MXU_DIM], b32_ref[...]) for c in range(0, 2 * A_W, MXU_DIM)], axis=-1)
    aqk = aqk * lax.rsqrt(ss * (1.0 / DA) + EPS) * gqk_ref[...]
    if is_ctx:
        ka_st_ref, va_st_ref, ckv_st_ref, ckr_st_ref = refs
        ka_st_ref[...] = aqk[:, A_W:]
    qaT = (aqk[:, :A_W] * (DA ** -0.5 * LOG2E)).T
    ak = aqk[:, A_W:]
    if not is_ctx:
        qaT = _rope_T(qaT, rT_ref, range(0, A_W, DA))
        ak = jnp.concatenate([_rope(ak[:, c:c + LANES], ra_ref) for c in range(0, A_W, LANES)], axis=-1)
    qaT_ref[...] = qaT.astype(BF16)
    for hh in range(H_A):
        ka_ref[hh] = ak[:, hh * 2 * DA:(hh + 1) * 2 * DA].astype(BF16)
    av = proj[:, OFF_AV:OFF_BB]
    _store_vT(av, vaT_ref)

    bb = proj[:, OFF_BB:OFF_BC]
    u = proj[:, OFF_BC:OFF_BX] * proj[:, OFF_BX:OFF_CKV]
    tm = u.shape[0]
    row = lax.broadcasted_iota(jnp.int32, u.shape, 0) & (seg - 1)
    prev = jnp.where(row == 0, 0.0, pltpu.roll(u, 1, 0))
    nxt = jnp.where(row == seg - 1, 0.0, pltpu.roll(u, tm - 1, 0))
    conv = prev * cw_ref[0:1, :] + u * cw_ref[1:2, :] + nxt * cw_ref[2:3, :] + cb_ref[...]
    ob_ref[...] = (bb * conv).astype(BF16)

    cq = proj[:, OFF_CQ:OFF_CKR]
    cqn = (_rms(cq) * gqa_ref[...]).astype(BF16)
    qc = jnp.dot(cqn, wqb_ref[...], preferred_element_type=F32)
    qcT = (_head_rms(qc, gqc_ref, b128_ref) * (QK_C ** -0.5 * LOG2E)).T
    if not is_ctx:
        qcT = _rope_T(qcT, rT_ref, range(NOPE_C, C_W, HEAD_PAD))
    qcT_ref[...] = qcT.astype(BF16)
    ckv_n = _rms(proj[:, OFF_CKV:OFF_CQ]) * gkva_ref[...]
    krc = proj[:, KR_CHUNK:KR_CHUNK + LANES]
    lane = lax.broadcasted_iota(jnp.int32, krc.shape, 1)
    kr_placed = jnp.where((lane >= NOPE_C) & (lane < QK_C), krc, 0.0)
    _mla_kv(ckv_n, kr_placed, wkb_ref, wvb_ref, gkc_ref, b128_ref, rc_ref, kc_ref, vcT_ref)
    if is_ctx:
        va_st_ref[...] = av
        ckv_st_ref[...] = ckv_n
        ckr_st_ref[...] = krc[:, NOPE_C:QK_C]


def _pre_mix(x, mod, lw, rope, *, B, T, is_ctx, tm, seg, nsub):
    N, D = x.shape
    assert (tm // nsub) % seg == 0 and (tm // nsub) % LANES == 0
    nt = N // tm
    tps = T // tm
    row_spec = lambda w: pl.BlockSpec((tm, w), lambda i: (i, 0))
    in_specs = [
        row_spec(D),
        pl.BlockSpec((1, 6, D), (lambda i: (0, 0, 0)) if is_ctx else (lambda i: (i // tps, 0, 0))),
        _const_spec((1, D)),
        _const_spec((D, IN_EXT)),
        _const_spec((2, MXU_DIM, MXU_DIM)),
        _const_spec((1, 2 * A_W)),
        _const_spec((1, Q_RANK)),
        _const_spec((1, KV_RANK)),
        _const_spec((1, HEAD_PAD)),
        _const_spec((1, HEAD_PAD)),
        _const_spec((Q_RANK, C_W)),
        _const_spec((KV_RANK, C_W)),
        _const_spec((KV_RANK, MIX_C)),
        _const_spec((3, W_B)),
        _const_spec((1, W_B)),
    ]
    args = [x, mod, lw["norm_mix"], lw["w_in"], lw["b32"], lw["gqk"], lw["gqa"], lw["gkva"],
            lw["gqc"], lw["gkc"], lw["w_qb"], lw["w_kb"], lw["w_vb"], lw["conv_w"], lw["conv_b"]]
    if not is_ctx:
        rspec = pl.BlockSpec((3, tm, LANES), lambda i: (0, i % tps, 0))
        in_specs += [rspec, rspec, pl.BlockSpec((4, 8, tm), lambda i: (0, 0, i % tps))]
        args += list(rope)
    colT = lambda w: pl.BlockSpec((w, tm), lambda i: (0, i))
    vT_spec = pl.BlockSpec((1, H_A, V_ROWS, tm), lambda i: (i // tps, 0, 0, i % tps))
    out_specs = [colT(A_W), pl.BlockSpec((H_A, tm, 2 * DA), lambda i: (0, i, 0)), vT_spec, row_spec(W_B),
                 colT(C_W), pl.BlockSpec((H_C, tm, HEAD_PAD), lambda i: (0, i, 0)), vT_spec]
    out_shape = [jax.ShapeDtypeStruct((A_W, N), BF16), jax.ShapeDtypeStruct((H_A, N, 2 * DA), BF16),
                 jax.ShapeDtypeStruct((B, H_A, V_ROWS, T), BF16), jax.ShapeDtypeStruct((N, W_B), BF16),
                 jax.ShapeDtypeStruct((C_W, N), BF16), jax.ShapeDtypeStruct((H_C, N, HEAD_PAD), BF16),
                 jax.ShapeDtypeStruct((B, H_C, V_ROWS, T), BF16)]
    if is_ctx:
        st_w = [A_W, A_W, KV_RANK, ROPE_C]
        out_specs += [row_spec(w) for w in st_w]
        out_shape += [jax.ShapeDtypeStruct((N, w), F32) for w in st_w]
    return pl.pallas_call(
        functools.partial(_pre_mix_kernel, is_ctx, seg, nsub),
        grid=(nt,),
        in_specs=in_specs,
        out_specs=out_specs,
        out_shape=out_shape,
        compiler_params=_cparams(("arbitrary",)),
        name="pre_mix_ctx" if is_ctx else "pre_mix_lat",
    )(*args)


def _cache_kv_kernel(ckv_ref, kr_ref, wkb_ref, wvb_ref, gkc_ref, ones_ref, kc_ref, vcT_ref):
    _mla_kv(ckv_ref[...], kr_ref[...], wkb_ref, wvb_ref, gkc_ref, ones_ref.at[1], None, kc_ref, vcT_ref)


def _cache_kv(ckv, kr_placed, lw, B, P):
    row_spec = lambda w: pl.BlockSpec((P, w), lambda i: (i, 0))
    return pl.pallas_call(
        _cache_kv_kernel,
        grid=(B,),
        in_specs=[row_spec(KV_RANK), row_spec(HEAD_PAD), _const_spec((KV_RANK, C_W)),
                  _const_spec((KV_RANK, MIX_C)), _const_spec((1, HEAD_PAD)),
                  _const_spec((2, MXU_DIM, MXU_DIM))],
        out_specs=[pl.BlockSpec((H_C, P, HEAD_PAD), lambda i: (0, i, 0)),
                   pl.BlockSpec((1, H_C, V_ROWS, P), lambda i: (i, 0, 0, 0))],
        out_shape=[jax.ShapeDtypeStruct((H_C, B * P, HEAD_PAD), BF16),
                   jax.ShapeDtypeStruct((B, H_C, V_ROWS, P), BF16)],
        compiler_params=_cparams(("arbitrary",)),
        name="cache_kv",
    )(ckv, kr_placed, lw["w_kb"], lw["w_vb"], lw["gkc"], lw["b32"])


def _colmax(s):
    tk, tq = s.shape
    r = jnp.max(s.reshape(tk // ROW_GROUP, ROW_GROUP, tq), axis=0)
    return jnp.max(r, axis=0, keepdims=True)


def _attn_kernel(diff, has_ctx, G, chunks, tk, l_init, *refs):
    refs = list(refs)
    q_ref = refs.pop(0)
    if has_ctx:
        kc_ref, kl_ref, vc_ref, vl_ref = refs[:4]
        refs = refs[4:]
    else:
        kl_ref, vl_ref = refs[:2]
        kc_ref = vc_ref = None
        refs = refs[2:]
    if diff:
        lam_ref, g_ref = refs[:2]
        refs = refs[2:]
    o_ref, sa, sb, os_ref = refs
    tq = q_ref.shape[-1]
    uw = tq if diff else tq // 2
    bufs = (sa, sb)
    nk = len(chunks)
    dq = q_ref.shape[0] // G
    gpt = 1 if nk % 2 == 0 else (2 if G % 2 == 0 else G)
    ntrip = G // gpt

    def q_units(g):
        start = g * dq
        if not isinstance(start, int):
            start = pl.multiple_of(start, dq)
        qh = q_ref[pl.ds(start, dq), :]
        if diff:
            row = lax.broadcasted_iota(jnp.int32, qh.shape, 0)
            zero = jnp.zeros_like(qh)
            return [jnp.where(row < DA, qh, zero), jnp.where(row >= DA, qh, zero)]
        return [qh[:, :uw], qh[:, uw:]]

    def kslice(h, j):
        src, off = chunks[j]
        return (kc_ref if src == 0 else kl_ref)[h, pl.ds(off, tk), :]

    def vslice(h, j):
        src, off = chunks[j]
        return (vc_ref if src == 0 else vl_ref)[0, h, :, pl.ds(off, tk)]

    def qk(g, qs, j, buf):
        kj = kslice(g, j)
        for u in range(2):
            buf[u] = jnp.dot(kj, qs[u], preferred_element_type=F32)

    def softmax(buf, carry):
        ps, out = [], []
        for u in range(2):
            m, acc = carry[u]
            s = buf[u]
            m_new = jnp.maximum(m, _colmax(s))
            alpha = jnp.exp2(m - m_new)
            ps.append(jnp.exp2(s - m_new).astype(BF16))
            out.append((m_new, alpha * acc))
        return ps, out

    def pv(g, j, ps, st):
        vj = vslice(g, j)
        return [(st[u][0], st[u][1] + jnp.dot(vj, ps[u], preferred_element_type=F32)) for u in range(2)]

    def finish(g, res):
        os_ = [acc[:V_A] / acc[V_A:V_A + 1] for (_, acc) in res]
        if diff:
            lv = lam_ref[...]
            lam = (jnp.exp(jnp.sum(lv[0:1] * lv[1:2], axis=-1, keepdims=True))
                   - jnp.exp(jnp.sum(lv[2:3] * lv[3:4], axis=-1, keepdims=True)) + l_init)
            o = os_[0] - lam * os_[1]
            os_ref[g] = _rms(o, axis=0) * g_ref[...] * (1.0 - l_init)
        else:
            os_ref[g, :, 0:uw] = os_[0]
            os_ref[g, :, uw:tq] = os_[1]

    def trip(t, _):
        step = 0
        g = t * gpt
        qs = q_units(g)
        for gi in range(gpt):
            g = t * gpt + gi
            carry = [(jnp.full((1, uw), -jnp.inf, F32), jnp.zeros((V_ROWS, uw), F32)) for _ in range(2)]
            for j in range(nk):
                src, dst = bufs[step % 2], bufs[(step + 1) % 2]
                qs_next = qs
                if j + 1 < nk:
                    qk(g, qs, j + 1, dst)
                elif gi + 1 < gpt:
                    qs_next = q_units(g + 1)
                    qk(g + 1, qs_next, 0, dst)
                elif ntrip > 1:
                    g_next = jnp.minimum(g + 1, G - 1)
                    qk(g_next, q_units(g_next), 0, dst)
                ps, st = softmax(src, carry)
                carry = pv(g, j, ps, st)
                qs = qs_next
                step += 1
            finish(g, carry)
        return 0

    qk(0, q_units(0), 0, sa)
    if ntrip > 1:
        lax.fori_loop(0, ntrip, trip, 0)
    else:
        trip(0, 0)
    for c in range(G // 2):
        pair = jnp.concatenate([os_ref[2 * c], os_ref[2 * c + 1]], axis=0)
        o_ref[:, c * LANES:(c + 1) * LANES] = pair.T.astype(o_ref.dtype)


def _attention(qT, k_lat, v_lat, k_ctx=None, v_ctx=None, *, diff, tq, tk, lam_vecs=None, subln=None,
               l_init=0.0):
    B, H, _, T = v_lat.shape
    N = qT.shape[1]
    d = k_lat.shape[2]
    has_ctx = k_ctx is not None
    chunks = [(1, o) for o in range(0, T, tk)]
    nq = T // tq
    in_specs = [pl.BlockSpec((qT.shape[0], tq), lambda b, i: (0, b * nq + i))]
    args = [qT]
    if has_ctx:
        P = v_ctx.shape[3]
        chunks = [(0, o) for o in range(0, P, tk)] + chunks
        in_specs += [pl.BlockSpec((H, P, d), lambda b, i: (0, b, 0)),
                     pl.BlockSpec((H, T, d), lambda b, i: (0, b, 0)),
                     pl.BlockSpec((1, H, V_ROWS, P), lambda b, i: (b, 0, 0, 0)),
                     pl.BlockSpec((1, H, V_ROWS, T), lambda b, i: (b, 0, 0, 0))]
        args += [k_ctx, k_lat, v_ctx, v_lat]
    else:
        in_specs += [pl.BlockSpec((H, T, d), lambda b, i: (0, b, 0)),
                     pl.BlockSpec((1, H, V_ROWS, T), lambda b, i: (b, 0, 0, 0))]
        args += [k_lat, v_lat]
    if diff:
        in_specs += [pl.BlockSpec((4, DA), lambda b, i: (0, 0)),
                     pl.BlockSpec((V_A, 1), lambda b, i: (0, 0))]
        args += [lam_vecs, subln]
    uw = tq if diff else tq // 2
    return pl.pallas_call(
        functools.partial(_attn_kernel, diff, has_ctx, H, tuple(chunks), tk, l_init),
        grid=(B, nq),
        in_specs=in_specs,
        out_specs=pl.BlockSpec((tq, H * V_A), lambda b, i: (b * nq + i, 0)),
        out_shape=jax.ShapeDtypeStruct((N, H * V_A), BF16),
        scratch_shapes=[pltpu.VMEM((2, tk, uw), F32), pltpu.VMEM((2, tk, uw), F32),
                        pltpu.VMEM((H, V_A, tq), F32)],
        compiler_params=_cparams(("arbitrary", "arbitrary")),
        name="attn_diff" if diff else "attn_mla",
    )(*args)


def _post_head(x_ref, oa_ref, ob_ref, oc_ref, mod_ref, nffn_ref, wout_ref):
    y = (jnp.dot(oa_ref[...], wout_ref[0:MIX_A, :], preferred_element_type=F32)
         + jnp.dot(ob_ref[...], wout_ref[MIX_A:MIX_A + W_B, :], preferred_element_type=F32)
         + jnp.dot(oc_ref[...], wout_ref[MIX_A + W_B:, :], preferred_element_type=F32))
    x1 = x_ref[...] + mod_ref[0, 2:3, :] * y
    h2 = _rms(x1) * nffn_ref[...] * (1.0 + mod_ref[0, 4:5, :]) + mod_ref[0, 3:4, :]
    return x1, h2


def _swiglu(hb, wg, wu, wd):
    g = jnp.dot(hb, wg, preferred_element_type=F32)
    u = jnp.dot(hb, wu, preferred_element_type=F32)
    a = (g * jax.nn.sigmoid(g) * u).astype(BF16)
    return jnp.dot(a, wd, preferred_element_type=F32)


def _post_dense_kernel(fchunks, x_ref, oa_ref, ob_ref, oc_ref, mod_ref, nffn_ref, wout_ref,
                       wg_ref, wu_ref, wd_ref, o_ref):
    x1, h2 = _post_head(x_ref, oa_ref, ob_ref, oc_ref, mod_ref, nffn_ref, wout_ref)
    hb = h2.astype(BF16)
    acc = None
    for (f0, f1) in fchunks:
        part = _swiglu(hb, wg_ref[:, f0:f1], wu_ref[:, f0:f1], wd_ref[f0:f1, :])
        acc = part if acc is None else acc + part
    o_ref[...] = x1 + mod_ref[0, 5:6, :] * acc


def _post_dense(x, oa, ob, oc, mod, lw, *, tm, tiles_per_mod):
    N, D = x.shape
    FF = lw["wg"].shape[1]
    cut = (FF // 2 // MXU_DIM + 1) * MXU_DIM if FF > 2 * MXU_DIM else FF
    fchunks = ((0, cut), (cut, FF)) if cut < FF else ((0, FF),)
    row = lambda w: pl.BlockSpec((tm, w), lambda i: (i, 0))
    return pl.pallas_call(
        functools.partial(_post_dense_kernel, fchunks),
        grid=(N // tm,),
        in_specs=[row(D), row(MIX_A), row(W_B), row(MIX_C),
                  pl.BlockSpec((1, 6, D), lambda i: (i // tiles_per_mod, 0, 0)),
                  _const_spec((1, D)), _const_spec(lw["w_out"].shape), _const_spec((D, FF)),
                  _const_spec((D, FF)), _const_spec((FF, D))],
        out_specs=row(D),
        out_shape=jax.ShapeDtypeStruct((N, D), F32),
        compiler_params=_cparams(("arbitrary",)),
        name="post_dense",
    )(x, oa, ob, oc, mod, lw["norm_ffn"], lw["w_out"], lw["wg"], lw["wu"], lw["wd"])


TOK_SUB = 8
MOE_ROWS = 256
DISPATCH_TOKENS = 256
COMBINE_TOKENS = 256


def _to_token_tiles(x, ref):
    tm = x.shape[0]
    for a in range(TOK_SUB):
        ref[pl.ds(a, tm, stride=TOK_SUB), :] = x[:, a * LANES:(a + 1) * LANES]


def _from_token_tiles(ref, tm):
    return jnp.concatenate([ref[pl.ds(a, tm, stride=TOK_SUB), :] for a in range(TOK_SUB)], axis=-1)


def _router_kernel(x_ref, oa_ref, ob_ref, oc_ref, mod_ref, nffn_ref, wout_ref, rw_ref,
                   x1_ref, h2t_ref, route_ref):
    x1, h2 = _post_head(x_ref, oa_ref, ob_ref, oc_ref, mod_ref, nffn_ref, wout_ref)
    x1_ref[...] = x1
    _to_token_tiles(h2, h2t_ref)
    rw = rw_ref[...]
    rhi = rw.astype(BF16)
    rlo = (rw - rhi.astype(F32)).astype(BF16)
    hhi = h2.astype(BF16)
    hlo = (h2 - hhi.astype(F32)).astype(BF16)
    both = jnp.dot(hhi, jnp.concatenate([rhi, rlo], axis=-1), preferred_element_type=F32)
    logits = (both[:, :LANES] + both[:, LANES:]
              + jnp.dot(hlo, rhi, preferred_element_type=F32))
    lane = lax.broadcasted_iota(jnp.int32, logits.shape, 1)
    neg = -jnp.inf
    lg = jnp.where(lane < N_EXPERTS, logits, neg)
    v1 = jnp.max(lg, axis=-1, keepdims=True)
    i1 = jnp.min(jnp.where(lg == v1, lane, LANES), axis=-1, keepdims=True)
    lg2 = jnp.where(lane == i1, neg, lg)
    v2 = jnp.max(lg2, axis=-1, keepdims=True)
    i2 = jnp.min(jnp.where(lg2 == v2, lane, LANES), axis=-1, keepdims=True)
    e2 = jnp.exp(v2 - v1)
    g1 = 1.0 / (1.0 + e2)
    g2 = e2 / (1.0 + e2)
    route_ref[...] = jnp.where(lane == 0, g1, jnp.where(lane == 1, g2, jnp.where(
        lane == 2, i1.astype(F32), jnp.where(lane == 3, i2.astype(F32), 0.0))))


def _router(x, oa, ob, oc, mod, lw, *, tm, tiles_per_mod):
    N, D = x.shape
    assert D == TOK_SUB * LANES
    row = lambda w: pl.BlockSpec((tm, w), lambda i: (i, 0))
    return pl.pallas_call(
        _router_kernel,
        grid=(N // tm,),
        in_specs=[row(D), row(MIX_A), row(W_B), row(MIX_C),
                  pl.BlockSpec((1, 6, D), lambda i: (i // tiles_per_mod, 0, 0)),
                  _const_spec((1, D)), _const_spec(lw["w_out"].shape), _const_spec((D, LANES))],
        out_specs=[row(D), pl.BlockSpec((tm * TOK_SUB, LANES), lambda i: (i, 0)), row(LANES)],
        out_shape=[jax.ShapeDtypeStruct((N, D), F32), jax.ShapeDtypeStruct((N * TOK_SUB, LANES), F32),
                   jax.ShapeDtypeStruct((N, LANES), F32)],
        compiler_params=_cparams(("arbitrary",)),
        name="moe_router",
    )(x, oa, ob, oc, mod, lw["norm_ffn"], lw["w_out"], lw["router"])


def _moe_plan(route, N):
    R, E = MOE_ROWS, N_EXPERTS
    es = route[:, 2:4].astype(jnp.int32).reshape(-1)
    oh = (es[:, None] == jnp.arange(E, dtype=jnp.int32)[None, :]).astype(jnp.int32)
    rank = jnp.sum((jnp.cumsum(oh, axis=0) - oh) * oh, axis=1)
    cnt = jnp.sum(oh, axis=0)
    gsz = ((cnt + R - 1) // R) * R
    gend = jnp.cumsum(gsz)
    pos = ((gend - gsz)[es] + rank).astype(jnp.int32)
    n_tiles = (2 * N) // R + E
    tile_start = jnp.arange(n_tiles, dtype=jnp.int32) * R
    tile_expert = jnp.minimum(jnp.sum(tile_start[:, None] >= gend[None, :], axis=1), E - 1).astype(jnp.int32)
    return pos, tile_expert


def _tile_rows(ref, row):
    return ref.at[pl.ds(row * TOK_SUB, TOK_SUB), :]


def _dispatch_kernel(pos_ref, h2t_ref, xs_in_hbm, xs_hbm, buf, sem):
    del xs_in_hbm
    i = pl.program_id(0)
    n = pl.num_programs(0)
    tmd = DISPATCH_TOKENS
    slot = i % 2

    def step_wait(buf_slot):
        for _ in range(2):
            pltpu.make_async_copy(buf.at[buf_slot], xs_hbm.at[pl.ds(0, tmd * TOK_SUB), :],
                                  sem.at[buf_slot]).wait()

    @pl.when(i >= 2)
    def _():
        step_wait(slot)

    buf[slot] = h2t_ref[...]
    for r in range(tmd):
        t = i * tmd + r
        for k in range(2):
            pltpu.make_async_copy(_tile_rows(buf.at[slot], r), _tile_rows(xs_hbm, pos_ref[2 * t + k]),
                                  sem.at[slot]).start(priority=k)

    @pl.when(i == n - 1)
    def _():
        step_wait(slot)

        @pl.when(n >= 2)
        def _():
            step_wait(1 - slot)


def _dispatch(h2t, pos, n_rows):
    N = pos.shape[0] // 2
    xs0 = jnp.zeros((n_rows * TOK_SUB, LANES), F32)
    tile = DISPATCH_TOKENS * TOK_SUB
    grid_spec = pltpu.PrefetchScalarGridSpec(
        num_scalar_prefetch=1,
        grid=(N // DISPATCH_TOKENS,),
        in_specs=[pl.BlockSpec((tile, LANES), lambda i, pos: (i, 0)), pl.BlockSpec(memory_space=pl.ANY)],
        out_specs=pl.BlockSpec(memory_space=pl.ANY),
        scratch_shapes=[pltpu.VMEM((2, tile, LANES), F32), pltpu.SemaphoreType.DMA((2,))],
    )
    return pl.pallas_call(
        _dispatch_kernel,
        grid_spec=grid_spec,
        out_shape=jax.ShapeDtypeStruct(xs0.shape, F32),
        input_output_aliases={2: 0},
        compiler_params=_cparams(("arbitrary",)),
        name="moe_dispatch",
    )(pos, h2t, xs0)


def _experts_kernel(te_ref, x_ref, wg_ref, wu_ref, wd_ref, y_ref):
    xb = _from_token_tiles(x_ref, MOE_ROWS).astype(BF16)
    _to_token_tiles(_swiglu(xb, wg_ref[0], wu_ref[0], wd_ref[0]), y_ref)


def _experts(xs, tile_expert, lw):
    E, D, FF = lw["wg"].shape
    R = MOE_ROWS
    n_tiles = tile_expert.shape[0]
    wspec = lambda shape: pl.BlockSpec((1,) + shape, lambda i, te: (te[i], 0, 0))
    rows = pl.BlockSpec((R * TOK_SUB, LANES), lambda i, te: (i, 0))
    grid_spec = pltpu.PrefetchScalarGridSpec(
        num_scalar_prefetch=1,
        grid=(n_tiles,),
        in_specs=[rows, wspec((D, FF)), wspec((D, FF)), wspec((FF, D))],
        out_specs=rows,
    )
    return pl.pallas_call(
        _experts_kernel,
        grid_spec=grid_spec,
        out_shape=jax.ShapeDtypeStruct(xs.shape, F32),
        compiler_params=_cparams(("arbitrary",)),
        name="moe_experts",
    )(tile_expert, xs, lw["wg"], lw["wu"], lw["wd"])


def _combine_kernel(pos_ref, x1_ref, ys_hbm, route_ref, mod_ref, o_ref, ybuf, sem):
    tm = x1_ref.shape[0]
    i = pl.program_id(0)
    n = pl.num_programs(0)
    slot = i % 2

    def issue(tile, buf_slot):
        for r in range(tm):
            t = tile * tm + r
            for k in range(2):
                pltpu.make_async_copy(_tile_rows(ys_hbm, pos_ref[2 * t + k]),
                                      _tile_rows(ybuf.at[buf_slot, k], r), sem.at[buf_slot]).start(priority=k)

    @pl.when(i == 0)
    def _():
        issue(0, 0)

    for k in range(2):
        pltpu.make_async_copy(ys_hbm.at[pl.ds(0, tm * TOK_SUB), :], ybuf.at[slot, k], sem.at[slot]).wait()

    @pl.when(i + 1 < n)
    def _():
        issue(i + 1, 1 - slot)

    g = route_ref[...]
    y = (g[:, 0:1] * _from_token_tiles(ybuf.at[slot, 0], tm)
         + g[:, 1:2] * _from_token_tiles(ybuf.at[slot, 1], tm))
    o_ref[...] = x1_ref[...] + mod_ref[0, 5:6, :] * y


def _combine(x1, ys, pos, route, mod, *, tm, tiles_per_mod):
    N, D = x1.shape
    row = lambda w: pl.BlockSpec((tm, w), lambda i, pos: (i, 0))
    grid_spec = pltpu.PrefetchScalarGridSpec(
        num_scalar_prefetch=1,
        grid=(N // tm,),
        in_specs=[row(D), pl.BlockSpec(memory_space=pl.ANY), row(LANES),
                  pl.BlockSpec((1, 6, D), lambda i, pos: (i // tiles_per_mod, 0, 0))],
        out_specs=row(D),
        scratch_shapes=[pltpu.VMEM((2, 2, tm * TOK_SUB, LANES), F32), pltpu.SemaphoreType.DMA((2,))],
    )
    return pl.pallas_call(
        _combine_kernel,
        grid_spec=grid_spec,
        out_shape=jax.ShapeDtypeStruct((N, D), F32),
        compiler_params=_cparams(("arbitrary",)),
        name="moe_combine",
    )(pos, x1, ys, route, mod)


def _post_moe(x, oa, ob, oc, mod, lw, *, tm, tiles_per_mod):
    N = x.shape[0]
    x1, h2t, route = _router(x, oa, ob, oc, mod, lw, tm=tm, tiles_per_mod=tiles_per_mod)
    pos, tile_expert = _moe_plan(route, N)
    xs = _dispatch(h2t, pos, tile_expert.shape[0] * MOE_ROWS)
    ys = _experts(xs, tile_expert, lw)
    tmc = min(COMBINE_TOKENS, N)
    return _combine(x1, ys, pos, route, mod, tm=tmc, tiles_per_mod=tiles_per_mod * (tm // tmc))


def _rope_tables(T):
    t = jnp.arange(T, dtype=jnp.int32)
    rows = (t // GRID_W).astype(F32)
    cols = (t % GRID_W).astype(F32)
    n = ROPE_C // 4
    inv = jnp.power(ROPE_THETA, -jnp.arange(n, dtype=F32) / n)
    j = jnp.arange(ROPE_C)
    pos = jnp.where(j[None, :] < ROPE_C // 2, rows[:, None], cols[:, None])
    ang = pos * inv[j % n][None, :]
    cos = jnp.cos(ang)
    sin = jnp.sin(ang)
    first = ((j % (2 * n)) < n)[None, :]
    s_up = jnp.where(first, -sin, 0.0)
    s_dn = jnp.where(first, 0.0, sin)
    tab32 = jnp.stack([cos, s_up, s_dn])
    tab_a = jnp.tile(tab32, (1, 1, LANES // ROPE_C))
    ident = jnp.stack([jnp.ones((T, NOPE_C), F32), jnp.zeros((T, NOPE_C), F32), jnp.zeros((T, NOPE_C), F32)])
    tail = jnp.stack([jnp.ones((T, HEAD_PAD - QK_C), F32), jnp.zeros((T, HEAD_PAD - QK_C), F32),
                      jnp.zeros((T, HEAD_PAD - QK_C), F32)])
    tab_c = jnp.concatenate([ident, tab32, tail], axis=-1)
    ang_r = inv[:, None] * rows[None, :]
    ang_c = inv[:, None] * cols[None, :]
    tab_t = jnp.stack([jnp.cos(ang_r), jnp.sin(ang_r), jnp.cos(ang_c), jnp.sin(ang_c)])
    return tab_a, tab_c, tab_t


def _pad_heads(w, width):
    lead = w.shape[:-1]
    w = w.reshape(lead + (H_C, width))
    w = jnp.pad(w, [(0, 0)] * len(lead) + [(0, 0), (0, HEAD_PAD - width)])
    return w.reshape(lead + (C_W,))


def _layer_weights(l, p):
    w_in = p["w_in"][l]
    D = w_in.shape[0]
    o_cq = 3 * A_W + 3 * W_B
    o_ckv = o_cq + Q_RANK
    o_ckr = o_ckv + KV_RANK
    w_ext = jnp.concatenate(
        [w_in[:, :o_cq], w_in[:, o_ckv:o_ckr], w_in[:, o_cq:o_ckv], w_in[:, o_ckr:],
         jnp.zeros((D, IN_EXT - OFF_CKR - ROPE_C), F32)], axis=1).astype(BF16)
    g = jnp.arange(MXU_DIM) // DA
    lw = {
        "w_in": w_ext,
        "b32": jnp.stack([(g[:, None] == g[None, :]), (g[:, None] // 4 == g[None, :] // 4)]).astype(BF16),
        "norm_mix": p["norm_mix"][l][None, :],
        "norm_ffn": p["norm_ffn"][l][None, :],
        "gqk": jnp.concatenate([jnp.tile(p["qnorm_a"][l], 2 * H_A), jnp.tile(p["knorm_a"][l], 2 * H_A)])[None, :],
        "gqa": p["norm_qa"][l][None, :],
        "gkva": p["norm_kva"][l][None, :],
        "gqc": jnp.pad(p["qnorm_c"][l], (0, HEAD_PAD - QK_C))[None, :],
        "gkc": jnp.pad(p["knorm_c"][l], (0, HEAD_PAD - QK_C))[None, :],
        "w_qb": _pad_heads(p["w_qb"][l], QK_C).astype(BF16),
        "w_kb": _pad_heads(p["w_kb"][l], NOPE_C).astype(BF16),
        "w_vb": p["w_vb"][l].astype(BF16),
        "conv_w": p["conv_w"][l],
        "conv_b": p["conv_b"][l][None, :],
        "w_out": p["w_out"][l].astype(BF16),
        "lam": jnp.stack([p["lambda_q1"][l], p["lambda_k1"][l], p["lambda_q2"][l], p["lambda_k2"][l]]),
        "subln": p["subln_a"][l][:, None],
    }
    i = l // 2
    if l % 2 == 0:
        lw["wg"] = p["ffn_w_gate"][i].astype(BF16)
        lw["wu"] = p["ffn_w_up"][i].astype(BF16)
        lw["wd"] = p["ffn_w_down"][i].astype(BF16)
    else:
        lw["router"] = jnp.pad(p["router_w"][i], ((0, 0), (0, LANES - N_EXPERTS)))
        lw["wg"] = p["moe_w_gate"][i].astype(BF16)
        lw["wu"] = p["moe_w_up"][i].astype(BF16)
        lw["wd"] = p["moe_w_down"][i].astype(BF16)
    return lw


def _cached_vT(v, B, P):
    vT = v.transpose(0, 2, 3, 1).astype(BF16)
    ones = jnp.ones((B, H_A, 1, P), BF16)
    zeros = jnp.zeros((B, H_A, V_ROWS - V_A - 1, P), BF16)
    return jnp.concatenate([vT, ones, zeros], axis=2)


def _mixer(x, mod, lw, l_init, *, B, T, ctx, rope):
    is_ctx = ctx is None
    tm = T if is_ctx else min(512, T)
    seg = T if is_ctx else GRID_W
    nsub = 1 if is_ctx else max(1, tm // 256)
    outs = _pre_mix(x, mod, lw, rope, B=B, T=T, is_ctx=is_ctx, tm=tm, seg=seg, nsub=nsub)
    qaT, ka, vaT, ob, qcT, kc, vcT = outs[:7]
    tq = min(256, T)
    tq_c = min(2 * tq, T)
    if is_ctx:
        tk = 512 if T % 512 == 0 else 256
        o_a = _attention(qaT, ka, vaT, diff=True, tq=tq, tk=tk, lam_vecs=lw["lam"], subln=lw["subln"],
                         l_init=l_init)
        o_c = _attention(qcT, kc, vcT, diff=False, tq=tq_c, tk=tk)
    else:
        ctx_k, ctx_v, ctx_ckv, ctx_kr = ctx
        P = ctx_k.shape[1]
        tk = 512 if (P % 512 == 0 and T % 512 == 0) else 256
        kr_placed = jnp.pad(ctx_kr.reshape(B * P, ROPE_C), ((0, 0), (NOPE_C, HEAD_PAD - QK_C)))
        kc_ctx, vc_ctx = _cache_kv(ctx_ckv.reshape(B * P, KV_RANK), kr_placed, lw, B, P)
        ka_ctx = ctx_k.reshape(B * P, H_A, 2 * DA).transpose(1, 0, 2).astype(BF16)
        o_a = _attention(qaT, ka, vaT, ka_ctx, _cached_vT(ctx_v, B, P), diff=True, tq=tq, tk=tk,
                         lam_vecs=lw["lam"], subln=lw["subln"], l_init=l_init)
        o_c = _attention(qcT, kc, vcT, kc_ctx, vc_ctx, diff=False, tq=tq_c, tk=tk)
    return (o_a, ob, o_c), outs[7:]


def kernel(x_prompt, x_sample, cache_diff_k, cache_diff_v, cache_mla_ckv, cache_mla_krope, c, c_ctx, w_ada, b_ada, norm_mix, norm_ffn, w_in, qnorm_a, knorm_a, lambda_q1, lambda_k1, lambda_q2, lambda_k2, subln_a, conv_w, conv_b, norm_qa, w_qb, norm_kva, w_kb, w_vb, qnorm_c, knorm_c, w_out, ffn_w_gate, ffn_w_up, ffn_w_down, router_w, moe_w_gate, moe_w_up, moe_w_down):
    p = dict(norm_mix=norm_mix, norm_ffn=norm_ffn, w_in=w_in, qnorm_a=qnorm_a, knorm_a=knorm_a,
             lambda_q1=lambda_q1, lambda_k1=lambda_k1, lambda_q2=lambda_q2, lambda_k2=lambda_k2,
             subln_a=subln_a, conv_w=conv_w, conv_b=conv_b, norm_qa=norm_qa, w_qb=w_qb,
             norm_kva=norm_kva, w_kb=w_kb, w_vb=w_vb, qnorm_c=qnorm_c, knorm_c=knorm_c, w_out=w_out,
             ffn_w_gate=ffn_w_gate, ffn_w_up=ffn_w_up, ffn_w_down=ffn_w_down, router_w=router_w,
             moe_w_gate=moe_w_gate, moe_w_up=moe_w_up, moe_w_down=moe_w_down)
    Bc, Sc, D = x_prompt.shape
    Bl, T, _ = x_sample.shape
    L = w_in.shape[0]
    nrow = 16
    cvec = jnp.concatenate([c, c_ctx[None, :], jnp.zeros((nrow - Bl - 1, D), F32)], axis=0)
    mod = _ada(cvec, w_ada, b_ada)
    rope = _rope_tables(T)
    xp = x_prompt.reshape(Bc * Sc, D)
    xs = x_sample.reshape(Bl * T, D)
    st = [[], [], [], []]
    for l in range(L):
        lw = _layer_weights(l, p)
        l_init = 0.8 - 0.6 * math.exp(-0.3 * l)
        mod_lat = mod[l, :Bl].reshape(Bl, 6, D)
        mod_ctx = mod[l, Bl:Bl + 1].reshape(1, 6, D)
        post = _post_dense if l % 2 == 0 else _post_moe
        heads, state = _mixer(xp, mod_ctx, lw, l_init, B=Bc, T=Sc, ctx=None, rope=None)
        for acc, s in zip(st, state):
            acc.append(s)
        xp = post(xp, *heads, mod_ctx, lw, tm=min(512, Bc * Sc), tiles_per_mod=Bc * Sc)
        ctx = (cache_diff_k[:, l], cache_diff_v[:, l], cache_mla_ckv[:, l], cache_mla_krope[:, l])
        heads, _ = _mixer(xs, mod_lat, lw, l_init, B=Bl, T=T, ctx=ctx, rope=rope)
        tm = min(512, T)
        xs = post(xs, *heads, mod_lat, lw, tm=tm, tiles_per_mod=T // tm)
    new_k = jnp.stack(st[0], axis=1).reshape(Bc, Sc, L, H_A, 2, DA).transpose(0, 2, 1, 3, 4, 5)
    new_v = jnp.stack(st[1], axis=1).reshape(Bc, Sc, L, H_A, V_A).transpose(0, 2, 1, 3, 4)
    new_ckv = jnp.stack(st[2], axis=1).reshape(Bc, Sc, L, KV_RANK).transpose(0, 2, 1, 3)
    new_kr = jnp.stack(st[3], axis=1).reshape(Bc, Sc, L, ROPE_C).transpose(0, 2, 1, 3)
    return (xp.reshape(Bc, Sc, D), xs.reshape(Bl, T, D), new_k, new_v, new_ckv, new_kr)
```

```python
import functools
import math

import jax
import jax.numpy as jnp
from jax import lax
from jax.experimental import pallas as pl
from jax.experimental.pallas import tpu as pltpu

F32 = jnp.float32
BF16 = jnp.bfloat16

GRID_W = 64
H_A = 6
DA = 32
V_A = 2 * DA
W_B = 256
H_C = 6
NOPE_C = 64
ROPE_C = 32
QK_C = NOPE_C + ROPE_C
V_C = 64
Q_RANK = 192
KV_RANK = 128
N_EXPERTS = 8
ROPE_THETA = 10000.0
EPS = 1e-6
LOG2E = 1.4426950408889634

LANES = 128
MXU_DIM = 256
HEAD_PAD = 128

A_W = H_A * 2 * DA
OFF_AQ = 0
OFF_AK = A_W
OFF_AV = 2 * A_W
OFF_BB = 3 * A_W
OFF_BC = OFF_BB + W_B
OFF_BX = OFF_BC + W_B
OFF_CKV = OFF_BX + W_B
OFF_CQ = OFF_CKV + KV_RANK
OFF_CKR = OFF_CQ + Q_RANK
KR_CHUNK = (OFF_CKR // LANES) * LANES
assert OFF_CKR - KR_CHUNK == NOPE_C
IN_EXT = KR_CHUNK + LANES
C_W = H_C * HEAD_PAD
MIX_A = H_A * V_A
MIX_C = H_C * V_C

ROW_GROUP = 32
V_ROWS = V_A + 16
assert V_A == V_C

VMEM_LIMIT = 56 * 1024 * 1024


def _cparams(sem):
    return pltpu.CompilerParams(dimension_semantics=sem, vmem_limit_bytes=VMEM_LIMIT)


def _const_spec(shape):
    nd = len(shape)
    return pl.BlockSpec(shape, lambda *_: (0,) * nd, pipeline_mode=pl.Buffered(1))


def _rms(x, axis=-1):
    return x * lax.rsqrt(jnp.mean(x * x, axis=axis, keepdims=True) + EPS)


def _split_dot(x, w):
    hi = x.astype(BF16)
    lo = (x - hi.astype(F32)).astype(BF16)
    return (jnp.dot(hi, w, preferred_element_type=F32)
            + jnp.dot(lo, w, preferred_element_type=F32))


def _ada_kernel(c_ref, w_ref, b_ref, o_ref):
    c = c_ref[...]
    s = (c * jax.nn.sigmoid(c)).astype(BF16)
    o_ref[0] = jnp.dot(s, w_ref[0].astype(BF16), preferred_element_type=F32) + b_ref[0]


def _ada(cvec, w_ada, b_ada):
    L, D, N6 = w_ada.shape
    R = cvec.shape[0]
    tn = 1536
    assert N6 % tn == 0
    return pl.pallas_call(
        _ada_kernel,
        grid=(L, N6 // tn),
        in_specs=[
            pl.BlockSpec((R, D), lambda l, j: (0, 0)),
            pl.BlockSpec((1, D, tn), lambda l, j: (l, 0, j)),
            pl.BlockSpec((1, 1, tn), lambda l, j: (l, 0, j)),
        ],
        out_specs=pl.BlockSpec((1, R, tn), lambda l, j: (l, 0, j)),
        out_shape=jax.ShapeDtypeStruct((L, R, N6), F32),
        compiler_params=_cparams(("arbitrary", "arbitrary")),
        name="ada_mod",
    )(cvec, w_ada, b_ada.reshape(L, 1, N6))


def _rope(x, tab_ref):
    up = pltpu.roll(x, LANES - 8, 1)
    dn = pltpu.roll(x, 8, 1)
    return x * tab_ref[0] + up * tab_ref[1] + dn * tab_ref[2]


def _rope_T(xT, tab_ref, starts):
    nb = 8
    blocks = [xT[nb * i:nb * (i + 1)] for i in range(xT.shape[0] // nb)]
    for g in starts:
        for k in range(2):
            c, s = tab_ref[2 * k], tab_ref[2 * k + 1]
            b = g // nb + 2 * k
            x1, x2 = blocks[b], blocks[b + 1]
            blocks[b] = x1 * c - x2 * s
            blocks[b + 1] = x1 * s + x2 * c
    return jnp.concatenate(blocks, axis=0)


def _store_vT(v, ref):
    tm = v.shape[0]
    vT = v.T.astype(BF16)
    row = lax.broadcasted_iota(jnp.int32, (V_ROWS - V_A, tm), 0)
    tail = jnp.where(row == 0, 1.0, 0.0).astype(BF16)
    for h in range(v.shape[1] // V_A):
        ref[0, h, 0:V_A, :] = vT[h * V_A:(h + 1) * V_A, :]
        ref[0, h, V_A:V_ROWS, :] = tail


def _head_rms(x, gain_ref, ones_ref):
    sq = x * x
    ss = jnp.concatenate(
        [_split_dot(sq[:, c:c + MXU_DIM], ones_ref[...]) for c in range(0, x.shape[1], MXU_DIM)], axis=-1)
    gain = jnp.concatenate([gain_ref[...]] * (x.shape[1] // HEAD_PAD), axis=-1)
    return x * lax.rsqrt(ss * (1.0 / QK_C) + EPS) * gain


def _mla_kv(ckv_n, kr_placed, wkb_ref, wvb_ref, gkc_ref, ones_ref, rope_ref, kc_ref, vcT_ref):
    cb = ckv_n.astype(BF16)
    kn = jnp.dot(cb, wkb_ref[...], preferred_element_type=F32)
    vc = jnp.dot(cb, wvb_ref[...], preferred_element_type=F32)
    kall = _head_rms(kn + jnp.concatenate([kr_placed] * H_C, axis=-1), gkc_ref, ones_ref)
    for h in range(H_C):
        kp = kall[:, h * HEAD_PAD:(h + 1) * HEAD_PAD]
        if rope_ref is not None:
            kp = _rope(kp, rope_ref)
        kc_ref[h] = kp.astype(BF16)
    _store_vT(vc, vcT_ref)


def _pre_mix_kernel(is_ctx, seg, nsub, *refs):
    refs = list(refs)
    sub = refs[0].shape[0] // nsub
    row_dim = {0: 0}
    n_in = 15 if is_ctx else 18
    if not is_ctx:
        row_dim.update({15: 1, 16: 1, 17: 2})
    for k, dim in enumerate((1, 1, 3, 0, 1, 1, 3) + ((0, 0, 0, 0) if is_ctx else ())):
        row_dim[n_in + k] = dim
    x_ref, mod_ref, nmix_ref, win_ref = refs[:4]
    projs = []
    for r in range(nsub):
        x = x_ref[pl.ds(r * sub, sub), :]
        h = _rms(x) * nmix_ref[...] * (1.0 + mod_ref[0, 1:2, :]) + mod_ref[0, 0:1, :]
        projs.append(jnp.dot(h.astype(BF16), win_ref[...], preferred_element_type=F32))
    for r in range(nsub):
        views = []
        for pos, ref in enumerate(refs):
            if pos in row_dim:
                idx = [slice(None)] * len(ref.shape)
                idx[row_dim[pos]] = pl.ds(r * sub, sub)
                ref = ref.at[tuple(idx)]
            views.append(ref)
        _pre_mix_rows(is_ctx, seg, projs[r], *views)


def _pre_mix_rows(is_ctx, seg, proj, *refs):
    (_, _, _, _, ones_ref, gqk_ref, gqa_ref, gkva_ref, gqc_ref,
     gkc_ref, wqb_ref, wkb_ref, wvb_ref, cw_ref, cb_ref) = refs[:15]
    b32_ref, b128_ref = ones_ref.at[0], ones_ref.at[1]
    refs = refs[15:]
    if is_ctx:
        ra_ref = rc_ref = None
    else:
        ra_ref, rc_ref, rT_ref = refs[:3]
        refs = refs[3:]
    qaT_ref, ka_ref, vaT_ref, ob_ref, qcT_ref, kc_ref, vcT_ref = refs[:7]
    refs = refs[7:]

    aqk = proj[:, OFF_AQ:OFF_AV]
    sq = aqk * aqk
    ss = jnp.concatenate(
        [_split_dot(sq[:, c:c + MXU_DIM], b32_ref[...]) for c in range(0, 2 * A_W, MXU_DIM)], axis=-1)
    aqk = aqk * lax.rsqrt(ss * (1.0 / DA) + EPS) * gqk_ref[...]
    if is_ctx:
        ka_st_ref, va_st_ref, ckv_st_ref, ckr_st_ref = refs
        ka_st_ref[...] = aqk[:, A_W:]
    qaT = (aqk[:, :A_W] * (DA ** -0.5 * LOG2E)).T
    ak = aqk[:, A_W:]
    if not is_ctx:
        qaT = _rope_T(qaT, rT_ref, range(0, A_W, DA))
        ak = jnp.concatenate([_rope(ak[:, c:c + LANES], ra_ref) for c in range(0, A_W, LANES)], axis=-1)
    qaT_ref[...] = qaT.astype(BF16)
    for hh in range(H_A):
        ka_ref[hh] = ak[:, hh * 2 * DA:(hh + 1) * 2 * DA].astype(BF16)
    av = proj[:, OFF_AV:OFF_BB]
    _store_vT(av, vaT_ref)

    bb = proj[:, OFF_BB:OFF_BC]
    u = proj[:, OFF_BC:OFF_BX] * proj[:, OFF_BX:OFF_CKV]
    tm = u.shape[0]
    row = lax.broadcasted_iota(jnp.int32, u.shape, 0) & (seg - 1)
    prev = jnp.where(row == 0, 0.0, pltpu.roll(u, 1, 0))
    nxt = jnp.where(row == seg - 1, 0.0, pltpu.roll(u, tm - 1, 0))
    conv = prev * cw_ref[0:1, :] + u * cw_ref[1:2, :] + nxt * cw_ref[2:3, :] + cb_ref[...]
    ob_ref[...] = (bb * conv).astype(BF16)

    cq = proj[:, OFF_CQ:OFF_CKR]
    cqn = (_rms(cq) * gqa_ref[...]).astype(BF16)
    qc = jnp.dot(cqn, wqb_ref[...], preferred_element_type=F32)
    qcT = (_head_rms(qc, gqc_ref, b128_ref) * (QK_C ** -0.5 * LOG2E)).T
    if not is_ctx:
        qcT = _rope_T(qcT, rT_ref, range(NOPE_C, C_W, HEAD_PAD))
    qcT_ref[...] = qcT.astype(BF16)
    ckv_n = _rms(proj[:, OFF_CKV:OFF_CQ]) * gkva_ref[...]
    krc = proj[:, KR_CHUNK:KR_CHUNK + LANES]
    lane = lax.broadcasted_iota(jnp.int32, krc.shape, 1)
    kr_placed = jnp.where((lane >= NOPE_C) & (lane < QK_C), krc, 0.0)
    _mla_kv(ckv_n, kr_placed, wkb_ref, wvb_ref, gkc_ref, b128_ref, rc_ref, kc_ref, vcT_ref)
    if is_ctx:
        va_st_ref[...] = av
        ckv_st_ref[...] = ckv_n
        ckr_st_ref[...] = krc[:, NOPE_C:QK_C]


def _pre_mix(x, mod, lw, rope, *, B, T, is_ctx, tm, seg, nsub):
    N, D = x.shape
    assert (tm // nsub) % seg == 0 and (tm // nsub) % LANES == 0
    nt = N // tm
    tps = T // tm
    row_spec = lambda w: pl.BlockSpec((tm, w), lambda i: (i, 0))
    in_specs = [
        row_spec(D),
        pl.BlockSpec((1, 6, D), (lambda i: (0, 0, 0)) if is_ctx else (lambda i: (i // tps, 0, 0))),
        _const_spec((1, D)),
        _const_spec((D, IN_EXT)),
        _const_spec((2, MXU_DIM, MXU_DIM)),
        _const_spec((1, 2 * A_W)),
        _const_spec((1, Q_RANK)),
        _const_spec((1, KV_RANK)),
        _const_spec((1, HEAD_PAD)),
        _const_spec((1, HEAD_PAD)),
        _const_spec((Q_RANK, C_W)),
        _const_spec((KV_RANK, C_W)),
        _const_spec((KV_RANK, MIX_C)),
        _const_spec((3, W_B)),
        _const_spec((1, W_B)),
    ]
    args = [x, mod, lw["norm_mix"], lw["w_in"], lw["b32"], lw["gqk"], lw["gqa"], lw["gkva"],
            lw["gqc"], lw["gkc"], lw["w_qb"], lw["w_kb"], lw["w_vb"], lw["conv_w"], lw["conv_b"]]
    if not is_ctx:
        rspec = pl.BlockSpec((3, tm, LANES), lambda i: (0, i % tps, 0))
        in_specs += [rspec, rspec, pl.BlockSpec((4, 8, tm), lambda i: (0, 0, i % tps))]
        args += list(rope)
    colT = lambda w: pl.BlockSpec((w, tm), lambda i: (0, i))
    vT_spec = pl.BlockSpec((1, H_A, V_ROWS, tm), lambda i: (i // tps, 0, 0, i % tps))
    out_specs = [colT(A_W), pl.BlockSpec((H_A, tm, 2 * DA), lambda i: (0, i, 0)), vT_spec, row_spec(W_B),
                 colT(C_W), pl.BlockSpec((H_C, tm, HEAD_PAD), lambda i: (0, i, 0)), vT_spec]
    out_shape = [jax.ShapeDtypeStruct((A_W, N), BF16), jax.ShapeDtypeStruct((H_A, N, 2 * DA), BF16),
                 jax.ShapeDtypeStruct((B, H_A, V_ROWS, T), BF16), jax.ShapeDtypeStruct((N, W_B), BF16),
                 jax.ShapeDtypeStruct((C_W, N), BF16), jax.ShapeDtypeStruct((H_C, N, HEAD_PAD), BF16),
                 jax.ShapeDtypeStruct((B, H_C, V_ROWS, T), BF16)]
    if is_ctx:
        st_w = [A_W, A_W, KV_RANK, ROPE_C]
        out_specs += [row_spec(w) for w in st_w]
        out_shape += [jax.ShapeDtypeStruct((N, w), F32) for w in st_w]
    return pl.pallas_call(
        functools.partial(_pre_mix_kernel, is_ctx, seg, nsub),
        grid=(nt,),
        in_specs=in_specs,
        out_specs=out_specs,
        out_shape=out_shape,
        compiler_params=_cparams(("arbitrary",)),
        name="pre_mix_ctx" if is_ctx else "pre_mix_lat",
    )(*args)


def _cache_kv_kernel(ckv_ref, kr_ref, wkb_ref, wvb_ref, gkc_ref, ones_ref, kc_ref, vcT_ref):
    _mla_kv(ckv_ref[...], kr_ref[...], wkb_ref, wvb_ref, gkc_ref, ones_ref.at[1], None, kc_ref, vcT_ref)


def _cache_kv(ckv, kr_placed, lw, B, P):
    row_spec = lambda w: pl.BlockSpec((P, w), lambda i: (i, 0))
    return pl.pallas_call(
        _cache_kv_kernel,
        grid=(B,),
        in_specs=[row_spec(KV_RANK), row_spec(HEAD_PAD), _const_spec((KV_RANK, C_W)),
                  _const_spec((KV_RANK, MIX_C)), _const_spec((1, HEAD_PAD)),
                  _const_spec((2, MXU_DIM, MXU_DIM))],
        out_specs=[pl.BlockSpec((H_C, P, HEAD_PAD), lambda i: (0, i, 0)),
                   pl.BlockSpec((1, H_C, V_ROWS, P), lambda i: (i, 0, 0, 0))],
        out_shape=[jax.ShapeDtypeStruct((H_C, B * P, HEAD_PAD), BF16),
                   jax.ShapeDtypeStruct((B, H_C, V_ROWS, P), BF16)],
        compiler_params=_cparams(("arbitrary",)),
        name="cache_kv",
    )(ckv, kr_placed, lw["w_kb"], lw["w_vb"], lw["gkc"], lw["b32"])


def _colmax(s):
    tk, tq = s.shape
    r = jnp.max(s.reshape(tk // ROW_GROUP, ROW_GROUP, tq), axis=0)
    return jnp.max(r, axis=0, keepdims=True)


def _attn_kernel(diff, has_ctx, G, chunks, tk, l_init, *refs):
    refs = list(refs)
    q_ref = refs.pop(0)
    if has_ctx:
        kc_ref, kl_ref, vc_ref, vl_ref = refs[:4]
        refs = refs[4:]
    else:
        kl_ref, vl_ref = refs[:2]
        kc_ref = vc_ref = None
        refs = refs[2:]
    if diff:
        lam_ref, g_ref = refs[:2]
        refs = refs[2:]
    o_ref, sa, sb, os_ref = refs
    tq = q_ref.shape[-1]
    uw = tq if diff else tq // 2
    bufs = (sa, sb)
    nk = len(chunks)
    dq = q_ref.shape[0] // G
    gpt = 1 if nk % 2 == 0 else (2 if G % 2 == 0 else G)
    ntrip = G // gpt

    def q_units(g):
        start = g * dq
        if not isinstance(start, int):
            start = pl.multiple_of(start, dq)
        qh = q_ref[pl.ds(start, dq), :]
        if diff:
            row = lax.broadcasted_iota(jnp.int32, qh.shape, 0)
            zero = jnp.zeros_like(qh)
            return [jnp.where(row < DA, qh, zero), jnp.where(row >= DA, qh, zero)]
        return [qh[:, :uw], qh[:, uw:]]

    def kslice(h, j):
        src, off = chunks[j]
        return (kc_ref if src == 0 else kl_ref)[h, pl.ds(off, tk), :]

    def vslice(h, j):
        src, off = chunks[j]
        return (vc_ref if src == 0 else vl_ref)[0, h, :, pl.ds(off, tk)]

    def qk(g, qs, j, buf):
        kj = kslice(g, j)
        for u in range(2):
            buf[u] = jnp.dot(kj, qs[u], preferred_element_type=F32)

    def softmax(buf, carry):
        ps, out = [], []
        for u in range(2):
            m, acc = carry[u]
            s = buf[u]
            m_new = jnp.maximum(m, _colmax(s))
            alpha = jnp.exp2(m - m_new)
            ps.append(jnp.exp2(s - m_new).astype(BF16))
            out.append((m_new, alpha * acc))
        return ps, out

    def pv(g, j, ps, st):
        vj = vslice(g, j)
        return [(st[u][0], st[u][1] + jnp.dot(vj, ps[u], preferred_element_type=F32)) for u in range(2)]

    def finish(g, res):
        os_ = [acc[:V_A] / acc[V_A:V_A + 1] for (_, acc) in res]
        if diff:
            lv = lam_ref[...]
            lam = (jnp.exp(jnp.sum(lv[0:1] * lv[1:2], axis=-1, keepdims=True))
                   - jnp.exp(jnp.sum(lv[2:3] * lv[3:4], axis=-1, keepdims=True)) + l_init)
            o = os_[0] - lam * os_[1]
            os_ref[g] = _rms(o, axis=0) * g_ref[...] * (1.0 - l_init)
        else:
            os_ref[g, :, 0:uw] = os_[0]
            os_ref[g, :, uw:tq] = os_[1]

    def trip(t, _):
        step = 0
        g = t * gpt
        qs = q_units(g)
        for gi in range(gpt):
            g = t * gpt + gi
            carry = [(jnp.full((1, uw), -jnp.inf, F32), jnp.zeros((V_ROWS, uw), F32)) for _ in range(2)]
            for j in range(nk):
                src, dst = bufs[step % 2], bufs[(step + 1) % 2]
                qs_next = qs
                if j + 1 < nk:
                    qk(g, qs, j + 1, dst)
                elif gi + 1 < gpt:
                    qs_next = q_units(g + 1)
                    qk(g + 1, qs_next, 0, dst)
                elif ntrip > 1:
                    g_next = jnp.minimum(g + 1, G - 1)
                    qk(g_next, q_units(g_next), 0, dst)
                ps, st = softmax(src, carry)
                carry = pv(g, j, ps, st)
                qs = qs_next
                step += 1
            finish(g, carry)
        return 0

    qk(0, q_units(0), 0, sa)
    if ntrip > 1:
        lax.fori_loop(0, ntrip, trip, 0)
    else:
        trip(0, 0)
    for c in range(G // 2):
        pair = jnp.concatenate([os_ref[2 * c], os_ref[2 * c + 1]], axis=0)
        o_ref[:, c * LANES:(c + 1) * LANES] = pair.T.astype(o_ref.dtype)


def _attention(qT, k_lat, v_lat, k_ctx=None, v_ctx=None, *, diff, tq, tk, lam_vecs=None, subln=None,
               l_init=0.0):
    B, H, _, T = v_lat.shape
    N = qT.shape[1]
    d = k_lat.shape[2]
    has_ctx = k_ctx is not None
    chunks = [(1, o) for o in range(0, T, tk)]
    nq = T // tq
    in_specs = [pl.BlockSpec((qT.shape[0], tq), lambda b, i: (0, b * nq + i))]
    args = [qT]
    if has_ctx:
        P = v_ctx.shape[3]
        chunks = [(0, o) for o in range(0, P, tk)] + chunks
        in_specs += [pl.BlockSpec((H, P, d), lambda b, i: (0, b, 0)),
                     pl.BlockSpec((H, T, d), lambda b, i: (0, b, 0)),
                     pl.BlockSpec((1, H, V_ROWS, P), lambda b, i: (b, 0, 0, 0)),
                     pl.BlockSpec((1, H, V_ROWS, T), lambda b, i: (b, 0, 0, 0))]
        args += [k_ctx, k_lat, v_ctx, v_lat]
    else:
        in_specs += [pl.BlockSpec((H, T, d), lambda b, i: (0, b, 0)),
                     pl.BlockSpec((1, H, V_ROWS, T), lambda b, i: (b, 0, 0, 0))]
        args += [k_lat, v_lat]
    if diff:
        in_specs += [pl.BlockSpec((4, DA), lambda b, i: (0, 0)),
                     pl.BlockSpec((V_A, 1), lambda b, i: (0, 0))]
        args += [lam_vecs, subln]
    uw = tq if diff else tq // 2
    return pl.pallas_call(
        functools.partial(_attn_kernel, diff, has_ctx, H, tuple(chunks), tk, l_init),
        grid=(B, nq),
        in_specs=in_specs,
        out_specs=pl.BlockSpec((tq, H * V_A), lambda b, i: (b * nq + i, 0)),
        out_shape=jax.ShapeDtypeStruct((N, H * V_A), BF16),
        scratch_shapes=[pltpu.VMEM((2, tk, uw), F32), pltpu.VMEM((2, tk, uw), F32),
                        pltpu.VMEM((H, V_A, tq), F32)],
        compiler_params=_cparams(("arbitrary", "arbitrary")),
        name="attn_diff" if diff else "attn_mla",
    )(*args)


def _post_head(x_ref, oa_ref, ob_ref, oc_ref, mod_ref, nffn_ref, wout_ref):
    mix = jnp.concatenate([oa_ref[...], ob_ref[...], oc_ref[...]], axis=-1)
    y = jnp.dot(mix, wout_ref[...], preferred_element_type=F32)
    x1 = x_ref[...] + mod_ref[0, 2:3, :] * y
    h2 = _rms(x1) * nffn_ref[...] * (1.0 + mod_ref[0, 4:5, :]) + mod_ref[0, 3:4, :]
    return x1, h2


def _swiglu(hb, wg, wu, wd):
    g = jnp.dot(hb, wg, preferred_element_type=F32)
    u = jnp.dot(hb, wu, preferred_element_type=F32)
    a = (g * jax.nn.sigmoid(g) * u).astype(BF16)
    return jnp.dot(a, wd, preferred_element_type=F32)


def _post_dense_kernel(fchunks, x_ref, oa_ref, ob_ref, oc_ref, mod_ref, nffn_ref, wout_ref,
                       wg_ref, wu_ref, wd_ref, o_ref):
    x1, h2 = _post_head(x_ref, oa_ref, ob_ref, oc_ref, mod_ref, nffn_ref, wout_ref)
    hb = h2.astype(BF16)
    acc = None
    for (f0, f1) in fchunks:
        part = _swiglu(hb, wg_ref[:, f0:f1], wu_ref[:, f0:f1], wd_ref[f0:f1, :])
        acc = part if acc is None else acc + part
    o_ref[...] = x1 + mod_ref[0, 5:6, :] * acc


def _post_dense(x, oa, ob, oc, mod, lw, *, tm, tiles_per_mod):
    N, D = x.shape
    FF = lw["wg"].shape[1]
    cut = (FF // 2 // MXU_DIM + 1) * MXU_DIM if FF > 2 * MXU_DIM else FF
    fchunks = ((0, cut), (cut, FF)) if cut < FF else ((0, FF),)
    row = lambda w: pl.BlockSpec((tm, w), lambda i: (i, 0))
    return pl.pallas_call(
        functools.partial(_post_dense_kernel, fchunks),
        grid=(N // tm,),
        in_specs=[row(D), row(MIX_A), row(W_B), row(MIX_C),
                  pl.BlockSpec((1, 6, D), lambda i: (i // tiles_per_mod, 0, 0)),
                  _const_spec((1, D)), _const_spec(lw["w_out"].shape), _const_spec((D, FF)),
                  _const_spec((D, FF)), _const_spec((FF, D))],
        out_specs=row(D),
        out_shape=jax.ShapeDtypeStruct((N, D), F32),
        compiler_params=_cparams(("arbitrary",)),
        name="post_dense",
    )(x, oa, ob, oc, mod, lw["norm_ffn"], lw["w_out"], lw["wg"], lw["wu"], lw["wd"])


TOK_SUB = 8
MOE_ROWS = 256
DISPATCH_TOKENS = 256
COMBINE_TOKENS = 256


def _to_token_tiles(x, ref):
    tm = x.shape[0]
    for a in range(TOK_SUB):
        ref[pl.ds(a, tm, stride=TOK_SUB), :] = x[:, a * LANES:(a + 1) * LANES]


def _from_token_tiles(ref, tm):
    return jnp.concatenate([ref[pl.ds(a, tm, stride=TOK_SUB), :] for a in range(TOK_SUB)], axis=-1)


def _router_kernel(x_ref, oa_ref, ob_ref, oc_ref, mod_ref, nffn_ref, wout_ref, rw_ref,
                   x1_ref, h2t_ref, route_ref):
    x1, h2 = _post_head(x_ref, oa_ref, ob_ref, oc_ref, mod_ref, nffn_ref, wout_ref)
    x1_ref[...] = x1
    _to_token_tiles(h2, h2t_ref)
    rw = rw_ref[...]
    rhi = rw.astype(BF16)
    rlo = (rw - rhi.astype(F32)).astype(BF16)
    hhi = h2.astype(BF16)
    hlo = (h2 - hhi.astype(F32)).astype(BF16)
    both = jnp.dot(hhi, jnp.concatenate([rhi, rlo], axis=-1), preferred_element_type=F32)
    logits = (both[:, :LANES] + both[:, LANES:]
              + jnp.dot(hlo, rhi, preferred_element_type=F32))
    lane = lax.broadcasted_iota(jnp.int32, logits.shape, 1)
    neg = -jnp.inf
    lg = jnp.where(lane < N_EXPERTS, logits, neg)
    v1 = jnp.max(lg, axis=-1, keepdims=True)
    i1 = jnp.min(jnp.where(lg == v1, lane, LANES), axis=-1, keepdims=True)
    lg2 = jnp.where(lane == i1, neg, lg)
    v2 = jnp.max(lg2, axis=-1, keepdims=True)
    i2 = jnp.min(jnp.where(lg2 == v2, lane, LANES), axis=-1, keepdims=True)
    e2 = jnp.exp(v2 - v1)
    g1 = 1.0 / (1.0 + e2)
    g2 = e2 / (1.0 + e2)
    route_ref[...] = jnp.where(lane == 0, g1, jnp.where(lane == 1, g2, jnp.where(
        lane == 2, i1.astype(F32), jnp.where(lane == 3, i2.astype(F32), 0.0))))


def _router(x, oa, ob, oc, mod, lw, *, tm, tiles_per_mod):
    N, D = x.shape
    assert D == TOK_SUB * LANES
    row = lambda w: pl.BlockSpec((tm, w), lambda i: (i, 0))
    return pl.pallas_call(
        _router_kernel,
        grid=(N // tm,),
        in_specs=[row(D), row(MIX_A), row(W_B), row(MIX_C),
                  pl.BlockSpec((1, 6, D), lambda i: (i // tiles_per_mod, 0, 0)),
                  _const_spec((1, D)), _const_spec(lw["w_out"].shape), _const_spec((D, LANES))],
        out_specs=[row(D), pl.BlockSpec((tm * TOK_SUB, LANES), lambda i: (i, 0)), row(LANES)],
        out_shape=[jax.ShapeDtypeStruct((N, D), F32), jax.ShapeDtypeStruct((N * TOK_SUB, LANES), F32),
                   jax.ShapeDtypeStruct((N, LANES), F32)],
        compiler_params=_cparams(("arbitrary",)),
        name="moe_router",
    )(x, oa, ob, oc, mod, lw["norm_ffn"], lw["w_out"], lw["router"])


def _moe_plan(route, N):
    R, E = MOE_ROWS, N_EXPERTS
    es = route[:, 2:4].astype(jnp.int32).reshape(-1)
    oh = (es[:, None] == jnp.arange(E, dtype=jnp.int32)[None, :]).astype(jnp.int32)
    rank = jnp.sum((jnp.cumsum(oh, axis=0) - oh) * oh, axis=1)
    cnt = jnp.sum(oh, axis=0)
    gsz = ((cnt + R - 1) // R) * R
    gend = jnp.cumsum(gsz)
    pos = ((gend - gsz)[es] + rank).astype(jnp.int32)
    n_tiles = (2 * N) // R + E
    tile_start = jnp.arange(n_tiles, dtype=jnp.int32) * R
    tile_expert = jnp.minimum(jnp.sum(tile_start[:, None] >= gend[None, :], axis=1), E - 1).astype(jnp.int32)
    return pos, tile_expert


def _tile_rows(ref, row):
    return ref.at[pl.ds(row * TOK_SUB, TOK_SUB), :]


def _dispatch_kernel(pos_ref, h2t_ref, xs_in_hbm, xs_hbm, buf, sem):
    del xs_in_hbm
    i = pl.program_id(0)
    n = pl.num_programs(0)
    tmd = DISPATCH_TOKENS
    slot = i % 2

    def step_wait(buf_slot):
        for _ in range(2):
            pltpu.make_async_copy(buf.at[buf_slot], xs_hbm.at[pl.ds(0, tmd * TOK_SUB), :],
                                  sem.at[buf_slot]).wait()

    @pl.when(i >= 2)
    def _():
        step_wait(slot)

    buf[slot] = h2t_ref[...]
    for r in range(tmd):
        t = i * tmd + r
        for k in range(2):
            pltpu.make_async_copy(_tile_rows(buf.at[slot], r), _tile_rows(xs_hbm, pos_ref[2 * t + k]),
                                  sem.at[slot]).start(priority=k)

    @pl.when(i == n - 1)
    def _():
        step_wait(slot)

        @pl.when(n >= 2)
        def _():
            step_wait(1 - slot)


def _dispatch(h2t, pos, n_rows):
    N = pos.shape[0] // 2
    xs0 = jnp.zeros((n_rows * TOK_SUB, LANES), F32)
    tile = DISPATCH_TOKENS * TOK_SUB
    grid_spec = pltpu.PrefetchScalarGridSpec(
        num_scalar_prefetch=1,
        grid=(N // DISPATCH_TOKENS,),
        in_specs=[pl.BlockSpec((tile, LANES), lambda i, pos: (i, 0)), pl.BlockSpec(memory_space=pl.ANY)],
        out_specs=pl.BlockSpec(memory_space=pl.ANY),
        scratch_shapes=[pltpu.VMEM((2, tile, LANES), F32), pltpu.SemaphoreType.DMA((2,))],
    )
    return pl.pallas_call(
        _dispatch_kernel,
        grid_spec=grid_spec,
        out_shape=jax.ShapeDtypeStruct(xs0.shape, F32),
        input_output_aliases={2: 0},
        compiler_params=_cparams(("arbitrary",)),
        name="moe_dispatch",
    )(pos, h2t, xs0)


def _experts_kernel(te_ref, x_ref, wg_ref, wu_ref, wd_ref, y_ref):
    xb = _from_token_tiles(x_ref, MOE_ROWS).astype(BF16)
    _to_token_tiles(_swiglu(xb, wg_ref[0], wu_ref[0], wd_ref[0]), y_ref)


def _experts(xs, tile_expert, lw):
    E, D, FF = lw["wg"].shape
    R = MOE_ROWS
    n_tiles = tile_expert.shape[0]
    wspec = lambda shape: pl.BlockSpec((1,) + shape, lambda i, te: (te[i], 0, 0))
    rows = pl.BlockSpec((R * TOK_SUB, LANES), lambda i, te: (i, 0))
    grid_spec = pltpu.PrefetchScalarGridSpec(
        num_scalar_prefetch=1,
        grid=(n_tiles,),
        in_specs=[rows, wspec((D, FF)), wspec((D, FF)), wspec((FF, D))],
        out_specs=rows,
    )
    return pl.pallas_call(
        _experts_kernel,
        grid_spec=grid_spec,
        out_shape=jax.ShapeDtypeStruct(xs.shape, F32),
        compiler_params=_cparams(("arbitrary",)),
        name="moe_experts",
    )(tile_expert, xs, lw["wg"], lw["wu"], lw["wd"])


def _combine_kernel(pos_ref, x1_ref, ys_hbm, route_ref, mod_ref, o_ref, ybuf, sem):
    tm = x1_ref.shape[0]
    i = pl.program_id(0)
    n = pl.num_programs(0)
    slot = i % 2

    def issue(tile, buf_slot):
        for r in range(tm):
            t = tile * tm + r
            for k in range(2):
                pltpu.make_async_copy(_tile_rows(ys_hbm, pos_ref[2 * t + k]),
                                      _tile_rows(ybuf.at[buf_slot, k], r), sem.at[buf_slot]).start(priority=k)

    @pl.when(i == 0)
    def _():
        issue(0, 0)

    for k in range(2):
        pltpu.make_async_copy(ys_hbm.at[pl.ds(0, tm * TOK_SUB), :], ybuf.at[slot, k], sem.at[slot]).wait()

    @pl.when(i + 1 < n)
    def _():
        issue(i + 1, 1 - slot)

    g = route_ref[...]
    y = (g[:, 0:1] * _from_token_tiles(ybuf.at[slot, 0], tm)
         + g[:, 1:2] * _from_token_tiles(ybuf.at[slot, 1], tm))
    o_ref[...] = x1_ref[...] + mod_ref[0, 5:6, :] * y


def _combine(x1, ys, pos, route, mod, *, tm, tiles_per_mod):
    N, D = x1.shape
    row = lambda w: pl.BlockSpec((tm, w), lambda i, pos: (i, 0))
    grid_spec = pltpu.PrefetchScalarGridSpec(
        num_scalar_prefetch=1,
        grid=(N // tm,),
        in_specs=[row(D), pl.BlockSpec(memory_space=pl.ANY), row(LANES),
                  pl.BlockSpec((1, 6, D), lambda i, pos: (i // tiles_per_mod, 0, 0))],
        out_specs=row(D),
        scratch_shapes=[pltpu.VMEM((2, 2, tm * TOK_SUB, LANES), F32), pltpu.SemaphoreType.DMA((2,))],
    )
    return pl.pallas_call(
        _combine_kernel,
        grid_spec=grid_spec,
        out_shape=jax.ShapeDtypeStruct((N, D), F32),
        compiler_params=_cparams(("arbitrary",)),
        name="moe_combine",
    )(pos, x1, ys, route, mod)


def _post_moe(x, oa, ob, oc, mod, lw, *, tm, tiles_per_mod):
    N = x.shape[0]
    x1, h2t, route = _router(x, oa, ob, oc, mod, lw, tm=tm, tiles_per_mod=tiles_per_mod)
    pos, tile_expert = _moe_plan(route, N)
    xs = _dispatch(h2t, pos, tile_expert.shape[0] * MOE_ROWS)
    ys = _experts(xs, tile_expert, lw)
    tmc = min(COMBINE_TOKENS, N)
    return _combine(x1, ys, pos, route, mod, tm=tmc, tiles_per_mod=tiles_per_mod * (tm // tmc))


def _rope_tables(T):
    t = jnp.arange(T, dtype=jnp.int32)
    rows = (t // GRID_W).astype(F32)
    cols = (t % GRID_W).astype(F32)
    n = ROPE_C // 4
    inv = jnp.power(ROPE_THETA, -jnp.arange(n, dtype=F32) / n)
    j = jnp.arange(ROPE_C)
    pos = jnp.where(j[None, :] < ROPE_C // 2, rows[:, None], cols[:, None])
    ang = pos * inv[j % n][None, :]
    cos = jnp.cos(ang)
    sin = jnp.sin(ang)
    first = ((j % (2 * n)) < n)[None, :]
    s_up = jnp.where(first, -sin, 0.0)
    s_dn = jnp.where(first, 0.0, sin)
    tab32 = jnp.stack([cos, s_up, s_dn])
    tab_a = jnp.tile(tab32, (1, 1, LANES // ROPE_C))
    ident = jnp.stack([jnp.ones((T, NOPE_C), F32), jnp.zeros((T, NOPE_C), F32), jnp.zeros((T, NOPE_C), F32)])
    tail = jnp.stack([jnp.ones((T, HEAD_PAD - QK_C), F32), jnp.zeros((T, HEAD_PAD - QK_C), F32),
                      jnp.zeros((T, HEAD_PAD - QK_C), F32)])
    tab_c = jnp.concatenate([ident, tab32, tail], axis=-1)
    ang_r = inv[:, None] * rows[None, :]
    ang_c = inv[:, None] * cols[None, :]
    tab_t = jnp.stack([jnp.cos(ang_r), jnp.sin(ang_r), jnp.cos(ang_c), jnp.sin(ang_c)])
    return tab_a, tab_c, tab_t


def _pad_heads(w, width):
    lead = w.shape[:-1]
    w = w.reshape(lead + (H_C, width))
    w = jnp.pad(w, [(0, 0)] * len(lead) + [(0, 0), (0, HEAD_PAD - width)])
    return w.reshape(lead + (C_W,))


def _layer_weights(l, p):
    w_in = p["w_in"][l]
    D = w_in.shape[0]
    o_cq = 3 * A_W + 3 * W_B
    o_ckv = o_cq + Q_RANK
    o_ckr = o_ckv + KV_RANK
    w_ext = jnp.concatenate(
        [w_in[:, :o_cq], w_in[:, o_ckv:o_ckr], w_in[:, o_cq:o_ckv], w_in[:, o_ckr:],
         jnp.zeros((D, IN_EXT - OFF_CKR - ROPE_C), F32)], axis=1).astype(BF16)
    g = jnp.arange(MXU_DIM) // DA
    lw = {
        "w_in": w_ext,
        "b32": jnp.stack([(g[:, None] == g[None, :]), (g[:, None] // 4 == g[None, :] // 4)]).astype(BF16),
        "norm_mix": p["norm_mix"][l][None, :],
        "norm_ffn": p["norm_ffn"][l][None, :],
        "gqk": jnp.concatenate([jnp.tile(p["qnorm_a"][l], 2 * H_A), jnp.tile(p["knorm_a"][l], 2 * H_A)])[None, :],
        "gqa": p["norm_qa"][l][None, :],
        "gkva": p["norm_kva"][l][None, :],
        "gqc": jnp.pad(p["qnorm_c"][l], (0, HEAD_PAD - QK_C))[None, :],
        "gkc": jnp.pad(p["knorm_c"][l], (0, HEAD_PAD - QK_C))[None, :],
        "w_qb": _pad_heads(p["w_qb"][l], QK_C).astype(BF16),
        "w_kb": _pad_heads(p["w_kb"][l], NOPE_C).astype(BF16),
        "w_vb": p["w_vb"][l].astype(BF16),
        "conv_w": p["conv_w"][l],
        "conv_b": p["conv_b"][l][None, :],
        "w_out": p["w_out"][l].astype(BF16),
        "lam": jnp.stack([p["lambda_q1"][l], p["lambda_k1"][l], p["lambda_q2"][l], p["lambda_k2"][l]]),
        "subln": p["subln_a"][l][:, None],
    }
    i = l // 2
    if l % 2 == 0:
        lw["wg"] = p["ffn_w_gate"][i].astype(BF16)
        lw["wu"] = p["ffn_w_up"][i].astype(BF16)
        lw["wd"] = p["ffn_w_down"][i].astype(BF16)
    else:
        lw["router"] = jnp.pad(p["router_w"][i], ((0, 0), (0, LANES - N_EXPERTS)))
        lw["wg"] = p["moe_w_gate"][i].astype(BF16)
        lw["wu"] = p["moe_w_up"][i].astype(BF16)
        lw["wd"] = p["moe_w_down"][i].astype(BF16)
    return lw


def _cached_vT(v, B, P):
    vT = v.transpose(0, 2, 3, 1).astype(BF16)
    ones = jnp.ones((B, H_A, 1, P), BF16)
    zeros = jnp.zeros((B, H_A, V_ROWS - V_A - 1, P), BF16)
    return jnp.concatenate([vT, ones, zeros], axis=2)


def _mixer(x, mod, lw, l_init, *, B, T, ctx, rope):
    is_ctx = ctx is None
    tm = T if is_ctx else min(512, T)
    seg = T if is_ctx else GRID_W
    nsub = 1 if is_ctx else max(1, tm // 256)
    outs = _pre_mix(x, mod, lw, rope, B=B, T=T, is_ctx=is_ctx, tm=tm, seg=seg, nsub=nsub)
    qaT, ka, vaT, ob, qcT, kc, vcT = outs[:7]
    tq = min(256, T)
    tq_c = min(2 * tq, T)
    if is_ctx:
        tk = 512 if T % 512 == 0 else 256
        o_a = _attention(qaT, ka, vaT, diff=True, tq=tq, tk=tk, lam_vecs=lw["lam"], subln=lw["subln"],
                         l_init=l_init)
        o_c = _attention(qcT, kc, vcT, diff=False, tq=tq_c, tk=tk)
    else:
        ctx_k, ctx_v, ctx_ckv, ctx_kr = ctx
        P = ctx_k.shape[1]
        tk = 512 if (P % 512 == 0 and T % 512 == 0) else 256
        kr_placed = jnp.pad(ctx_kr.reshape(B * P, ROPE_C), ((0, 0), (NOPE_C, HEAD_PAD - QK_C)))
        kc_ctx, vc_ctx = _cache_kv(ctx_ckv.reshape(B * P, KV_RANK), kr_placed, lw, B, P)
        ka_ctx = ctx_k.reshape(B * P, H_A, 2 * DA).transpose(1, 0, 2).astype(BF16)
        o_a = _attention(qaT, ka, vaT, ka_ctx, _cached_vT(ctx_v, B, P), diff=True, tq=tq, tk=tk,
                         lam_vecs=lw["lam"], subln=lw["subln"], l_init=l_init)
        o_c = _attention(qcT, kc, vcT, kc_ctx, vc_ctx, diff=False, tq=tq_c, tk=tk)
    return (o_a, ob, o_c), outs[7:]


def kernel(x_prompt, x_sample, cache_diff_k, cache_diff_v, cache_mla_ckv, cache_mla_krope, c, c_ctx, w_ada, b_ada, norm_mix, norm_ffn, w_in, qnorm_a, knorm_a, lambda_q1, lambda_k1, lambda_q2, lambda_k2, subln_a, conv_w, conv_b, norm_qa, w_qb, norm_kva, w_kb, w_vb, qnorm_c, knorm_c, w_out, ffn_w_gate, ffn_w_up, ffn_w_down, router_w, moe_w_gate, moe_w_up, moe_w_down):
    p = dict(norm_mix=norm_mix, norm_ffn=norm_ffn, w_in=w_in, qnorm_a=qnorm_a, knorm_a=knorm_a,
             lambda_q1=lambda_q1, lambda_k1=lambda_k1, lambda_q2=lambda_q2, lambda_k2=lambda_k2,
             subln_a=subln_a, conv_w=conv_w, conv_b=conv_b, norm_qa=norm_qa, w_qb=w_qb,
             norm_kva=norm_kva, w_kb=w_kb, w_vb=w_vb, qnorm_c=qnorm_c, knorm_c=knorm_c, w_out=w_out,
             ffn_w_gate=ffn_w_gate, ffn_w_up=ffn_w_up, ffn_w_down=ffn_w_down, router_w=router_w,
             moe_w_gate=moe_w_gate, moe_w_up=moe_w_up, moe_w_down=moe_w_down)
    Bc, Sc, D = x_prompt.shape
    Bl, T, _ = x_sample.shape
    L = w_in.shape[0]
    nrow = 16
    cvec = jnp.concatenate([c, c_ctx[None, :], jnp.zeros((nrow - Bl - 1, D), F32)], axis=0)
    mod = _ada(cvec, w_ada, b_ada)
    rope = _rope_tables(T)
    xp = x_prompt.reshape(Bc * Sc, D)
    xs = x_sample.reshape(Bl * T, D)
    st = [[], [], [], []]
    for l in range(L):
        lw = _layer_weights(l, p)
        l_init = 0.8 - 0.6 * math.exp(-0.3 * l)
        mod_lat = mod[l, :Bl].reshape(Bl, 6, D)
        mod_ctx = mod[l, Bl:Bl + 1].reshape(1, 6, D)
        post = _post_dense if l % 2 == 0 else _post_moe
        heads, state = _mixer(xp, mod_ctx, lw, l_init, B=Bc, T=Sc, ctx=None, rope=None)
        for acc, s in zip(st, state):
            acc.append(s)
        xp = post(xp, *heads, mod_ctx, lw, tm=min(512, Bc * Sc), tiles_per_mod=Bc * Sc)
        ctx = (cache_diff_k[:, l], cache_diff_v[:, l], cache_mla_ckv[:, l], cache_mla_krope[:, l])
        heads, _ = _mixer(xs, mod_lat, lw, l_init, B=Bl, T=T, ctx=ctx, rope=rope)
        tm = min(512, T)
        xs = post(xs, *heads, mod_lat, lw, tm=tm, tiles_per_mod=T // tm)
    new_k = jnp.stack(st[0], axis=1).reshape(Bc, Sc, L, H_A, 2, DA).transpose(0, 2, 1, 3, 4, 5)
    new_v = jnp.stack(st[1], axis=1).reshape(Bc, Sc, L, H_A, V_A).transpose(0, 2, 1, 3, 4)
    new_ckv = jnp.stack(st[2], axis=1).reshape(Bc, Sc, L, KV_RANK).transpose(0, 2, 1, 3)
    new_kr = jnp.stack(st[3], axis=1).reshape(Bc, Sc, L, ROPE_C).transpose(0, 2, 1, 3)
    return (xp.reshape(Bc, Sc, D), xs.reshape(Bl, T, D), new_k, new_v, new_ckv, new_kr)
```

```python
import functools
import math

import jax
import jax.numpy as jnp
from jax import lax
from jax.experimental import pallas as pl
from jax.experimental.pallas import tpu as pltpu

F32 = jnp.float32
BF16 = jnp.bfloat16

GRID_W = 64
H_A = 6
DA = 32
V_A = 2 * DA
W_B = 256
H_C = 6
NOPE_C = 64
ROPE_C = 32
QK_C = NOPE_C + ROPE_C
V_C = 64
Q_RANK = 192
KV_RANK = 128
N_EXPERTS = 8
ROPE_THETA = 10000.0
EPS = 1e-6
LOG2E = 1.4426950408889634

LANES = 128
MXU_DIM = 256
HEAD_PAD = 128

A_W = H_A * 2 * DA
OFF_AQ = 0
OFF_AK = A_W
OFF_AV = 2 * A_W
OFF_BB = 3 * A_W
OFF_BC = OFF_BB + W_B
OFF_BX = OFF_BC + W_B
OFF_CKV = OFF_BX + W_B
OFF_CQ = OFF_CKV + KV_RANK
OFF_CKR = OFF_CQ + Q_RANK
KR_CHUNK = (OFF_CKR // LANES) * LANES
assert OFF_CKR - KR_CHUNK == NOPE_C
IN_EXT = KR_CHUNK + LANES
C_W = H_C * HEAD_PAD
MIX_A = H_A * V_A
MIX_C = H_C * V_C

ROW_GROUP = 32
V_ROWS = V_A + 16
assert V_A == V_C

VMEM_LIMIT = 56 * 1024 * 1024


def _cparams(sem):
    return pltpu.CompilerParams(dimension_semantics=sem, vmem_limit_bytes=VMEM_LIMIT)


def _const_spec(shape):
    nd = len(shape)
    return pl.BlockSpec(shape, lambda *_: (0,) * nd, pipeline_mode=pl.Buffered(1))


def _rms(x, axis=-1):
    return x * lax.rsqrt(jnp.mean(x * x, axis=axis, keepdims=True) + EPS)


def _split_dot(x, w):
    hi = x.astype(BF16)
    lo = (x - hi.astype(F32)).astype(BF16)
    return (jnp.dot(hi, w, preferred_element_type=F32)
            + jnp.dot(lo, w, preferred_element_type=F32))


def _ada_kernel(c_ref, w_ref, b_ref, o_ref):
    c = c_ref[...]
    s = (c * jax.nn.sigmoid(c)).astype(BF16)
    o_ref[0] = jnp.dot(s, w_ref[0].astype(BF16), preferred_element_type=F32) + b_ref[0]


def _ada(cvec, w_ada, b_ada):
    L, D, N6 = w_ada.shape
    R = cvec.shape[0]
    tn = 1536
    assert N6 % tn == 0
    return pl.pallas_call(
        _ada_kernel,
        grid=(L, N6 // tn),
        in_specs=[
            pl.BlockSpec((R, D), lambda l, j: (0, 0)),
            pl.BlockSpec((1, D, tn), lambda l, j: (l, 0, j)),
            pl.BlockSpec((1, 1, tn), lambda l, j: (l, 0, j)),
        ],
        out_specs=pl.BlockSpec((1, R, tn), lambda l, j: (l, 0, j)),
        out_shape=jax.ShapeDtypeStruct((L, R, N6), F32),
        compiler_params=_cparams(("arbitrary", "arbitrary")),
        name="ada_mod",
    )(cvec, w_ada, b_ada.reshape(L, 1, N6))


def _rope(x, tab_ref):
    up = pltpu.roll(x, LANES - 8, 1)
    dn = pltpu.roll(x, 8, 1)
    return x * tab_ref[0] + up * tab_ref[1] + dn * tab_ref[2]


def _rope_T(xT, tab_ref, starts):
    nb = 8
    blocks = [xT[nb * i:nb * (i + 1)] for i in range(xT.shape[0] // nb)]
    for g in starts:
        for k in range(2):
            c, s = tab_ref[2 * k], tab_ref[2 * k + 1]
            b = g // nb + 2 * k
            x1, x2 = blocks[b], blocks[b + 1]
            blocks[b] = x1 * c - x2 * s
            blocks[b + 1] = x1 * s + x2 * c
    return jnp.concatenate(blocks, axis=0)


def _store_vT(v, ref):
    tm = v.shape[0]
    vT = v.T.astype(BF16)
    row = lax.broadcasted_iota(jnp.int32, (V_ROWS - V_A, tm), 0)
    tail = jnp.where(row == 0, 1.0, 0.0).astype(BF16)
    for h in range(v.shape[1] // V_A):
        ref[0, h, 0:V_A, :] = vT[h * V_A:(h + 1) * V_A, :]
        ref[0, h, V_A:V_ROWS, :] = tail


def _head_rms(x, gain_ref, ones_ref):
    sq = x * x
    ss = jnp.concatenate(
        [_split_dot(sq[:, c:c + MXU_DIM], ones_ref[...]) for c in range(0, x.shape[1], MXU_DIM)], axis=-1)
    gain = jnp.concatenate([gain_ref[...]] * (x.shape[1] // HEAD_PAD), axis=-1)
    return x * lax.rsqrt(ss * (1.0 / QK_C) + EPS) * gain


def _mla_kv(ckv_n, kr_placed, wkb_ref, wvb_ref, gkc_ref, ones_ref, rope_ref, kc_ref, vcT_ref):
    cb = ckv_n.astype(BF16)
    kn = jnp.dot(cb, wkb_ref[...], preferred_element_type=F32)
    vc = jnp.dot(cb, wvb_ref[...], preferred_element_type=F32)
    kall = _head_rms(kn + jnp.concatenate([kr_placed] * H_C, axis=-1), gkc_ref, ones_ref)
    for h in range(H_C):
        kp = kall[:, h * HEAD_PAD:(h + 1) * HEAD_PAD]
        if rope_ref is not None:
            kp = _rope(kp, rope_ref)
        kc_ref[h] = kp.astype(BF16)
    _store_vT(vc, vcT_ref)


def _pre_mix_kernel(is_ctx, seg, nsub, *refs):
    refs = list(refs)
    sub = refs[0].shape[0] // nsub
    row_dim = {0: 0}
    n_in = 15 if is_ctx else 18
    if not is_ctx:
        row_dim.update({15: 1, 16: 1, 17: 2})
    for k, dim in enumerate((1, 1, 3, 0, 1, 1, 3) + ((0, 0, 0, 0) if is_ctx else ())):
        row_dim[n_in + k] = dim
    x_ref, mod_ref, nmix_ref, win_ref = refs[:4]
    projs = []
    for r in range(nsub):
        x = x_ref[pl.ds(r * sub, sub), :]
        h = _rms(x) * nmix_ref[...] * (1.0 + mod_ref[0, 1:2, :]) + mod_ref[0, 0:1, :]
        projs.append(jnp.dot(h.astype(BF16), win_ref[...], preferred_element_type=F32))
    for r in range(nsub):
        views = []
        for pos, ref in enumerate(refs):
            if pos in row_dim:
                idx = [slice(None)] * len(ref.shape)
                idx[row_dim[pos]] = pl.ds(r * sub, sub)
                ref = ref.at[tuple(idx)]
            views.append(ref)
        _pre_mix_rows(is_ctx, seg, projs[r], *views)


def _pre_mix_rows(is_ctx, seg, proj, *refs):
    (_, _, _, _, ones_ref, gqk_ref, gqa_ref, gkva_ref, gqc_ref,
     gkc_ref, wqb_ref, wkb_ref, wvb_ref, cw_ref, cb_ref) = refs[:15]
    b32_ref, b128_ref = ones_ref.at[0], ones_ref.at[1]
    refs = refs[15:]
    if is_ctx:
        ra_ref = rc_ref = None
    else:
        ra_ref, rc_ref, rT_ref = refs[:3]
        refs = refs[3:]
    qaT_ref, ka_ref, vaT_ref, ob_ref, qcT_ref, kc_ref, vcT_ref = refs[:7]
    refs = refs[7:]

    aqk = proj[:, OFF_AQ:OFF_AV]
    sq = aqk * aqk
    ss = jnp.concatenate(
        [_split_dot(sq[:, c:c + MXU_DIM], b32_ref[...]) for c in range(0, 2 * A_W, MXU_DIM)], axis=-1)
    aqk = aqk * lax.rsqrt(ss * (1.0 / DA) + EPS) * gqk_ref[...]
    if is_ctx:
        ka_st_ref, va_st_ref, ckv_st_ref, ckr_st_ref = refs
        ka_st_ref[...] = aqk[:, A_W:]
    qaT = (aqk[:, :A_W] * (DA ** -0.5 * LOG2E)).T
    ak = aqk[:, A_W:]
    if not is_ctx:
        qaT = _rope_T(qaT, rT_ref, range(0, A_W, DA))
        ak = jnp.concatenate([_rope(ak[:, c:c + LANES], ra_ref) for c in range(0, A_W, LANES)], axis=-1)
    qaT_ref[...] = qaT.astype(BF16)
    for hh in range(H_A):
        ka_ref[hh] = ak[:, hh * 2 * DA:(hh + 1) * 2 * DA].astype(BF16)
    av = proj[:, OFF_AV:OFF_BB]
    _store_vT(av, vaT_ref)

    bb = proj[:, OFF_BB:OFF_BC]
    u = proj[:, OFF_BC:OFF_BX] * proj[:, OFF_BX:OFF_CKV]
    tm = u.shape[0]
    row = lax.broadcasted_iota(jnp.int32, u.shape, 0) & (seg - 1)
    prev = jnp.where(row == 0, 0.0, pltpu.roll(u, 1, 0))
    nxt = jnp.where(row == seg - 1, 0.0, pltpu.roll(u, tm - 1, 0))
    conv = prev * cw_ref[0:1, :] + u * cw_ref[1:2, :] + nxt * cw_ref[2:3, :] + cb_ref[...]
    ob_ref[...] = (bb * conv).astype(BF16)

    cq = proj[:, OFF_CQ:OFF_CKR]
    cqn = (_rms(cq) * gqa_ref[...]).astype(BF16)
    qc = jnp.dot(cqn, wqb_ref[...], preferred_element_type=F32)
    qcT = (_head_rms(qc, gqc_ref, b128_ref) * (QK_C ** -0.5 * LOG2E)).T
    if not is_ctx:
        qcT = _rope_T(qcT, rT_ref, range(NOPE_C, C_W, HEAD_PAD))
    qcT_ref[...] = qcT.astype(BF16)
    ckv_n = _rms(proj[:, OFF_CKV:OFF_CQ]) * gkva_ref[...]
    krc = proj[:, KR_CHUNK:KR_CHUNK + LANES]
    lane = lax.broadcasted_iota(jnp.int32, krc.shape, 1)
    kr_placed = jnp.where((lane >= NOPE_C) & (lane < QK_C), krc, 0.0)
    _mla_kv(ckv_n, kr_placed, wkb_ref, wvb_ref, gkc_ref, b128_ref, rc_ref, kc_ref, vcT_ref)
    if is_ctx:
        va_st_ref[...] = av
        ckv_st_ref[...] = ckv_n
        ckr_st_ref[...] = krc[:, NOPE_C:QK_C]


def _pre_mix(x, mod, lw, rope, *, B, T, is_ctx, tm, seg, nsub):
    N, D = x.shape
    assert (tm // nsub) % seg == 0 and (tm // nsub) % LANES == 0
    nt = N // tm
    tps = T // tm
    row_spec = lambda w: pl.BlockSpec((tm, w), lambda i: (i, 0))
    in_specs = [
        row_spec(D),
        pl.BlockSpec((1, 6, D), (lambda i: (0, 0, 0)) if is_ctx else (lambda i: (i // tps, 0, 0))),
        _const_spec((1, D)),
        _const_spec((D, IN_EXT)),
        _const_spec((2, MXU_DIM, MXU_DIM)),
        _const_spec((1, 2 * A_W)),
        _const_spec((1, Q_RANK)),
        _const_spec((1, KV_RANK)),
        _const_spec((1, HEAD_PAD)),
        _const_spec((1, HEAD_PAD)),
        _const_spec((Q_RANK, C_W)),
        _const_spec((KV_RANK, C_W)),
        _const_spec((KV_RANK, MIX_C)),
        _const_spec((3, W_B)),
        _const_spec((1, W_B)),
    ]
    args = [x, mod, lw["norm_mix"], lw["w_in"], lw["b32"], lw["gqk"], lw["gqa"], lw["gkva"],
            lw["gqc"], lw["gkc"], lw["w_qb"], lw["w_kb"], lw["w_vb"], lw["conv_w"], lw["conv_b"]]
    if not is_ctx:
        rspec = pl.BlockSpec((3, tm, LANES), lambda i: (0, i % tps, 0))
        in_specs += [rspec, rspec, pl.BlockSpec((4, 8, tm), lambda i: (0, 0, i % tps))]
        args += list(rope)
    colT = lambda w: pl.BlockSpec((w, tm), lambda i: (0, i))
    vT_spec = pl.BlockSpec((1, H_A, V_ROWS, tm), lambda i: (i // tps, 0, 0, i % tps))
    out_specs = [colT(A_W), pl.BlockSpec((H_A, tm, 2 * DA), lambda i: (0, i, 0)), vT_spec, row_spec(W_B),
                 colT(C_W), pl.BlockSpec((H_C, tm, HEAD_PAD), lambda i: (0, i, 0)), vT_spec]
    out_shape = [jax.ShapeDtypeStruct((A_W, N), BF16), jax.ShapeDtypeStruct((H_A, N, 2 * DA), BF16),
                 jax.ShapeDtypeStruct((B, H_A, V_ROWS, T), BF16), jax.ShapeDtypeStruct((N, W_B), BF16),
                 jax.ShapeDtypeStruct((C_W, N), BF16), jax.ShapeDtypeStruct((H_C, N, HEAD_PAD), BF16),
                 jax.ShapeDtypeStruct((B, H_C, V_ROWS, T), BF16)]
    if is_ctx:
        st_w = [A_W, A_W, KV_RANK, ROPE_C]
        out_specs += [row_spec(w) for w in st_w]
        out_shape += [jax.ShapeDtypeStruct((N, w), F32) for w in st_w]
    return pl.pallas_call(
        functools.partial(_pre_mix_kernel, is_ctx, seg, nsub),
        grid=(nt,),
        in_specs=in_specs,
        out_specs=out_specs,
        out_shape=out_shape,
        compiler_params=_cparams(("arbitrary",)),
        name="pre_mix_ctx" if is_ctx else "pre_mix_lat",
    )(*args)


def _cache_kv_kernel(ckv_ref, kr_ref, wkb_ref, wvb_ref, gkc_ref, ones_ref, kc_ref, vcT_ref):
    _mla_kv(ckv_ref[...], kr_ref[...], wkb_ref, wvb_ref, gkc_ref, ones_ref.at[1], None, kc_ref, vcT_ref)


def _cache_kv(ckv, kr_placed, lw, B, P):
    row_spec = lambda w: pl.BlockSpec((P, w), lambda i: (i, 0))
    return pl.pallas_call(
        _cache_kv_kernel,
        grid=(B,),
        in_specs=[row_spec(KV_RANK), row_spec(HEAD_PAD), _const_spec((KV_RANK, C_W)),
                  _const_spec((KV_RANK, MIX_C)), _const_spec((1, HEAD_PAD)),
                  _const_spec((2, MXU_DIM, MXU_DIM))],
        out_specs=[pl.BlockSpec((H_C, P, HEAD_PAD), lambda i: (0, i, 0)),
                   pl.BlockSpec((1, H_C, V_ROWS, P), lambda i: (i, 0, 0, 0))],
        out_shape=[jax.ShapeDtypeStruct((H_C, B * P, HEAD_PAD), BF16),
                   jax.ShapeDtypeStruct((B, H_C, V_ROWS, P), BF16)],
        compiler_params=_cparams(("arbitrary",)),
        name="cache_kv",
    )(ckv, kr_placed, lw["w_kb"], lw["w_vb"], lw["gkc"], lw["b32"])


def _colmax(s):
    tk, tq = s.shape
    r = jnp.max(s.reshape(tk // ROW_GROUP, ROW_GROUP, tq), axis=0)
    return jnp.max(r, axis=0, keepdims=True)


def _attn_kernel(diff, has_ctx, G, chunks, tk, l_init, *refs):
    refs = list(refs)
    q_ref = refs.pop(0)
    if has_ctx:
        kc_ref, kl_ref, vc_ref, vl_ref = refs[:4]
        refs = refs[4:]
    else:
        kl_ref, vl_ref = refs[:2]
        kc_ref = vc_ref = None
        refs = refs[2:]
    if diff:
        lam_ref, g_ref = refs[:2]
        refs = refs[2:]
    o_ref, sa, sb, os_ref = refs
    tq = q_ref.shape[-1]
    uw = tq if diff else tq // 2
    bufs = (sa, sb)
    nk = len(chunks)
    dq = q_ref.shape[0] // G
    gpt = 1 if nk % 2 == 0 else (2 if G % 2 == 0 else G)
    ntrip = G // gpt

    def q_units(g):
        start = g * dq
        if not isinstance(start, int):
            start = pl.multiple_of(start, dq)
        qh = q_ref[pl.ds(start, dq), :]
        if diff:
            row = lax.broadcasted_iota(jnp.int32, qh.shape, 0)
            zero = jnp.zeros_like(qh)
            return [jnp.where(row < DA, qh, zero), jnp.where(row >= DA, qh, zero)]
        return [qh[:, :uw], qh[:, uw:]]

    def kslice(h, j):
        src, off = chunks[j]
        return (kc_ref if src == 0 else kl_ref)[h, pl.ds(off, tk), :]

    def vslice(h, j):
        src, off = chunks[j]
        return (vc_ref if src == 0 else vl_ref)[0, h, :, pl.ds(off, tk)]

    def qk(g, qs, j, buf):
        kj = kslice(g, j)
        for u in range(2):
            buf[u] = jnp.dot(kj, qs[u], preferred_element_type=F32)

    def softmax(buf, carry):
        ps, out = [], []
        for u in range(2):
            m, acc = carry[u]
            s = buf[u]
            m_new = jnp.maximum(m, _colmax(s))
            alpha = jnp.exp2(m - m_new)
            ps.append(jnp.exp2((s - m_new).astype(BF16)))
            out.append((m_new, alpha * acc))
        return ps, out

    def pv(g, j, ps, st):
        vj = vslice(g, j)
        return [(st[u][0], st[u][1] + jnp.dot(vj, ps[u], preferred_element_type=F32)) for u in range(2)]

    def finish(g, res):
        os_ = [acc[:V_A] / acc[V_A:V_A + 1] for (_, acc) in res]
        if diff:
            lv = lam_ref[...]
            lam = (jnp.exp(jnp.sum(lv[0:1] * lv[1:2], axis=-1, keepdims=True))
                   - jnp.exp(jnp.sum(lv[2:3] * lv[3:4], axis=-1, keepdims=True)) + l_init)
            o = os_[0] - lam * os_[1]
            os_ref[g] = _rms(o, axis=0) * g_ref[...] * (1.0 - l_init)
        else:
            os_ref[g, :, 0:uw] = os_[0]
            os_ref[g, :, uw:tq] = os_[1]

    def trip(t, _):
        step = 0
        g = t * gpt
        qs = q_units(g)
        for gi in range(gpt):
            g = t * gpt + gi
            carry = [(jnp.full((1, uw), -jnp.inf, F32), jnp.zeros((V_ROWS, uw), F32)) for _ in range(2)]
            for j in range(nk):
                src, dst = bufs[step % 2], bufs[(step + 1) % 2]
                qs_next = qs
                if j + 1 < nk:
                    qk(g, qs, j + 1, dst)
                elif gi + 1 < gpt:
                    qs_next = q_units(g + 1)
                    qk(g + 1, qs_next, 0, dst)
                elif ntrip > 1:
                    g_next = jnp.minimum(g + 1, G - 1)
                    qk(g_next, q_units(g_next), 0, dst)
                ps, st = softmax(src, carry)
                carry = pv(g, j, ps, st)
                qs = qs_next
                step += 1
            finish(g, carry)
        return 0

    qk(0, q_units(0), 0, sa)
    if ntrip > 1:
        lax.fori_loop(0, ntrip, trip, 0)
    else:
        trip(0, 0)
    for c in range(G // 2):
        pair = jnp.concatenate([os_ref[2 * c], os_ref[2 * c + 1]], axis=0)
        o_ref[:, c * LANES:(c + 1) * LANES] = pair.T.astype(o_ref.dtype)


def _attention(qT, k_lat, v_lat, k_ctx=None, v_ctx=None, *, diff, tq, tk, lam_vecs=None, subln=None,
               l_init=0.0):
    B, H, _, T = v_lat.shape
    N = qT.shape[1]
    d = k_lat.shape[2]
    has_ctx = k_ctx is not None
    chunks = [(1, o) for o in range(0, T, tk)]
    nq = T // tq
    in_specs = [pl.BlockSpec((qT.shape[0], tq), lambda b, i: (0, b * nq + i))]
    args = [qT]
    if has_ctx:
        P = v_ctx.shape[3]
        chunks = [(0, o) for o in range(0, P, tk)] + chunks
        in_specs += [pl.BlockSpec((H, P, d), lambda b, i: (0, b, 0)),
                     pl.BlockSpec((H, T, d), lambda b, i: (0, b, 0)),
                     pl.BlockSpec((1, H, V_ROWS, P), lambda b, i: (b, 0, 0, 0)),
                     pl.BlockSpec((1, H, V_ROWS, T), lambda b, i: (b, 0, 0, 0))]
        args += [k_ctx, k_lat, v_ctx, v_lat]
    else:
        in_specs += [pl.BlockSpec((H, T, d), lambda b, i: (0, b, 0)),
                     pl.BlockSpec((1, H, V_ROWS, T), lambda b, i: (b, 0, 0, 0))]
        args += [k_lat, v_lat]
    if diff:
        in_specs += [pl.BlockSpec((4, DA), lambda b, i: (0, 0)),
                     pl.BlockSpec((V_A, 1), lambda b, i: (0, 0))]
        args += [lam_vecs, subln]
    uw = tq if diff else tq // 2
    return pl.pallas_call(
        functools.partial(_attn_kernel, diff, has_ctx, H, tuple(chunks), tk, l_init),
        grid=(B, nq),
        in_specs=in_specs,
        out_specs=pl.BlockSpec((tq, H * V_A), lambda b, i: (b * nq + i, 0)),
        out_shape=jax.ShapeDtypeStruct((N, H * V_A), BF16),
        scratch_shapes=[pltpu.VMEM((2, tk, uw), F32), pltpu.VMEM((2, tk, uw), F32),
                        pltpu.VMEM((H, V_A, tq), F32)],
        compiler_params=_cparams(("arbitrary", "arbitrary")),
        name="attn_diff" if diff else "attn_mla",
    )(*args)


def _post_head(x_ref, oa_ref, ob_ref, oc_ref, mod_ref, nffn_ref, wout_ref):
    mix = jnp.concatenate([oa_ref[...], ob_ref[...], oc_ref[...]], axis=-1)
    y = jnp.dot(mix, wout_ref[...], preferred_element_type=F32)
    x1 = x_ref[...] + mod_ref[0, 2:3, :] * y
    h2 = _rms(x1) * nffn_ref[...] * (1.0 + mod_ref[0, 4:5, :]) + mod_ref[0, 3:4, :]
    return x1, h2


def _swiglu(hb, wg, wu, wd):
    g = jnp.dot(hb, wg, preferred_element_type=F32)
    u = jnp.dot(hb, wu, preferred_element_type=F32)
    a = (g * jax.nn.sigmoid(g) * u).astype(BF16)
    return jnp.dot(a, wd, preferred_element_type=F32)


def _post_dense_kernel(fchunks, x_ref, oa_ref, ob_ref, oc_ref, mod_ref, nffn_ref, wout_ref,
                       wg_ref, wu_ref, wd_ref, o_ref):
    x1, h2 = _post_head(x_ref, oa_ref, ob_ref, oc_ref, mod_ref, nffn_ref, wout_ref)
    hb = h2.astype(BF16)
    acc = None
    for (f0, f1) in fchunks:
        part = _swiglu(hb, wg_ref[:, f0:f1], wu_ref[:, f0:f1], wd_ref[f0:f1, :])
        acc = part if acc is None else acc + part
    o_ref[...] = x1 + mod_ref[0, 5:6, :] * acc


def _post_dense(x, oa, ob, oc, mod, lw, *, tm, tiles_per_mod):
    N, D = x.shape
    FF = lw["wg"].shape[1]
    cut = (FF // 2 // MXU_DIM + 1) * MXU_DIM if FF > 2 * MXU_DIM else FF
    fchunks = ((0, cut), (cut, FF)) if cut < FF else ((0, FF),)
    row = lambda w: pl.BlockSpec((tm, w), lambda i: (i, 0))
    return pl.pallas_call(
        functools.partial(_post_dense_kernel, fchunks),
        grid=(N // tm,),
        in_specs=[row(D), row(MIX_A), row(W_B), row(MIX_C),
                  pl.BlockSpec((1, 6, D), lambda i: (i // tiles_per_mod, 0, 0)),
                  _const_spec((1, D)), _const_spec(lw["w_out"].shape), _const_spec((D, FF)),
                  _const_spec((D, FF)), _const_spec((FF, D))],
        out_specs=row(D),
        out_shape=jax.ShapeDtypeStruct((N, D), F32),
        compiler_params=_cparams(("arbitrary",)),
        name="post_dense",
    )(x, oa, ob, oc, mod, lw["norm_ffn"], lw["w_out"], lw["wg"], lw["wu"], lw["wd"])


TOK_SUB = 8
MOE_ROWS = 256
DISPATCH_TOKENS = 256
COMBINE_TOKENS = 256


def _to_token_tiles(x, ref):
    tm = x.shape[0]
    for a in range(TOK_SUB):
        ref[pl.ds(a, tm, stride=TOK_SUB), :] = x[:, a * LANES:(a + 1) * LANES]


def _from_token_tiles(ref, tm):
    return jnp.concatenate([ref[pl.ds(a, tm, stride=TOK_SUB), :] for a in range(TOK_SUB)], axis=-1)


def _router_kernel(x_ref, oa_ref, ob_ref, oc_ref, mod_ref, nffn_ref, wout_ref, rw_ref,
                   x1_ref, h2t_ref, route_ref):
    x1, h2 = _post_head(x_ref, oa_ref, ob_ref, oc_ref, mod_ref, nffn_ref, wout_ref)
    x1_ref[...] = x1
    _to_token_tiles(h2, h2t_ref)
    rw = rw_ref[...]
    rhi = rw.astype(BF16)
    rlo = (rw - rhi.astype(F32)).astype(BF16)
    hhi = h2.astype(BF16)
    hlo = (h2 - hhi.astype(F32)).astype(BF16)
    both = jnp.dot(hhi, jnp.concatenate([rhi, rlo], axis=-1), preferred_element_type=F32)
    logits = (both[:, :LANES] + both[:, LANES:]
              + jnp.dot(hlo, rhi, preferred_element_type=F32))
    lane = lax.broadcasted_iota(jnp.int32, logits.shape, 1)
    neg = -jnp.inf
    lg = jnp.where(lane < N_EXPERTS, logits, neg)
    v1 = jnp.max(lg, axis=-1, keepdims=True)
    i1 = jnp.min(jnp.where(lg == v1, lane, LANES), axis=-1, keepdims=True)
    lg2 = jnp.where(lane == i1, neg, lg)
    v2 = jnp.max(lg2, axis=-1, keepdims=True)
    i2 = jnp.min(jnp.where(lg2 == v2, lane, LANES), axis=-1, keepdims=True)
    e2 = jnp.exp(v2 - v1)
    g1 = 1.0 / (1.0 + e2)
    g2 = e2 / (1.0 + e2)
    route_ref[...] = jnp.where(lane == 0, g1, jnp.where(lane == 1, g2, jnp.where(
        lane == 2, i1.astype(F32), jnp.where(lane == 3, i2.astype(F32), 0.0))))


def _router(x, oa, ob, oc, mod, lw, *, tm, tiles_per_mod):
    N, D = x.shape
    assert D == TOK_SUB * LANES
    row = lambda w: pl.BlockSpec((tm, w), lambda i: (i, 0))
    return pl.pallas_call(
        _router_kernel,
        grid=(N // tm,),
        in_specs=[row(D), row(MIX_A), row(W_B), row(MIX_C),
                  pl.BlockSpec((1, 6, D), lambda i: (i // tiles_per_mod, 0, 0)),
                  _const_spec((1, D)), _const_spec(lw["w_out"].shape), _const_spec((D, LANES))],
        out_specs=[row(D), pl.BlockSpec((tm * TOK_SUB, LANES), lambda i: (i, 0)), row(LANES)],
        out_shape=[jax.ShapeDtypeStruct((N, D), F32), jax.ShapeDtypeStruct((N * TOK_SUB, LANES), F32),
                   jax.ShapeDtypeStruct((N, LANES), F32)],
        compiler_params=_cparams(("arbitrary",)),
        name="moe_router",
    )(x, oa, ob, oc, mod, lw["norm_ffn"], lw["w_out"], lw["router"])


def _moe_plan(route, N):
    R, E = MOE_ROWS, N_EXPERTS
    es = route[:, 2:4].astype(jnp.int32).reshape(-1)
    oh = (es[:, None] == jnp.arange(E, dtype=jnp.int32)[None, :]).astype(jnp.int32)
    rank = jnp.sum((jnp.cumsum(oh, axis=0) - oh) * oh, axis=1)
    cnt = jnp.sum(oh, axis=0)
    gsz = ((cnt + R - 1) // R) * R
    gend = jnp.cumsum(gsz)
    pos = ((gend - gsz)[es] + rank).astype(jnp.int32)
    n_tiles = (2 * N) // R + E
    tile_start = jnp.arange(n_tiles, dtype=jnp.int32) * R
    tile_expert = jnp.minimum(jnp.sum(tile_start[:, None] >= gend[None, :], axis=1), E - 1).astype(jnp.int32)
    return pos, tile_expert


def _tile_rows(ref, row):
    return ref.at[pl.ds(row * TOK_SUB, TOK_SUB), :]


def _dispatch_kernel(pos_ref, h2t_ref, xs_in_hbm, xs_hbm, buf, sem):
    del xs_in_hbm
    i = pl.program_id(0)
    n = pl.num_programs(0)
    tmd = DISPATCH_TOKENS
    slot = i % 2

    def step_wait(buf_slot):
        for _ in range(2):
            pltpu.make_async_copy(buf.at[buf_slot], xs_hbm.at[pl.ds(0, tmd * TOK_SUB), :],
                                  sem.at[buf_slot]).wait()

    @pl.when(i >= 2)
    def _():
        step_wait(slot)

    buf[slot] = h2t_ref[...]
    for r in range(tmd):
        t = i * tmd + r
        for k in range(2):
            pltpu.make_async_copy(_tile_rows(buf.at[slot], r), _tile_rows(xs_hbm, pos_ref[2 * t + k]),
                                  sem.at[slot]).start(priority=k)

    @pl.when(i == n - 1)
    def _():
        step_wait(slot)

        @pl.when(n >= 2)
        def _():
            step_wait(1 - slot)


def _dispatch(h2t, pos, n_rows):
    N = pos.shape[0] // 2
    xs0 = jnp.zeros((n_rows * TOK_SUB, LANES), F32)
    tile = DISPATCH_TOKENS * TOK_SUB
    grid_spec = pltpu.PrefetchScalarGridSpec(
        num_scalar_prefetch=1,
        grid=(N // DISPATCH_TOKENS,),
        in_specs=[pl.BlockSpec((tile, LANES), lambda i, pos: (i, 0)), pl.BlockSpec(memory_space=pl.ANY)],
        out_specs=pl.BlockSpec(memory_space=pl.ANY),
        scratch_shapes=[pltpu.VMEM((2, tile, LANES), F32), pltpu.SemaphoreType.DMA((2,))],
    )
    return pl.pallas_call(
        _dispatch_kernel,
        grid_spec=grid_spec,
        out_shape=jax.ShapeDtypeStruct(xs0.shape, F32),
        input_output_aliases={2: 0},
        compiler_params=_cparams(("arbitrary",)),
        name="moe_dispatch",
    )(pos, h2t, xs0)


def _experts_kernel(te_ref, x_ref, wg_ref, wu_ref, wd_ref, y_ref):
    xb = _from_token_tiles(x_ref, MOE_ROWS).astype(BF16)
    _to_token_tiles(_swiglu(xb, wg_ref[0], wu_ref[0], wd_ref[0]), y_ref)


def _experts(xs, tile_expert, lw):
    E, D, FF = lw["wg"].shape
    R = MOE_ROWS
    n_tiles = tile_expert.shape[0]
    wspec = lambda shape: pl.BlockSpec((1,) + shape, lambda i, te: (te[i], 0, 0))
    rows = pl.BlockSpec((R * TOK_SUB, LANES), lambda i, te: (i, 0))
    grid_spec = pltpu.PrefetchScalarGridSpec(
        num_scalar_prefetch=1,
        grid=(n_tiles,),
        in_specs=[rows, wspec((D, FF)), wspec((D, FF)), wspec((FF, D))],
        out_specs=rows,
    )
    return pl.pallas_call(
        _experts_kernel,
        grid_spec=grid_spec,
        out_shape=jax.ShapeDtypeStruct(xs.shape, F32),
        compiler_params=_cparams(("arbitrary",)),
        name="moe_experts",
    )(tile_expert, xs, lw["wg"], lw["wu"], lw["wd"])


def _combine_kernel(pos_ref, x1_ref, ys_hbm, route_ref, mod_ref, o_ref, ybuf, sem):
    tm = x1_ref.shape[0]
    i = pl.program_id(0)
    n = pl.num_programs(0)
    slot = i % 2

    def issue(tile, buf_slot):
        for r in range(tm):
            t = tile * tm + r
            for k in range(2):
                pltpu.make_async_copy(_tile_rows(ys_hbm, pos_ref[2 * t + k]),
                                      _tile_rows(ybuf.at[buf_slot, k], r), sem.at[buf_slot]).start(priority=k)

    @pl.when(i == 0)
    def _():
        issue(0, 0)

    for k in range(2):
        pltpu.make_async_copy(ys_hbm.at[pl.ds(0, tm * TOK_SUB), :], ybuf.at[slot, k], sem.at[slot]).wait()

    @pl.when(i + 1 < n)
    def _():
        issue(i + 1, 1 - slot)

    g = route_ref[...]
    y = (g[:, 0:1] * _from_token_tiles(ybuf.at[slot, 0], tm)
         + g[:, 1:2] * _from_token_tiles(ybuf.at[slot, 1], tm))
    o_ref[...] = x1_ref[...] + mod_ref[0, 5:6, :] * y


def _combine(x1, ys, pos, route, mod, *, tm, tiles_per_mod):
    N, D = x1.shape
    row = lambda w: pl.BlockSpec((tm, w), lambda i, pos: (i, 0))
    grid_spec = pltpu.PrefetchScalarGridSpec(
        num_scalar_prefetch=1,
        grid=(N // tm,),
        in_specs=[row(D), pl.BlockSpec(memory_space=pl.ANY), row(LANES),
                  pl.BlockSpec((1, 6, D), lambda i, pos: (i // tiles_per_mod, 0, 0))],
        out_specs=row(D),
        scratch_shapes=[pltpu.VMEM((2, 2, tm * TOK_SUB, LANES), F32), pltpu.SemaphoreType.DMA((2,))],
    )
    return pl.pallas_call(
        _combine_kernel,
        grid_spec=grid_spec,
        out_shape=jax.ShapeDtypeStruct((N, D), F32),
        compiler_params=_cparams(("arbitrary",)),
        name="moe_combine",
    )(pos, x1, ys, route, mod)


def _post_moe(x, oa, ob, oc, mod, lw, *, tm, tiles_per_mod):
    N = x.shape[0]
    x1, h2t, route = _router(x, oa, ob, oc, mod, lw, tm=tm, tiles_per_mod=tiles_per_mod)
    pos, tile_expert = _moe_plan(route, N)
    xs = _dispatch(h2t, pos, tile_expert.shape[0] * MOE_ROWS)
    ys = _experts(xs, tile_expert, lw)
    tmc = min(COMBINE_TOKENS, N)
    return _combine(x1, ys, pos, route, mod, tm=tmc, tiles_per_mod=tiles_per_mod * (tm // tmc))


def _rope_tables(T):
    t = jnp.arange(T, dtype=jnp.int32)
    rows = (t // GRID_W).astype(F32)
    cols = (t % GRID_W).astype(F32)
    n = ROPE_C // 4
    inv = jnp.power(ROPE_THETA, -jnp.arange(n, dtype=F32) / n)
    j = jnp.arange(ROPE_C)
    pos = jnp.where(j[None, :] < ROPE_C // 2, rows[:, None], cols[:, None])
    ang = pos * inv[j % n][None, :]
    cos = jnp.cos(ang)
    sin = jnp.sin(ang)
    first = ((j % (2 * n)) < n)[None, :]
    s_up = jnp.where(first, -sin, 0.0)
    s_dn = jnp.where(first, 0.0, sin)
    tab32 = jnp.stack([cos, s_up, s_dn])
    tab_a = jnp.tile(tab32, (1, 1, LANES // ROPE_C))
    ident = jnp.stack([jnp.ones((T, NOPE_C), F32), jnp.zeros((T, NOPE_C), F32), jnp.zeros((T, NOPE_C), F32)])
    tail = jnp.stack([jnp.ones((T, HEAD_PAD - QK_C), F32), jnp.zeros((T, HEAD_PAD - QK_C), F32),
                      jnp.zeros((T, HEAD_PAD - QK_C), F32)])
    tab_c = jnp.concatenate([ident, tab32, tail], axis=-1)
    ang_r = inv[:, None] * rows[None, :]
    ang_c = inv[:, None] * cols[None, :]
    tab_t = jnp.stack([jnp.cos(ang_r), jnp.sin(ang_r), jnp.cos(ang_c), jnp.sin(ang_c)])
    return tab_a, tab_c, tab_t


def _pad_heads(w, width):
    lead = w.shape[:-1]
    w = w.reshape(lead + (H_C, width))
    w = jnp.pad(w, [(0, 0)] * len(lead) + [(0, 0), (0, HEAD_PAD - width)])
    return w.reshape(lead + (C_W,))


def _layer_weights(l, p):
    w_in = p["w_in"][l]
    D = w_in.shape[0]
    o_cq = 3 * A_W + 3 * W_B
    o_ckv = o_cq + Q_RANK
    o_ckr = o_ckv + KV_RANK
    w_ext = jnp.concatenate(
        [w_in[:, :o_cq], w_in[:, o_ckv:o_ckr], w_in[:, o_cq:o_ckv], w_in[:, o_ckr:],
         jnp.zeros((D, IN_EXT - OFF_CKR - ROPE_C), F32)], axis=1).astype(BF16)
    g = jnp.arange(MXU_DIM) // DA
    lw = {
        "w_in": w_ext,
        "b32": jnp.stack([(g[:, None] == g[None, :]), (g[:, None] // 4 == g[None, :] // 4)]).astype(BF16),
        "norm_mix": p["norm_mix"][l][None, :],
        "norm_ffn": p["norm_ffn"][l][None, :],
        "gqk": jnp.concatenate([jnp.tile(p["qnorm_a"][l], 2 * H_A), jnp.tile(p["knorm_a"][l], 2 * H_A)])[None, :],
        "gqa": p["norm_qa"][l][None, :],
        "gkva": p["norm_kva"][l][None, :],
        "gqc": jnp.pad(p["qnorm_c"][l], (0, HEAD_PAD - QK_C))[None, :],
        "gkc": jnp.pad(p["knorm_c"][l], (0, HEAD_PAD - QK_C))[None, :],
        "w_qb": _pad_heads(p["w_qb"][l], QK_C).astype(BF16),
        "w_kb": _pad_heads(p["w_kb"][l], NOPE_C).astype(BF16),
        "w_vb": p["w_vb"][l].astype(BF16),
        "conv_w": p["conv_w"][l],
        "conv_b": p["conv_b"][l][None, :],
        "w_out": p["w_out"][l].astype(BF16),
        "lam": jnp.stack([p["lambda_q1"][l], p["lambda_k1"][l], p["lambda_q2"][l], p["lambda_k2"][l]]),
        "subln": p["subln_a"][l][:, None],
    }
    i = l // 2
    if l % 2 == 0:
        lw["wg"] = p["ffn_w_gate"][i].astype(BF16)
        lw["wu"] = p["ffn_w_up"][i].astype(BF16)
        lw["wd"] = p["ffn_w_down"][i].astype(BF16)
    else:
        lw["router"] = jnp.pad(p["router_w"][i], ((0, 0), (0, LANES - N_EXPERTS)))
        lw["wg"] = p["moe_w_gate"][i].astype(BF16)
        lw["wu"] = p["moe_w_up"][i].astype(BF16)
        lw["wd"] = p["moe_w_down"][i].astype(BF16)
    return lw


def _cached_vT(v, B, P):
    vT = v.transpose(0, 2, 3, 1).astype(BF16)
    ones = jnp.ones((B, H_A, 1, P), BF16)
    zeros = jnp.zeros((B, H_A, V_ROWS - V_A - 1, P), BF16)
    return jnp.concatenate([vT, ones, zeros], axis=2)


def _mixer(x, mod, lw, l_init, *, B, T, ctx, rope):
    is_ctx = ctx is None
    tm = T if is_ctx else min(512, T)
    seg = T if is_ctx else GRID_W
    nsub = 1 if is_ctx else max(1, tm // 256)
    outs = _pre_mix(x, mod, lw, rope, B=B, T=T, is_ctx=is_ctx, tm=tm, seg=seg, nsub=nsub)
    qaT, ka, vaT, ob, qcT, kc, vcT = outs[:7]
    tq = min(256, T)
    tq_c = min(2 * tq, T)
    if is_ctx:
        tk = 512 if T % 512 == 0 else 256
        o_a = _attention(qaT, ka, vaT, diff=True, tq=tq, tk=tk, lam_vecs=lw["lam"], subln=lw["subln"],
                         l_init=l_init)
        o_c = _attention(qcT, kc, vcT, diff=False, tq=tq_c, tk=tk)
    else:
        ctx_k, ctx_v, ctx_ckv, ctx_kr = ctx
        P = ctx_k.shape[1]
        tk = 512 if (P % 512 == 0 and T % 512 == 0) else 256
        kr_placed = jnp.pad(ctx_kr.reshape(B * P, ROPE_C), ((0, 0), (NOPE_C, HEAD_PAD - QK_C)))
        kc_ctx, vc_ctx = _cache_kv(ctx_ckv.reshape(B * P, KV_RANK), kr_placed, lw, B, P)
        ka_ctx = ctx_k.reshape(B * P, H_A, 2 * DA).transpose(1, 0, 2).astype(BF16)
        o_a = _attention(qaT, ka, vaT, ka_ctx, _cached_vT(ctx_v, B, P), diff=True, tq=tq, tk=tk,
                         lam_vecs=lw["lam"], subln=lw["subln"], l_init=l_init)
        o_c = _attention(qcT, kc, vcT, kc_ctx, vc_ctx, diff=False, tq=tq_c, tk=tk)
    return (o_a, ob, o_c), outs[7:]


def kernel(x_prompt, x_sample, cache_diff_k, cache_diff_v, cache_mla_ckv, cache_mla_krope, c, c_ctx, w_ada, b_ada, norm_mix, norm_ffn, w_in, qnorm_a, knorm_a, lambda_q1, lambda_k1, lambda_q2, lambda_k2, subln_a, conv_w, conv_b, norm_qa, w_qb, norm_kva, w_kb, w_vb, qnorm_c, knorm_c, w_out, ffn_w_gate, ffn_w_up, ffn_w_down, router_w, moe_w_gate, moe_w_up, moe_w_down):
    p = dict(norm_mix=norm_mix, norm_ffn=norm_ffn, w_in=w_in, qnorm_a=qnorm_a, knorm_a=knorm_a,
             lambda_q1=lambda_q1, lambda_k1=lambda_k1, lambda_q2=lambda_q2, lambda_k2=lambda_k2,
             subln_a=subln_a, conv_w=conv_w, conv_b=conv_b, norm_qa=norm_qa, w_qb=w_qb,
             norm_kva=norm_kva, w_kb=w_kb, w_vb=w_vb, qnorm_c=qnorm_c, knorm_c=knorm_c, w_out=w_out,
             ffn_w_gate=ffn_w_gate, ffn_w_up=ffn_w_up, ffn_w_down=ffn_w_down, router_w=router_w,
             moe_w_gate=moe_w_gate, moe_w_up=moe_w_up, moe_w_down=moe_w_down)
    Bc, Sc, D = x_prompt.shape
    Bl, T, _ = x_sample.shape
    L = w_in.shape[0]
    nrow = 16
    cvec = jnp.concatenate([c, c_ctx[None, :], jnp.zeros((nrow - Bl - 1, D), F32)], axis=0)
    mod = _ada(cvec, w_ada, b_ada)
    rope = _rope_tables(T)
    xp = x_prompt.reshape(Bc * Sc, D)
    xs = x_sample.reshape(Bl * T, D)
    st = [[], [], [], []]
    for l in range(L):
        lw = _layer_weights(l, p)
        l_init = 0.8 - 0.6 * math.exp(-0.3 * l)
        mod_lat = mod[l, :Bl].reshape(Bl, 6, D)
        mod_ctx = mod[l, Bl:Bl + 1].reshape(1, 6, D)
        post = _post_dense if l % 2 == 0 else _post_moe
        heads, state = _mixer(xp, mod_ctx, lw, l_init, B=Bc, T=Sc, ctx=None, rope=None)
        for acc, s in zip(st, state):
            acc.append(s)
        xp = post(xp, *heads, mod_ctx, lw, tm=min(512, Bc * Sc), tiles_per_mod=Bc * Sc)
        ctx = (cache_diff_k[:, l], cache_diff_v[:, l], cache_mla_ckv[:, l], cache_mla_krope[:, l])
        heads, _ = _mixer(xs, mod_lat, lw, l_init, B=Bl, T=T, ctx=ctx, rope=rope)
        tm = min(512, T)
        xs = post(xs, *heads, mod_lat, lw, tm=tm, tiles_per_mod=T // tm)
    new_k = jnp.stack(st[0], axis=1).reshape(Bc, Sc, L, H_A, 2, DA).transpose(0, 2, 1, 3, 4, 5)
    new_v = jnp.stack(st[1], axis=1).reshape(Bc, Sc, L, H_A, V_A).transpose(0, 2, 1, 3, 4)
    new_ckv = jnp.stack(st[2], axis=1).reshape(Bc, Sc, L, KV_RANK).transpose(0, 2, 1, 3)
    new_kr = jnp.stack(st[3], axis=1).reshape(Bc, Sc, L, ROPE_C).transpose(0, 2, 1, 3)
    return (xp.reshape(Bc, Sc, D), xs.reshape(Bl, T, D), new_k, new_v, new_ckv, new_kr)
```

```python
import functools
import math

import jax
import jax.numpy as jnp
from jax import lax
from jax.experimental import pallas as pl
from jax.experimental.pallas import tpu as pltpu

F32 = jnp.float32
BF16 = jnp.bfloat16

GRID_W = 64
H_A = 6
DA = 32
V_A = 2 * DA
W_B = 256
H_C = 6
NOPE_C = 64
ROPE_C = 32
QK_C = NOPE_C + ROPE_C
V_C = 64
Q_RANK = 192
KV_RANK = 128
N_EXPERTS = 8
ROPE_THETA = 10000.0
EPS = 1e-6
LOG2E = 1.4426950408889634

LANES = 128
MXU_DIM = 256
HEAD_PAD = 128

A_W = H_A * 2 * DA
OFF_AQ = 0
OFF_AK = A_W
OFF_AV = 2 * A_W
OFF_BB = 3 * A_W
OFF_BC = OFF_BB + W_B
OFF_BX = OFF_BC + W_B
OFF_CKV = OFF_BX + W_B
OFF_CQ = OFF_CKV + KV_RANK
OFF_CKR = OFF_CQ + Q_RANK
KR_CHUNK = (OFF_CKR // LANES) * LANES
assert OFF_CKR - KR_CHUNK == NOPE_C
IN_EXT = KR_CHUNK + LANES
C_W = H_C * HEAD_PAD
MIX_A = H_A * V_A
MIX_C = H_C * V_C

ROW_GROUP = 32
V_ROWS = V_A + 16
assert V_A == V_C

VMEM_LIMIT = 56 * 1024 * 1024


def _cparams(sem):
    return pltpu.CompilerParams(dimension_semantics=sem, vmem_limit_bytes=VMEM_LIMIT)


def _const_spec(shape):
    nd = len(shape)
    return pl.BlockSpec(shape, lambda *_: (0,) * nd, pipeline_mode=pl.Buffered(1))


def _rms(x, axis=-1):
    return x * lax.rsqrt(jnp.mean(x * x, axis=axis, keepdims=True) + EPS)


def _split_dot(x, w):
    hi = x.astype(BF16)
    lo = (x - hi.astype(F32)).astype(BF16)
    return (jnp.dot(hi, w, preferred_element_type=F32)
            + jnp.dot(lo, w, preferred_element_type=F32))


def _ada_kernel(c_ref, w_ref, b_ref, o_ref):
    c = c_ref[...]
    s = (c * jax.nn.sigmoid(c)).astype(BF16)
    o_ref[0] = jnp.dot(s, w_ref[0].astype(BF16), preferred_element_type=F32) + b_ref[0]


def _ada(cvec, w_ada, b_ada):
    L, D, N6 = w_ada.shape
    R = cvec.shape[0]
    tn = 1536
    assert N6 % tn == 0
    return pl.pallas_call(
        _ada_kernel,
        grid=(L, N6 // tn),
        in_specs=[
            pl.BlockSpec((R, D), lambda l, j: (0, 0)),
            pl.BlockSpec((1, D, tn), lambda l, j: (l, 0, j)),
            pl.BlockSpec((1, 1, tn), lambda l, j: (l, 0, j)),
        ],
        out_specs=pl.BlockSpec((1, R, tn), lambda l, j: (l, 0, j)),
        out_shape=jax.ShapeDtypeStruct((L, R, N6), F32),
        compiler_params=_cparams(("arbitrary", "arbitrary")),
        name="ada_mod",
    )(cvec, w_ada, b_ada.reshape(L, 1, N6))


def _rope(x, tab_ref):
    up = pltpu.roll(x, LANES - 8, 1)
    dn = pltpu.roll(x, 8, 1)
    return x * tab_ref[0] + up * tab_ref[1] + dn * tab_ref[2]


def _rope_T(xT, tab_ref, starts):
    nb = 8
    blocks = [xT[nb * i:nb * (i + 1)] for i in range(xT.shape[0] // nb)]
    for g in starts:
        for k in range(2):
            c, s = tab_ref[2 * k], tab_ref[2 * k + 1]
            b = g // nb + 2 * k
            x1, x2 = blocks[b], blocks[b + 1]
            blocks[b] = x1 * c - x2 * s
            blocks[b + 1] = x1 * s + x2 * c
    return jnp.concatenate(blocks, axis=0)


def _store_vT(v, ref):
    tm = v.shape[0]
    vT = v.T.astype(BF16)
    row = lax.broadcasted_iota(jnp.int32, (V_ROWS - V_A, tm), 0)
    tail = jnp.where(row == 0, 1.0, 0.0).astype(BF16)
    for h in range(v.shape[1] // V_A):
        ref[0, h, 0:V_A, :] = vT[h * V_A:(h + 1) * V_A, :]
        ref[0, h, V_A:V_ROWS, :] = tail


def _head_rms(x, gain_ref, ones_ref):
    sq = x * x
    ss = jnp.concatenate(
        [_split_dot(sq[:, c:c + MXU_DIM], ones_ref[...]) for c in range(0, x.shape[1], MXU_DIM)], axis=-1)
    gain = jnp.concatenate([gain_ref[...]] * (x.shape[1] // HEAD_PAD), axis=-1)
    return x * lax.rsqrt(ss * (1.0 / QK_C) + EPS) * gain


def _mla_kv(ckv_n, kr_placed, wkb_ref, wvb_ref, gkc_ref, ones_ref, rope_ref, kc_ref, vcT_ref):
    cb = ckv_n.astype(BF16)
    kn = jnp.dot(cb, wkb_ref[...], preferred_element_type=F32)
    vc = jnp.dot(cb, wvb_ref[...], preferred_element_type=F32)
    kall = _head_rms(kn + jnp.concatenate([kr_placed] * H_C, axis=-1), gkc_ref, ones_ref)
    for h in range(H_C):
        kp = kall[:, h * HEAD_PAD:(h + 1) * HEAD_PAD]
        if rope_ref is not None:
            kp = _rope(kp, rope_ref)
        kc_ref[h] = kp.astype(BF16)
    _store_vT(vc, vcT_ref)


def _pre_mix_kernel(is_ctx, seg, nsub, *refs):
    refs = list(refs)
    sub = refs[0].shape[0] // nsub
    row_dim = {0: 0}
    n_in = 15 if is_ctx else 18
    if not is_ctx:
        row_dim.update({15: 1, 16: 1, 17: 2})
    for k, dim in enumerate((1, 1, 3, 0, 1, 1, 3) + ((0, 0, 0, 0) if is_ctx else ())):
        row_dim[n_in + k] = dim
    x_ref, mod_ref, nmix_ref, win_ref = refs[:4]
    projs = []
    for r in range(nsub):
        x = x_ref[pl.ds(r * sub, sub), :]
        h = _rms(x) * nmix_ref[...] * (1.0 + mod_ref[0, 1:2, :]) + mod_ref[0, 0:1, :]
        projs.append(jnp.dot(h.astype(BF16), win_ref[...], preferred_element_type=F32))
    for r in range(nsub):
        views = []
        for pos, ref in enumerate(refs):
            if pos in row_dim:
                idx = [slice(None)] * len(ref.shape)
                idx[row_dim[pos]] = pl.ds(r * sub, sub)
                ref = ref.at[tuple(idx)]
            views.append(ref)
        _pre_mix_rows(is_ctx, seg, projs[r], *views)


def _pre_mix_rows(is_ctx, seg, proj, *refs):
    (_, _, _, _, ones_ref, gqk_ref, gqa_ref, gkva_ref, gqc_ref,
     gkc_ref, wqb_ref, wkb_ref, wvb_ref, cw_ref, cb_ref) = refs[:15]
    b32_ref, b128_ref = ones_ref.at[0], ones_ref.at[1]
    refs = refs[15:]
    if is_ctx:
        ra_ref = rc_ref = None
    else:
        ra_ref, rc_ref, rT_ref = refs[:3]
        refs = refs[3:]
    qaT_ref, ka_ref, vaT_ref, ob_ref, qcT_ref, kc_ref, vcT_ref = refs[:7]
    refs = refs[7:]

    aqk = proj[:, OFF_AQ:OFF_AV]
    sq = aqk * aqk
    ss = jnp.concatenate(
        [_split_dot(sq[:, c:c + MXU_DIM], b32_ref[...]) for c in range(0, 2 * A_W, MXU_DIM)], axis=-1)
    aqk = aqk * lax.rsqrt(ss * (1.0 / DA) + EPS) * gqk_ref[...]
    if is_ctx:
        ka_st_ref, va_st_ref, ckv_st_ref, ckr_st_ref = refs
        ka_st_ref[...] = aqk[:, A_W:]
    qaT = (aqk[:, :A_W] * (DA ** -0.5 * LOG2E)).T
    ak = aqk[:, A_W:]
    if not is_ctx:
        qaT = _rope_T(qaT, rT_ref, range(0, A_W, DA))
        ak = jnp.concatenate([_rope(ak[:, c:c + LANES], ra_ref) for c in range(0, A_W, LANES)], axis=-1)
    qaT_ref[...] = qaT.astype(BF16)
    for hh in range(H_A):
        ka_ref[hh] = ak[:, hh * 2 * DA:(hh + 1) * 2 * DA].astype(BF16)
    av = proj[:, OFF_AV:OFF_BB]
    _store_vT(av, vaT_ref)

    bb = proj[:, OFF_BB:OFF_BC]
    u = proj[:, OFF_BC:OFF_BX] * proj[:, OFF_BX:OFF_CKV]
    tm = u.shape[0]
    row = lax.broadcasted_iota(jnp.int32, u.shape, 0) & (seg - 1)
    prev = jnp.where(row == 0, 0.0, pltpu.roll(u, 1, 0))
    nxt = jnp.where(row == seg - 1, 0.0, pltpu.roll(u, tm - 1, 0))
    conv = prev * cw_ref[0:1, :] + u * cw_ref[1:2, :] + nxt * cw_ref[2:3, :] + cb_ref[...]
    ob_ref[...] = (bb * conv).astype(BF16)

    cq = proj[:, OFF_CQ:OFF_CKR]
    cqn = (_rms(cq) * gqa_ref[...]).astype(BF16)
    qc = jnp.dot(cqn, wqb_ref[...], preferred_element_type=F32)
    qcT = (_head_rms(qc, gqc_ref, b128_ref) * (QK_C ** -0.5 * LOG2E)).T
    if not is_ctx:
        qcT = _rope_T(qcT, rT_ref, range(NOPE_C, C_W, HEAD_PAD))
    qcT_ref[...] = qcT.astype(BF16)
    ckv_n = _rms(proj[:, OFF_CKV:OFF_CQ]) * gkva_ref[...]
    krc = proj[:, KR_CHUNK:KR_CHUNK + LANES]
    lane = lax.broadcasted_iota(jnp.int32, krc.shape, 1)
    kr_placed = jnp.where((lane >= NOPE_C) & (lane < QK_C), krc, 0.0)
    _mla_kv(ckv_n, kr_placed, wkb_ref, wvb_ref, gkc_ref, b128_ref, rc_ref, kc_ref, vcT_ref)
    if is_ctx:
        va_st_ref[...] = av
        ckv_st_ref[...] = ckv_n
        ckr_st_ref[...] = krc[:, NOPE_C:QK_C]


def _pre_mix(x, mod, lw, rope, *, B, T, is_ctx, tm, seg, nsub):
    N, D = x.shape
    assert (tm // nsub) % seg == 0 and (tm // nsub) % LANES == 0
    nt = N // tm
    tps = T // tm
    row_spec = lambda w: pl.BlockSpec((tm, w), lambda i: (i, 0))
    in_specs = [
        row_spec(D),
        pl.BlockSpec((1, 6, D), (lambda i: (0, 0, 0)) if is_ctx else (lambda i: (i // tps, 0, 0))),
        _const_spec((1, D)),
        _const_spec((D, IN_EXT)),
        _const_spec((2, MXU_DIM, MXU_DIM)),
        _const_spec((1, 2 * A_W)),
        _const_spec((1, Q_RANK)),
        _const_spec((1, KV_RANK)),
        _const_spec((1, HEAD_PAD)),
        _const_spec((1, HEAD_PAD)),
        _const_spec((Q_RANK, C_W)),
        _const_spec((KV_RANK, C_W)),
        _const_spec((KV_RANK, MIX_C)),
        _const_spec((3, W_B)),
        _const_spec((1, W_B)),
    ]
    args = [x, mod, lw["norm_mix"], lw["w_in"], lw["b32"], lw["gqk"], lw["gqa"], lw["gkva"],
            lw["gqc"], lw["gkc"], lw["w_qb"], lw["w_kb"], lw["w_vb"], lw["conv_w"], lw["conv_b"]]
    if not is_ctx:
        rspec = pl.BlockSpec((3, tm, LANES), lambda i: (0, i % tps, 0))
        in_specs += [rspec, rspec, pl.BlockSpec((4, 8, tm), lambda i: (0, 0, i % tps))]
        args += list(rope)
    colT = lambda w: pl.BlockSpec((w, tm), lambda i: (0, i))
    vT_spec = pl.BlockSpec((1, H_A, V_ROWS, tm), lambda i: (i // tps, 0, 0, i % tps))
    out_specs = [colT(A_W), pl.BlockSpec((H_A, tm, 2 * DA), lambda i: (0, i, 0)), vT_spec, row_spec(W_B),
                 colT(C_W), pl.BlockSpec((H_C, tm, HEAD_PAD), lambda i: (0, i, 0)), vT_spec]
    out_shape = [jax.ShapeDtypeStruct((A_W, N), BF16), jax.ShapeDtypeStruct((H_A, N, 2 * DA), BF16),
                 jax.ShapeDtypeStruct((B, H_A, V_ROWS, T), BF16), jax.ShapeDtypeStruct((N, W_B), BF16),
                 jax.ShapeDtypeStruct((C_W, N), BF16), jax.ShapeDtypeStruct((H_C, N, HEAD_PAD), BF16),
                 jax.ShapeDtypeStruct((B, H_C, V_ROWS, T), BF16)]
    if is_ctx:
        st_w = [A_W, A_W, KV_RANK, ROPE_C]
        out_specs += [row_spec(w) for w in st_w]
        out_shape += [jax.ShapeDtypeStruct((N, w), F32) for w in st_w]
    return pl.pallas_call(
        functools.partial(_pre_mix_kernel, is_ctx, seg, nsub),
        grid=(nt,),
        in_specs=in_specs,
        out_specs=out_specs,
        out_shape=out_shape,
        compiler_params=_cparams(("arbitrary",)),
        name="pre_mix_ctx" if is_ctx else "pre_mix_lat",
    )(*args)


def _cache_kv_kernel(dk_ref, dv_ref, ckv_ref, kr_ref, wkb_ref, wvb_ref, gkc_ref, ones_ref,
                     ka_ref, vaT_ref, kc_ref, vcT_ref, kr_s):
    dk = dk_ref[0, 0]
    for hh in range(H_A):
        ka_ref[hh] = dk[:, hh * 2 * DA:(hh + 1) * 2 * DA].astype(BF16)
    _store_vT(dv_ref[0, 0], vaT_ref)
    kr_s[...] = jnp.zeros_like(kr_s)
    kr_s[:, NOPE_C:QK_C] = kr_ref[0, 0]
    _mla_kv(ckv_ref[0, 0], kr_s[...], wkb_ref, wvb_ref, gkc_ref, ones_ref.at[1], None, kc_ref, vcT_ref)


def _cache_kv(cache, l, lw):
    ck, cv, ckv, ckr = cache
    B, L, P = ck.shape[:3]
    ck = ck.reshape(B, L, P, A_W)
    cv = cv.reshape(B, L, P, A_W)
    layer = lambda w: pl.BlockSpec((1, 1, P, w), lambda i: (i, l, 0, 0))
    vT_spec = pl.BlockSpec((1, H_A, V_ROWS, P), lambda i: (i, 0, 0, 0))
    return pl.pallas_call(
        _cache_kv_kernel,
        grid=(B,),
        in_specs=[layer(A_W), layer(A_W), layer(KV_RANK), layer(ROPE_C), _const_spec((KV_RANK, C_W)),
                  _const_spec((KV_RANK, MIX_C)), _const_spec((1, HEAD_PAD)),
                  _const_spec((2, MXU_DIM, MXU_DIM))],
        out_specs=[pl.BlockSpec((H_A, P, 2 * DA), lambda i: (0, i, 0)), vT_spec,
                   pl.BlockSpec((H_C, P, HEAD_PAD), lambda i: (0, i, 0)), vT_spec],
        out_shape=[jax.ShapeDtypeStruct((H_A, B * P, 2 * DA), BF16),
                   jax.ShapeDtypeStruct((B, H_A, V_ROWS, P), BF16),
                   jax.ShapeDtypeStruct((H_C, B * P, HEAD_PAD), BF16),
                   jax.ShapeDtypeStruct((B, H_C, V_ROWS, P), BF16)],
        scratch_shapes=[pltpu.VMEM((P, HEAD_PAD), F32)],
        compiler_params=_cparams(("arbitrary",)),
        name="cache_kv",
    )(ck, cv, ckv, ckr, lw["w_kb"], lw["w_vb"], lw["gkc"], lw["b32"])


def _colmax(s):
    tk, tq = s.shape
    r = jnp.max(s.reshape(tk // ROW_GROUP, ROW_GROUP, tq), axis=0)
    return jnp.max(r, axis=0, keepdims=True)


def _attn_kernel(diff, has_ctx, G, chunks, tk, l_init, *refs):
    refs = list(refs)
    q_ref = refs.pop(0)
    if has_ctx:
        kc_ref, kl_ref, vc_ref, vl_ref = refs[:4]
        refs = refs[4:]
    else:
        kl_ref, vl_ref = refs[:2]
        kc_ref = vc_ref = None
        refs = refs[2:]
    if diff:
        lam_ref, g_ref = refs[:2]
        refs = refs[2:]
    o_ref, sa, sb, os_ref = refs
    tq = q_ref.shape[-1]
    uw = tq if diff else tq // 2
    bufs = (sa, sb)
    nk = len(chunks)
    dq = q_ref.shape[0] // G
    gpt = 1 if nk % 2 == 0 else (2 if G % 2 == 0 else G)
    ntrip = G // gpt

    def q_units(g):
        start = g * dq
        if not isinstance(start, int):
            start = pl.multiple_of(start, dq)
        qh = q_ref[pl.ds(start, dq), :]
        if diff:
            row = lax.broadcasted_iota(jnp.int32, qh.shape, 0)
            zero = jnp.zeros_like(qh)
            return [jnp.where(row < DA, qh, zero), jnp.where(row >= DA, qh, zero)]
        return [qh[:, :uw], qh[:, uw:]]

    def kslice(h, j):
        src, off = chunks[j]
        return (kc_ref if src == 0 else kl_ref)[h, pl.ds(off, tk), :]

    def vslice(h, j):
        src, off = chunks[j]
        return (vc_ref if src == 0 else vl_ref)[0, h, :, pl.ds(off, tk)]

    def qk(g, qs, j, buf):
        kj = kslice(g, j)
        for u in range(2):
            buf[u] = jnp.dot(kj, qs[u], preferred_element_type=F32)

    def softmax(buf, carry):
        ps, out = [], []
        for u in range(2):
            m, acc = carry[u]
            s = buf[u]
            m_new = jnp.maximum(m, _colmax(s))
            alpha = jnp.exp2(m - m_new)
            ps.append(jnp.exp2(s - m_new).astype(BF16))
            out.append((m_new, alpha * acc))
        return ps, out

    def pv(g, j, ps, st):
        vj = vslice(g, j)
        return [(st[u][0], st[u][1] + jnp.dot(vj, ps[u], preferred_element_type=F32)) for u in range(2)]

    def finish(g, res):
        os_ = [acc[:V_A] / acc[V_A:V_A + 1] for (_, acc) in res]
        if diff:
            lv = lam_ref[...]
            lam = (jnp.exp(jnp.sum(lv[0:1] * lv[1:2], axis=-1, keepdims=True))
                   - jnp.exp(jnp.sum(lv[2:3] * lv[3:4], axis=-1, keepdims=True)) + l_init)
            o = os_[0] - lam * os_[1]
            os_ref[g] = _rms(o, axis=0) * g_ref[...] * (1.0 - l_init)
        else:
            os_ref[g, :, 0:uw] = os_[0]
            os_ref[g, :, uw:tq] = os_[1]

    def trip(t, _):
        step = 0
        g = t * gpt
        qs = q_units(g)
        for gi in range(gpt):
            g = t * gpt + gi
            carry = [(jnp.full((1, uw), -jnp.inf, F32), jnp.zeros((V_ROWS, uw), F32)) for _ in range(2)]
            for j in range(nk):
                src, dst = bufs[step % 2], bufs[(step + 1) % 2]
                qs_next = qs
                if j + 1 < nk:
                    qk(g, qs, j + 1, dst)
                elif gi + 1 < gpt:
                    qs_next = q_units(g + 1)
                    qk(g + 1, qs_next, 0, dst)
                elif ntrip > 1:
                    g_next = jnp.minimum(g + 1, G - 1)
                    qk(g_next, q_units(g_next), 0, dst)
                ps, st = softmax(src, carry)
                carry = pv(g, j, ps, st)
                qs = qs_next
                step += 1
            finish(g, carry)
        return 0

    qk(0, q_units(0), 0, sa)
    if ntrip > 1:
        lax.fori_loop(0, ntrip, trip, 0)
    else:
        trip(0, 0)
    for c in range(G // 2):
        pair = jnp.concatenate([os_ref[2 * c], os_ref[2 * c + 1]], axis=0)
        o_ref[:, c * LANES:(c + 1) * LANES] = pair.T.astype(o_ref.dtype)


def _attention(qT, k_lat, v_lat, k_ctx=None, v_ctx=None, *, diff, tq, tk, lam_vecs=None, subln=None,
               l_init=0.0):
    B, H, _, T = v_lat.shape
    N = qT.shape[1]
    d = k_lat.shape[2]
    has_ctx = k_ctx is not None
    chunks = [(1, o) for o in range(0, T, tk)]
    nq = T // tq
    in_specs = [pl.BlockSpec((qT.shape[0], tq), lambda b, i: (0, b * nq + i))]
    args = [qT]
    if has_ctx:
        P = v_ctx.shape[3]
        chunks = [(0, o) for o in range(0, P, tk)] + chunks
        in_specs += [pl.BlockSpec((H, P, d), lambda b, i: (0, b, 0)),
                     pl.BlockSpec((H, T, d), lambda b, i: (0, b, 0)),
                     pl.BlockSpec((1, H, V_ROWS, P), lambda b, i: (b, 0, 0, 0)),
                     pl.BlockSpec((1, H, V_ROWS, T), lambda b, i: (b, 0, 0, 0))]
        args += [k_ctx, k_lat, v_ctx, v_lat]
    else:
        in_specs += [pl.BlockSpec((H, T, d), lambda b, i: (0, b, 0)),
                     pl.BlockSpec((1, H, V_ROWS, T), lambda b, i: (b, 0, 0, 0))]
        args += [k_lat, v_lat]
    if diff:
        in_specs += [pl.BlockSpec((4, DA), lambda b, i: (0, 0)),
                     pl.BlockSpec((V_A, 1), lambda b, i: (0, 0))]
        args += [lam_vecs, subln]
    uw = tq if diff else tq // 2
    return pl.pallas_call(
        functools.partial(_attn_kernel, diff, has_ctx, H, tuple(chunks), tk, l_init),
        grid=(B, nq),
        in_specs=in_specs,
        out_specs=pl.BlockSpec((tq, H * V_A), lambda b, i: (b * nq + i, 0)),
        out_shape=jax.ShapeDtypeStruct((N, H * V_A), BF16),
        scratch_shapes=[pltpu.VMEM((2, tk, uw), F32), pltpu.VMEM((2, tk, uw), F32),
                        pltpu.VMEM((H, V_A, tq), F32)],
        compiler_params=_cparams(("arbitrary", "arbitrary")),
        name="attn_diff" if diff else "attn_mla",
    )(*args)


def _post_head(x_ref, oa_ref, ob_ref, oc_ref, mod_ref, nffn_ref, wout_ref):
    mix = jnp.concatenate([oa_ref[...], ob_ref[...], oc_ref[...]], axis=-1)
    y = jnp.dot(mix, wout_ref[...], preferred_element_type=F32)
    x1 = x_ref[...] + mod_ref[0, 2:3, :] * y
    h2 = _rms(x1) * nffn_ref[...] * (1.0 + mod_ref[0, 4:5, :]) + mod_ref[0, 3:4, :]
    return x1, h2


def _swiglu(hb, wg, wu, wd):
    g = jnp.dot(hb, wg, preferred_element_type=F32)
    u = jnp.dot(hb, wu, preferred_element_type=F32)
    a = (g * jax.nn.sigmoid(g) * u).astype(BF16)
    return jnp.dot(a, wd, preferred_element_type=F32)


def _post_dense_kernel(fchunks, x_ref, oa_ref, ob_ref, oc_ref, mod_ref, nffn_ref, wout_ref,
                       wg_ref, wu_ref, wd_ref, o_ref):
    x1, h2 = _post_head(x_ref, oa_ref, ob_ref, oc_ref, mod_ref, nffn_ref, wout_ref)
    hb = h2.astype(BF16)
    acc = None
    for (f0, f1) in fchunks:
        part = _swiglu(hb, wg_ref[:, f0:f1], wu_ref[:, f0:f1], wd_ref[f0:f1, :])
        acc = part if acc is None else acc + part
    o_ref[...] = x1 + mod_ref[0, 5:6, :] * acc


def _post_dense(x, oa, ob, oc, mod, lw, *, tm, tiles_per_mod):
    N, D = x.shape
    FF = lw["wg"].shape[1]
    cut = (FF // 2 // MXU_DIM + 1) * MXU_DIM if FF > 2 * MXU_DIM else FF
    fchunks = ((0, cut), (cut, FF)) if cut < FF else ((0, FF),)
    row = lambda w: pl.BlockSpec((tm, w), lambda i: (i, 0))
    return pl.pallas_call(
        functools.partial(_post_dense_kernel, fchunks),
        grid=(N // tm,),
        in_specs=[row(D), row(MIX_A), row(W_B), row(MIX_C),
                  pl.BlockSpec((1, 6, D), lambda i: (i // tiles_per_mod, 0, 0)),
                  _const_spec((1, D)), _const_spec(lw["w_out"].shape), _const_spec((D, FF)),
                  _const_spec((D, FF)), _const_spec((FF, D))],
        out_specs=row(D),
        out_shape=jax.ShapeDtypeStruct((N, D), F32),
        compiler_params=_cparams(("arbitrary",)),
        name="post_dense",
    )(x, oa, ob, oc, mod, lw["norm_ffn"], lw["w_out"], lw["wg"], lw["wu"], lw["wd"])


TOK_SUB = 8
MOE_ROWS = 512
DISPATCH_TOKENS = 512
COMBINE_TOKENS = 512


def _to_token_tiles(x, ref):
    tm = x.shape[0]
    for a in range(TOK_SUB):
        ref[pl.ds(a, tm, stride=TOK_SUB), :] = x[:, a * LANES:(a + 1) * LANES]


def _from_token_tiles(ref, tm):
    return jnp.concatenate([ref[pl.ds(a, tm, stride=TOK_SUB), :] for a in range(TOK_SUB)], axis=-1)


def _router_kernel(x_ref, oa_ref, ob_ref, oc_ref, mod_ref, nffn_ref, wout_ref, rw_ref,
                   x1_ref, h2t_ref, route_ref):
    x1, h2 = _post_head(x_ref, oa_ref, ob_ref, oc_ref, mod_ref, nffn_ref, wout_ref)
    x1_ref[...] = x1
    _to_token_tiles(h2, h2t_ref)
    rw = rw_ref[...]
    rhi = rw.astype(BF16)
    rlo = (rw - rhi.astype(F32)).astype(BF16)
    hhi = h2.astype(BF16)
    hlo = (h2 - hhi.astype(F32)).astype(BF16)
    both = jnp.dot(hhi, jnp.concatenate([rhi, rlo], axis=-1), preferred_element_type=F32)
    logits = (both[:, :LANES] + both[:, LANES:]
              + jnp.dot(hlo, rhi, preferred_element_type=F32))
    lane = lax.broadcasted_iota(jnp.int32, logits.shape, 1)
    neg = -jnp.inf
    lg = jnp.where(lane < N_EXPERTS, logits, neg)
    v1 = jnp.max(lg, axis=-1, keepdims=True)
    i1 = jnp.min(jnp.where(lg == v1, lane, LANES), axis=-1, keepdims=True)
    lg2 = jnp.where(lane == i1, neg, lg)
    v2 = jnp.max(lg2, axis=-1, keepdims=True)
    i2 = jnp.min(jnp.where(lg2 == v2, lane, LANES), axis=-1, keepdims=True)
    e2 = jnp.exp(v2 - v1)
    g1 = 1.0 / (1.0 + e2)
    g2 = e2 / (1.0 + e2)
    route_ref[...] = jnp.where(lane == 0, g1, jnp.where(lane == 1, g2, jnp.where(
        lane == 2, i1.astype(F32), jnp.where(lane == 3, i2.astype(F32), 0.0))))


def _router(x, oa, ob, oc, mod, lw, *, tm, tiles_per_mod):
    N, D = x.shape
    assert D == TOK_SUB * LANES
    row = lambda w: pl.BlockSpec((tm, w), lambda i: (i, 0))
    return pl.pallas_call(
        _router_kernel,
        grid=(N // tm,),
        in_specs=[row(D), row(MIX_A), row(W_B), row(MIX_C),
                  pl.BlockSpec((1, 6, D), lambda i: (i // tiles_per_mod, 0, 0)),
                  _const_spec((1, D)), _const_spec(lw["w_out"].shape), _const_spec((D, LANES))],
        out_specs=[row(D), pl.BlockSpec((tm * TOK_SUB, LANES), lambda i: (i, 0)), row(LANES)],
        out_shape=[jax.ShapeDtypeStruct((N, D), F32), jax.ShapeDtypeStruct((N * TOK_SUB, LANES), F32),
                   jax.ShapeDtypeStruct((N, LANES), F32)],
        compiler_params=_cparams(("arbitrary",)),
        name="moe_router",
    )(x, oa, ob, oc, mod, lw["norm_ffn"], lw["w_out"], lw["router"])


def _moe_plan(route, N):
    R, E = MOE_ROWS, N_EXPERTS
    es = route[:, 2:4].astype(jnp.int32).reshape(-1)
    oh = (es[:, None] == jnp.arange(E, dtype=jnp.int32)[None, :]).astype(jnp.int32)
    rank = jnp.sum((jnp.cumsum(oh, axis=0) - oh) * oh, axis=1)
    cnt = jnp.sum(oh, axis=0)
    gsz = ((cnt + R - 1) // R) * R
    gend = jnp.cumsum(gsz)
    pos = ((gend - gsz)[es] + rank).astype(jnp.int32)
    n_tiles = (2 * N) // R + E
    tile_start = jnp.arange(n_tiles, dtype=jnp.int32) * R
    tile_expert = jnp.minimum(jnp.sum(tile_start[:, None] >= gend[None, :], axis=1), E - 1).astype(jnp.int32)
    return pos, tile_expert


def _tile_rows(ref, row):
    return ref.at[pl.ds(row * TOK_SUB, TOK_SUB), :]


def _dispatch_kernel(pos_ref, h2t_ref, xs_in_hbm, xs_hbm, buf, sem):
    del xs_in_hbm
    i = pl.program_id(0)
    n = pl.num_programs(0)
    tmd = DISPATCH_TOKENS
    slot = i % 2

    def step_wait(buf_slot):
        for _ in range(2):
            pltpu.make_async_copy(buf.at[buf_slot], xs_hbm.at[pl.ds(0, tmd * TOK_SUB), :],
                                  sem.at[buf_slot]).wait()

    @pl.when(i >= 2)
    def _():
        step_wait(slot)

    buf[slot] = h2t_ref[...]
    for r in range(tmd):
        t = i * tmd + r
        for k in range(2):
            pltpu.make_async_copy(_tile_rows(buf.at[slot], r), _tile_rows(xs_hbm, pos_ref[2 * t + k]),
                                  sem.at[slot]).start(priority=k)

    @pl.when(i == n - 1)
    def _():
        step_wait(slot)

        @pl.when(n >= 2)
        def _():
            step_wait(1 - slot)


def _dispatch(h2t, pos, n_rows):
    N = pos.shape[0] // 2
    xs0 = jnp.zeros((n_rows * TOK_SUB, LANES), F32)
    tile = DISPATCH_TOKENS * TOK_SUB
    grid_spec = pltpu.PrefetchScalarGridSpec(
        num_scalar_prefetch=1,
        grid=(N // DISPATCH_TOKENS,),
        in_specs=[pl.BlockSpec((tile, LANES), lambda i, pos: (i, 0)), pl.BlockSpec(memory_space=pl.ANY)],
        out_specs=pl.BlockSpec(memory_space=pl.ANY),
        scratch_shapes=[pltpu.VMEM((2, tile, LANES), F32), pltpu.SemaphoreType.DMA((2,))],
    )
    return pl.pallas_call(
        _dispatch_kernel,
        grid_spec=grid_spec,
        out_shape=jax.ShapeDtypeStruct(xs0.shape, F32),
        input_output_aliases={2: 0},
        compiler_params=_cparams(("arbitrary",)),
        name="moe_dispatch",
    )(pos, h2t, xs0)


def _experts_kernel(te_ref, x_ref, wg_ref, wu_ref, wd_ref, y_ref):
    xb = _from_token_tiles(x_ref, MOE_ROWS).astype(BF16)
    _to_token_tiles(_swiglu(xb, wg_ref[0], wu_ref[0], wd_ref[0]), y_ref)


def _experts(xs, tile_expert, lw):
    E, D, FF = lw["wg"].shape
    R = MOE_ROWS
    n_tiles = tile_expert.shape[0]
    wspec = lambda shape: pl.BlockSpec((1,) + shape, lambda i, te: (te[i], 0, 0))
    rows = pl.BlockSpec((R * TOK_SUB, LANES), lambda i, te: (i, 0))
    grid_spec = pltpu.PrefetchScalarGridSpec(
        num_scalar_prefetch=1,
        grid=(n_tiles,),
        in_specs=[rows, wspec((D, FF)), wspec((D, FF)), wspec((FF, D))],
        out_specs=rows,
    )
    return pl.pallas_call(
        _experts_kernel,
        grid_spec=grid_spec,
        out_shape=jax.ShapeDtypeStruct(xs.shape, F32),
        compiler_params=_cparams(("arbitrary",)),
        name="moe_experts",
    )(tile_expert, xs, lw["wg"], lw["wu"], lw["wd"])


def _combine_kernel(pos_ref, x1_ref, ys_hbm, route_ref, mod_ref, o_ref, ybuf, sem):
    tm = x1_ref.shape[0]
    i = pl.program_id(0)
    n = pl.num_programs(0)
    slot = i % 2

    def issue(tile, buf_slot):
        for r in range(tm):
            t = tile * tm + r
            for k in range(2):
                pltpu.make_async_copy(_tile_rows(ys_hbm, pos_ref[2 * t + k]),
                                      _tile_rows(ybuf.at[buf_slot, k], r), sem.at[buf_slot]).start(priority=k)

    @pl.when(i == 0)
    def _():
        issue(0, 0)

    for k in range(2):
        pltpu.make_async_copy(ys_hbm.at[pl.ds(0, tm * TOK_SUB), :], ybuf.at[slot, k], sem.at[slot]).wait()

    @pl.when(i + 1 < n)
    def _():
        issue(i + 1, 1 - slot)

    g = route_ref[...]
    y = (g[:, 0:1] * _from_token_tiles(ybuf.at[slot, 0], tm)
         + g[:, 1:2] * _from_token_tiles(ybuf.at[slot, 1], tm))
    o_ref[...] = x1_ref[...] + mod_ref[0, 5:6, :] * y


def _combine(x1, ys, pos, route, mod, *, tm, tiles_per_mod):
    N, D = x1.shape
    row = lambda w: pl.BlockSpec((tm, w), lambda i, pos: (i, 0))
    grid_spec = pltpu.PrefetchScalarGridSpec(
        num_scalar_prefetch=1,
        grid=(N // tm,),
        in_specs=[row(D), pl.BlockSpec(memory_space=pl.ANY), row(LANES),
                  pl.BlockSpec((1, 6, D), lambda i, pos: (i // tiles_per_mod, 0, 0))],
        out_specs=row(D),
        scratch_shapes=[pltpu.VMEM((2, 2, tm * TOK_SUB, LANES), F32), pltpu.SemaphoreType.DMA((2,))],
    )
    return pl.pallas_call(
        _combine_kernel,
        grid_spec=grid_spec,
        out_shape=jax.ShapeDtypeStruct((N, D), F32),
        compiler_params=_cparams(("arbitrary",)),
        name="moe_combine",
    )(pos, x1, ys, route, mod)


def _post_moe(x, oa, ob, oc, mod, lw, *, tm, tiles_per_mod):
    N = x.shape[0]
    x1, h2t, route = _router(x, oa, ob, oc, mod, lw, tm=tm, tiles_per_mod=tiles_per_mod)
    pos, tile_expert = _moe_plan(route, N)
    xs = _dispatch(h2t, pos, tile_expert.shape[0] * MOE_ROWS)
    ys = _experts(xs, tile_expert, lw)
    tmc = min(COMBINE_TOKENS, N)
    return _combine(x1, ys, pos, route, mod, tm=tmc, tiles_per_mod=tiles_per_mod * (tm // tmc))


def _rope_tables(T):
    t = jnp.arange(T, dtype=jnp.int32)
    rows = (t // GRID_W).astype(F32)
    cols = (t % GRID_W).astype(F32)
    n = ROPE_C // 4
    inv = jnp.power(ROPE_THETA, -jnp.arange(n, dtype=F32) / n)
    j = jnp.arange(ROPE_C)
    pos = jnp.where(j[None, :] < ROPE_C // 2, rows[:, None], cols[:, None])
    ang = pos * inv[j % n][None, :]
    cos = jnp.cos(ang)
    sin = jnp.sin(ang)
    first = ((j % (2 * n)) < n)[None, :]
    s_up = jnp.where(first, -sin, 0.0)
    s_dn = jnp.where(first, 0.0, sin)
    tab32 = jnp.stack([cos, s_up, s_dn])
    tab_a = jnp.tile(tab32, (1, 1, LANES // ROPE_C))
    ident = jnp.stack([jnp.ones((T, NOPE_C), F32), jnp.zeros((T, NOPE_C), F32), jnp.zeros((T, NOPE_C), F32)])
    tail = jnp.stack([jnp.ones((T, HEAD_PAD - QK_C), F32), jnp.zeros((T, HEAD_PAD - QK_C), F32),
                      jnp.zeros((T, HEAD_PAD - QK_C), F32)])
    tab_c = jnp.concatenate([ident, tab32, tail], axis=-1)
    ang_r = inv[:, None] * rows[None, :]
    ang_c = inv[:, None] * cols[None, :]
    tab_t = jnp.stack([jnp.cos(ang_r), jnp.sin(ang_r), jnp.cos(ang_c), jnp.sin(ang_c)])
    return tab_a, tab_c, tab_t


def _pad_heads(w, width):
    lead = w.shape[:-1]
    w = w.reshape(lead + (H_C, width))
    w = jnp.pad(w, [(0, 0)] * len(lead) + [(0, 0), (0, HEAD_PAD - width)])
    return w.reshape(lead + (C_W,))


def _layer_weights(l, p):
    w_in = p["w_in"][l]
    D = w_in.shape[0]
    o_cq = 3 * A_W + 3 * W_B
    o_ckv = o_cq + Q_RANK
    o_ckr = o_ckv + KV_RANK
    w_ext = jnp.concatenate(
        [w_in[:, :o_cq], w_in[:, o_ckv:o_ckr], w_in[:, o_cq:o_ckv], w_in[:, o_ckr:],
         jnp.zeros((D, IN_EXT - OFF_CKR - ROPE_C), F32)], axis=1).astype(BF16)
    g = jnp.arange(MXU_DIM) // DA
    lw = {
        "w_in": w_ext,
        "b32": jnp.stack([(g[:, None] == g[None, :]), (g[:, None] // 4 == g[None, :] // 4)]).astype(BF16),
        "norm_mix": p["norm_mix"][l][None, :],
        "norm_ffn": p["norm_ffn"][l][None, :],
        "gqk": jnp.concatenate([jnp.tile(p["qnorm_a"][l], 2 * H_A), jnp.tile(p["knorm_a"][l], 2 * H_A)])[None, :],
        "gqa": p["norm_qa"][l][None, :],
        "gkva": p["norm_kva"][l][None, :],
        "gqc": jnp.pad(p["qnorm_c"][l], (0, HEAD_PAD - QK_C))[None, :],
        "gkc": jnp.pad(p["knorm_c"][l], (0, HEAD_PAD - QK_C))[None, :],
        "w_qb": _pad_heads(p["w_qb"][l], QK_C).astype(BF16),
        "w_kb": _pad_heads(p["w_kb"][l], NOPE_C).astype(BF16),
        "w_vb": p["w_vb"][l].astype(BF16),
        "conv_w": p["conv_w"][l],
        "conv_b": p["conv_b"][l][None, :],
        "w_out": p["w_out"][l].astype(BF16),
        "lam": jnp.stack([p["lambda_q1"][l], p["lambda_k1"][l], p["lambda_q2"][l], p["lambda_k2"][l]]),
        "subln": p["subln_a"][l][:, None],
    }
    i = l // 2
    if l % 2 == 0:
        lw["wg"] = p["ffn_w_gate"][i].astype(BF16)
        lw["wu"] = p["ffn_w_up"][i].astype(BF16)
        lw["wd"] = p["ffn_w_down"][i].astype(BF16)
    else:
        lw["router"] = jnp.pad(p["router_w"][i], ((0, 0), (0, LANES - N_EXPERTS)))
        lw["wg"] = p["moe_w_gate"][i].astype(BF16)
        lw["wu"] = p["moe_w_up"][i].astype(BF16)
        lw["wd"] = p["moe_w_down"][i].astype(BF16)
    return lw


def _mixer(x, mod, lw, l_init, *, B, T, ctx, rope):
    is_ctx = ctx is None
    tm = T if is_ctx else min(512, T)
    seg = T if is_ctx else GRID_W
    nsub = 1 if is_ctx else max(1, tm // 256)
    outs = _pre_mix(x, mod, lw, rope, B=B, T=T, is_ctx=is_ctx, tm=tm, seg=seg, nsub=nsub)
    qaT, ka, vaT, ob, qcT, kc, vcT = outs[:7]
    tq = min(256, T)
    tq_c = min(2 * tq, T)
    if is_ctx:
        tk = 512 if T % 512 == 0 else 256
        o_a = _attention(qaT, ka, vaT, diff=True, tq=tq, tk=tk, lam_vecs=lw["lam"], subln=lw["subln"],
                         l_init=l_init)
        o_c = _attention(qcT, kc, vcT, diff=False, tq=tq_c, tk=tk)
    else:
        cache, l = ctx
        P = cache[0].shape[2]
        tk = 512 if (P % 512 == 0 and T % 512 == 0) else 256
        ka_ctx, va_ctx, kc_ctx, vc_ctx = _cache_kv(cache, l, lw)
        o_a = _attention(qaT, ka, vaT, ka_ctx, va_ctx, diff=True, tq=tq, tk=tk,
                         lam_vecs=lw["lam"], subln=lw["subln"], l_init=l_init)
        o_c = _attention(qcT, kc, vcT, kc_ctx, vc_ctx, diff=False, tq=tq_c, tk=tk)
    return (o_a, ob, o_c), outs[7:]


def kernel(x_prompt, x_sample, cache_diff_k, cache_diff_v, cache_mla_ckv, cache_mla_krope, c, c_ctx, w_ada, b_ada, norm_mix, norm_ffn, w_in, qnorm_a, knorm_a, lambda_q1, lambda_k1, lambda_q2, lambda_k2, subln_a, conv_w, conv_b, norm_qa, w_qb, norm_kva, w_kb, w_vb, qnorm_c, knorm_c, w_out, ffn_w_gate, ffn_w_up, ffn_w_down, router_w, moe_w_gate, moe_w_up, moe_w_down):
    p = dict(norm_mix=norm_mix, norm_ffn=norm_ffn, w_in=w_in, qnorm_a=qnorm_a, knorm_a=knorm_a,
             lambda_q1=lambda_q1, lambda_k1=lambda_k1, lambda_q2=lambda_q2, lambda_k2=lambda_k2,
             subln_a=subln_a, conv_w=conv_w, conv_b=conv_b, norm_qa=norm_qa, w_qb=w_qb,
             norm_kva=norm_kva, w_kb=w_kb, w_vb=w_vb, qnorm_c=qnorm_c, knorm_c=knorm_c, w_out=w_out,
             ffn_w_gate=ffn_w_gate, ffn_w_up=ffn_w_up, ffn_w_down=ffn_w_down, router_w=router_w,
             moe_w_gate=moe_w_gate, moe_w_up=moe_w_up, moe_w_down=moe_w_down)
    Bc, Sc, D = x_prompt.shape
    Bl, T, _ = x_sample.shape
    L = w_in.shape[0]
    nrow = 16
    cvec = jnp.concatenate([c, c_ctx[None, :], jnp.zeros((nrow - Bl - 1, D), F32)], axis=0)
    mod = _ada(cvec, w_ada, b_ada)
    rope = _rope_tables(T)
    xp = x_prompt.reshape(Bc * Sc, D)
    xs = x_sample.reshape(Bl * T, D)
    st = [[], [], [], []]
    for l in range(L):
        lw = _layer_weights(l, p)
        l_init = 0.8 - 0.6 * math.exp(-0.3 * l)
        mod_lat = mod[l, :Bl].reshape(Bl, 6, D)
        mod_ctx = mod[l, Bl:Bl + 1].reshape(1, 6, D)
        post = _post_dense if l % 2 == 0 else _post_moe
        heads, state = _mixer(xp, mod_ctx, lw, l_init, B=Bc, T=Sc, ctx=None, rope=None)
        for acc, s in zip(st, state):
            acc.append(s)
        xp = post(xp, *heads, mod_ctx, lw, tm=min(512, Bc * Sc), tiles_per_mod=Bc * Sc)
        ctx = ((cache_diff_k, cache_diff_v, cache_mla_ckv, cache_mla_krope), l)
        heads, _ = _mixer(xs, mod_lat, lw, l_init, B=Bl, T=T, ctx=ctx, rope=rope)
        tm = min(512, T)
        xs = post(xs, *heads, mod_lat, lw, tm=tm, tiles_per_mod=T // tm)
    new_k = jnp.stack(st[0], axis=1).reshape(Bc, Sc, L, H_A, 2, DA).transpose(0, 2, 1, 3, 4, 5)
    new_v = jnp.stack(st[1], axis=1).reshape(Bc, Sc, L, H_A, V_A).transpose(0, 2, 1, 3, 4)
    new_ckv = jnp.stack(st[2], axis=1).reshape(Bc, Sc, L, KV_RANK).transpose(0, 2, 1, 3)
    new_kr = jnp.stack(st[3], axis=1).reshape(Bc, Sc, L, ROPE_C).transpose(0, 2, 1, 3)
    return (xp.reshape(Bc, Sc, D), xs.reshape(Bl, T, D), new_k, new_v, new_ckv, new_kr)
```

```python
import functools
import math

import jax
import jax.numpy as jnp
from jax import lax
from jax.experimental import pallas as pl
from jax.experimental.pallas import tpu as pltpu

F32 = jnp.float32
BF16 = jnp.bfloat16

GRID_W = 64
H_A = 6
DA = 32
V_A = 2 * DA
W_B = 256
H_C = 6
NOPE_C = 64
ROPE_C = 32
QK_C = NOPE_C + ROPE_C
V_C = 64
Q_RANK = 192
KV_RANK = 128
N_EXPERTS = 8
ROPE_THETA = 10000.0
EPS = 1e-6
LOG2E = 1.4426950408889634

LANES = 128
MXU_DIM = 256
HEAD_PAD = 128

A_W = H_A * 2 * DA
OFF_AQ = 0
OFF_AK = A_W
OFF_AV = 2 * A_W
OFF_BB = 3 * A_W
OFF_BC = OFF_BB + W_B
OFF_BX = OFF_BC + W_B
OFF_CKV = OFF_BX + W_B
OFF_CQ = OFF_CKV + KV_RANK
OFF_CKR = OFF_CQ + Q_RANK
KR_CHUNK = (OFF_CKR // LANES) * LANES
assert OFF_CKR - KR_CHUNK == NOPE_C
IN_EXT = KR_CHUNK + LANES
C_W = H_C * HEAD_PAD
MIX_A = H_A * V_A
MIX_C = H_C * V_C

ROW_GROUP = 32
V_ROWS = V_A + 16
assert V_A == V_C

VMEM_LIMIT = 56 * 1024 * 1024


def _cparams(sem):
    return pltpu.CompilerParams(dimension_semantics=sem, vmem_limit_bytes=VMEM_LIMIT)


def _const_spec(shape):
    nd = len(shape)
    return pl.BlockSpec(shape, lambda *_: (0,) * nd, pipeline_mode=pl.Buffered(1))


def _rms(x, axis=-1):
    return x * lax.rsqrt(jnp.mean(x * x, axis=axis, keepdims=True) + EPS)


def _split_dot(x, w):
    hi = x.astype(BF16)
    lo = (x - hi.astype(F32)).astype(BF16)
    return (jnp.dot(hi, w, preferred_element_type=F32)
            + jnp.dot(lo, w, preferred_element_type=F32))


def _ada_kernel(c_ref, w_ref, b_ref, o_ref):
    c = c_ref[...]
    s = (c * jax.nn.sigmoid(c)).astype(BF16)
    o_ref[0] = jnp.dot(s, w_ref[0].astype(BF16), preferred_element_type=F32) + b_ref[0]


def _ada(cvec, w_ada, b_ada):
    L, D, N6 = w_ada.shape
    R = cvec.shape[0]
    tn = 1536
    assert N6 % tn == 0
    return pl.pallas_call(
        _ada_kernel,
        grid=(L, N6 // tn),
        in_specs=[
            pl.BlockSpec((R, D), lambda l, j: (0, 0)),
            pl.BlockSpec((1, D, tn), lambda l, j: (l, 0, j)),
            pl.BlockSpec((1, 1, tn), lambda l, j: (l, 0, j)),
        ],
        out_specs=pl.BlockSpec((1, R, tn), lambda l, j: (l, 0, j)),
        out_shape=jax.ShapeDtypeStruct((L, R, N6), F32),
        compiler_params=_cparams(("arbitrary", "arbitrary")),
        name="ada_mod",
    )(cvec, w_ada, b_ada.reshape(L, 1, N6))


def _rope(x, tab_ref):
    up = pltpu.roll(x, LANES - 8, 1)
    dn = pltpu.roll(x, 8, 1)
    return x * tab_ref[0] + up * tab_ref[1] + dn * tab_ref[2]


def _rope_T(xT, tab_ref, starts):
    nb = 8
    blocks = [xT[nb * i:nb * (i + 1)] for i in range(xT.shape[0] // nb)]
    for g in starts:
        for k in range(2):
            c, s = tab_ref[2 * k], tab_ref[2 * k + 1]
            b = g // nb + 2 * k
            x1, x2 = blocks[b], blocks[b + 1]
            blocks[b] = x1 * c - x2 * s
            blocks[b + 1] = x1 * s + x2 * c
    return jnp.concatenate(blocks, axis=0)


def _store_vT(v, ref):
    tm = v.shape[0]
    vT = v.T.astype(BF16)
    row = lax.broadcasted_iota(jnp.int32, (V_ROWS - V_A, tm), 0)
    tail = jnp.where(row == 0, 1.0, 0.0).astype(BF16)
    for h in range(v.shape[1] // V_A):
        ref[0, h, 0:V_A, :] = vT[h * V_A:(h + 1) * V_A, :]
        ref[0, h, V_A:V_ROWS, :] = tail


def _head_rms(x, gain_ref, ones_ref):
    sq = x * x
    ss = jnp.concatenate(
        [_split_dot(sq[:, c:c + MXU_DIM], ones_ref[...]) for c in range(0, x.shape[1], MXU_DIM)], axis=-1)
    gain = jnp.concatenate([gain_ref[...]] * (x.shape[1] // HEAD_PAD), axis=-1)
    return x * lax.rsqrt(ss * (1.0 / QK_C) + EPS) * gain


def _mla_kv(ckv_n, kr_placed, wkb_ref, wvb_ref, gkc_ref, ones_ref, rope_ref, kc_ref, vcT_ref):
    cb = ckv_n.astype(BF16)
    kn = jnp.dot(cb, wkb_ref[...], preferred_element_type=F32)
    vc = jnp.dot(cb, wvb_ref[...], preferred_element_type=F32)
    kall = _head_rms(kn + jnp.concatenate([kr_placed] * H_C, axis=-1), gkc_ref, ones_ref)
    for h in range(H_C):
        kp = kall[:, h * HEAD_PAD:(h + 1) * HEAD_PAD]
        if rope_ref is not None:
            kp = _rope(kp, rope_ref)
        kc_ref[h] = kp.astype(BF16)
    _store_vT(vc, vcT_ref)


def _pre_mix_kernel(is_ctx, seg, nsub, *refs):
    refs = list(refs)
    sub = refs[0].shape[0] // nsub
    row_dim = {0: 0}
    n_in = 15 if is_ctx else 18
    if not is_ctx:
        row_dim.update({15: 1, 16: 1, 17: 2})
    for k, dim in enumerate((1, 1, 3, 0, 1, 1, 3) + ((0, 0, 0, 0) if is_ctx else ())):
        row_dim[n_in + k] = dim
    x_ref, mod_ref, nmix_ref, win_ref = refs[:4]
    projs = []
    for r in range(nsub):
        x = x_ref[pl.ds(r * sub, sub), :]
        h = _rms(x) * nmix_ref[...] * (1.0 + mod_ref[0, 1:2, :]) + mod_ref[0, 0:1, :]
        projs.append(jnp.dot(h.astype(BF16), win_ref[...], preferred_element_type=F32))
    for r in range(nsub):
        views = []
        for pos, ref in enumerate(refs):
            if pos in row_dim:
                idx = [slice(None)] * len(ref.shape)
                idx[row_dim[pos]] = pl.ds(r * sub, sub)
                ref = ref.at[tuple(idx)]
            views.append(ref)
        _pre_mix_rows(is_ctx, seg, projs[r], *views)


def _pre_mix_rows(is_ctx, seg, proj, *refs):
    (_, _, _, _, ones_ref, gqk_ref, gqa_ref, gkva_ref, gqc_ref,
     gkc_ref, wqb_ref, wkb_ref, wvb_ref, cw_ref, cb_ref) = refs[:15]
    b32_ref, b128_ref = ones_ref.at[0], ones_ref.at[1]
    refs = refs[15:]
    if is_ctx:
        ra_ref = rc_ref = None
    else:
        ra_ref, rc_ref, rT_ref = refs[:3]
        refs = refs[3:]
    qaT_ref, ka_ref, vaT_ref, ob_ref, qcT_ref, kc_ref, vcT_ref = refs[:7]
    refs = refs[7:]

    aqk = proj[:, OFF_AQ:OFF_AV]
    sq = aqk * aqk
    ss = jnp.concatenate(
        [_split_dot(sq[:, c:c + MXU_DIM], b32_ref[...]) for c in range(0, 2 * A_W, MXU_DIM)], axis=-1)
    aqk = aqk * lax.rsqrt(ss * (1.0 / DA) + EPS) * gqk_ref[...]
    if is_ctx:
        ka_st_ref, va_st_ref, ckv_st_ref, ckr_st_ref = refs
        ka_st_ref[...] = aqk[:, A_W:]
    qaT = (aqk[:, :A_W] * (DA ** -0.5 * LOG2E)).T
    ak = aqk[:, A_W:]
    if not is_ctx:
        qaT = _rope_T(qaT, rT_ref, range(0, A_W, DA))
        ak = jnp.concatenate([_rope(ak[:, c:c + LANES], ra_ref) for c in range(0, A_W, LANES)], axis=-1)
    qaT_ref[...] = qaT.astype(BF16)
    for hh in range(H_A):
        ka_ref[hh] = ak[:, hh * 2 * DA:(hh + 1) * 2 * DA].astype(BF16)
    av = proj[:, OFF_AV:OFF_BB]
    _store_vT(av, vaT_ref)

    bb = proj[:, OFF_BB:OFF_BC]
    u = proj[:, OFF_BC:OFF_BX] * proj[:, OFF_BX:OFF_CKV]
    tm = u.shape[0]
    row = lax.broadcasted_iota(jnp.int32, u.shape, 0) & (seg - 1)
    prev = jnp.where(row == 0, 0.0, pltpu.roll(u, 1, 0))
    nxt = jnp.where(row == seg - 1, 0.0, pltpu.roll(u, tm - 1, 0))
    conv = prev * cw_ref[0:1, :] + u * cw_ref[1:2, :] + nxt * cw_ref[2:3, :] + cb_ref[...]
    ob_ref[...] = (bb * conv).astype(BF16)

    cq = proj[:, OFF_CQ:OFF_CKR]
    cqn = (_rms(cq) * gqa_ref[...]).astype(BF16)
    qc = jnp.dot(cqn, wqb_ref[...], preferred_element_type=F32)
    qcT = (_head_rms(qc, gqc_ref, b128_ref) * (QK_C ** -0.5 * LOG2E)).T
    if not is_ctx:
        qcT = _rope_T(qcT, rT_ref, range(NOPE_C, C_W, HEAD_PAD))
    qcT_ref[...] = qcT.astype(BF16)
    ckv_n = _rms(proj[:, OFF_CKV:OFF_CQ]) * gkva_ref[...]
    krc = proj[:, KR_CHUNK:KR_CHUNK + LANES]
    lane = lax.broadcasted_iota(jnp.int32, krc.shape, 1)
    kr_placed = jnp.where((lane >= NOPE_C) & (lane < QK_C), krc, 0.0)
    _mla_kv(ckv_n, kr_placed, wkb_ref, wvb_ref, gkc_ref, b128_ref, rc_ref, kc_ref, vcT_ref)
    if is_ctx:
        va_st_ref[...] = av
        ckv_st_ref[...] = ckv_n
        ckr_st_ref[...] = krc[:, NOPE_C:QK_C]


def _pre_mix(x, mod, lw, rope, *, B, T, is_ctx, tm, seg, nsub):
    N, D = x.shape
    assert (tm // nsub) % seg == 0 and (tm // nsub) % LANES == 0
    nt = N // tm
    tps = T // tm
    row_spec = lambda w: pl.BlockSpec((tm, w), lambda i: (i, 0))
    in_specs = [
        row_spec(D),
        pl.BlockSpec((1, 6, D), (lambda i: (0, 0, 0)) if is_ctx else (lambda i: (i // tps, 0, 0))),
        _const_spec((1, D)),
        _const_spec((D, IN_EXT)),
        _const_spec((2, MXU_DIM, MXU_DIM)),
        _const_spec((1, 2 * A_W)),
        _const_spec((1, Q_RANK)),
        _const_spec((1, KV_RANK)),
        _const_spec((1, HEAD_PAD)),
        _const_spec((1, HEAD_PAD)),
        _const_spec((Q_RANK, C_W)),
        _const_spec((KV_RANK, C_W)),
        _const_spec((KV_RANK, MIX_C)),
        _const_spec((3, W_B)),
        _const_spec((1, W_B)),
    ]
    args = [x, mod, lw["norm_mix"], lw["w_in"], lw["b32"], lw["gqk"], lw["gqa"], lw["gkva"],
            lw["gqc"], lw["gkc"], lw["w_qb"], lw["w_kb"], lw["w_vb"], lw["conv_w"], lw["conv_b"]]
    if not is_ctx:
        rspec = pl.BlockSpec((3, tm, LANES), lambda i: (0, i % tps, 0))
        in_specs += [rspec, rspec, pl.BlockSpec((4, 8, tm), lambda i: (0, 0, i % tps))]
        args += list(rope)
    colT = lambda w: pl.BlockSpec((w, tm), lambda i: (0, i))
    vT_spec = pl.BlockSpec((1, H_A, V_ROWS, tm), lambda i: (i // tps, 0, 0, i % tps))
    out_specs = [colT(A_W), pl.BlockSpec((H_A, tm, 2 * DA), lambda i: (0, i, 0)), vT_spec, row_spec(W_B),
                 colT(C_W), pl.BlockSpec((H_C, tm, HEAD_PAD), lambda i: (0, i, 0)), vT_spec]
    out_shape = [jax.ShapeDtypeStruct((A_W, N), BF16), jax.ShapeDtypeStruct((H_A, N, 2 * DA), BF16),
                 jax.ShapeDtypeStruct((B, H_A, V_ROWS, T), BF16), jax.ShapeDtypeStruct((N, W_B), BF16),
                 jax.ShapeDtypeStruct((C_W, N), BF16), jax.ShapeDtypeStruct((H_C, N, HEAD_PAD), BF16),
                 jax.ShapeDtypeStruct((B, H_C, V_ROWS, T), BF16)]
    if is_ctx:
        st_w = [A_W, A_W, KV_RANK, ROPE_C]
        out_specs += [row_spec(w) for w in st_w]
        out_shape += [jax.ShapeDtypeStruct((N, w), F32) for w in st_w]
    return pl.pallas_call(
        functools.partial(_pre_mix_kernel, is_ctx, seg, nsub),
        grid=(nt,),
        in_specs=in_specs,
        out_specs=out_specs,
        out_shape=out_shape,
        compiler_params=_cparams(("arbitrary",)),
        name="pre_mix_ctx" if is_ctx else "pre_mix_lat",
    )(*args)


def _cache_kv_kernel(dk_ref, dv_ref, ckv_ref, kr_ref, wkb_ref, wvb_ref, gkc_ref, ones_ref,
                     ka_ref, vaT_ref, kc_ref, vcT_ref, kr_s):
    dk = dk_ref[0, 0]
    for hh in range(H_A):
        ka_ref[hh] = dk[:, hh * 2 * DA:(hh + 1) * 2 * DA].astype(BF16)
    _store_vT(dv_ref[0, 0], vaT_ref)
    kr_s[...] = jnp.zeros_like(kr_s)
    kr_s[:, NOPE_C:QK_C] = kr_ref[0, 0]
    _mla_kv(ckv_ref[0, 0], kr_s[...], wkb_ref, wvb_ref, gkc_ref, ones_ref.at[1], None, kc_ref, vcT_ref)


def _cache_kv(cache, l, lw):
    ck, cv, ckv, ckr = cache
    B, L, P = ck.shape[:3]
    ck = ck.reshape(B, L, P, A_W)
    cv = cv.reshape(B, L, P, A_W)
    layer = lambda w: pl.BlockSpec((1, 1, P, w), lambda i: (i, l, 0, 0))
    vT_spec = pl.BlockSpec((1, H_A, V_ROWS, P), lambda i: (i, 0, 0, 0))
    return pl.pallas_call(
        _cache_kv_kernel,
        grid=(B,),
        in_specs=[layer(A_W), layer(A_W), layer(KV_RANK), layer(ROPE_C), _const_spec((KV_RANK, C_W)),
                  _const_spec((KV_RANK, MIX_C)), _const_spec((1, HEAD_PAD)),
                  _const_spec((2, MXU_DIM, MXU_DIM))],
        out_specs=[pl.BlockSpec((H_A, P, 2 * DA), lambda i: (0, i, 0)), vT_spec,
                   pl.BlockSpec((H_C, P, HEAD_PAD), lambda i: (0, i, 0)), vT_spec],
        out_shape=[jax.ShapeDtypeStruct((H_A, B * P, 2 * DA), BF16),
                   jax.ShapeDtypeStruct((B, H_A, V_ROWS, P), BF16),
                   jax.ShapeDtypeStruct((H_C, B * P, HEAD_PAD), BF16),
                   jax.ShapeDtypeStruct((B, H_C, V_ROWS, P), BF16)],
        scratch_shapes=[pltpu.VMEM((P, HEAD_PAD), F32)],
        compiler_params=_cparams(("arbitrary",)),
        name="cache_kv",
    )(ck, cv, ckv, ckr, lw["w_kb"], lw["w_vb"], lw["gkc"], lw["b32"])


def _colmax(s):
    tk, tq = s.shape
    r = jnp.max(s.reshape(tk // ROW_GROUP, ROW_GROUP, tq), axis=0)
    return jnp.max(r, axis=0, keepdims=True)


def _attn_kernel(diff, has_ctx, G, chunks, tk, l_init, *refs):
    refs = list(refs)
    q_ref = refs.pop(0)
    if has_ctx:
        kc_ref, kl_ref, vc_ref, vl_ref = refs[:4]
        refs = refs[4:]
    else:
        kl_ref, vl_ref = refs[:2]
        kc_ref = vc_ref = None
        refs = refs[2:]
    if diff:
        lam_ref, g_ref = refs[:2]
        refs = refs[2:]
    o_ref, sa, sb, os_ref = refs
    tq = q_ref.shape[-1]
    uw = tq if diff else tq // 2
    bufs = (sa, sb)
    nk = len(chunks)
    dq = q_ref.shape[0] // G
    gpt = 1 if nk % 2 == 0 else (2 if G % 2 == 0 else G)
    ntrip = G // gpt

    def q_units(g):
        start = g * dq
        if not isinstance(start, int):
            start = pl.multiple_of(start, dq)
        qh = q_ref[pl.ds(start, dq), :]
        if diff:
            row = lax.broadcasted_iota(jnp.int32, qh.shape, 0)
            zero = jnp.zeros_like(qh)
            return [jnp.where(row < DA, qh, zero), jnp.where(row >= DA, qh, zero)]
        return [qh[:, :uw], qh[:, uw:]]

    def kslice(h, j):
        src, off = chunks[j]
        return (kc_ref if src == 0 else kl_ref)[h, pl.ds(off, tk), :]

    def vslice(h, j):
        src, off = chunks[j]
        return (vc_ref if src == 0 else vl_ref)[0, h, :, pl.ds(off, tk)]

    def qk(g, qs, j, buf):
        kj = kslice(g, j)
        for u in range(2):
            buf[u] = jnp.dot(kj, qs[u], preferred_element_type=F32)

    def softmax(buf, carry):
        ps, out = [], []
        for u in range(2):
            m, acc = carry[u]
            s = buf[u]
            m_new = jnp.maximum(m, _colmax(s))
            alpha = jnp.exp2(m - m_new)
            ps.append(jnp.exp2(s - m_new).astype(BF16))
            out.append((m_new, alpha * acc))
        return ps, out

    def pv(g, j, ps, st):
        vj = vslice(g, j)
        return [(st[u][0], st[u][1] + jnp.dot(vj, ps[u], preferred_element_type=F32)) for u in range(2)]

    def finish(g, res):
        os_ = [acc[:V_A] / acc[V_A:V_A + 1] for (_, acc) in res]
        if diff:
            lv = lam_ref[...]
            lam = (jnp.exp(jnp.sum(lv[0:1] * lv[1:2], axis=-1, keepdims=True))
                   - jnp.exp(jnp.sum(lv[2:3] * lv[3:4], axis=-1, keepdims=True)) + l_init)
            o = os_[0] - lam * os_[1]
            os_ref[g] = _rms(o, axis=0) * g_ref[...] * (1.0 - l_init)
        else:
            os_ref[g, :, 0:uw] = os_[0]
            os_ref[g, :, uw:tq] = os_[1]

    def trip(t, _):
        step = 0
        g = t * gpt
        qs = q_units(g)
        for gi in range(gpt):
            g = t * gpt + gi
            carry = [(jnp.full((1, uw), -jnp.inf, F32), jnp.zeros((V_ROWS, uw), F32)) for _ in range(2)]
            for j in range(nk):
                src, dst = bufs[step % 2], bufs[(step + 1) % 2]
                qs_next = qs
                if j + 1 < nk:
                    qk(g, qs, j + 1, dst)
                elif gi + 1 < gpt:
                    qs_next = q_units(g + 1)
                    qk(g + 1, qs_next, 0, dst)
                elif ntrip > 1:
                    g_next = jnp.minimum(g + 1, G - 1)
                    qk(g_next, q_units(g_next), 0, dst)
                ps, st = softmax(src, carry)
                carry = pv(g, j, ps, st)
                qs = qs_next
                step += 1
            finish(g, carry)
        return 0

    qk(0, q_units(0), 0, sa)
    if ntrip > 1:
        lax.fori_loop(0, ntrip, trip, 0)
    else:
        trip(0, 0)
    for c in range(G // 2):
        pair = jnp.concatenate([os_ref[2 * c], os_ref[2 * c + 1]], axis=0)
        o_ref[:, c * LANES:(c + 1) * LANES] = pair.T.astype(o_ref.dtype)


def _attention(qT, k_lat, v_lat, k_ctx=None, v_ctx=None, *, diff, tq, tk, lam_vecs=None, subln=None,
               l_init=0.0):
    B, H, _, T = v_lat.shape
    N = qT.shape[1]
    d = k_lat.shape[2]
    has_ctx = k_ctx is not None
    chunks = [(1, o) for o in range(0, T, tk)]
    nq = T // tq
    in_specs = [pl.BlockSpec((qT.shape[0], tq), lambda b, i: (0, b * nq + i))]
    args = [qT]
    if has_ctx:
        P = v_ctx.shape[3]
        chunks = [(0, o) for o in range(0, P, tk)] + chunks
        in_specs += [pl.BlockSpec((H, P, d), lambda b, i: (0, b, 0)),
                     pl.BlockSpec((H, T, d), lambda b, i: (0, b, 0)),
                     pl.BlockSpec((1, H, V_ROWS, P), lambda b, i: (b, 0, 0, 0)),
                     pl.BlockSpec((1, H, V_ROWS, T), lambda b, i: (b, 0, 0, 0))]
        args += [k_ctx, k_lat, v_ctx, v_lat]
    else:
        in_specs += [pl.BlockSpec((H, T, d), lambda b, i: (0, b, 0)),
                     pl.BlockSpec((1, H, V_ROWS, T), lambda b, i: (b, 0, 0, 0))]
        args += [k_lat, v_lat]
    if diff:
        in_specs += [pl.BlockSpec((4, DA), lambda b, i: (0, 0)),
                     pl.BlockSpec((V_A, 1), lambda b, i: (0, 0))]
        args += [lam_vecs, subln]
    uw = tq if diff else tq // 2
    return pl.pallas_call(
        functools.partial(_attn_kernel, diff, has_ctx, H, tuple(chunks), tk, l_init),
        grid=(B, nq),
        in_specs=in_specs,
        out_specs=pl.BlockSpec((tq, H * V_A), lambda b, i: (b * nq + i, 0)),
        out_shape=jax.ShapeDtypeStruct((N, H * V_A), BF16),
        scratch_shapes=[pltpu.VMEM((2, tk, uw), F32), pltpu.VMEM((2, tk, uw), F32),
                        pltpu.VMEM((H, V_A, tq), F32)],
        compiler_params=_cparams(("arbitrary", "arbitrary")),
        name="attn_diff" if diff else "attn_mla",
    )(*args)


def _post_head(x_ref, oa_ref, ob_ref, oc_ref, mod_ref, nffn_ref, wout_ref):
    mix = jnp.concatenate([oa_ref[...], ob_ref[...], oc_ref[...]], axis=-1)
    y = jnp.dot(mix, wout_ref[...], preferred_element_type=F32)
    x1 = x_ref[...] + mod_ref[0, 2:3, :] * y
    h2 = _rms(x1) * nffn_ref[...] * (1.0 + mod_ref[0, 4:5, :]) + mod_ref[0, 3:4, :]
    return x1, h2


def _swiglu(hb, wg, wu, wd):
    g = jnp.dot(hb, wg, preferred_element_type=F32)
    u = jnp.dot(hb, wu, preferred_element_type=F32)
    a = (g * jax.nn.sigmoid(g) * u).astype(BF16)
    return jnp.dot(a, wd, preferred_element_type=F32)


def _post_dense_kernel(fchunks, x_ref, oa_ref, ob_ref, oc_ref, mod_ref, nffn_ref, wout_ref,
                       wg_ref, wu_ref, wd_ref, o_ref):
    x1, h2 = _post_head(x_ref, oa_ref, ob_ref, oc_ref, mod_ref, nffn_ref, wout_ref)
    hb = h2.astype(BF16)
    acc = None
    for (f0, f1) in fchunks:
        part = _swiglu(hb, wg_ref[:, f0:f1], wu_ref[:, f0:f1], wd_ref[f0:f1, :])
        acc = part if acc is None else acc + part
    o_ref[...] = x1 + mod_ref[0, 5:6, :] * acc


def _post_dense(x, oa, ob, oc, mod, lw, *, tm, tiles_per_mod):
    N, D = x.shape
    FF = lw["wg"].shape[1]
    cut = (FF // 2 // MXU_DIM + 1) * MXU_DIM if FF > 2 * MXU_DIM else FF
    fchunks = ((0, cut), (cut, FF)) if cut < FF else ((0, FF),)
    row = lambda w: pl.BlockSpec((tm, w), lambda i: (i, 0))
    return pl.pallas_call(
        functools.partial(_post_dense_kernel, fchunks),
        grid=(N // tm,),
        in_specs=[row(D), row(MIX_A), row(W_B), row(MIX_C),
                  pl.BlockSpec((1, 6, D), lambda i: (i // tiles_per_mod, 0, 0)),
                  _const_spec((1, D)), _const_spec(lw["w_out"].shape), _const_spec((D, FF)),
                  _const_spec((D, FF)), _const_spec((FF, D))],
        out_specs=row(D),
        out_shape=jax.ShapeDtypeStruct((N, D), F32),
        compiler_params=_cparams(("arbitrary",)),
        name="post_dense",
    )(x, oa, ob, oc, mod, lw["norm_ffn"], lw["w_out"], lw["wg"], lw["wu"], lw["wd"])


TOK_SUB = 8
MOE_ROWS = 512
DISPATCH_TOKENS = 512
COMBINE_TOKENS = 512


def _to_token_tiles(x, ref):
    tm = x.shape[0]
    for a in range(TOK_SUB):
        ref[pl.ds(a, tm, stride=TOK_SUB), :] = x[:, a * LANES:(a + 1) * LANES]


def _from_token_tiles(ref, tm):
    return jnp.concatenate([ref[pl.ds(a, tm, stride=TOK_SUB), :] for a in range(TOK_SUB)], axis=-1)


def _router_kernel(x_ref, oa_ref, ob_ref, oc_ref, mod_ref, nffn_ref, wout_ref, rw_ref,
                   x1_ref, h2t_ref, route_ref):
    x1, h2 = _post_head(x_ref, oa_ref, ob_ref, oc_ref, mod_ref, nffn_ref, wout_ref)
    x1_ref[...] = x1
    _to_token_tiles(h2, h2t_ref)
    rw = rw_ref[...]
    rhi = rw.astype(BF16)
    rlo = (rw - rhi.astype(F32)).astype(BF16)
    hhi = h2.astype(BF16)
    hlo = (h2 - hhi.astype(F32)).astype(BF16)
    both = jnp.dot(hhi, jnp.concatenate([rhi, rlo], axis=-1), preferred_element_type=F32)
    logits = (both[:, :LANES] + both[:, LANES:]
              + jnp.dot(hlo, rhi, preferred_element_type=F32))
    lane = lax.broadcasted_iota(jnp.int32, logits.shape, 1)
    neg = -jnp.inf
    lg = jnp.where(lane < N_EXPERTS, logits, neg)
    v1 = jnp.max(lg, axis=-1, keepdims=True)
    i1 = jnp.min(jnp.where(lg == v1, lane, LANES), axis=-1, keepdims=True)
    lg2 = jnp.where(lane == i1, neg, lg)
    v2 = jnp.max(lg2, axis=-1, keepdims=True)
    i2 = jnp.min(jnp.where(lg2 == v2, lane, LANES), axis=-1, keepdims=True)
    e2 = jnp.exp(v2 - v1)
    g1 = 1.0 / (1.0 + e2)
    g2 = e2 / (1.0 + e2)
    route_ref[...] = jnp.where(lane == 0, g1, jnp.where(lane == 1, g2, jnp.where(
        lane == 2, i1.astype(F32), jnp.where(lane == 3, i2.astype(F32), 0.0))))


def _router(x, oa, ob, oc, mod, lw, *, tm, tiles_per_mod):
    N, D = x.shape
    assert D == TOK_SUB * LANES
    row = lambda w: pl.BlockSpec((tm, w), lambda i: (i, 0))
    return pl.pallas_call(
        _router_kernel,
        grid=(N // tm,),
        in_specs=[row(D), row(MIX_A), row(W_B), row(MIX_C),
                  pl.BlockSpec((1, 6, D), lambda i: (i // tiles_per_mod, 0, 0)),
                  _const_spec((1, D)), _const_spec(lw["w_out"].shape), _const_spec((D, LANES))],
        out_specs=[row(D), pl.BlockSpec((tm * TOK_SUB, LANES), lambda i: (i, 0)), row(LANES)],
        out_shape=[jax.ShapeDtypeStruct((N, D), F32), jax.ShapeDtypeStruct((N * TOK_SUB, LANES), F32),
                   jax.ShapeDtypeStruct((N, LANES), F32)],
        compiler_params=_cparams(("arbitrary",)),
        name="moe_router",
    )(x, oa, ob, oc, mod, lw["norm_ffn"], lw["w_out"], lw["router"])


def _moe_plan(route, N):
    R, E = MOE_ROWS, N_EXPERTS
    es = route[:, 2:4].astype(jnp.int32).reshape(-1)
    oh = (es[:, None] == jnp.arange(E, dtype=jnp.int32)[None, :]).astype(jnp.int32)
    rank = jnp.sum((jnp.cumsum(oh, axis=0) - oh) * oh, axis=1)
    cnt = jnp.sum(oh, axis=0)
    gsz = ((cnt + R - 1) // R) * R
    gend = jnp.cumsum(gsz)
    pos = ((gend - gsz)[es] + rank).astype(jnp.int32)
    n_tiles = (2 * N) // R + E
    tile_start = jnp.arange(n_tiles, dtype=jnp.int32) * R
    tile_expert = jnp.minimum(jnp.sum(tile_start[:, None] >= gend[None, :], axis=1), E - 1).astype(jnp.int32)
    ztiles = jnp.concatenate([jnp.clip(gend // R - 1, 0, n_tiles - 1),
                              jnp.arange(n_tiles - E, n_tiles)]).astype(jnp.int32)
    return pos, tile_expert, ztiles


def _tile_rows(ref, row):
    return ref.at[pl.ds(row * TOK_SUB, TOK_SUB), :]


def _dispatch_kernel(pos_ref, ztile_ref, h2t_ref, xs_hbm, buf, zbuf, sem, zsem):
    i = pl.program_id(0)
    n = pl.num_programs(0)
    tmd = DISPATCH_TOKENS
    slot = i % 2
    zrows = MOE_ROWS * TOK_SUB

    def step_wait(buf_slot):
        for _ in range(2):
            pltpu.make_async_copy(buf.at[buf_slot], xs_hbm.at[pl.ds(0, tmd * TOK_SUB), :],
                                  sem.at[buf_slot]).wait()

    @pl.when(i == 0)
    def _():
        zbuf[...] = jnp.zeros_like(zbuf)
        for z in range(ztile_ref.shape[0]):
            fill = pltpu.make_async_copy(zbuf, xs_hbm.at[pl.ds(ztile_ref[z] * zrows, zrows), :], zsem.at[0])
            fill.start()
            fill.wait()

    @pl.when(i >= 2)
    def _():
        step_wait(slot)

    buf[slot] = h2t_ref[...]
    for r in range(tmd):
        t = i * tmd + r
        for k in range(2):
            pltpu.make_async_copy(_tile_rows(buf.at[slot], r), _tile_rows(xs_hbm, pos_ref[2 * t + k]),
                                  sem.at[slot]).start(priority=k)

    @pl.when(i == n - 1)
    def _():
        step_wait(slot)

        @pl.when(n >= 2)
        def _():
            step_wait(1 - slot)


def _dispatch(h2t, pos, ztiles, n_rows):
    N = pos.shape[0] // 2
    tile = DISPATCH_TOKENS * TOK_SUB
    grid_spec = pltpu.PrefetchScalarGridSpec(
        num_scalar_prefetch=2,
        grid=(N // DISPATCH_TOKENS,),
        in_specs=[pl.BlockSpec((tile, LANES), lambda i, pos, zt: (i, 0))],
        out_specs=pl.BlockSpec(memory_space=pl.ANY),
        scratch_shapes=[pltpu.VMEM((2, tile, LANES), F32), pltpu.VMEM((MOE_ROWS * TOK_SUB, LANES), F32),
                        pltpu.SemaphoreType.DMA((2,)), pltpu.SemaphoreType.DMA((1,))],
    )
    return pl.pallas_call(
        _dispatch_kernel,
        grid_spec=grid_spec,
        out_shape=jax.ShapeDtypeStruct((n_rows * TOK_SUB, LANES), F32),
        compiler_params=_cparams(("arbitrary",)),
        name="moe_dispatch",
    )(pos, ztiles, h2t)


def _experts_kernel(te_ref, x_ref, wg_ref, wu_ref, wd_ref, y_ref):
    xb = _from_token_tiles(x_ref, MOE_ROWS).astype(BF16)
    _to_token_tiles(_swiglu(xb, wg_ref[0], wu_ref[0], wd_ref[0]), y_ref)


def _experts(xs, tile_expert, lw):
    E, D, FF = lw["wg"].shape
    R = MOE_ROWS
    n_tiles = tile_expert.shape[0]
    wspec = lambda shape: pl.BlockSpec((1,) + shape, lambda i, te: (te[i], 0, 0))
    rows = pl.BlockSpec((R * TOK_SUB, LANES), lambda i, te: (i, 0))
    grid_spec = pltpu.PrefetchScalarGridSpec(
        num_scalar_prefetch=1,
        grid=(n_tiles,),
        in_specs=[rows, wspec((D, FF)), wspec((D, FF)), wspec((FF, D))],
        out_specs=rows,
    )
    return pl.pallas_call(
        _experts_kernel,
        grid_spec=grid_spec,
        out_shape=jax.ShapeDtypeStruct(xs.shape, F32),
        compiler_params=_cparams(("arbitrary",)),
        name="moe_experts",
    )(tile_expert, xs, lw["wg"], lw["wu"], lw["wd"])


def _combine_kernel(pos_ref, x1_ref, ys_hbm, route_ref, mod_ref, o_ref, ybuf, sem):
    tm = x1_ref.shape[0]
    i = pl.program_id(0)
    n = pl.num_programs(0)
    slot = i % 2

    def issue(tile, buf_slot):
        for r in range(tm):
            t = tile * tm + r
            for k in range(2):
                pltpu.make_async_copy(_tile_rows(ys_hbm, pos_ref[2 * t + k]),
                                      _tile_rows(ybuf.at[buf_slot, k], r), sem.at[buf_slot]).start(priority=k)

    @pl.when(i == 0)
    def _():
        issue(0, 0)

    for k in range(2):
        pltpu.make_async_copy(ys_hbm.at[pl.ds(0, tm * TOK_SUB), :], ybuf.at[slot, k], sem.at[slot]).wait()

    @pl.when(i + 1 < n)
    def _():
        issue(i + 1, 1 - slot)

    g = route_ref[...]
    y = (g[:, 0:1] * _from_token_tiles(ybuf.at[slot, 0], tm)
         + g[:, 1:2] * _from_token_tiles(ybuf.at[slot, 1], tm))
    o_ref[...] = x1_ref[...] + mod_ref[0, 5:6, :] * y


def _combine(x1, ys, pos, route, mod, *, tm, tiles_per_mod):
    N, D = x1.shape
    row = lambda w: pl.BlockSpec((tm, w), lambda i, pos: (i, 0))
    grid_spec = pltpu.PrefetchScalarGridSpec(
        num_scalar_prefetch=1,
        grid=(N // tm,),
        in_specs=[row(D), pl.BlockSpec(memory_space=pl.ANY), row(LANES),
                  pl.BlockSpec((1, 6, D), lambda i, pos: (i // tiles_per_mod, 0, 0))],
        out_specs=row(D),
        scratch_shapes=[pltpu.VMEM((2, 2, tm * TOK_SUB, LANES), F32), pltpu.SemaphoreType.DMA((2,))],
    )
    return pl.pallas_call(
        _combine_kernel,
        grid_spec=grid_spec,
        out_shape=jax.ShapeDtypeStruct((N, D), F32),
        compiler_params=_cparams(("arbitrary",)),
        name="moe_combine",
    )(pos, x1, ys, route, mod)


def _post_moe(x, oa, ob, oc, mod, lw, *, tm, tiles_per_mod):
    N = x.shape[0]
    x1, h2t, route = _router(x, oa, ob, oc, mod, lw, tm=tm, tiles_per_mod=tiles_per_mod)
    pos, tile_expert, ztiles = _moe_plan(route, N)
    xs = _dispatch(h2t, pos, ztiles, tile_expert.shape[0] * MOE_ROWS)
    ys = _experts(xs, tile_expert, lw)
    tmc = min(COMBINE_TOKENS, N)
    return _combine(x1, ys, pos, route, mod, tm=tmc, tiles_per_mod=tiles_per_mod * (tm // tmc))


def _rope_tables(T):
    t = jnp.arange(T, dtype=jnp.int32)
    rows = (t // GRID_W).astype(F32)
    cols = (t % GRID_W).astype(F32)
    n = ROPE_C // 4
    inv = jnp.power(ROPE_THETA, -jnp.arange(n, dtype=F32) / n)
    j = jnp.arange(ROPE_C)
    pos = jnp.where(j[None, :] < ROPE_C // 2, rows[:, None], cols[:, None])
    ang = pos * inv[j % n][None, :]
    cos = jnp.cos(ang)
    sin = jnp.sin(ang)
    first = ((j % (2 * n)) < n)[None, :]
    s_up = jnp.where(first, -sin, 0.0)
    s_dn = jnp.where(first, 0.0, sin)
    tab32 = jnp.stack([cos, s_up, s_dn])
    tab_a = jnp.tile(tab32, (1, 1, LANES // ROPE_C))
    ident = jnp.stack([jnp.ones((T, NOPE_C), F32), jnp.zeros((T, NOPE_C), F32), jnp.zeros((T, NOPE_C), F32)])
    tail = jnp.stack([jnp.ones((T, HEAD_PAD - QK_C), F32), jnp.zeros((T, HEAD_PAD - QK_C), F32),
                      jnp.zeros((T, HEAD_PAD - QK_C), F32)])
    tab_c = jnp.concatenate([ident, tab32, tail], axis=-1)
    ang_r = inv[:, None] * rows[None, :]
    ang_c = inv[:, None] * cols[None, :]
    tab_t = jnp.stack([jnp.cos(ang_r), jnp.sin(ang_r), jnp.cos(ang_c), jnp.sin(ang_c)])
    return tab_a, tab_c, tab_t


def _pad_heads(w, width):
    lead = w.shape[:-1]
    w = w.reshape(lead + (H_C, width))
    w = jnp.pad(w, [(0, 0)] * len(lead) + [(0, 0), (0, HEAD_PAD - width)])
    return w.reshape(lead + (C_W,))


def _layer_weights(l, p):
    w_in = p["w_in"][l]
    D = w_in.shape[0]
    o_cq = 3 * A_W + 3 * W_B
    o_ckv = o_cq + Q_RANK
    o_ckr = o_ckv + KV_RANK
    w_ext = jnp.concatenate(
        [w_in[:, :o_cq], w_in[:, o_ckv:o_ckr], w_in[:, o_cq:o_ckv], w_in[:, o_ckr:],
         jnp.zeros((D, IN_EXT - OFF_CKR - ROPE_C), F32)], axis=1).astype(BF16)
    g = jnp.arange(MXU_DIM) // DA
    lw = {
        "w_in": w_ext,
        "b32": jnp.stack([(g[:, None] == g[None, :]), (g[:, None] // 4 == g[None, :] // 4)]).astype(BF16),
        "norm_mix": p["norm_mix"][l][None, :],
        "norm_ffn": p["norm_ffn"][l][None, :],
        "gqk": jnp.concatenate([jnp.tile(p["qnorm_a"][l], 2 * H_A), jnp.tile(p["knorm_a"][l], 2 * H_A)])[None, :],
        "gqa": p["norm_qa"][l][None, :],
        "gkva": p["norm_kva"][l][None, :],
        "gqc": jnp.pad(p["qnorm_c"][l], (0, HEAD_PAD - QK_C))[None, :],
        "gkc": jnp.pad(p["knorm_c"][l], (0, HEAD_PAD - QK_C))[None, :],
        "w_qb": _pad_heads(p["w_qb"][l], QK_C).astype(BF16),
        "w_kb": _pad_heads(p["w_kb"][l], NOPE_C).astype(BF16),
        "w_vb": p["w_vb"][l].astype(BF16),
        "conv_w": p["conv_w"][l],
        "conv_b": p["conv_b"][l][None, :],
        "w_out": p["w_out"][l].astype(BF16),
        "lam": jnp.stack([p["lambda_q1"][l], p["lambda_k1"][l], p["lambda_q2"][l], p["lambda_k2"][l]]),
        "subln": p["subln_a"][l][:, None],
    }
    i = l // 2
    if l % 2 == 0:
        lw["wg"] = p["ffn_w_gate"][i].astype(BF16)
        lw["wu"] = p["ffn_w_up"][i].astype(BF16)
        lw["wd"] = p["ffn_w_down"][i].astype(BF16)
    else:
        lw["router"] = jnp.pad(p["router_w"][i], ((0, 0), (0, LANES - N_EXPERTS)))
        lw["wg"] = p["moe_w_gate"][i].astype(BF16)
        lw["wu"] = p["moe_w_up"][i].astype(BF16)
        lw["wd"] = p["moe_w_down"][i].astype(BF16)
    return lw


def _mixer(x, mod, lw, l_init, *, B, T, ctx, rope):
    is_ctx = ctx is None
    tm = T if is_ctx else min(512, T)
    seg = T if is_ctx else GRID_W
    nsub = 1 if is_ctx else max(1, tm // 256)
    outs = _pre_mix(x, mod, lw, rope, B=B, T=T, is_ctx=is_ctx, tm=tm, seg=seg, nsub=nsub)
    qaT, ka, vaT, ob, qcT, kc, vcT = outs[:7]
    tq = min(256, T)
    tq_c = min(2 * tq, T)
    if is_ctx:
        tk = 512 if T % 512 == 0 else 256
        o_a = _attention(qaT, ka, vaT, diff=True, tq=tq, tk=tk, lam_vecs=lw["lam"], subln=lw["subln"],
                         l_init=l_init)
        o_c = _attention(qcT, kc, vcT, diff=False, tq=tq_c, tk=tk)
    else:
        cache, l = ctx
        P = cache[0].shape[2]
        tk = 512 if (P % 512 == 0 and T % 512 == 0) else 256
        ka_ctx, va_ctx, kc_ctx, vc_ctx = _cache_kv(cache, l, lw)
        o_a = _attention(qaT, ka, vaT, ka_ctx, va_ctx, diff=True, tq=tq, tk=tk,
                         lam_vecs=lw["lam"], subln=lw["subln"], l_init=l_init)
        o_c = _attention(qcT, kc, vcT, kc_ctx, vc_ctx, diff=False, tq=tq_c, tk=tk)
    return (o_a, ob, o_c), outs[7:]


def kernel(x_prompt, x_sample, cache_diff_k, cache_diff_v, cache_mla_ckv, cache_mla_krope, c, c_ctx, w_ada, b_ada, norm_mix, norm_ffn, w_in, qnorm_a, knorm_a, lambda_q1, lambda_k1, lambda_q2, lambda_k2, subln_a, conv_w, conv_b, norm_qa, w_qb, norm_kva, w_kb, w_vb, qnorm_c, knorm_c, w_out, ffn_w_gate, ffn_w_up, ffn_w_down, router_w, moe_w_gate, moe_w_up, moe_w_down):
    p = dict(norm_mix=norm_mix, norm_ffn=norm_ffn, w_in=w_in, qnorm_a=qnorm_a, knorm_a=knorm_a,
             lambda_q1=lambda_q1, lambda_k1=lambda_k1, lambda_q2=lambda_q2, lambda_k2=lambda_k2,
             subln_a=subln_a, conv_w=conv_w, conv_b=conv_b, norm_qa=norm_qa, w_qb=w_qb,
             norm_kva=norm_kva, w_kb=w_kb, w_vb=w_vb, qnorm_c=qnorm_c, knorm_c=knorm_c, w_out=w_out,
             ffn_w_gate=ffn_w_gate, ffn_w_up=ffn_w_up, ffn_w_down=ffn_w_down, router_w=router_w,
             moe_w_gate=moe_w_gate, moe_w_up=moe_w_up, moe_w_down=moe_w_down)
    Bc, Sc, D = x_prompt.shape
    Bl, T, _ = x_sample.shape
    L = w_in.shape[0]
    nrow = 16
    cvec = jnp.concatenate([c, c_ctx[None, :], jnp.zeros((nrow - Bl - 1, D), F32)], axis=0)
    mod = _ada(cvec, w_ada, b_ada)
    rope = _rope_tables(T)
    xp = x_prompt.reshape(Bc * Sc, D)
    xs = x_sample.reshape(Bl * T, D)
    st = [[], [], [], []]
    for l in range(L):
        lw = _layer_weights(l, p)
        l_init = 0.8 - 0.6 * math.exp(-0.3 * l)
        mod_lat = mod[l, :Bl].reshape(Bl, 6, D)
        mod_ctx = mod[l, Bl:Bl + 1].reshape(1, 6, D)
        post = _post_dense if l % 2 == 0 else _post_moe
        heads, state = _mixer(xp, mod_ctx, lw, l_init, B=Bc, T=Sc, ctx=None, rope=None)
        for acc, s in zip(st, state):
            acc.append(s)
        xp = post(xp, *heads, mod_ctx, lw, tm=min(512, Bc * Sc), tiles_per_mod=Bc * Sc)
        ctx = ((cache_diff_k, cache_diff_v, cache_mla_ckv, cache_mla_krope), l)
        heads, _ = _mixer(xs, mod_lat, lw, l_init, B=Bl, T=T, ctx=ctx, rope=rope)
        tm = min(512, T)
        xs = post(xs, *heads, mod_lat, lw, tm=tm, tiles_per_mod=T // tm)
    new_k = jnp.stack(st[0], axis=1).reshape(Bc, Sc, L, H_A, 2, DA).transpose(0, 2, 1, 3, 4, 5)
    new_v = jnp.stack(st[1], axis=1).reshape(Bc, Sc, L, H_A, V_A).transpose(0, 2, 1, 3, 4)
    new_ckv = jnp.stack(st[2], axis=1).reshape(Bc, Sc, L, KV_RANK).transpose(0, 2, 1, 3)
    new_kr = jnp.stack(st[3], axis=1).reshape(Bc, Sc, L, ROPE_C).transpose(0, 2, 1, 3)
    return (xp.reshape(Bc, Sc, D), xs.reshape(Bl, T, D), new_k, new_v, new_ckv, new_kr)
```

```python
import functools
import math

import jax
import jax.numpy as jnp
from jax import lax
from jax.experimental import pallas as pl
from jax.experimental.pallas import tpu as pltpu

F32 = jnp.float32
BF16 = jnp.bfloat16

GRID_W = 64
H_A = 6
DA = 32
V_A = 2 * DA
W_B = 256
H_C = 6
NOPE_C = 64
ROPE_C = 32
QK_C = NOPE_C + ROPE_C
V_C = 64
Q_RANK = 192
KV_RANK = 128
N_EXPERTS = 8
ROPE_THETA = 10000.0
EPS = 1e-6
LOG2E = 1.4426950408889634

LANES = 128
MXU_DIM = 256
HEAD_PAD = 128

A_W = H_A * 2 * DA
OFF_AQ = 0
OFF_AV = 2 * A_W
OFF_BB = 3 * A_W
OFF_BC = OFF_BB + W_B
OFF_BX = OFF_BC + W_B
OFF_CKV = OFF_BX + W_B
OFF_CQ = OFF_CKV + KV_RANK
OFF_CKR = OFF_CQ + Q_RANK
KR_CHUNK = (OFF_CKR // LANES) * LANES
assert OFF_CKR - KR_CHUNK == NOPE_C
IN_EXT = KR_CHUNK + LANES
C_W = H_C * HEAD_PAD
MIX_A = H_A * V_A
MIX_C = H_C * V_C

TOKEN_TILE = 512
PRE_MIX_SLICE = 256
QUERY_UNIT = 256
KEY_CHUNK = 512
ROW_GROUP = 32
V_ROWS = V_A + 16
assert V_A == V_C

VMEM_LIMIT = 56 * 1024 * 1024


def _cparams(sem):
    return pltpu.CompilerParams(dimension_semantics=sem, vmem_limit_bytes=VMEM_LIMIT)


def _const_spec(shape):
    nd = len(shape)
    return pl.BlockSpec(shape, lambda *_: (0,) * nd, pipeline_mode=pl.Buffered(1))


def _rms(x, axis=-1):
    return x * lax.rsqrt(jnp.mean(x * x, axis=axis, keepdims=True) + EPS)


def _split_dot(x, w):
    hi = x.astype(BF16)
    lo = (x - hi.astype(F32)).astype(BF16)
    return (jnp.dot(hi, w, preferred_element_type=F32)
            + jnp.dot(lo, w, preferred_element_type=F32))


def _ada_kernel(c_ref, w_ref, b_ref, o_ref):
    c = c_ref[...]
    s = (c * jax.nn.sigmoid(c)).astype(BF16)
    o_ref[0] = jnp.dot(s, w_ref[0].astype(BF16), preferred_element_type=F32) + b_ref[0]


def _ada(cvec, w_ada, b_ada):
    L, D, N6 = w_ada.shape
    R = cvec.shape[0]
    tn = 1536
    assert N6 % tn == 0
    return pl.pallas_call(
        _ada_kernel,
        grid=(L, N6 // tn),
        in_specs=[
            pl.BlockSpec((R, D), lambda l, j: (0, 0)),
            pl.BlockSpec((1, D, tn), lambda l, j: (l, 0, j)),
            pl.BlockSpec((1, 1, tn), lambda l, j: (l, 0, j)),
        ],
        out_specs=pl.BlockSpec((1, R, tn), lambda l, j: (l, 0, j)),
        out_shape=jax.ShapeDtypeStruct((L, R, N6), F32),
        compiler_params=_cparams(("arbitrary", "arbitrary")),
        name="ada_mod",
    )(cvec, w_ada, b_ada.reshape(L, 1, N6))


def _rope(x, tab_ref):
    up = pltpu.roll(x, LANES - 8, 1)
    dn = pltpu.roll(x, 8, 1)
    return x * tab_ref[0] + up * tab_ref[1] + dn * tab_ref[2]


def _rope_T(xT, tab_ref, starts):
    nb = 8
    blocks = [xT[nb * i:nb * (i + 1)] for i in range(xT.shape[0] // nb)]
    for g in starts:
        for k in range(2):
            c, s = tab_ref[2 * k], tab_ref[2 * k + 1]
            b = g // nb + 2 * k
            x1, x2 = blocks[b], blocks[b + 1]
            blocks[b] = x1 * c - x2 * s
            blocks[b + 1] = x1 * s + x2 * c
    return jnp.concatenate(blocks, axis=0)


def _store_vT(v, ref):
    tm = v.shape[0]
    vT = v.T.astype(BF16)
    row = lax.broadcasted_iota(jnp.int32, (V_ROWS - V_A, tm), 0)
    tail = jnp.where(row == 0, 1.0, 0.0).astype(BF16)
    for h in range(v.shape[1] // V_A):
        ref[0, h, 0:V_A, :] = vT[h * V_A:(h + 1) * V_A, :]
        ref[0, h, V_A:V_ROWS, :] = tail


def _head_rms(x, gain_ref, ones_ref):
    sq = x * x
    ss = jnp.concatenate(
        [_split_dot(sq[:, c:c + MXU_DIM], ones_ref[...]) for c in range(0, x.shape[1], MXU_DIM)], axis=-1)
    gain = jnp.concatenate([gain_ref[...]] * (x.shape[1] // HEAD_PAD), axis=-1)
    return x * lax.rsqrt(ss * (1.0 / QK_C) + EPS) * gain


def _mla_kv(ckv_n, kr_placed, wkb_ref, wvb_ref, gkc_ref, ones_ref, rope_ref, kc_ref, vcT_ref):
    cb = ckv_n.astype(BF16)
    kn = jnp.dot(cb, wkb_ref[...], preferred_element_type=F32)
    vc = jnp.dot(cb, wvb_ref[...], preferred_element_type=F32)
    kall = _head_rms(kn + jnp.concatenate([kr_placed] * H_C, axis=-1), gkc_ref, ones_ref)
    for h in range(H_C):
        kp = kall[:, h * HEAD_PAD:(h + 1) * HEAD_PAD]
        if rope_ref is not None:
            kp = _rope(kp, rope_ref)
        kc_ref[h] = kp.astype(BF16)
    _store_vT(vc, vcT_ref)


def _pre_mix_kernel(is_ctx, seg, nsub, *refs):
    refs = list(refs)
    sub = refs[0].shape[0] // nsub
    row_dim = {0: 0}
    n_in = 15 if is_ctx else 18
    if not is_ctx:
        row_dim.update({15: 1, 16: 1, 17: 2})
    for k, dim in enumerate((1, 1, 3, 0, 1, 1, 3) + ((0, 0, 0, 0) if is_ctx else ())):
        row_dim[n_in + k] = dim
    x_ref, mod_ref, nmix_ref, win_ref = refs[:4]
    projs = []
    for r in range(nsub):
        x = x_ref[pl.ds(r * sub, sub), :]
        h = _rms(x) * nmix_ref[...] * (1.0 + mod_ref[0, 1:2, :]) + mod_ref[0, 0:1, :]
        projs.append(jnp.dot(h.astype(BF16), win_ref[...], preferred_element_type=F32))
    for r in range(nsub):
        views = []
        for pos, ref in enumerate(refs):
            if pos in row_dim:
                idx = [slice(None)] * len(ref.shape)
                idx[row_dim[pos]] = pl.ds(r * sub, sub)
                ref = ref.at[tuple(idx)]
            views.append(ref)
        _pre_mix_rows(is_ctx, seg, projs[r], *views)


def _pre_mix_rows(is_ctx, seg, proj, *refs):
    (_, _, _, _, ones_ref, gqk_ref, gqa_ref, gkva_ref, gqc_ref,
     gkc_ref, wqb_ref, wkb_ref, wvb_ref, cw_ref, cb_ref) = refs[:15]
    b32_ref, b128_ref = ones_ref.at[0], ones_ref.at[1]
    refs = refs[15:]
    if is_ctx:
        ra_ref = rc_ref = None
    else:
        ra_ref, rc_ref, rT_ref = refs[:3]
        refs = refs[3:]
    qaT_ref, ka_ref, vaT_ref, ob_ref, qcT_ref, kc_ref, vcT_ref = refs[:7]
    refs = refs[7:]

    aqk = proj[:, OFF_AQ:OFF_AV]
    sq = aqk * aqk
    ss = jnp.concatenate(
        [_split_dot(sq[:, c:c + MXU_DIM], b32_ref[...]) for c in range(0, 2 * A_W, MXU_DIM)], axis=-1)
    aqk = aqk * lax.rsqrt(ss * (1.0 / DA) + EPS) * gqk_ref[...]
    if is_ctx:
        ka_st_ref, va_st_ref, ckv_st_ref, ckr_st_ref = refs
        ka_st_ref[...] = aqk[:, A_W:]
    qaT = (aqk[:, :A_W] * (DA ** -0.5 * LOG2E)).T
    ak = aqk[:, A_W:]
    if not is_ctx:
        qaT = _rope_T(qaT, rT_ref, range(0, A_W, DA))
        ak = jnp.concatenate([_rope(ak[:, c:c + LANES], ra_ref) for c in range(0, A_W, LANES)], axis=-1)
    qaT_ref[...] = qaT.astype(BF16)
    for hh in range(H_A):
        ka_ref[hh] = ak[:, hh * 2 * DA:(hh + 1) * 2 * DA].astype(BF16)
    av = proj[:, OFF_AV:OFF_BB]
    _store_vT(av, vaT_ref)

    bb = proj[:, OFF_BB:OFF_BC]
    u = proj[:, OFF_BC:OFF_BX] * proj[:, OFF_BX:OFF_CKV]
    tm = u.shape[0]
    row = lax.broadcasted_iota(jnp.int32, u.shape, 0) & (seg - 1)
    prev = jnp.where(row == 0, 0.0, pltpu.roll(u, 1, 0))
    nxt = jnp.where(row == seg - 1, 0.0, pltpu.roll(u, tm - 1, 0))
    conv = prev * cw_ref[0:1, :] + u * cw_ref[1:2, :] + nxt * cw_ref[2:3, :] + cb_ref[...]
    ob_ref[...] = (bb * conv).astype(BF16)

    cq = proj[:, OFF_CQ:OFF_CKR]
    cqn = (_rms(cq) * gqa_ref[...]).astype(BF16)
    qc = jnp.dot(cqn, wqb_ref[...], preferred_element_type=F32)
    qcT = (_head_rms(qc, gqc_ref, b128_ref) * (QK_C ** -0.5 * LOG2E)).T
    if not is_ctx:
        qcT = _rope_T(qcT, rT_ref, range(NOPE_C, C_W, HEAD_PAD))
    qcT_ref[...] = qcT.astype(BF16)
    ckv_n = _rms(proj[:, OFF_CKV:OFF_CQ]) * gkva_ref[...]
    krc = proj[:, KR_CHUNK:KR_CHUNK + LANES]
    lane = lax.broadcasted_iota(jnp.int32, krc.shape, 1)
    kr_placed = jnp.where((lane >= NOPE_C) & (lane < QK_C), krc, 0.0)
    _mla_kv(ckv_n, kr_placed, wkb_ref, wvb_ref, gkc_ref, b128_ref, rc_ref, kc_ref, vcT_ref)
    if is_ctx:
        va_st_ref[...] = av
        ckv_st_ref[...] = ckv_n
        ckr_st_ref[...] = krc[:, NOPE_C:QK_C]


def _pre_mix(x, mod, lw, rope, *, B, T, is_ctx, tm, seg, nsub):
    N, D = x.shape
    assert (tm // nsub) % seg == 0 and (tm // nsub) % LANES == 0
    nt = N // tm
    tps = T // tm
    row_spec = lambda w: pl.BlockSpec((tm, w), lambda i: (i, 0))
    in_specs = [
        row_spec(D),
        pl.BlockSpec((1, 6, D), (lambda i: (0, 0, 0)) if is_ctx else (lambda i: (i // tps, 0, 0))),
        _const_spec((1, D)),
        _const_spec((D, IN_EXT)),
        _const_spec((2, MXU_DIM, MXU_DIM)),
        _const_spec((1, 2 * A_W)),
        _const_spec((1, Q_RANK)),
        _const_spec((1, KV_RANK)),
        _const_spec((1, HEAD_PAD)),
        _const_spec((1, HEAD_PAD)),
        _const_spec((Q_RANK, C_W)),
        _const_spec((KV_RANK, C_W)),
        _const_spec((KV_RANK, MIX_C)),
        _const_spec((3, W_B)),
        _const_spec((1, W_B)),
    ]
    args = [x, mod, lw["norm_mix"], lw["w_in"], lw["ones"], lw["gqk"], lw["gqa"], lw["gkva"],
            lw["gqc"], lw["gkc"], lw["w_qb"], lw["w_kb"], lw["w_vb"], lw["conv_w"], lw["conv_b"]]
    if not is_ctx:
        rspec = pl.BlockSpec((3, tm, LANES), lambda i: (0, i % tps, 0))
        in_specs += [rspec, rspec, pl.BlockSpec((4, 8, tm), lambda i: (0, 0, i % tps))]
        args += list(rope)
    colT = lambda w: pl.BlockSpec((w, tm), lambda i: (0, i))
    vT_spec = pl.BlockSpec((1, H_A, V_ROWS, tm), lambda i: (i // tps, 0, 0, i % tps))
    out_specs = [colT(A_W), pl.BlockSpec((H_A, tm, 2 * DA), lambda i: (0, i, 0)), vT_spec, row_spec(W_B),
                 colT(C_W), pl.BlockSpec((H_C, tm, HEAD_PAD), lambda i: (0, i, 0)), vT_spec]
    out_shape = [jax.ShapeDtypeStruct((A_W, N), BF16), jax.ShapeDtypeStruct((H_A, N, 2 * DA), BF16),
                 jax.ShapeDtypeStruct((B, H_A, V_ROWS, T), BF16), jax.ShapeDtypeStruct((N, W_B), BF16),
                 jax.ShapeDtypeStruct((C_W, N), BF16), jax.ShapeDtypeStruct((H_C, N, HEAD_PAD), BF16),
                 jax.ShapeDtypeStruct((B, H_C, V_ROWS, T), BF16)]
    if is_ctx:
        st_w = [A_W, A_W, KV_RANK, ROPE_C]
        out_specs += [row_spec(w) for w in st_w]
        out_shape += [jax.ShapeDtypeStruct((N, w), F32) for w in st_w]
    return pl.pallas_call(
        functools.partial(_pre_mix_kernel, is_ctx, seg, nsub),
        grid=(nt,),
        in_specs=in_specs,
        out_specs=out_specs,
        out_shape=out_shape,
        compiler_params=_cparams(("arbitrary",)),
        name="pre_mix_ctx" if is_ctx else "pre_mix_lat",
    )(*args)


def _cache_kv_kernel(dk_ref, dv_ref, ckv_ref, kr_ref, wkb_ref, wvb_ref, gkc_ref, ones_ref,
                     ka_ref, vaT_ref, kc_ref, vcT_ref, kr_s):
    dk = dk_ref[0, 0]
    for hh in range(H_A):
        ka_ref[hh] = dk[:, hh * 2 * DA:(hh + 1) * 2 * DA].astype(BF16)
    _store_vT(dv_ref[0, 0], vaT_ref)
    kr_s[...] = jnp.zeros_like(kr_s)
    kr_s[:, NOPE_C:QK_C] = kr_ref[0, 0]
    _mla_kv(ckv_ref[0, 0], kr_s[...], wkb_ref, wvb_ref, gkc_ref, ones_ref.at[1], None, kc_ref, vcT_ref)


def _cache_kv(cache, l, lw):
    ck, cv, ckv, ckr = cache
    B, L, P = ck.shape[:3]
    ck = ck.reshape(B, L, P, A_W)
    cv = cv.reshape(B, L, P, A_W)
    layer = lambda w: pl.BlockSpec((1, 1, P, w), lambda i: (i, l, 0, 0))
    vT_spec = pl.BlockSpec((1, H_A, V_ROWS, P), lambda i: (i, 0, 0, 0))
    return pl.pallas_call(
        _cache_kv_kernel,
        grid=(B,),
        in_specs=[layer(A_W), layer(A_W), layer(KV_RANK), layer(ROPE_C), _const_spec((KV_RANK, C_W)),
                  _const_spec((KV_RANK, MIX_C)), _const_spec((1, HEAD_PAD)),
                  _const_spec((2, MXU_DIM, MXU_DIM))],
        out_specs=[pl.BlockSpec((H_A, P, 2 * DA), lambda i: (0, i, 0)), vT_spec,
                   pl.BlockSpec((H_C, P, HEAD_PAD), lambda i: (0, i, 0)), vT_spec],
        out_shape=[jax.ShapeDtypeStruct((H_A, B * P, 2 * DA), BF16),
                   jax.ShapeDtypeStruct((B, H_A, V_ROWS, P), BF16),
                   jax.ShapeDtypeStruct((H_C, B * P, HEAD_PAD), BF16),
                   jax.ShapeDtypeStruct((B, H_C, V_ROWS, P), BF16)],
        scratch_shapes=[pltpu.VMEM((P, HEAD_PAD), F32)],
        compiler_params=_cparams(("arbitrary",)),
        name="cache_kv",
    )(ck, cv, ckv, ckr, lw["w_kb"], lw["w_vb"], lw["gkc"], lw["ones"])


def _colmax(s):
    tk, tq = s.shape
    r = jnp.max(s.reshape(tk // ROW_GROUP, ROW_GROUP, tq), axis=0)
    return jnp.max(r, axis=0, keepdims=True)


def _attn_kernel(diff, has_ctx, G, chunks, tk, l_init, *refs):
    refs = list(refs)
    q_ref = refs.pop(0)
    if has_ctx:
        kc_ref, kl_ref, vc_ref, vl_ref = refs[:4]
        refs = refs[4:]
    else:
        kl_ref, vl_ref = refs[:2]
        kc_ref = vc_ref = None
        refs = refs[2:]
    if diff:
        lam_ref, g_ref = refs[:2]
        refs = refs[2:]
    o_ref, sa, sb, os_ref = refs
    tq = q_ref.shape[-1]
    uw = tq if diff else tq // 2
    bufs = (sa, sb)
    nk = len(chunks)
    dq = q_ref.shape[0] // G
    gpt = 1 if nk % 2 == 0 else (2 if G % 2 == 0 else G)
    ntrip = G // gpt

    def q_units(g):
        start = g * dq
        if not isinstance(start, int):
            start = pl.multiple_of(start, dq)
        qh = q_ref[pl.ds(start, dq), :]
        if diff:
            row = lax.broadcasted_iota(jnp.int32, qh.shape, 0)
            zero = jnp.zeros_like(qh)
            return [jnp.where(row < DA, qh, zero), jnp.where(row >= DA, qh, zero)]
        return [qh[:, :uw], qh[:, uw:]]

    def kslice(h, j):
        src, off = chunks[j]
        return (kc_ref if src == 0 else kl_ref)[h, pl.ds(off, tk), :]

    def vslice(h, j):
        src, off = chunks[j]
        return (vc_ref if src == 0 else vl_ref)[0, h, :, pl.ds(off, tk)]

    def qk(g, qs, j, buf):
        kj = kslice(g, j)
        for u in range(2):
            buf[u] = jnp.dot(kj, qs[u], preferred_element_type=F32)

    def softmax(buf, carry):
        ps, out = [], []
        for u in range(2):
            m, acc = carry[u]
            s = buf[u]
            m_new = jnp.maximum(m, _colmax(s))
            alpha = jnp.exp2(m - m_new)
            ps.append(jnp.exp2(s - m_new).astype(BF16))
            out.append((m_new, alpha * acc))
        return ps, out

    def pv(g, j, ps, st):
        vj = vslice(g, j)
        return [(st[u][0], st[u][1] + jnp.dot(vj, ps[u], preferred_element_type=F32)) for u in range(2)]

    def finish(g, res):
        os_ = [acc[:V_A] / acc[V_A:V_A + 1] for (_, acc) in res]
        if diff:
            lv = lam_ref[...]
            lam = (jnp.exp(jnp.sum(lv[0:1] * lv[1:2], axis=-1, keepdims=True))
                   - jnp.exp(jnp.sum(lv[2:3] * lv[3:4], axis=-1, keepdims=True)) + l_init)
            o = os_[0] - lam * os_[1]
            os_ref[g] = _rms(o, axis=0) * g_ref[...] * (1.0 - l_init)
        else:
            os_ref[g, :, 0:uw] = os_[0]
            os_ref[g, :, uw:tq] = os_[1]

    def trip(t, _):
        step = 0
        g = t * gpt
        qs = q_units(g)
        for gi in range(gpt):
            g = t * gpt + gi
            carry = [(jnp.full((1, uw), -jnp.inf, F32), jnp.zeros((V_ROWS, uw), F32)) for _ in range(2)]
            for j in range(nk):
                src, dst = bufs[step % 2], bufs[(step + 1) % 2]
                qs_next = qs
                if j + 1 < nk:
                    qk(g, qs, j + 1, dst)
                elif gi + 1 < gpt:
                    qs_next = q_units(g + 1)
                    qk(g + 1, qs_next, 0, dst)
                elif ntrip > 1:
                    g_next = jnp.minimum(g + 1, G - 1)
                    qk(g_next, q_units(g_next), 0, dst)
                ps, st = softmax(src, carry)
                carry = pv(g, j, ps, st)
                qs = qs_next
                step += 1
            finish(g, carry)
        return 0

    qk(0, q_units(0), 0, sa)
    if ntrip > 1:
        lax.fori_loop(0, ntrip, trip, 0)
    else:
        trip(0, 0)
    for c in range(G // 2):
        pair = jnp.concatenate([os_ref[2 * c], os_ref[2 * c + 1]], axis=0)
        o_ref[:, c * LANES:(c + 1) * LANES] = pair.T.astype(o_ref.dtype)


def _attention(qT, k_lat, v_lat, k_ctx=None, v_ctx=None, *, diff, tq, tk, lam_vecs=None, subln=None,
               l_init=0.0):
    B, H, _, T = v_lat.shape
    N = qT.shape[1]
    d = k_lat.shape[2]
    has_ctx = k_ctx is not None
    chunks = [(1, o) for o in range(0, T, tk)]
    nq = T // tq
    in_specs = [pl.BlockSpec((qT.shape[0], tq), lambda b, i: (0, b * nq + i))]
    args = [qT]
    if has_ctx:
        P = v_ctx.shape[3]
        chunks = [(0, o) for o in range(0, P, tk)] + chunks
        in_specs += [pl.BlockSpec((H, P, d), lambda b, i: (0, b, 0)),
                     pl.BlockSpec((H, T, d), lambda b, i: (0, b, 0)),
                     pl.BlockSpec((1, H, V_ROWS, P), lambda b, i: (b, 0, 0, 0)),
                     pl.BlockSpec((1, H, V_ROWS, T), lambda b, i: (b, 0, 0, 0))]
        args += [k_ctx, k_lat, v_ctx, v_lat]
    else:
        in_specs += [pl.BlockSpec((H, T, d), lambda b, i: (0, b, 0)),
                     pl.BlockSpec((1, H, V_ROWS, T), lambda b, i: (b, 0, 0, 0))]
        args += [k_lat, v_lat]
    if diff:
        in_specs += [pl.BlockSpec((4, DA), lambda b, i: (0, 0)),
                     pl.BlockSpec((V_A, 1), lambda b, i: (0, 0))]
        args += [lam_vecs, subln]
    uw = tq if diff else tq // 2
    return pl.pallas_call(
        functools.partial(_attn_kernel, diff, has_ctx, H, tuple(chunks), tk, l_init),
        grid=(B, nq),
        in_specs=in_specs,
        out_specs=pl.BlockSpec((tq, H * V_A), lambda b, i: (b * nq + i, 0)),
        out_shape=jax.ShapeDtypeStruct((N, H * V_A), BF16),
        scratch_shapes=[pltpu.VMEM((2, tk, uw), F32), pltpu.VMEM((2, tk, uw), F32),
                        pltpu.VMEM((H, V_A, tq), F32)],
        compiler_params=_cparams(("arbitrary", "arbitrary")),
        name="attn_diff" if diff else "attn_mla",
    )(*args)


def _post_head(x_ref, oa_ref, ob_ref, oc_ref, mod_ref, nffn_ref, wout_ref):
    mix = jnp.concatenate([oa_ref[...], ob_ref[...], oc_ref[...]], axis=-1)
    y = jnp.dot(mix, wout_ref[...], preferred_element_type=F32)
    x1 = x_ref[...] + mod_ref[0, 2:3, :] * y
    h2 = _rms(x1) * nffn_ref[...] * (1.0 + mod_ref[0, 4:5, :]) + mod_ref[0, 3:4, :]
    return x1, h2


def _swiglu(hb, wg, wu, wd):
    g = jnp.dot(hb, wg, preferred_element_type=F32)
    u = jnp.dot(hb, wu, preferred_element_type=F32)
    a = (g * jax.nn.sigmoid(g) * u).astype(BF16)
    return jnp.dot(a, wd, preferred_element_type=F32)


def _post_dense_kernel(fchunks, x_ref, oa_ref, ob_ref, oc_ref, mod_ref, nffn_ref, wout_ref,
                       wg_ref, wu_ref, wd_ref, o_ref):
    x1, h2 = _post_head(x_ref, oa_ref, ob_ref, oc_ref, mod_ref, nffn_ref, wout_ref)
    hb = h2.astype(BF16)
    acc = None
    for (f0, f1) in fchunks:
        part = _swiglu(hb, wg_ref[:, f0:f1], wu_ref[:, f0:f1], wd_ref[f0:f1, :])
        acc = part if acc is None else acc + part
    o_ref[...] = x1 + mod_ref[0, 5:6, :] * acc


def _post_dense(x, oa, ob, oc, mod, lw, *, tm, tiles_per_mod):
    N, D = x.shape
    FF = lw["wg"].shape[1]
    cut = (FF // 2 // MXU_DIM + 1) * MXU_DIM if FF > 2 * MXU_DIM else FF
    fchunks = ((0, cut), (cut, FF)) if cut < FF else ((0, FF),)
    row = lambda w: pl.BlockSpec((tm, w), lambda i: (i, 0))
    return pl.pallas_call(
        functools.partial(_post_dense_kernel, fchunks),
        grid=(N // tm,),
        in_specs=[row(D), row(MIX_A), row(W_B), row(MIX_C),
                  pl.BlockSpec((1, 6, D), lambda i: (i // tiles_per_mod, 0, 0)),
                  _const_spec((1, D)), _const_spec(lw["w_out"].shape), _const_spec((D, FF)),
                  _const_spec((D, FF)), _const_spec((FF, D))],
        out_specs=row(D),
        out_shape=jax.ShapeDtypeStruct((N, D), F32),
        compiler_params=_cparams(("arbitrary",)),
        name="post_dense",
    )(x, oa, ob, oc, mod, lw["norm_ffn"], lw["w_out"], lw["wg"], lw["wu"], lw["wd"])


TOK_SUB = 8
MOE_ROWS = 512
DISPATCH_TOKENS = 512
COMBINE_TOKENS = 512


def _to_token_tiles(x, ref):
    tm = x.shape[0]
    for a in range(TOK_SUB):
        ref[pl.ds(a, tm, stride=TOK_SUB), :] = x[:, a * LANES:(a + 1) * LANES]


def _from_token_tiles(ref, tm):
    return jnp.concatenate([ref[pl.ds(a, tm, stride=TOK_SUB), :] for a in range(TOK_SUB)], axis=-1)


def _router_kernel(x_ref, oa_ref, ob_ref, oc_ref, mod_ref, nffn_ref, wout_ref, rw_ref,
                   x1_ref, h2t_ref, route_ref):
    x1, h2 = _post_head(x_ref, oa_ref, ob_ref, oc_ref, mod_ref, nffn_ref, wout_ref)
    x1_ref[...] = x1
    _to_token_tiles(h2, h2t_ref)
    rw = rw_ref[...]
    rhi = rw.astype(BF16)
    rlo = (rw - rhi.astype(F32)).astype(BF16)
    hhi = h2.astype(BF16)
    hlo = (h2 - hhi.astype(F32)).astype(BF16)
    both = jnp.dot(hhi, jnp.concatenate([rhi, rlo], axis=-1), preferred_element_type=F32)
    logits = (both[:, :LANES] + both[:, LANES:]
              + jnp.dot(hlo, rhi, preferred_element_type=F32))
    lane = lax.broadcasted_iota(jnp.int32, logits.shape, 1)
    neg = -jnp.inf
    lg = jnp.where(lane < N_EXPERTS, logits, neg)
    v1 = jnp.max(lg, axis=-1, keepdims=True)
    i1 = jnp.min(jnp.where(lg == v1, lane, LANES), axis=-1, keepdims=True)
    lg2 = jnp.where(lane == i1, neg, lg)
    v2 = jnp.max(lg2, axis=-1, keepdims=True)
    i2 = jnp.min(jnp.where(lg2 == v2, lane, LANES), axis=-1, keepdims=True)
    e2 = jnp.exp(v2 - v1)
    g1 = 1.0 / (1.0 + e2)
    g2 = e2 / (1.0 + e2)
    route_ref[...] = jnp.where(lane == 0, g1, jnp.where(lane == 1, g2, jnp.where(
        lane == 2, i1.astype(F32), jnp.where(lane == 3, i2.astype(F32), 0.0))))


def _router(x, oa, ob, oc, mod, lw, *, tm, tiles_per_mod):
    N, D = x.shape
    assert D == TOK_SUB * LANES
    row = lambda w: pl.BlockSpec((tm, w), lambda i: (i, 0))
    return pl.pallas_call(
        _router_kernel,
        grid=(N // tm,),
        in_specs=[row(D), row(MIX_A), row(W_B), row(MIX_C),
                  pl.BlockSpec((1, 6, D), lambda i: (i // tiles_per_mod, 0, 0)),
                  _const_spec((1, D)), _const_spec(lw["w_out"].shape), _const_spec((D, LANES))],
        out_specs=[row(D), pl.BlockSpec((tm * TOK_SUB, LANES), lambda i: (i, 0)), row(LANES)],
        out_shape=[jax.ShapeDtypeStruct((N, D), F32), jax.ShapeDtypeStruct((N * TOK_SUB, LANES), F32),
                   jax.ShapeDtypeStruct((N, LANES), F32)],
        compiler_params=_cparams(("arbitrary",)),
        name="moe_router",
    )(x, oa, ob, oc, mod, lw["norm_ffn"], lw["w_out"], lw["router"])


def _moe_plan(route, N):
    R, E = MOE_ROWS, N_EXPERTS
    es = route[:, 2:4].astype(jnp.int32).reshape(-1)
    oh = (es[:, None] == jnp.arange(E, dtype=jnp.int32)[None, :]).astype(jnp.int32)
    rank = jnp.sum((jnp.cumsum(oh, axis=0) - oh) * oh, axis=1)
    cnt = jnp.sum(oh, axis=0)
    gsz = ((cnt + R - 1) // R) * R
    gend = jnp.cumsum(gsz)
    pos = ((gend - gsz)[es] + rank).astype(jnp.int32)
    n_tiles = (2 * N) // R + E
    tile_start = jnp.arange(n_tiles, dtype=jnp.int32) * R
    tile_expert = jnp.minimum(jnp.sum(tile_start[:, None] >= gend[None, :], axis=1), E - 1).astype(jnp.int32)
    ztiles = jnp.concatenate([jnp.clip(gend // R - 1, 0, n_tiles - 1),
                              jnp.arange(n_tiles - E, n_tiles)]).astype(jnp.int32)
    return pos, tile_expert, ztiles


def _tile_rows(ref, row):
    return ref.at[pl.ds(row * TOK_SUB, TOK_SUB), :]


def _dispatch_kernel(pos_ref, ztile_ref, h2t_ref, xs_hbm, buf, zbuf, sem, zsem):
    i = pl.program_id(0)
    n = pl.num_programs(0)
    tmd = DISPATCH_TOKENS
    slot = i % 2
    zrows = MOE_ROWS * TOK_SUB

    def step_wait(buf_slot):
        for _ in range(2):
            pltpu.make_async_copy(buf.at[buf_slot], xs_hbm.at[pl.ds(0, tmd * TOK_SUB), :],
                                  sem.at[buf_slot]).wait()

    @pl.when(i == 0)
    def _():
        zbuf[...] = jnp.zeros_like(zbuf)
        for z in range(ztile_ref.shape[0]):
            fill = pltpu.make_async_copy(zbuf, xs_hbm.at[pl.ds(ztile_ref[z] * zrows, zrows), :], zsem.at[0])
            fill.start()
            fill.wait()

    @pl.when(i >= 2)
    def _():
        step_wait(slot)

    buf[slot] = h2t_ref[...]
    for r in range(tmd):
        t = i * tmd + r
        for k in range(2):
            pltpu.make_async_copy(_tile_rows(buf.at[slot], r), _tile_rows(xs_hbm, pos_ref[2 * t + k]),
                                  sem.at[slot]).start(priority=k)

    @pl.when(i == n - 1)
    def _():
        step_wait(slot)

        @pl.when(n >= 2)
        def _():
            step_wait(1 - slot)


def _dispatch(h2t, pos, ztiles, n_rows):
    N = pos.shape[0] // 2
    tile = DISPATCH_TOKENS * TOK_SUB
    grid_spec = pltpu.PrefetchScalarGridSpec(
        num_scalar_prefetch=2,
        grid=(N // DISPATCH_TOKENS,),
        in_specs=[pl.BlockSpec((tile, LANES), lambda i, pos, zt: (i, 0))],
        out_specs=pl.BlockSpec(memory_space=pl.ANY),
        scratch_shapes=[pltpu.VMEM((2, tile, LANES), F32), pltpu.VMEM((MOE_ROWS * TOK_SUB, LANES), F32),
                        pltpu.SemaphoreType.DMA((2,)), pltpu.SemaphoreType.DMA((1,))],
    )
    return pl.pallas_call(
        _dispatch_kernel,
        grid_spec=grid_spec,
        out_shape=jax.ShapeDtypeStruct((n_rows * TOK_SUB, LANES), F32),
        compiler_params=_cparams(("arbitrary",)),
        name="moe_dispatch",
    )(pos, ztiles, h2t)


def _experts_kernel(te_ref, x_ref, wg_ref, wu_ref, wd_ref, y_ref):
    xb = _from_token_tiles(x_ref, MOE_ROWS).astype(BF16)
    _to_token_tiles(_swiglu(xb, wg_ref[0], wu_ref[0], wd_ref[0]), y_ref)


def _experts(xs, tile_expert, lw):
    E, D, FF = lw["wg"].shape
    R = MOE_ROWS
    n_tiles = tile_expert.shape[0]
    wspec = lambda shape: pl.BlockSpec((1,) + shape, lambda i, te: (te[i], 0, 0))
    rows = pl.BlockSpec((R * TOK_SUB, LANES), lambda i, te: (i, 0))
    grid_spec = pltpu.PrefetchScalarGridSpec(
        num_scalar_prefetch=1,
        grid=(n_tiles,),
        in_specs=[rows, wspec((D, FF)), wspec((D, FF)), wspec((FF, D))],
        out_specs=rows,
    )
    return pl.pallas_call(
        _experts_kernel,
        grid_spec=grid_spec,
        out_shape=jax.ShapeDtypeStruct(xs.shape, F32),
        compiler_params=_cparams(("arbitrary",)),
        name="moe_experts",
    )(tile_expert, xs, lw["wg"], lw["wu"], lw["wd"])


def _combine_kernel(pos_ref, x1_ref, ys_hbm, route_ref, mod_ref, o_ref, ybuf, sem):
    tm = x1_ref.shape[0]
    i = pl.program_id(0)
    n = pl.num_programs(0)
    slot = i % 2

    def issue(tile, buf_slot):
        for r in range(tm):
            t = tile * tm + r
            for k in range(2):
                pltpu.make_async_copy(_tile_rows(ys_hbm, pos_ref[2 * t + k]),
                                      _tile_rows(ybuf.at[buf_slot, k], r), sem.at[buf_slot]).start(priority=k)

    @pl.when(i == 0)
    def _():
        issue(0, 0)

    for k in range(2):
        pltpu.make_async_copy(ys_hbm.at[pl.ds(0, tm * TOK_SUB), :], ybuf.at[slot, k], sem.at[slot]).wait()

    @pl.when(i + 1 < n)
    def _():
        issue(i + 1, 1 - slot)

    g = route_ref[...]
    y = (g[:, 0:1] * _from_token_tiles(ybuf.at[slot, 0], tm)
         + g[:, 1:2] * _from_token_tiles(ybuf.at[slot, 1], tm))
    o_ref[...] = x1_ref[...] + mod_ref[0, 5:6, :] * y


def _combine(x1, ys, pos, route, mod, *, tm, tiles_per_mod):
    N, D = x1.shape
    row = lambda w: pl.BlockSpec((tm, w), lambda i, pos: (i, 0))
    grid_spec = pltpu.PrefetchScalarGridSpec(
        num_scalar_prefetch=1,
        grid=(N // tm,),
        in_specs=[row(D), pl.BlockSpec(memory_space=pl.ANY), row(LANES),
                  pl.BlockSpec((1, 6, D), lambda i, pos: (i // tiles_per_mod, 0, 0))],
        out_specs=row(D),
        scratch_shapes=[pltpu.VMEM((2, 2, tm * TOK_SUB, LANES), F32), pltpu.SemaphoreType.DMA((2,))],
    )
    return pl.pallas_call(
        _combine_kernel,
        grid_spec=grid_spec,
        out_shape=jax.ShapeDtypeStruct((N, D), F32),
        compiler_params=_cparams(("arbitrary",)),
        name="moe_combine",
    )(pos, x1, ys, route, mod)


def _post_moe(x, oa, ob, oc, mod, lw, *, tm, tiles_per_mod):
    N = x.shape[0]
    x1, h2t, route = _router(x, oa, ob, oc, mod, lw, tm=tm, tiles_per_mod=tiles_per_mod)
    pos, tile_expert, ztiles = _moe_plan(route, N)
    xs = _dispatch(h2t, pos, ztiles, tile_expert.shape[0] * MOE_ROWS)
    ys = _experts(xs, tile_expert, lw)
    tmc = min(COMBINE_TOKENS, N)
    return _combine(x1, ys, pos, route, mod, tm=tmc, tiles_per_mod=tiles_per_mod * (tm // tmc))


def _rope_tables(T):
    t = jnp.arange(T, dtype=jnp.int32)
    rows = (t // GRID_W).astype(F32)
    cols = (t % GRID_W).astype(F32)
    n = ROPE_C // 4
    inv = jnp.power(ROPE_THETA, -jnp.arange(n, dtype=F32) / n)
    j = jnp.arange(ROPE_C)
    pos = jnp.where(j[None, :] < ROPE_C // 2, rows[:, None], cols[:, None])
    ang = pos * inv[j % n][None, :]
    cos = jnp.cos(ang)
    sin = jnp.sin(ang)
    first = ((j % (2 * n)) < n)[None, :]
    s_up = jnp.where(first, -sin, 0.0)
    s_dn = jnp.where(first, 0.0, sin)
    tab32 = jnp.stack([cos, s_up, s_dn])
    tab_a = jnp.tile(tab32, (1, 1, LANES // ROPE_C))
    ident = jnp.stack([jnp.ones((T, NOPE_C), F32), jnp.zeros((T, NOPE_C), F32), jnp.zeros((T, NOPE_C), F32)])
    tail = jnp.stack([jnp.ones((T, HEAD_PAD - QK_C), F32), jnp.zeros((T, HEAD_PAD - QK_C), F32),
                      jnp.zeros((T, HEAD_PAD - QK_C), F32)])
    tab_c = jnp.concatenate([ident, tab32, tail], axis=-1)
    ang_r = inv[:, None] * rows[None, :]
    ang_c = inv[:, None] * cols[None, :]
    tab_t = jnp.stack([jnp.cos(ang_r), jnp.sin(ang_r), jnp.cos(ang_c), jnp.sin(ang_c)])
    return tab_a, tab_c, tab_t


def _pad_heads(w, width):
    lead = w.shape[:-1]
    w = w.reshape(lead + (H_C, width))
    w = jnp.pad(w, [(0, 0)] * len(lead) + [(0, 0), (0, HEAD_PAD - width)])
    return w.reshape(lead + (C_W,))


def _layer_weights(l, p):
    w_in = p["w_in"][l]
    D = w_in.shape[0]
    o_cq = 3 * A_W + 3 * W_B
    o_ckv = o_cq + Q_RANK
    o_ckr = o_ckv + KV_RANK
    w_ext = jnp.concatenate(
        [w_in[:, :o_cq], w_in[:, o_ckv:o_ckr], w_in[:, o_cq:o_ckv], w_in[:, o_ckr:],
         jnp.zeros((D, IN_EXT - OFF_CKR - ROPE_C), F32)], axis=1).astype(BF16)
    g = jnp.arange(MXU_DIM) // DA
    lw = {
        "w_in": w_ext,
        "ones": jnp.stack([(g[:, None] == g[None, :]), (g[:, None] // 4 == g[None, :] // 4)]).astype(BF16),
        "norm_mix": p["norm_mix"][l][None, :],
        "norm_ffn": p["norm_ffn"][l][None, :],
        "gqk": jnp.concatenate([jnp.tile(p["qnorm_a"][l], 2 * H_A), jnp.tile(p["knorm_a"][l], 2 * H_A)])[None, :],
        "gqa": p["norm_qa"][l][None, :],
        "gkva": p["norm_kva"][l][None, :],
        "gqc": jnp.pad(p["qnorm_c"][l], (0, HEAD_PAD - QK_C))[None, :],
        "gkc": jnp.pad(p["knorm_c"][l], (0, HEAD_PAD - QK_C))[None, :],
        "w_qb": _pad_heads(p["w_qb"][l], QK_C).astype(BF16),
        "w_kb": _pad_heads(p["w_kb"][l], NOPE_C).astype(BF16),
        "w_vb": p["w_vb"][l].astype(BF16),
        "conv_w": p["conv_w"][l],
        "conv_b": p["conv_b"][l][None, :],
        "w_out": p["w_out"][l].astype(BF16),
        "lam": jnp.stack([p["lambda_q1"][l], p["lambda_k1"][l], p["lambda_q2"][l], p["lambda_k2"][l]]),
        "subln": p["subln_a"][l][:, None],
    }
    i = l // 2
    if l % 2 == 0:
        lw["wg"] = p["ffn_w_gate"][i].astype(BF16)
        lw["wu"] = p["ffn_w_up"][i].astype(BF16)
        lw["wd"] = p["ffn_w_down"][i].astype(BF16)
    else:
        lw["router"] = jnp.pad(p["router_w"][i], ((0, 0), (0, LANES - N_EXPERTS)))
        lw["wg"] = p["moe_w_gate"][i].astype(BF16)
        lw["wu"] = p["moe_w_up"][i].astype(BF16)
        lw["wd"] = p["moe_w_down"][i].astype(BF16)
    return lw


def _mixer(x, mod, lw, l_init, *, B, T, ctx, rope):
    is_ctx = ctx is None
    tm = T if is_ctx else min(TOKEN_TILE, T)
    seg = T if is_ctx else GRID_W
    nsub = 1 if is_ctx else max(1, tm // PRE_MIX_SLICE)
    outs = _pre_mix(x, mod, lw, rope, B=B, T=T, is_ctx=is_ctx, tm=tm, seg=seg, nsub=nsub)
    qaT, ka, vaT, ob, qcT, kc, vcT = outs[:7]
    tq = min(QUERY_UNIT, T)
    tq_c = min(2 * tq, T)
    small = KEY_CHUNK // 2
    if is_ctx:
        tk = KEY_CHUNK if T % KEY_CHUNK == 0 else small
        o_a = _attention(qaT, ka, vaT, diff=True, tq=tq, tk=tk, lam_vecs=lw["lam"], subln=lw["subln"],
                         l_init=l_init)
        o_c = _attention(qcT, kc, vcT, diff=False, tq=tq_c, tk=tk)
    else:
        cache, l = ctx
        P = cache[0].shape[2]
        tk = KEY_CHUNK if (P % KEY_CHUNK == 0 and T % KEY_CHUNK == 0) else small
        ka_ctx, va_ctx, kc_ctx, vc_ctx = _cache_kv(cache, l, lw)
        o_a = _attention(qaT, ka, vaT, ka_ctx, va_ctx, diff=True, tq=tq, tk=tk,
                         lam_vecs=lw["lam"], subln=lw["subln"], l_init=l_init)
        o_c = _attention(qcT, kc, vcT, kc_ctx, vc_ctx, diff=False, tq=tq_c, tk=tk)
    return (o_a, ob, o_c), outs[7:]


def kernel(x_prompt, x_sample, cache_diff_k, cache_diff_v, cache_mla_ckv, cache_mla_krope, c, c_ctx, w_ada, b_ada, norm_mix, norm_ffn, w_in, qnorm_a, knorm_a, lambda_q1, lambda_k1, lambda_q2, lambda_k2, subln_a, conv_w, conv_b, norm_qa, w_qb, norm_kva, w_kb, w_vb, qnorm_c, knorm_c, w_out, ffn_w_gate, ffn_w_up, ffn_w_down, router_w, moe_w_gate, moe_w_up, moe_w_down):
    p = dict(norm_mix=norm_mix, norm_ffn=norm_ffn, w_in=w_in, qnorm_a=qnorm_a, knorm_a=knorm_a,
             lambda_q1=lambda_q1, lambda_k1=lambda_k1, lambda_q2=lambda_q2, lambda_k2=lambda_k2,
             subln_a=subln_a, conv_w=conv_w, conv_b=conv_b, norm_qa=norm_qa, w_qb=w_qb,
             norm_kva=norm_kva, w_kb=w_kb, w_vb=w_vb, qnorm_c=qnorm_c, knorm_c=knorm_c, w_out=w_out,
             ffn_w_gate=ffn_w_gate, ffn_w_up=ffn_w_up, ffn_w_down=ffn_w_down, router_w=router_w,
             moe_w_gate=moe_w_gate, moe_w_up=moe_w_up, moe_w_down=moe_w_down)
    Bc, Sc, D = x_prompt.shape
    Bl, T, _ = x_sample.shape
    L = w_in.shape[0]
    nrow = 16
    cvec = jnp.concatenate([c, c_ctx[None, :], jnp.zeros((nrow - Bl - 1, D), F32)], axis=0)
    mod = _ada(cvec, w_ada, b_ada)
    rope = _rope_tables(T)
    xp = x_prompt.reshape(Bc * Sc, D)
    xs = x_sample.reshape(Bl * T, D)
    st = [[], [], [], []]
    for l in range(L):
        lw = _layer_weights(l, p)
        l_init = 0.8 - 0.6 * math.exp(-0.3 * l)
        mod_lat = mod[l, :Bl].reshape(Bl, 6, D)
        mod_ctx = mod[l, Bl:Bl + 1].reshape(1, 6, D)
        post = _post_dense if l % 2 == 0 else _post_moe
        heads, state = _mixer(xp, mod_ctx, lw, l_init, B=Bc, T=Sc, ctx=None, rope=None)
        for acc, s in zip(st, state):
            acc.append(s)
        xp = post(xp, *heads, mod_ctx, lw, tm=min(TOKEN_TILE, Bc * Sc), tiles_per_mod=Bc * Sc)
        ctx = ((cache_diff_k, cache_diff_v, cache_mla_ckv, cache_mla_krope), l)
        heads, _ = _mixer(xs, mod_lat, lw, l_init, B=Bl, T=T, ctx=ctx, rope=rope)
        tm = min(TOKEN_TILE, T)
        xs = post(xs, *heads, mod_lat, lw, tm=tm, tiles_per_mod=T // tm)
    new_k = jnp.stack(st[0], axis=1).reshape(Bc, Sc, L, H_A, 2, DA).transpose(0, 2, 1, 3, 4, 5)
    new_v = jnp.stack(st[1], axis=1).reshape(Bc, Sc, L, H_A, V_A).transpose(0, 2, 1, 3, 4)
    new_ckv = jnp.stack(st[2], axis=1).reshape(Bc, Sc, L, KV_RANK).transpose(0, 2, 1, 3)
    new_kr = jnp.stack(st[3], axis=1).reshape(Bc, Sc, L, ROPE_C).transpose(0, 2, 1, 3)
    return (xp.reshape(Bc, Sc, D), xs.reshape(Bl, T, D), new_k, new_v, new_ckv, new_kr)
```
